```python
import math
import jax
import jax.numpy as jnp
from jax import lax
import numpy as np

D_MODEL = 1024
BATCH = 2
SEQ = 8192
DEPTH = 2
DEC_BATCH = 128
DEC_SEQ = 8
PAST_LEN = 2048
PAGE_SIZE = 128

MIX = D_MODEL
HEAD_DIM = 64
NSA_WIDTH = D_MODEL // 2
NSA_HEADS = NSA_WIDTH // HEAD_DIM
NSA_KV_HEADS = 2
NSA_GROUP = NSA_HEADS // NSA_KV_HEADS
NSA_BLOCK = 64
NSA_TOPK = 16
WINDOW = 512
N_NSA_KV = 6
DN_WIDTH = D_MODEL // 4
DN_HEADS = DN_WIDTH // HEAD_DIM
DN_DK = HEAD_DIM
DN_DV = HEAD_DIM
DN_CONV_W = 4
DN_CHUNK = 64
DN_CONV_DIM = 3 * DN_WIDTH
S5_WIDTH = D_MODEL - NSA_WIDTH - DN_WIDTH
S5_GROUP_CH = 16
S5_GROUPS = S5_WIDTH // S5_GROUP_CH
S5_STATE = 64
D_FF = 4 * D_MODEL
Q_BLOCK = 128
NORM_EPS = 1e-6
BIG = 1e9
NEG_INF = -1e30
SPLIT_WIDTHS = (NSA_WIDTH, N_NSA_KV * NSA_KV_HEADS * HEAD_DIM, NSA_HEADS * 3,
                3 * DN_WIDTH, DN_HEADS, DN_HEADS, DN_WIDTH, S5_WIDTH)
IN_WIDTH = sum(SPLIT_WIDTHS)

kernel_name = 'hymba_nsa_gdn_s5_decode_step'


def split_cols(a, widths):
    outs, start = [], 0
    for wdt in widths:
        outs.append(a[..., start:start + wdt])
        start += wdt
    return outs


def rmsnorm(x, g):
    xf = x.astype(jnp.float32)
    y = xf * lax.rsqrt(jnp.mean(xf * xf, axis=-1, keepdims=True) + NORM_EPS)
    return (y * g.astype(jnp.float32)).astype(x.dtype)


def l2norm(a):
    return a * lax.rsqrt(jnp.sum(a * a, axis=-1, keepdims=True) + 1e-6)


def masked_softmax(s, mask, axes):
    s = jnp.where(mask, s, NEG_INF)
    m = jnp.max(s, axis=axes, keepdims=True)
    p = jnp.exp(s - m) * mask
    return p / jnp.maximum(jnp.sum(p, axis=axes, keepdims=True), 1e-30)


def alibi_slopes():
    h = jnp.arange(1, NSA_HEADS + 1, dtype=jnp.float32)
    return jnp.exp2(-8.0 * h / NSA_HEADS).reshape(NSA_KV_HEADS, NSA_GROUP)


def nsa_attend(q, k_cmp, v_cmp, k_sel, v_sel, k_win, v_win, gates, w_ck, w_cv, q0, w0):
    B, Lq = q.shape[0], q.shape[1]
    Sk = k_cmp.shape[1]
    G, R, dh = NSA_KV_HEADS, NSA_GROUP, HEAD_DIM
    f32 = jnp.float32
    slopes = alibi_slopes()
    qg = q.reshape(B, Lq, G, R, dh)
    t = q0 + jnp.arange(Lq)
    nb = Sk // NSA_BLOCK
    kcb = k_cmp[:, :nb * NSA_BLOCK].reshape(B, nb, NSA_BLOCK, G, dh)
    vcb = v_cmp[:, :nb * NSA_BLOCK].reshape(B, nb, NSA_BLOCK, G, dh)
    kc = jnp.einsum('bnjgd,gj->bngd', kcb, w_ck)
    vc = jnp.einsum('bnjgd,gj->bngd', vcb, w_cv)
    blk = jnp.arange(nb)
    s_c = jnp.einsum('bqgrd,bngd->bgrqn', qg, kc, preferred_element_type=f32)
    dist_c = t[:, None].astype(f32) - (blk * NSA_BLOCK + (NSA_BLOCK - 1) / 2.0)[None, :]
    ok_c = ((blk + 1) * NSA_BLOCK - 1)[None, :] <= t[:, None]
    p_c = masked_softmax(s_c - slopes[None, :, :, None, None] * dist_c, ok_c, axes=-1)
    o_c = jnp.einsum('bgrqn,bngd->bqgrd', p_c, vc.astype(f32))
    nbs = -(-Sk // NSA_BLOCK)
    imp = jnp.pad(jnp.sum(p_c, axis=2), ((0, 0), (0, 0), (0, 0), (0, nbs - nb)))
    cur = t // NSA_BLOCK
    cand = jnp.arange(nbs)
    forced = (cand[None, :] == 0) | (cand[None, :] == cur[:, None]) | (cand[None, :] == cur[:, None] - 1)
    score = jnp.where(cand[None, :] > cur[:, None], -BIG, jnp.where(forced, BIG, imp))
    n_sel = min(NSA_TOPK, nbs)
    _, idx = lax.top_k(score, n_sel)
    valid = idx <= cur[None, None, :, None]
    pad_k = nbs * NSA_BLOCK - Sk

    def to_blocks(a):
        a = jnp.pad(a, ((0, 0), (0, pad_k), (0, 0), (0, 0)))
        return a.reshape(B, nbs, NSA_BLOCK, G, dh).transpose(0, 3, 1, 2, 4)

    ksb, vsb = to_blocks(k_sel), to_blocks(v_sel)
    kw_pad = jnp.pad(k_win, ((0, 0), (WINDOW, 0), (0, 0), (0, 0)))
    vw_pad = jnp.pad(v_win, ((0, 0), (WINDOW, 0), (0, 0), (0, 0)))
    qb_len = math.gcd(Lq, Q_BLOCK)
    nq = Lq // qb_len
    span = qb_len + WINDOW - 1
    bi = jnp.arange(B)[:, None, None, None]
    gi = jnp.arange(G)[None, :, None, None]

    def body(xs):
        qb, ib, vb, i0 = xs
        tb = q0 + i0 + jnp.arange(qb_len)
        kg = ksb[bi, gi, ib]
        vg = vsb[bi, gi, ib]
        pos = ib[..., None] * NSA_BLOCK + jnp.arange(NSA_BLOCK)
        dist = tb[None, None, :, None, None] - pos
        m_s = (dist >= 0) & vb[..., None]
        s = jnp.einsum('bqgrd,bgqnjd->bgrqnj', qb, kg, preferred_element_type=f32)
        s = s - slopes[None, :, :, None, None, None] * dist[:, :, None].astype(f32)
        p = masked_softmax(s, m_s[:, :, None], axes=(-2, -1))
        o_s = jnp.einsum('bgrqnj,bgqnjd->bqgrd', p, vg.astype(f32))
        start = q0 + i0 - w0 + 1
        kwb = lax.dynamic_slice_in_dim(kw_pad, start, span, axis=1)
        vwb = lax.dynamic_slice_in_dim(vw_pad, start, span, axis=1)
        wpos = q0 + i0 - WINDOW + 1 + jnp.arange(span)
        dist_w = tb[:, None] - wpos[None, :]
        m_w = (dist_w >= 0) & (dist_w < WINDOW) & (wpos >= w0)[None, :]
        s_w = jnp.einsum('bqgrd,bkgd->bgrqk', qb, kwb, preferred_element_type=f32)
        p_w = masked_softmax(s_w - slopes[None, :, :, None, None] * dist_w.astype(f32), m_w, axes=-1)
        o_w = jnp.einsum('bgrqk,bkgd->bqgrd', p_w, vwb.astype(f32))
        return o_s, o_w

    xs = (qg.reshape(B, nq, qb_len, G, R, dh).transpose(1, 0, 2, 3, 4, 5),
          idx.reshape(B, G, nq, qb_len, n_sel).transpose(2, 0, 1, 3, 4),
          valid.reshape(B, G, nq, qb_len, n_sel).transpose(2, 0, 1, 3, 4),
          jnp.arange(nq) * qb_len)
    o_s, o_w = lax.map(body, xs)
    o_s = o_s.transpose(1, 0, 2, 3, 4, 5).reshape(B, Lq, G, R, dh)
    o_w = o_w.transpose(1, 0, 2, 3, 4, 5).reshape(B, Lq, G, R, dh)
    gt = gates.reshape(B, Lq, G, R, 3)
    o = gt[..., 0:1] * o_c + gt[..., 1:2] * o_s + gt[..., 2:3] * o_w
    return o.reshape(B, Lq, NSA_WIDTH)


def short_conv(x, buf, w):
    L = x.shape[1]
    xc = jnp.concatenate([buf.astype(x.dtype), x], axis=1)
    y = sum(xc[:, j:j + L] * w[:, j] for j in range(DN_CONV_W))
    return jax.nn.silu(y), xc[:, xc.shape[1] - (DN_CONV_W - 1):]


def gated_delta_chunked(q, k, v, g, beta, s0):
    B, L, H, dk = q.shape
    dv = v.shape[-1]
    C = min(DN_CHUNK, L)
    n = -(-L // C)
    pad = n * C - L
    if pad:
        pw = ((0, 0), (0, pad), (0, 0), (0, 0))
        q, k, v = jnp.pad(q, pw), jnp.pad(k, pw), jnp.pad(v, pw)
        g, beta = jnp.pad(g, pw[:3]), jnp.pad(beta, pw[:3])
    qc, kc, vc = [a.reshape(B, n, C, H, a.shape[-1]).transpose(1, 0, 3, 2, 4) for a in (q, k, v)]
    gc = jnp.cumsum(g.reshape(B, n, C, H).transpose(1, 0, 3, 2), axis=-1)
    bc = beta.reshape(B, n, C, H).transpose(1, 0, 3, 2)
    tri = jnp.tril(jnp.ones((C, C), dtype=bool))
    strict = jnp.tril(jnp.ones((C, C), dtype=bool), -1)
    diff = gc[..., :, None] - gc[..., None, :]
    decay = jnp.where(tri, jnp.exp(jnp.where(tri, diff, 0.0)), 0.0)
    kb = kc * bc[..., None]
    a_mat = jnp.where(strict, jnp.einsum('nbhid,nbhjd->nbhij', kb, kc) * decay, 0.0)
    eye = jnp.eye(C, dtype=a_mat.dtype)
    rhs = jnp.concatenate([vc * bc[..., None], kb * jnp.exp(gc)[..., None]], axis=-1)
    sol = lax.linalg.triangular_solve(a_mat + eye, rhs, left_side=True, lower=True)
    u, w = sol[..., :dv], sol[..., dv:]
    a_qk = jnp.where(tri, jnp.einsum('nbhid,nbhjd->nbhij', qc, kc) * decay, 0.0)

    def step(S, xs):
        q_i, k_i, u_i, w_i, g_i, aqk_i = xs
        v_new = u_i - jnp.einsum('bhck,bhkv->bhcv', w_i, S)
        o_i = (jnp.einsum('bhck,bhkv->bhcv', q_i * jnp.exp(g_i)[..., None], S)
               + jnp.einsum('bhij,bhjv->bhiv', aqk_i, v_new))
        g_last = g_i[..., -1]
        S = (S * jnp.exp(g_last)[..., None, None]
             + jnp.einsum('bhck,bhcv->bhkv', k_i * jnp.exp(g_last[..., None] - g_i)[..., None], v_new))
        return S, o_i

    s_fin, o = lax.scan(step, s0, (qc, kc, u, w, gc, a_qk))
    o = o.transpose(1, 0, 3, 2, 4).reshape(B, n * C, H, dv)[:, :L]
    return o, s_fin


def deltanet_mix(qkv, a, b, z, conv_buf, s0, conv_w, a_log, dt_bias, norm_w):
    B, L, _ = qkv.shape
    f32 = jnp.float32
    y, new_buf = short_conv(qkv, conv_buf, conv_w)
    y = y.astype(f32)
    q = l2norm(y[..., :DN_WIDTH].reshape(B, L, DN_HEADS, DN_DK)) * DN_DK ** -0.5
    k = l2norm(y[..., DN_WIDTH:2 * DN_WIDTH].reshape(B, L, DN_HEADS, DN_DK))
    v = y[..., 2 * DN_WIDTH:].reshape(B, L, DN_HEADS, DN_DV)
    beta = jax.nn.sigmoid(b.astype(f32))
    g = -jnp.exp(a_log.astype(f32)) * jax.nn.softplus(a.astype(f32) + dt_bias.astype(f32))
    o, s_new = gated_delta_chunked(q, k, v, g, beta, s0.astype(f32))
    o = rmsnorm(o, norm_w) * jax.nn.silu(z.astype(f32).reshape(B, L, DN_HEADS, DN_DV))
    return o.reshape(B, L, DN_WIDTH), new_buf, s_new


def complex_affine_combine(e1, e2):
    a1r, a1i, b1r, b1i = e1
    a2r, a2i, b2r, b2i = e2
    return (a2r * a1r - a2i * a1i, a2r * a1i + a2i * a1r,
            a2r * b1r - a2i * b1i + b2r, a2r * b1i + a2i * b1r + b2i)


def s5_mix(u, h0_re, h0_im, lam_re, lam_im, log_step, b_re, b_im, c_re, c_im, d, w_glu):
    B, L, _ = u.shape
    f32 = jnp.float32
    uf = u.astype(f32)
    ug = uf.reshape(B, L, S5_GROUPS, S5_GROUP_CH)
    lr, li = lam_re.astype(f32), lam_im.astype(f32)
    dt = jnp.exp(log_step.astype(f32))[:, None]
    mag = jnp.exp(lr * dt)
    ar, ai = mag * jnp.cos(li * dt), mag * jnp.sin(li * dt)
    den = lr * lr + li * li
    fr = ((ar - 1.0) * lr + ai * li) / den
    fi = (ai * lr - (ar - 1.0) * li) / den
    br, bi = b_re.astype(f32), b_im.astype(f32)
    bbr = fr[..., None] * br - fi[..., None] * bi
    bbi = fr[..., None] * bi + fi[..., None] * br
    bu_r = jnp.einsum('gpc,blgc->blgp', bbr, ug)
    bu_i = jnp.einsum('gpc,blgc->blgp', bbi, ug)
    a_r = jnp.broadcast_to(ar, bu_r.shape)
    a_i = jnp.broadcast_to(ai, bu_i.shape)
    cum_r, cum_i, hs_r, hs_i = lax.associative_scan(complex_affine_combine, (a_r, a_i, bu_r, bu_i), axis=1)
    h0r = h0_re.astype(f32)[:, None]
    h0i = h0_im.astype(f32)[:, None]
    h_r = cum_r * h0r - cum_i * h0i + hs_r
    h_i = cum_r * h0i + cum_i * h0r + hs_i
    y = (jnp.einsum('gcp,blgp->blgc', c_re.astype(f32), h_r)
         - jnp.einsum('gcp,blgp->blgc', c_im.astype(f32), h_i))
    y = y.reshape(B, L, S5_WIDTH) + d.astype(f32) * uf
    gl = jax.nn.gelu(y)
    out = gl * jax.nn.sigmoid(gl @ w_glu.astype(f32))
    return out, h_r[:, -1], h_i[:, -1]


def trunk_layer(x, past_kv, win_buf, conv_buf, dn_s0, s5_re0, s5_im0, p):
    B, L, _ = x.shape
    f32 = jnp.float32
    q0 = past_kv.shape[1]
    w0 = q0 - win_buf.shape[1]
    h = rmsnorm(x, p['ln1'])
    nsa_q, nsa_kv, nsa_gate, dn_qkv, dn_a, dn_b, dn_z, s5_u = split_cols(h @ p['w_in'], SPLIT_WIDTHS)
    kv = nsa_kv.reshape(B, L, N_NSA_KV, NSA_KV_HEADS, HEAD_DIM)
    new_kv = kv[:, :, :4]
    full = jnp.concatenate([past_kv.astype(kv.dtype), new_kv], axis=1)
    wseq = jnp.concatenate([win_buf.astype(kv.dtype), kv[:, :, 4:]], axis=1)
    q = nsa_q.reshape(B, L, NSA_HEADS, HEAD_DIM) * HEAD_DIM ** -0.5
    gates = jax.nn.sigmoid(nsa_gate.astype(f32)).reshape(B, L, NSA_HEADS, 3)
    o_nsa = nsa_attend(q, full[:, :, 0], full[:, :, 1], full[:, :, 2], full[:, :, 3],
                       wseq[:, :, 0], wseq[:, :, 1], gates, p['nsa_wck'], p['nsa_wcv'], q0, w0)
    new_win = wseq[:, wseq.shape[1] - min(WINDOW, q0 + L):]
    o_dn, new_conv, new_dn = deltanet_mix(dn_qkv, dn_a, dn_b, dn_z, conv_buf, dn_s0, p['dn_conv_w'],
                                          p['dn_a_log'], p['dn_dt_bias'], p['dn_norm_w'])
    o_s5, s5_re, s5_im = s5_mix(s5_u, s5_re0, s5_im0, p['s5_lambda_re'], p['s5_lambda_im'], p['s5_log_step'],
                                p['s5_b_re'], p['s5_b_im'], p['s5_c_re'], p['s5_c_im'], p['s5_d'], p['s5_w_glu'])
    mix = jnp.concatenate([o_nsa, o_dn, o_s5], axis=-1).astype(x.dtype)
    x = x + mix @ p['w_out']
    h2 = rmsnorm(x, p['ln2'])
    x = x + jnp.square(jax.nn.relu(h2 @ p['mlp_up'])) @ p['mlp_down']
    return x, (new_kv, new_win, new_conv, new_dn, s5_re, s5_im)


def setup_inputs(seed: int = 0) -> dict:
    key = jax.random.key(seed)
    ks = iter(jax.random.split(key, 48))
    f32 = jnp.float32

    def nrm(shape, scale):
        return jax.random.normal(next(ks), shape, f32) * scale

    def unif(shape, lo, hi):
        return jax.random.uniform(next(ks), shape, f32, lo, hi)

    n_pages = PAST_LEN // PAGE_SIZE
    n_used = DEC_BATCH * n_pages
    n_phys = n_used + (n_used + 3) // 4
    G, dh = NSA_KV_HEADS, HEAD_DIM
    x_prompt = nrm((BATCH, SEQ, D_MODEL), 1.0)
    x_sample = nrm((DEC_BATCH, DEC_SEQ, D_MODEL), 1.0)
    cache_nsa_kv = nrm((DEPTH, n_phys, PAGE_SIZE, 4, G, dh), 1.0)
    cache_win_kv = nrm((DEPTH, DEC_BATCH, min(WINDOW, PAST_LEN), 2, G, dh), 1.0)
    state_dn_conv = nrm((DEPTH, DEC_BATCH, DN_CONV_W - 1, DN_CONV_DIM), 1.0)
    state_dn = nrm((DEPTH, DEC_BATCH, DN_HEADS, DN_DK, DN_DV), 0.3)
    state_s5_re = nrm((DEPTH, DEC_BATCH, S5_GROUPS, S5_STATE), 0.2)
    state_s5_im = nrm((DEPTH, DEC_BATCH, S5_GROUPS, S5_STATE), 0.2)
    page_table = jax.random.permutation(next(ks), n_phys)[:n_used].reshape(DEC_BATCH, n_pages).astype(jnp.int32)
    ln1 = 1.0 + nrm((DEPTH, D_MODEL), 0.02)
    ln2 = 1.0 + nrm((DEPTH, D_MODEL), 0.02)
    ln_f = 1.0 + nrm((D_MODEL,), 0.02)
    w_in = nrm((DEPTH, D_MODEL, IN_WIDTH), D_MODEL ** -0.5)
    w_out = nrm((DEPTH, MIX, D_MODEL), MIX ** -0.5)
    nsa_wck = 1.0 / NSA_BLOCK + nrm((DEPTH, G, NSA_BLOCK), NSA_BLOCK ** -0.5)
    nsa_wcv = 1.0 / NSA_BLOCK + nrm((DEPTH, G, NSA_BLOCK), NSA_BLOCK ** -0.5)
    dn_conv_w = nrm((DEPTH, DN_CONV_DIM, DN_CONV_W), DN_CONV_W ** -0.5)
    dn_a_log = jnp.log(unif((DEPTH, DN_HEADS), 1.0, 16.0))
    dt = jnp.exp(unif((DEPTH, DN_HEADS), math.log(1e-3), math.log(0.1)))
    dn_dt_bias = dt + jnp.log(-jnp.expm1(-dt))
    dn_norm_w = 1.0 + nrm((DEPTH, DN_DV), 0.02)
    s5_lambda_re = -0.5 + nrm((DEPTH, S5_GROUPS, S5_STATE), 0.01)
    s5_lambda_im = math.pi * jnp.arange(S5_STATE, dtype=f32) + nrm((DEPTH, S5_GROUPS, S5_STATE), 0.01)
    s5_log_step = unif((DEPTH, S5_GROUPS), math.log(1e-3), math.log(0.1))
    s5_b_re = nrm((DEPTH, S5_GROUPS, S5_STATE, S5_GROUP_CH), (2 * S5_GROUP_CH) ** -0.5)
    s5_b_im = nrm((DEPTH, S5_GROUPS, S5_STATE, S5_GROUP_CH), (2 * S5_GROUP_CH) ** -0.5)
    s5_c_re = nrm((DEPTH, S5_GROUPS, S5_GROUP_CH, S5_STATE), S5_STATE ** -0.5)
    s5_c_im = nrm((DEPTH, S5_GROUPS, S5_GROUP_CH, S5_STATE), S5_STATE ** -0.5)
    s5_d = nrm((DEPTH, S5_WIDTH), 1.0)
    s5_w_glu = nrm((DEPTH, S5_WIDTH, S5_WIDTH), S5_WIDTH ** -0.5)
    mlp_up = nrm((DEPTH, D_MODEL, D_FF), D_MODEL ** -0.5)
    mlp_down = nrm((DEPTH, D_FF, D_MODEL), D_FF ** -0.5)
    return {'x_prompt': x_prompt, 'x_sample': x_sample, 'cache_nsa_kv': cache_nsa_kv,
            'cache_win_kv': cache_win_kv, 'state_dn_conv': state_dn_conv, 'state_dn': state_dn,
            'state_s5_re': state_s5_re, 'state_s5_im': state_s5_im, 'page_table': page_table,
            'ln1': ln1, 'ln2': ln2, 'ln_f': ln_f, 'w_in': w_in, 'w_out': w_out,
            'nsa_wck': nsa_wck, 'nsa_wcv': nsa_wcv, 'dn_conv_w': dn_conv_w, 'dn_a_log': dn_a_log,
            'dn_dt_bias': dn_dt_bias, 'dn_norm_w': dn_norm_w, 's5_lambda_re': s5_lambda_re,
            's5_lambda_im': s5_lambda_im, 's5_log_step': s5_log_step, 's5_b_re': s5_b_re,
            's5_b_im': s5_b_im, 's5_c_re': s5_c_re, 's5_c_im': s5_c_im, 's5_d': s5_d,
            's5_w_glu': s5_w_glu, 'mlp_up': mlp_up, 'mlp_down': mlp_down}


def reference(x_prompt, x_sample, cache_nsa_kv, cache_win_kv, state_dn_conv, state_dn, state_s5_re,
              state_s5_im, page_table, ln1, ln2, ln_f, w_in, w_out, nsa_wck, nsa_wcv, dn_conv_w,
              dn_a_log, dn_dt_bias, dn_norm_w, s5_lambda_re, s5_lambda_im, s5_log_step, s5_b_re,
              s5_b_im, s5_c_re, s5_c_im, s5_d, s5_w_glu, mlp_up, mlp_down):
    B = x_prompt.shape[0]
    DB = x_sample.shape[0]
    n_pages = page_table.shape[1]
    page = cache_nsa_kv.shape[2]
    G, dh = NSA_KV_HEADS, HEAD_DIM
    f32 = jnp.float32
    xp, xs = x_prompt, x_sample
    outs_p, outs_s = [], []
    for l in range(DEPTH):
        p = {'ln1': ln1[l], 'ln2': ln2[l], 'w_in': w_in[l], 'w_out': w_out[l],
             'nsa_wck': nsa_wck[l], 'nsa_wcv': nsa_wcv[l], 'dn_conv_w': dn_conv_w[l],
             'dn_a_log': dn_a_log[l], 'dn_dt_bias': dn_dt_bias[l], 'dn_norm_w': dn_norm_w[l],
             's5_lambda_re': s5_lambda_re[l], 's5_lambda_im': s5_lambda_im[l],
             's5_log_step': s5_log_step[l], 's5_b_re': s5_b_re[l], 's5_b_im': s5_b_im[l],
             's5_c_re': s5_c_re[l], 's5_c_im': s5_c_im[l], 's5_d': s5_d[l], 's5_w_glu': s5_w_glu[l],
             'mlp_up': mlp_up[l], 'mlp_down': mlp_down[l]}
        xp, st_p = trunk_layer(
            xp, jnp.zeros((B, 0, 4, G, dh), xp.dtype), jnp.zeros((B, 0, 2, G, dh), xp.dtype),
            jnp.zeros((B, DN_CONV_W - 1, DN_CONV_DIM), xp.dtype),
            jnp.zeros((B, DN_HEADS, DN_DK, DN_DV), f32),
            jnp.zeros((B, S5_GROUPS, S5_STATE), f32), jnp.zeros((B, S5_GROUPS, S5_STATE), f32), p)
        past = cache_nsa_kv[l, page_table].reshape(DB, n_pages * page, 4, G, dh)
        xs, st_s = trunk_layer(xs, past, cache_win_kv[l], state_dn_conv[l], state_dn[l],
                               state_s5_re[l], state_s5_im[l], p)
        outs_p.append(st_p)
        outs_s.append(st_s)
    y_prompt = rmsnorm(xp, ln_f)
    y_sample = rmsnorm(xs, ln_f)

    def stk(outs, i):
        return jnp.stack([o[i] for o in outs], axis=0)

    kv_p, kv_s = stk(outs_p, 0), stk(outs_s, 0)
    win_p, win_s = stk(outs_p, 1), stk(outs_s, 1)
    conv_p, conv_s = stk(outs_p, 2), stk(outs_s, 2)
    dn_p, dn_s = stk(outs_p, 3), stk(outs_s, 3)
    s5re_p, s5re_s = stk(outs_p, 4), stk(outs_s, 4)
    s5im_p, s5im_s = stk(outs_p, 5), stk(outs_s, 5)
    return (y_prompt, y_sample, kv_p, kv_s, win_p, win_s, conv_p, conv_s, dn_p, dn_s,
            s5re_p, s5re_s, s5im_p, s5im_s)
```

```python
import functools
import math

import jax
import jax.numpy as jnp
import numpy as np
from jax import lax
from jax.experimental import pallas as pl
from jax.experimental.pallas import tpu as pltpu

F32 = jnp.float32
BF16 = jnp.bfloat16
HIGHEST = lax.Precision.HIGHEST

D_MODEL = 1024
HEAD_DIM = 64
NSA_HEADS = 8
NSA_KV_HEADS = 2
NSA_GROUP = 4
NSA_BLOCK = 64
NSA_TOPK = 16
WINDOW = 512
DN_HEADS = 4
DN_WIDTH = 256
DN_CONV_W = 4
S5_GROUPS = 16
S5_GROUP_CH = 16
S5_STATE = 64
S5_WIDTH = 256
S5_LANES = S5_GROUPS * S5_STATE
D_FF = 4096
NORM_EPS = 1e-6
BIG = 1e9
NEG = -1e30
LANE = 128
VMEM_LIMIT = 48 * 1024 * 1024

_NT = (((1,), (1,)), ((), ()))


def _slope(h):
    return 2.0 ** (-(h + 1))


def _cparams(sem):
    return pltpu.CompilerParams(dimension_semantics=sem, vmem_limit_bytes=VMEM_LIMIT)


def _dotf(a, b):
    return jnp.dot(a, b, preferred_element_type=F32)


def _dot_nt(a, b):
    return lax.dot_general(a, b, _NT, preferred_element_type=F32)


def _dot_hi(a, b):
    return jnp.dot(a, b, preferred_element_type=F32, precision=HIGHEST)


def _dot_nt_hi(a, b):
    return lax.dot_general(a, b, _NT, preferred_element_type=F32, precision=HIGHEST)


def _proj_kernel(x_ref, g_ref, wq_ref, wkv_ref, wdn_ref, wz_ref, wu_ref, ws_ref, wexp_ref,
                 q_ref, kv_ref, dn_ref, z_ref, u_ref, s_ref, pool_ref):
    x = x_ref[...]
    h = x * lax.rsqrt(jnp.mean(x * x, axis=-1, keepdims=True) + NORM_EPS) * g_ref[...]
    hb = h.astype(BF16)
    q_ref[...] = _dotf(hb, wq_ref[...])
    kv = _dotf(hb, wkv_ref[...])
    kv_ref[...] = kv
    dn_ref[...] = _dotf(hb, wdn_ref[...])
    z_ref[...] = _dotf(hb, wz_ref[...])
    u_ref[...] = _dotf(hb, wu_ref[...])
    s_ref[...] = _dotf(hb, ws_ref[...])
    tm = x.shape[0]
    kc = kv[:, :2 * LANE].reshape(tm // NSA_BLOCK, NSA_BLOCK, 2 * LANE) * wexp_ref[...][None]
    pool_ref[...] = jnp.sum(kc, axis=1)


def _proj(x, ln, wq, wkv, wdn, wz, wu, ws, wexp, tm=512):
    n = x.shape[0]
    const = lambda i: (0, 0)
    row = lambda i: (i, 0)
    outs = [(n, 1024), (n, 768), (n, 768), (n, 256), (n, 256), (n, LANE), (n // NSA_BLOCK, 2 * LANE)]
    return pl.pallas_call(
        _proj_kernel,
        grid=(n // tm,),
        in_specs=[pl.BlockSpec((tm, D_MODEL), row), pl.BlockSpec((1, D_MODEL), const)]
        + [pl.BlockSpec(w.shape, const) for w in (wq, wkv, wdn, wz, wu, ws, wexp)],
        out_specs=[pl.BlockSpec((tm, s[1]), row) for s in outs[:-1]]
        + [pl.BlockSpec((tm // NSA_BLOCK, 2 * LANE), row)],
        out_shape=[jax.ShapeDtypeStruct(s, F32) for s in outs],
        compiler_params=_cparams(("parallel",)),
        name="proj",
    )(x, ln, wq, wkv, wdn, wz, wu, ws, wexp)


def _topk_blocks(score, blk, nblk):
    sel = jnp.zeros(score.shape, dtype=jnp.bool_)
    for _ in range(NSA_TOPK):
        m = jnp.max(score, axis=0, keepdims=True)
        idx = jnp.min(jnp.where(score == m, blk, nblk), axis=0, keepdims=True)
        pick = blk == idx
        sel = sel | pick
        score = jnp.where(pick, -jnp.inf, score)
    return sel


def _cmp_heads(q_tile, kc, vct, t, blk, nblk_valid, write_oc):
    nblk = blk.shape[0]
    ok = ((blk + 1) * NSA_BLOCK - 1 <= t) & (blk < nblk_valid)
    okf = ok.astype(F32)
    dist = t.astype(F32) - (blk.astype(F32) * NSA_BLOCK + (NSA_BLOCK - 1) / 2.0)
    cur = t // NSA_BLOCK
    forced = (blk == 0) | (blk == cur) | (blk == cur - 1)
    sels = []
    for g in range(NSA_KV_HEADS):
        imp = jnp.zeros(blk.shape, F32)
        for r in range(NSA_GROUP):
            h = g * NSA_GROUP + r
            q = (q_tile(h) * HEAD_DIM ** -0.5).astype(BF16)
            s = _dot_nt(kc, q)
            s = jnp.where(ok, s - _slope(h) * dist, NEG)
            m = jnp.max(s, axis=0, keepdims=True)
            p = jnp.exp(s - m) * okf
            p = p / jnp.maximum(jnp.sum(p, axis=0, keepdims=True), 1e-30)
            imp = imp + p
            write_oc(h, _dotf(vct, p.astype(BF16)))
        score = jnp.where(blk > cur, -BIG, jnp.where(forced, BIG, imp))
        sel = _topk_blocks(score, blk, nblk) & (blk <= cur)
        sels.append(sel)
    return sels


def _cmp_kernel(q_ref, kcv_ref, oc_ref, sel_ref, *, tq):
    qt = pl.program_id(1)
    kcv = kcv_ref[...]
    nblk = kcv.shape[0]
    kc = kcv[:, :LANE].astype(BF16)
    vct = kcv[:, LANE:].T.astype(BF16)
    blk = lax.broadcasted_iota(jnp.int32, (nblk, tq), 0)
    t = qt * tq + lax.broadcasted_iota(jnp.int32, (nblk, tq), 1)

    def write_oc(h, oct):
        oc_ref[:, h * LANE:(h + 1) * LANE] = oct.T

    sels = _cmp_heads(lambda h: q_ref[:, h * LANE:(h + 1) * LANE], kc, vct, t, blk, nblk, write_oc)
    for g in range(NSA_KV_HEADS):
        sel_ref[:, g * LANE:(g + 1) * LANE] = sels[g].astype(F32).T


def _cmp_prompt(qh, pool, nb, lq, tq=256):
    nq = lq // tq
    nblk = lq // NSA_BLOCK
    assert nblk == LANE
    return pl.pallas_call(
        functools.partial(_cmp_kernel, tq=tq),
        grid=(nb, nq),
        in_specs=[pl.BlockSpec((tq, 1024), lambda b, i: (b * nq + i, 0)),
                  pl.BlockSpec((nblk, 2 * LANE), lambda b, i: (b, 0))],
        out_specs=[pl.BlockSpec((tq, 1024), lambda b, i: (b * nq + i, 0)),
                   pl.BlockSpec((tq, 2 * LANE), lambda b, i: (b * nq + i, 0))],
        out_shape=[jax.ShapeDtypeStruct((nb * lq, 1024), F32),
                   jax.ShapeDtypeStruct((nb * lq, 2 * LANE), F32)],
        compiler_params=_cparams(("parallel", "parallel")),
        name="nsa_cmp",
    )(qh, pool)


def _sel_lhs(sel_g, q_h, h):
    nio = lax.broadcasted_iota(jnp.int32, sel_g.shape, 1).astype(F32)
    bias = jnp.where(sel_g > 0.5, (NSA_BLOCK * _slope(h)) * nio, NEG)
    return jnp.concatenate([bias.astype(BF16), (q_h * HEAD_DIM ** -0.5).astype(BF16)], axis=1)


def _sel_kernel(qt_tab, kt_tab, q_ref, sel_ref, k_ref, v_ref, o_ref, lhs_s, m_s, l_s, acc_s, *, tq, tk):
    i = pl.program_id(1)
    qt = qt_tab[i]
    kt = kt_tab[i]

    @pl.when(kt == 0)
    def _():
        for h in range(NSA_HEADS):
            g = h // NSA_GROUP
            lhs_s[h] = _sel_lhs(sel_ref[:, g * LANE:(g + 1) * LANE], q_ref[:, h * LANE:(h + 1) * LANE], h)
        m_s[...] = jnp.full(m_s.shape, NEG, F32)
        l_s[...] = jnp.zeros(l_s.shape, F32)
        acc_s[...] = jnp.zeros(acc_s.shape, F32)

    kpos_r = kt * tk + lax.broadcasted_iota(jnp.int32, (tk, LANE), 0)
    onehot = (lax.broadcasted_iota(jnp.int32, (tk, LANE), 1) == kpos_r // NSA_BLOCK).astype(BF16)
    rhs = jnp.concatenate([onehot, k_ref[...].astype(BF16)], axis=1)
    v = v_ref[...].astype(BF16)
    kcol = kt * tk + lax.broadcasted_iota(jnp.int32, (tq, tk), 1)
    qrow = qt * tq + lax.broadcasted_iota(jnp.int32, (tq, tk), 0)
    causal = kcol <= qrow
    jrow = (lax.broadcasted_iota(jnp.int32, (1, tk), 1) % NSA_BLOCK).astype(F32)
    for h in range(NSA_HEADS):
        s = _dot_nt(lhs_s[h], rhs) + _slope(h) * jrow
        s = jnp.where(causal, s, NEG)
        m_old = m_s[h]
        m_new = jnp.maximum(m_old, jnp.max(s, axis=1, keepdims=True))
        alpha = jnp.exp(m_old - m_new)
        p = jnp.exp(s - m_new)
        l_s[h] = alpha * l_s[h] + jnp.sum(p, axis=1, keepdims=True)
        acc_s[h] = alpha * acc_s[h] + _dotf(p.astype(BF16), v)
        m_s[h] = m_new

    @pl.when(kt == qt)
    def _():
        for h in range(NSA_HEADS):
            o_ref[:, h * LANE:(h + 1) * LANE] = acc_s[h] / l_s[h]


def _sel_prompt(qh, sel, kv, nb, lq, tq=256):
    tk = tq
    nq = lq // tq
    pairs = [(a, b) for a in range(nq) for b in range(a + 1)]
    qt_tab = jnp.asarray([p[0] for p in pairs], jnp.int32)
    kt_tab = jnp.asarray([p[1] for p in pairs], jnp.int32)
    qmap = lambda b, i, qt, kt: (b * nq + qt[i], 0)
    grid_spec = pltpu.PrefetchScalarGridSpec(
        num_scalar_prefetch=2,
        grid=(nb, len(pairs)),
        in_specs=[pl.BlockSpec((tq, 1024), qmap),
                  pl.BlockSpec((tq, 2 * LANE), qmap),
                  pl.BlockSpec((tk, LANE), lambda b, i, qt, kt: (b * nq + kt[i], 2)),
                  pl.BlockSpec((tk, LANE), lambda b, i, qt, kt: (b * nq + kt[i], 3))],
        out_specs=pl.BlockSpec((tq, 1024), qmap),
        scratch_shapes=[pltpu.VMEM((NSA_HEADS, tq, 2 * LANE), BF16),
                        pltpu.VMEM((NSA_HEADS, tq, 1), F32),
                        pltpu.VMEM((NSA_HEADS, tq, 1), F32),
                        pltpu.VMEM((NSA_HEADS, tq, LANE), F32)])
    return pl.pallas_call(
        functools.partial(_sel_kernel, tq=tq, tk=tk),
        grid_spec=grid_spec,
        out_shape=jax.ShapeDtypeStruct((nb * lq, 1024), F32),
        compiler_params=_cparams(("parallel", "arbitrary")),
        name="nsa_sel",
    )(qt_tab, kt_tab, qh, sel, kv, kv)


def _gate_mix(small, oc, os_, ow, h):
    gt = jax.nn.sigmoid(small[:, 3 * h:3 * h + 3])
    return gt[:, 0:1] * oc + gt[:, 1:2] * os_ + gt[:, 2:3] * ow


def _win_kernel(q_ref, k0_ref, k1_ref, k2_ref, v0_ref, v1_ref, v2_ref, oc_ref, os_ref, sm_ref, o_ref, *, tq):
    qt = pl.program_id(1)
    kcat = jnp.concatenate([k0_ref[...], k1_ref[...], k2_ref[...]], axis=0).astype(BF16)
    vcat = jnp.concatenate([v0_ref[...], v1_ref[...], v2_ref[...]], axis=0).astype(BF16)
    row = lax.broadcasted_iota(jnp.int32, (tq, 3 * tq), 0)
    col = lax.broadcasted_iota(jnp.int32, (tq, 3 * tq), 1)
    dist = row - col + 2 * tq
    valid = (dist >= 0) & (dist < WINDOW) & ((qt - 2) * tq + col >= 0)
    distf = dist.astype(F32)
    small = sm_ref[...]
    for h in range(NSA_HEADS):
        sl = slice(h * LANE, (h + 1) * LANE)
        q = (q_ref[:, sl] * HEAD_DIM ** -0.5).astype(BF16)
        s = _dot_nt(q, kcat)
        s = jnp.where(valid, s - _slope(h) * distf, NEG)
        m = jnp.max(s, axis=1, keepdims=True)
        p = jnp.exp(s - m)
        ow = _dotf(p.astype(BF16), vcat) / jnp.sum(p, axis=1, keepdims=True)
        o_ref[:, sl] = _gate_mix(small, oc_ref[:, sl], os_ref[:, sl], ow, h).astype(o_ref.dtype)


def _win_prompt(qh, kv, oc, osel, small, nb, lq, tq=256):
    assert 2 * tq >= WINDOW - 1
    nq = lq // tq
    row = lambda b, i: (b * nq + i, 0)

    def kmap(back, c):
        return lambda b, i: (b * nq + jnp.maximum(i - back, 0), c)

    return pl.pallas_call(
        functools.partial(_win_kernel, tq=tq),
        grid=(nb, nq),
        in_specs=[pl.BlockSpec((tq, 1024), row)]
        + [pl.BlockSpec((tq, LANE), kmap(back, 4)) for back in (2, 1, 0)]
        + [pl.BlockSpec((tq, LANE), kmap(back, 5)) for back in (2, 1, 0)]
        + [pl.BlockSpec((tq, 1024), row), pl.BlockSpec((tq, 1024), row), pl.BlockSpec((tq, LANE), row)],
        out_specs=pl.BlockSpec((tq, 1024), row),
        out_shape=jax.ShapeDtypeStruct((nb * lq, 1024), BF16),
        compiler_params=_cparams(("parallel", "parallel")),
        name="nsa_win",
    )(qh, kv, kv, kv, kv, kv, kv, oc, osel, small)


def _per_head(idx_h, fn):
    out = jnp.zeros(idx_h.shape, F32)
    for h in range(NSA_HEADS):
        out = jnp.where(idx_h == h, fn(h), out)
    return out


def _pad_rows(a, rows):
    return jnp.concatenate([a, jnp.zeros((rows - a.shape[0], a.shape[1]), a.dtype)], axis=0)


def _nsa_sample_kernel(pt_ref, q_ref, kvn_ref, sm_ref, *rest, past, lq, npages, page):
    pages = rest[:npages]
    win_ref, wt_ref, o_ref, ktc_s, vtc_s, rt_s, vt_s = rest[npages:]
    b = pl.program_id(0)
    nkeys = npages * page
    nrow = NSA_HEADS * lq
    nb_complete = (past + lq) // NSA_BLOCK
    new_blk = past // NSA_BLOCK
    assert past % NSA_BLOCK + lq <= NSA_BLOCK and nkeys == past and new_blk < LANE

    @pl.when(b == 0)
    def _():
        n_i = lax.broadcasted_iota(jnp.int32, (LANE, nkeys), 0)
        pos_i = lax.broadcasted_iota(jnp.int32, (LANE, nkeys), 1)
        rt_s[0:LANE, :] = (n_i == pos_i // NSA_BLOCK).astype(BF16)

    for i in range(npages):
        sl = slice(i * page, (i + 1) * page)
        pg = pages[i]
        ktc_s[:, sl] = pg[0:LANE, :].astype(BF16)
        vtc_s[:, sl] = pg[LANE:2 * LANE, :].astype(BF16)
        rt_s[LANE:2 * LANE, sl] = pg[2 * LANE:3 * LANE, :].astype(BF16)
        vt_s[:, sl] = pg[3 * LANE:4 * LANE, :].astype(BF16)

    q8 = q_ref[...]
    qf = jnp.concatenate([q8[:, h * LANE:(h + 1) * LANE] for h in range(NSA_HEADS)], axis=0) * HEAD_DIM ** -0.5
    qb = qf.astype(BF16)
    kvn = kvn_ref[...]
    lane_lo = lax.broadcasted_iota(jnp.int32, (LANE, LANE), 1) < HEAD_DIM

    ktc = ktc_s[...]
    vtc = vtc_s[...]
    kc = jnp.where(lane_lo, _dot_nt(wt_ref[0, 0], ktc), _dot_nt(wt_ref[0, 1], ktc)).astype(BF16)
    vc = jnp.where(lane_lo, _dot_nt(wt_ref[1, 0], vtc), _dot_nt(wt_ref[1, 1], vtc))
    vct = vc.T.astype(BF16)
    blk = lax.broadcasted_iota(jnp.int32, (LANE, nrow), 0)
    col = lax.broadcasted_iota(jnp.int32, (LANE, nrow), 1)
    t = past + col % lq
    slope_c = _per_head(col // lq, _slope)
    ok = ((blk + 1) * NSA_BLOCK - 1 <= t) & (blk < nb_complete)
    dist = t.astype(F32) - (blk.astype(F32) * NSA_BLOCK + (NSA_BLOCK - 1) / 2.0)
    s = jnp.where(ok, _dot_nt(kc, qb) - slope_c * dist, NEG)
    m = jnp.max(s, axis=0, keepdims=True)
    p = jnp.exp(s - m) * ok.astype(F32)
    p = p / jnp.maximum(jnp.sum(p, axis=0, keepdims=True), 1e-30)
    o_c = _dotf(vct, p.astype(BF16)).T
    ri = lax.broadcasted_iota(jnp.int32, (nrow, nrow), 0)
    ci = lax.broadcasted_iota(jnp.int32, (nrow, nrow), 1)
    gsum = ((ri // (NSA_GROUP * lq) == ci // (NSA_GROUP * lq)) & (ri % lq == ci % lq)).astype(F32)
    imp = _dot_hi(p, gsum)
    cur = t // NSA_BLOCK
    forced = (blk == 0) | (blk == cur) | (blk == cur - 1)
    score = jnp.where(blk > cur, -BIG, jnp.where(forced, BIG, imp))
    sel = (_topk_blocks(score, blk, LANE) & (blk <= cur)).astype(F32).T

    rowi = lax.broadcasted_iota(jnp.int32, (nrow, LANE), 0)
    lanei = lax.broadcasted_iota(jnp.int32, (nrow, LANE), 1)
    slope_r = _per_head(rowi[:, 0:1] // lq, _slope)
    tok_r = rowi % lq
    bias = jnp.where(sel > 0.5, (NSA_BLOCK * slope_r) * lanei.astype(F32), NEG)
    lhs = jnp.concatenate([bias.astype(BF16), qb], axis=1)
    jrow = (lax.broadcasted_iota(jnp.int32, (1, nkeys), 1) % NSA_BLOCK).astype(F32)
    s_p = _dotf(lhs, rt_s[...]) + slope_r * jrow
    new_ok = (lanei <= tok_r) & (lanei < lq)
    bias_new = jnp.sum(jnp.where(lanei == new_blk, bias, 0.0), axis=1, keepdims=True)
    jnew = (past % NSA_BLOCK + lanei).astype(F32)
    kn = _pad_rows(kvn[:, 2 * LANE:3 * LANE], LANE).astype(BF16)
    s_n = jnp.where(new_ok, _dot_nt(qb, kn) + bias_new + slope_r * jnew, NEG)
    m = jnp.maximum(jnp.max(s_p, axis=1, keepdims=True), jnp.max(s_n, axis=1, keepdims=True))
    p_p = jnp.exp(s_p - m)
    p_n = jnp.exp(s_n - m)
    vn = _pad_rows(kvn[:, 3 * LANE:4 * LANE], LANE).astype(BF16)
    o_s = (_dot_nt(p_p.astype(BF16), vt_s[...]) + _dotf(p_n.astype(BF16), vn)) / (
        jnp.sum(p_p, axis=1, keepdims=True) + jnp.sum(p_n, axis=1, keepdims=True))

    wlen = win_ref.shape[1]
    wi = lax.broadcasted_iota(jnp.int32, (nrow, wlen), 1)
    tok_w = lax.broadcasted_iota(jnp.int32, (nrow, wlen), 0) % lq
    dist_w = wlen + tok_w - wi
    s_w = _dotf(qb, win_ref[0:LANE, :].astype(BF16))
    s_w = jnp.where(dist_w < WINDOW, s_w - slope_r * dist_w.astype(F32), NEG)
    kwn = _pad_rows(kvn[:, 4 * LANE:5 * LANE], LANE).astype(BF16)
    s_wn = jnp.where(new_ok, _dot_nt(qb, kwn) - slope_r * (tok_r - lanei).astype(F32), NEG)
    m = jnp.maximum(jnp.max(s_w, axis=1, keepdims=True), jnp.max(s_wn, axis=1, keepdims=True))
    p_w = jnp.exp(s_w - m)
    p_wn = jnp.exp(s_wn - m)
    vwn = _pad_rows(kvn[:, 5 * LANE:6 * LANE], LANE).astype(BF16)
    o_w = (_dot_nt(p_w.astype(BF16), win_ref[LANE:2 * LANE, :].astype(BF16)) + _dotf(p_wn.astype(BF16), vwn)) / (
        jnp.sum(p_w, axis=1, keepdims=True) + jnp.sum(p_wn, axis=1, keepdims=True))

    small = sm_ref[...]
    gate = [jax.nn.sigmoid(jnp.concatenate([small[:, 3 * h + c:3 * h + c + 1] for h in range(NSA_HEADS)], axis=0))
            for c in range(3)]
    o = gate[0] * o_c + gate[1] * o_s + gate[2] * o_w
    for h in range(NSA_HEADS):
        o_ref[:, h * LANE:(h + 1) * LANE] = o[h * lq:(h + 1) * lq].astype(o_ref.dtype)


def _pool_weights_t(wck, wcv, past):
    pos = jnp.arange(past)
    onehot = (jnp.arange(LANE)[:, None] == (pos // NSA_BLOCK)[None, :]).astype(F32)
    w = jnp.stack([wck, wcv])[:, :, pos % NSA_BLOCK]
    return (w[:, :, None, :] * onehot[None, None]).astype(BF16)


def _nsa_sample(page_table, qh, kv, small, cache_t, win_t, wt, *, layer, row0, lq, past):
    nb, npages = page_table.shape
    page = cache_t.shape[-1]
    rb = row0 // lq
    tokmap = lambda b, pt: (rb + b, 0)

    def page_spec(i):
        return pl.BlockSpec((None, None, 4 * LANE, page), lambda b, pt: (layer, pt[b, i], 0, 0))

    grid_spec = pltpu.PrefetchScalarGridSpec(
        num_scalar_prefetch=1,
        grid=(nb,),
        in_specs=[pl.BlockSpec((lq, 1024), tokmap), pl.BlockSpec((lq, 768), tokmap),
                  pl.BlockSpec((lq, LANE), tokmap)]
        + [page_spec(i) for i in range(npages)]
        + [pl.BlockSpec((None, None, 2 * LANE, win_t.shape[-1]), lambda b, pt: (layer, b, 0, 0)),
           pl.BlockSpec(wt.shape, lambda b, pt: (0, 0, 0, 0))],
        out_specs=pl.BlockSpec((lq, 1024), lambda b, pt: (b, 0)),
        scratch_shapes=[pltpu.VMEM((LANE, past), BF16), pltpu.VMEM((LANE, past), BF16),
                        pltpu.VMEM((2 * LANE, past), BF16), pltpu.VMEM((LANE, past), BF16)])
    return pl.pallas_call(
        functools.partial(_nsa_sample_kernel, past=past, lq=lq, npages=npages, page=page),
        grid_spec=grid_spec,
        out_shape=jax.ShapeDtypeStruct((nb * lq, 1024), BF16),
        compiler_params=_cparams(("arbitrary",)),
        name="nsa_sample",
    )(page_table, qh, kv, small, *([cache_t] * npages), win_t, wt)


GDN_ROWS = 64


def _softplus(x):
    return jnp.maximum(x, 0.0) + jnp.log(1.0 + jnp.exp(-jnp.abs(x)))


def _unit_lower_inverse(a, seg):
    n = a.shape[0]
    eye = (lax.broadcasted_iota(jnp.int32, (n, n), 0) == lax.broadcasted_iota(jnp.int32, (n, n), 1)).astype(F32)
    inv = eye - a
    pw = a
    span = 2
    while span < seg:
        pw = _dot_hi(pw, pw)
        inv = inv + _dot_hi(inv, pw)
        span *= 2
    return inv


def _gdn_kernel(x_ref, hist_ref, sm_ref, z_ref, cw_ref, alog_ref, dtb_ref, nw_ref, s0_ref,
                o_ref, s_ref, *, seg, zero_first_hist):
    c = pl.program_id(1)
    rows = GDN_ROWS
    nseg = rows // seg

    @pl.when(c == 0)
    def _():
        s_ref[...] = s0_ref[...]

    x = x_ref[...]
    hist = hist_ref[...]
    if zero_first_hist:
        hist = jnp.where(c == 0, 0.0, hist)
    tpos = lax.broadcasted_iota(jnp.int32, x.shape, 0) % seg
    cw = cw_ref[...]
    y = x * cw[DN_CONV_W - 1:DN_CONV_W]
    for sh in range(1, DN_CONV_W):
        xs = jnp.where(tpos >= sh, pltpu.roll(x, sh, axis=0), pltpu.roll(hist, sh, axis=0))
        y = y + xs * cw[DN_CONV_W - 1 - sh:DN_CONV_W - sh]
    y = y * jax.nn.sigmoid(y)

    ri = lax.broadcasted_iota(jnp.int32, (rows, rows), 0)
    ci = lax.broadcasted_iota(jnp.int32, (rows, rows), 1)
    same = (ri // seg) == (ci // seg)
    tri = same & (ci <= ri)
    strict = same & (ci < ri)
    trif = tri.astype(F32)
    small = sm_ref[...]
    z = z_ref[...]
    outs = []
    for h in range(DN_HEADS):
        sl = slice(h * HEAD_DIM, (h + 1) * HEAD_DIM)
        yq = y[:, sl]
        yk = y[:, DN_WIDTH + h * HEAD_DIM:DN_WIDTH + (h + 1) * HEAD_DIM]
        v = y[:, 2 * DN_WIDTH + h * HEAD_DIM:2 * DN_WIDTH + (h + 1) * HEAD_DIM]
        q = yq * lax.rsqrt(jnp.sum(yq * yq, axis=-1, keepdims=True) + 1e-6) * HEAD_DIM ** -0.5
        k = yk * lax.rsqrt(jnp.sum(yk * yk, axis=-1, keepdims=True) + 1e-6)
        beta = jax.nn.sigmoid(small[:, 28 + h:29 + h])
        g = -jnp.exp(alog_ref[:, h:h + 1]) * _softplus(small[:, 24 + h:25 + h] + dtb_ref[:, h:h + 1])
        gcb = _dot_hi(trif, jnp.broadcast_to(g, (rows, rows)))
        diff = gcb - gcb.T
        decay = jnp.where(tri, jnp.exp(jnp.where(tri, diff, 0.0)), 0.0)
        kb = k * beta
        a_mat = jnp.where(strict, _dot_nt_hi(kb, k) * decay, 0.0)
        tinv = _unit_lower_inverse(a_mat, seg)
        egc = jnp.exp(gcb[:, :HEAD_DIM])
        sol = _dot_hi(tinv, jnp.concatenate([v * beta, kb * egc], axis=1))
        u = sol[:, :HEAD_DIM]
        w = sol[:, HEAD_DIM:]
        a_qk = jnp.where(tri, _dot_nt_hi(q, k) * decay, 0.0)
        qg = q * egc
        vn, oq, glast = [], [], []
        for s in range(nseg):
            rs = slice(s * seg, (s + 1) * seg)
            st = s_ref[s, h]
            vn.append(u[rs] - _dot_hi(w[rs], st))
            oq.append(_dot_hi(qg[rs], st))
            glast.append(gcb[(s + 1) * seg - 1:(s + 1) * seg, :HEAD_DIM])
        v_new = vn[0] if nseg == 1 else jnp.concatenate(vn, axis=0)
        o = (oq[0] if nseg == 1 else jnp.concatenate(oq, axis=0)) + _dot_hi(a_qk, v_new)
        gl_rows = glast[0] if nseg == 1 else jnp.concatenate(
            [jnp.broadcast_to(gl, (seg, HEAD_DIM)) for gl in glast], axis=0)
        kdt = (k * jnp.exp(gl_rows - gcb[:, :HEAD_DIM])).T
        for s in range(nseg):
            kds = kdt if nseg == 1 else jnp.where((ci // seg) == s, kdt, 0.0)
            s_ref[s, h] = s_ref[s, h] * jnp.exp(glast[s]) + _dot_hi(kds, v_new)
        o = o * lax.rsqrt(jnp.mean(o * o, axis=-1, keepdims=True) + NORM_EPS) * nw_ref[...]
        zh = z[:, sl]
        outs.append(o * (zh * jax.nn.sigmoid(zh)))
    o_ref[...] = jnp.concatenate(outs, axis=1).astype(o_ref.dtype)


def _gdn(dnqkv, hist, small, z, cw, alog, dtb, nw, s0, *, seg, nb, nchunk, row0, hist_map, zero_first_hist):
    nseg = GDN_ROWS // seg
    rmap = lambda b, c: (row0 + b * nchunk + c, 0)
    const = lambda b, c: (0, 0)
    smap = lambda b, c: (b, 0, 0, 0)
    sblock = (nseg, DN_HEADS, HEAD_DIM, HEAD_DIM)
    return pl.pallas_call(
        functools.partial(_gdn_kernel, seg=seg, zero_first_hist=zero_first_hist),
        grid=(nb, nchunk),
        in_specs=[pl.BlockSpec((GDN_ROWS, 768), rmap), pl.BlockSpec((GDN_ROWS, 768), hist_map),
                  pl.BlockSpec((GDN_ROWS, LANE), rmap), pl.BlockSpec((GDN_ROWS, 256), rmap),
                  pl.BlockSpec(cw.shape, const), pl.BlockSpec(alog.shape, const),
                  pl.BlockSpec(dtb.shape, const), pl.BlockSpec(nw.shape, const),
                  pl.BlockSpec(sblock, smap)],
        out_specs=[pl.BlockSpec((GDN_ROWS, 256), lambda b, c: (b * nchunk + c, 0)),
                   pl.BlockSpec(sblock, smap)],
        out_shape=[jax.ShapeDtypeStruct((nb * nchunk * GDN_ROWS, 256), BF16),
                   jax.ShapeDtypeStruct(s0.shape, F32)],
        compiler_params=_cparams(("parallel", "arbitrary")),
        name="gdn",
    )(dnqkv, hist, small, z, cw, alog, dtb, nw, s0)


def _s5_discretize(lre_ref, lim_ref, lstep_ref):
    lr = lre_ref[...]
    li = lim_ref[...]
    dt = jnp.exp(lstep_ref[...])
    mag = jnp.exp(lr * dt)
    ar = mag * jnp.cos(li * dt)
    ai = mag * jnp.sin(li * dt)
    den = lr * lr + li * li
    fr = ((ar - 1.0) * lr + ai * li) / den
    fi = (ai * lr - (ar - 1.0) * li) / den
    return ar, ai, fr, fi


def _s5_kernel(*refs, R, S, with_y):
    if with_y:
        (u_ref, h0r_ref, h0i_ref, lre_ref, lim_ref, lstep_ref, bre_ref, bim_ref, cre_ref, cim_ref, d_ref,
         wglu_ref, y_ref, hr_ref, hi_ref, ar_s, ai_s, bbr_s, bbi_s, xr_s, xi_s) = refs
    else:
        (u_ref, h0r_ref, h0i_ref, lre_ref, lim_ref, lstep_ref, bre_ref, bim_ref,
         hr_ref, hi_ref, ar_s, ai_s, bbr_s, bbi_s, xr_s, xi_s) = refs
    i = pl.program_id(0)

    @pl.when(i == 0)
    def _():
        ar, ai, fr, fi = _s5_discretize(lre_ref, lim_ref, lstep_ref)
        ar_s[...] = ar
        ai_s[...] = ai
        bbr_s[...] = (fr * bre_ref[...] - fi * bim_ref[...]).astype(BF16)
        bbi_s[...] = (fr * bim_ref[...] + fi * bre_ref[...]).astype(BF16)
        hr_ref[...] = h0r_ref[...]
        hi_ref[...] = h0i_ref[...]

    u = u_ref[...]
    ub = u.astype(BF16)
    xr_s[...] = _dotf(ub, bbr_s[...])
    xi_s[...] = _dotf(ub, bbi_s[...])
    ar = jnp.broadcast_to(ar_s[...], (R, S5_LANES))
    ai = jnp.broadcast_to(ai_s[...], (R, S5_LANES))

    def advance(hr, hi, off):
        nr = ar * hr - ai * hi + xr_s[pl.ds(off, R), :]
        ni = ar * hi + ai * hr + xi_s[pl.ds(off, R), :]
        xr_s[pl.ds(off, R), :] = nr
        xi_s[pl.ds(off, R), :] = ni

    advance(hr_ref[...], hi_ref[...], 0)

    def step(s, carry):
        prev = pl.multiple_of((s - 1) * R, R)
        advance(xr_s[pl.ds(prev, R), :], xi_s[pl.ds(prev, R), :], pl.multiple_of(s * R, R))
        return carry

    lax.fori_loop(1, S, step, 0)
    hr_ref[...] = xr_s[pl.ds((S - 1) * R, R), :]
    hi_ref[...] = xi_s[pl.ds((S - 1) * R, R), :]
    if with_y:
        y = _dotf(xr_s[...].astype(BF16), cre_ref[...]) - _dotf(xi_s[...].astype(BF16), cim_ref[...])
        y = y + d_ref[...] * u
        gl = jax.nn.gelu(y)
        y_ref[...] = (gl * jax.nn.sigmoid(_dotf(gl.astype(BF16), wglu_ref[...]))).astype(y_ref.dtype)


def _s5_scan(u_rows, h0r, h0i, prm, *, R, S, with_y):
    n = u_rows.shape[0]
    rows = R * S
    const = lambda i: (0, 0)
    rmap = lambda i: (i, 0)
    ins = [u_rows, h0r, h0i, prm["lre"], prm["lim"], prm["lstep"], prm["bre"], prm["bim"]]
    if with_y:
        ins += [prm["cre"], prm["cim"], prm["d"], prm["wglu"]]
    in_specs = [pl.BlockSpec((rows, S5_WIDTH), rmap)] + [pl.BlockSpec(a.shape, const) for a in ins[1:]]
    st_spec = pl.BlockSpec((R, S5_LANES), const)
    st_shape = jax.ShapeDtypeStruct((R, S5_LANES), F32)
    out_specs = [st_spec, st_spec]
    out_shape = [st_shape, st_shape]
    if with_y:
        out_specs = [pl.BlockSpec((rows, S5_WIDTH), rmap)] + out_specs
        out_shape = [jax.ShapeDtypeStruct((n, S5_WIDTH), BF16)] + out_shape
    return pl.pallas_call(
        functools.partial(_s5_kernel, R=R, S=S, with_y=with_y),
        grid=(n // rows,),
        in_specs=in_specs,
        out_specs=out_specs,
        out_shape=out_shape,
        scratch_shapes=[pltpu.VMEM((1, S5_LANES), F32), pltpu.VMEM((1, S5_LANES), F32),
                        pltpu.VMEM((S5_WIDTH, S5_LANES), BF16), pltpu.VMEM((S5_WIDTH, S5_LANES), BF16),
                        pltpu.VMEM((rows, S5_LANES), F32), pltpu.VMEM((rows, S5_LANES), F32)],
        compiler_params=_cparams(("arbitrary",)),
        name="s5_scan",
    )(*ins)


def _s5_carry_kernel(er_ref, ei_ref, lre_ref, lim_ref, lstep_ref, ir_ref, ii_ref, fr_ref, fi_ref, *, nseg, nsteps):
    ar, ai, _, _ = _s5_discretize(lre_ref, lim_ref, lstep_ref)
    pr, pi_ = ar, ai
    n = 1
    while n < nsteps:
        pr, pi_ = pr * pr - pi_ * pi_, 2.0 * pr * pi_
        n *= 2
    for b in range(er_ref.shape[0] // nseg):
        cr = jnp.zeros((1, S5_LANES), F32)
        ci = jnp.zeros((1, S5_LANES), F32)
        for s in range(nseg):
            r = b * nseg + s
            ir_ref[r:r + 1, :] = cr
            ii_ref[r:r + 1, :] = ci
            er = er_ref[r:r + 1, :]
            ei = ei_ref[r:r + 1, :]
            cr, ci = pr * cr - pi_ * ci + er, pr * ci + pi_ * cr + ei
        fr_ref[b:b + 1, :] = cr
        fi_ref[b:b + 1, :] = ci


def _s5_carry(er, ei, prm, *, nseg, nsteps):
    assert nsteps & (nsteps - 1) == 0
    nb = er.shape[0] // nseg
    full = lambda a: pl.BlockSpec(a.shape, lambda: (0,) * a.ndim)
    ins = [er, ei, prm["lre"], prm["lim"], prm["lstep"]]
    outs = [jax.ShapeDtypeStruct(er.shape, F32)] * 2 + [jax.ShapeDtypeStruct((nb, S5_LANES), F32)] * 2
    return pl.pallas_call(
        functools.partial(_s5_carry_kernel, nseg=nseg, nsteps=nsteps),
        in_specs=[full(a) for a in ins],
        out_specs=[full(o) for o in outs],
        out_shape=outs,
        name="s5_carry",
    )(*ins)


S5_SEGS = 8


def _s5_prep(lre, lim, lstep, bre, bim, cre, cim, d, wglu):
    eye = jnp.eye(S5_GROUPS, dtype=F32)
    bexp = lambda b: jnp.einsum("gpc,gh->gchp", b, eye).reshape(S5_WIDTH, S5_LANES)
    cexp = lambda c: jnp.einsum("gcp,gh->gphc", c, eye).reshape(S5_LANES, S5_WIDTH).astype(BF16)
    return {"lre": lre.reshape(1, S5_LANES), "lim": lim.reshape(1, S5_LANES),
            "lstep": jnp.repeat(lstep, S5_STATE).reshape(1, S5_LANES),
            "bre": bexp(bre), "bim": bexp(bim), "cre": cexp(cre), "cim": cexp(cim),
            "d": d.reshape(1, S5_WIDTH), "wglu": wglu.astype(BF16)}


def _s5_prompt(u, prm, nb, lq, steps_per_tile=32):
    nsteps = lq // S5_SEGS
    R = nb * S5_SEGS
    u_rows = u.reshape(nb, S5_SEGS, nsteps, S5_WIDTH).transpose(2, 0, 1, 3).reshape(nsteps * R, S5_WIDTH)
    zero = jnp.zeros((R, S5_LANES), F32)
    er, ei = _s5_scan(u_rows, zero, zero, prm, R=R, S=steps_per_tile, with_y=False)
    ir, ii, fr, fi = _s5_carry(er, ei, prm, nseg=S5_SEGS, nsteps=nsteps)
    y_rows, _, _ = _s5_scan(u_rows, ir, ii, prm, R=R, S=steps_per_tile, with_y=True)
    y = y_rows.reshape(nsteps, nb, S5_SEGS, S5_WIDTH).transpose(1, 2, 0, 3).reshape(nb * lq, S5_WIDTH)
    return y, fr, fi


def _s5_sample(u, h0r, h0i, prm, nb, lq, steps_per_tile=4):
    u_rows = u.reshape(nb, lq, S5_WIDTH).transpose(1, 0, 2).reshape(lq * nb, S5_WIDTH)
    y_rows, fr, fi = _s5_scan(u_rows, h0r, h0i, prm, R=nb, S=steps_per_tile, with_y=True)
    y = y_rows.reshape(lq, nb, S5_WIDTH).transpose(1, 0, 2).reshape(nb * lq, S5_WIDTH)
    return y, fr, fi


def _sample_conv_hist(buf):
    nb = buf.shape[0]
    blk = buf.reshape(nb // 8, 8, DN_CONV_W - 1, 768)
    blk = jnp.roll(blk, -1, axis=1)
    blk = jnp.pad(blk, ((0, 0), (0, 0), (8 - (DN_CONV_W - 1), 0), (0, 0)))
    return blk.reshape(nb * 8, 768)


def _rms(x, g):
    return x * lax.rsqrt(jnp.mean(x * x, axis=-1, keepdims=True) + NORM_EPS) * g


def _mlp_kernel(x_ref, on_ref, od_ref, os_ref, wn_ref, wd_ref, ws_ref, ln2_ref, up_ref, dn_ref, lnf_ref,
                o_ref, h2_s, *, final_norm):
    j = pl.program_id(1)

    @pl.when(j == 0)
    def _():
        x1 = (x_ref[...] + _dotf(on_ref[...], wn_ref[...]) + _dotf(od_ref[...], wd_ref[...])
              + _dotf(os_ref[...], ws_ref[...]))
        o_ref[...] = x1
        h2_s[...] = _rms(x1, ln2_ref[...]).astype(BF16)

    a = jnp.maximum(_dotf(h2_s[...], up_ref[...]), 0.0)
    o_ref[...] += _dotf((a * a).astype(BF16), dn_ref[...])

    if final_norm:
        @pl.when(j == pl.num_programs(1) - 1)
        def _():
            o_ref[...] = _rms(o_ref[...], lnf_ref[...])


def _mlp(x, o_nsa, o_dn, o_s5, wn, wd, ws, ln2, up, dn, lnf, *, final_norm, tm=512, tf=1024):
    n = x.shape[0]
    row = lambda i, j: (i, 0)
    const = lambda i, j: (0, 0)
    return pl.pallas_call(
        functools.partial(_mlp_kernel, final_norm=final_norm),
        grid=(n // tm, D_FF // tf),
        in_specs=[pl.BlockSpec((tm, D_MODEL), row), pl.BlockSpec((tm, 1024), row),
                  pl.BlockSpec((tm, DN_WIDTH), row), pl.BlockSpec((tm, S5_WIDTH), row),
                  pl.BlockSpec(wn.shape, const), pl.BlockSpec(wd.shape, const), pl.BlockSpec(ws.shape, const),
                  pl.BlockSpec((1, D_MODEL), const),
                  pl.BlockSpec((D_MODEL, tf), lambda i, j: (0, j)),
                  pl.BlockSpec((tf, D_MODEL), lambda i, j: (j, 0)),
                  pl.BlockSpec((1, D_MODEL), const)],
        out_specs=pl.BlockSpec((tm, D_MODEL), row),
        out_shape=jax.ShapeDtypeStruct((n, D_MODEL), F32),
        scratch_shapes=[pltpu.VMEM((tm, D_MODEL), BF16)],
        compiler_params=_cparams(("parallel", "arbitrary")),
        name="out_mlp",
    )(x, o_nsa, o_dn, o_s5, wn, wd, ws, ln2, up, dn, lnf)


_NSA_W = NSA_HEADS * HEAD_DIM
_KV_W = 6 * NSA_KV_HEADS * HEAD_DIM
_GATE_W = NSA_HEADS * 3
_OFF_KV = _NSA_W
_OFF_GATE = _OFF_KV + _KV_W
_OFF_DN = _OFF_GATE + _GATE_W
_OFF_A = _OFF_DN + 3 * DN_WIDTH
_OFF_B = _OFF_A + DN_HEADS
_OFF_Z = _OFF_B + DN_HEADS
_OFF_U = _OFF_Z + DN_WIDTH


def _pad_heads(w, axis):
    w = jnp.moveaxis(w, axis, 0).reshape((NSA_HEADS, HEAD_DIM) + w.shape[:axis] + w.shape[axis + 1:])
    out = jnp.zeros((NSA_HEADS, LANE) + w.shape[2:], w.dtype)
    for h in range(NSA_HEADS):
        g = h // NSA_GROUP
        out = out.at[h, g * HEAD_DIM:(g + 1) * HEAD_DIM].set(w[h])
    out = out.reshape((NSA_HEADS * LANE,) + w.shape[2:])
    return jnp.moveaxis(out, 0, axis)


def _layer_weights(w_in, w_out, wck, wcv):
    wq = _pad_heads(w_in[:, :_NSA_W], 1).astype(BF16)
    wkv = w_in[:, _OFF_KV:_OFF_GATE].astype(BF16)
    wdn = w_in[:, _OFF_DN:_OFF_A].astype(BF16)
    wz = w_in[:, _OFF_Z:_OFF_U].astype(BF16)
    wu = w_in[:, _OFF_U:].astype(BF16)
    wsm = jnp.zeros((D_MODEL, LANE), F32)
    wsm = wsm.at[:, :_GATE_W].set(w_in[:, _OFF_GATE:_OFF_DN])
    wsm = wsm.at[:, _GATE_W:_GATE_W + 2 * DN_HEADS].set(w_in[:, _OFF_A:_OFF_Z]).astype(BF16)
    wexp = jnp.concatenate([jnp.repeat(wck.T, HEAD_DIM, axis=1), jnp.repeat(wcv.T, HEAD_DIM, axis=1)], axis=1)
    won = _pad_heads(w_out[:_NSA_W], 0).astype(BF16)
    wod = w_out[_NSA_W:_NSA_W + DN_WIDTH].astype(BF16)
    wos = w_out[_NSA_W + DN_WIDTH:].astype(BF16)
    return wq, wkv, wdn, wz, wu, wsm, wexp, won, wod, wos


def kernel(x_prompt, x_sample, cache_nsa_kv, cache_win_kv, state_dn_conv, state_dn, state_s5_re, state_s5_im,
           page_table, ln1, ln2, ln_f, w_in, w_out, nsa_wck, nsa_wcv, dn_conv_w, dn_a_log, dn_dt_bias,
           dn_norm_w, s5_lambda_re, s5_lambda_im, s5_log_step, s5_b_re, s5_b_im, s5_c_re, s5_c_im, s5_d,
           s5_w_glu, mlp_up, mlp_down):
    nbp, lp, _ = x_prompt.shape
    nbs, ls, _ = x_sample.shape
    depth = w_in.shape[0]
    rows_p = nbp * lp
    rows_s = nbs * ls
    n_phys, page = cache_nsa_kv.shape[1], cache_nsa_kv.shape[2]
    past = page_table.shape[1] * page
    wlen = cache_win_kv.shape[2]
    G, dh = NSA_KV_HEADS, HEAD_DIM
    assert lp % (S5_SEGS * 32) == 0 and nbs % 8 == 0 and ls == 8 and wlen == WINDOW and lp >= WINDOW

    x = jnp.concatenate([x_prompt.reshape(rows_p, D_MODEL), x_sample.reshape(rows_s, D_MODEL)], axis=0)
    cache_t = cache_nsa_kv.transpose(0, 1, 3, 4, 5, 2).reshape(depth, n_phys, 4 * G * dh, page)
    win_t = cache_win_kv.transpose(0, 1, 3, 4, 5, 2).reshape(depth, nbs, 2 * G * dh, wlen)
    nchunk = lp // GDN_ROWS
    outs_p, outs_s = [], []
    for l in range(depth):
        wq, wkv, wdn, wz, wu, wsm, wexp, won, wod, wos = _layer_weights(w_in[l], w_out[l], nsa_wck[l], nsa_wcv[l])
        qh, kv, dn, z, u, small, pool = _proj(x, ln1[l][None], wq, wkv, wdn, wz, wu, wsm, wexp)

        oc, sel = _cmp_prompt(qh, pool, nbp, lp)
        osel = _sel_prompt(qh, sel, kv, nbp, lp)
        on_p = _win_prompt(qh, kv, oc, osel, small, nbp, lp)
        wt = _pool_weights_t(nsa_wck[l], nsa_wcv[l], past)
        on_s = _nsa_sample(page_table, qh, kv, small, cache_t, win_t, wt, layer=l, row0=rows_p, lq=ls, past=past)

        gdn_w = (dn_conv_w[l].T, dn_a_log[l][None], dn_dt_bias[l][None], dn_norm_w[l][None])
        od_p, dn_p = _gdn(dn, dn, small, z, *gdn_w, jnp.zeros((nbp, DN_HEADS, dh, dh), F32),
                          seg=GDN_ROWS, nb=nbp, nchunk=nchunk, row0=0,
                          hist_map=lambda b, c: (jnp.maximum(b * nchunk + c - 1, 0), 0), zero_first_hist=True)
        od_s, dn_s = _gdn(dn, _sample_conv_hist(state_dn_conv[l]), small, z, *gdn_w, state_dn[l],
                          seg=ls, nb=rows_s // GDN_ROWS, nchunk=1, row0=rows_p // GDN_ROWS,
                          hist_map=lambda b, c: (b, 0), zero_first_hist=False)

        prm = _s5_prep(s5_lambda_re[l], s5_lambda_im[l], s5_log_step[l], s5_b_re[l], s5_b_im[l],
                       s5_c_re[l], s5_c_im[l], s5_d[l], s5_w_glu[l])
        os_p, s5r_p, s5i_p = _s5_prompt(u[:rows_p], prm, nbp, lp)
        os_s, s5r_s, s5i_s = _s5_sample(u[rows_p:], state_s5_re[l].reshape(nbs, S5_LANES),
                                        state_s5_im[l].reshape(nbs, S5_LANES), prm, nbs, ls)

        x = _mlp(x, jnp.concatenate([on_p, on_s], axis=0), jnp.concatenate([od_p, od_s], axis=0),
                 jnp.concatenate([os_p, os_s], axis=0), won, wod, wos, ln2[l][None],
                 mlp_up[l].astype(BF16), mlp_down[l].astype(BF16), ln_f[None], final_norm=(l == depth - 1))

        kv_p = kv[:rows_p].reshape(nbp, lp, 6, G, dh)
        kv_s = kv[rows_p:].reshape(nbs, ls, 6, G, dh)
        win_s = jnp.concatenate([cache_win_kv[l], kv_s[:, :, 4:]], axis=1)[:, -WINDOW:]
        dn_rows_p = dn[:rows_p].reshape(nbp, lp, 3 * DN_WIDTH)
        dn_rows_s = dn[rows_p:].reshape(nbs, ls, 3 * DN_WIDTH)
        outs_p.append((kv_p[:, :, :4], kv_p[:, -WINDOW:, 4:], dn_rows_p[:, -(DN_CONV_W - 1):], dn_p,
                       s5r_p.reshape(nbp, S5_GROUPS, S5_STATE), s5i_p.reshape(nbp, S5_GROUPS, S5_STATE)))
        outs_s.append((kv_s[:, :, :4], win_s, dn_rows_s[:, -(DN_CONV_W - 1):], dn_s,
                       s5r_s.reshape(nbs, S5_GROUPS, S5_STATE), s5i_s.reshape(nbs, S5_GROUPS, S5_STATE)))

    y_prompt = x[:rows_p].reshape(nbp, lp, D_MODEL)
    y_sample = x[rows_p:].reshape(nbs, ls, D_MODEL)
    res = [y_prompt, y_sample]
    for i in range(6):
        res.append(jnp.stack([o[i] for o in outs_p], axis=0))
        res.append(jnp.stack([o[i] for o in outs_s], axis=0))
    return tuple(res)
```

```python
import functools
import math

import jax
import jax.numpy as jnp
import numpy as np
from jax import lax
from jax.experimental import pallas as pl
from jax.experimental.pallas import tpu as pltpu

F32 = jnp.float32
BF16 = jnp.bfloat16
HIGHEST = lax.Precision.HIGHEST

D_MODEL = 1024
HEAD_DIM = 64
NSA_HEADS = 8
NSA_KV_HEADS = 2
NSA_GROUP = 4
NSA_BLOCK = 64
NSA_TOPK = 16
WINDOW = 512
DN_HEADS = 4
DN_WIDTH = 256
DN_CONV_W = 4
S5_GROUPS = 16
S5_GROUP_CH = 16
S5_STATE = 64
S5_WIDTH = 256
S5_LANES = S5_GROUPS * S5_STATE
D_FF = 4096
NORM_EPS = 1e-6
BIG = 1e9
NEG = -1e30
LANE = 128
NSA_TQ = 256
VMEM_LIMIT = 48 * 1024 * 1024

_NT = (((1,), (1,)), ((), ()))


def _slope(h):
    return 2.0 ** (-(h + 1))


def _cparams(sem):
    return pltpu.CompilerParams(dimension_semantics=sem, vmem_limit_bytes=VMEM_LIMIT)


def _dotf(a, b):
    return jnp.dot(a, b, preferred_element_type=F32)


def _dot_nt(a, b):
    return lax.dot_general(a, b, _NT, preferred_element_type=F32)


def _dot_hi(a, b):
    return jnp.dot(a, b, preferred_element_type=F32, precision=HIGHEST)


def _dot_nt_hi(a, b):
    return lax.dot_general(a, b, _NT, preferred_element_type=F32, precision=HIGHEST)


def _proj_kernel(x_ref, g_ref, wq_ref, wkv_ref, wdn_ref, wz_ref, wu_ref, ws_ref, wexp_ref,
                 q_ref, kv_ref, dn_ref, z_ref, u_ref, s_ref, pool_ref):
    x = x_ref[...]
    h = x * lax.rsqrt(jnp.mean(x * x, axis=-1, keepdims=True) + NORM_EPS) * g_ref[...]
    hb = h.astype(BF16)
    q_ref[...] = _dotf(hb, wq_ref[...])
    kv = _dotf(hb, wkv_ref[...])
    kv_ref[...] = kv
    dn_ref[...] = _dotf(hb, wdn_ref[...])
    z_ref[...] = _dotf(hb, wz_ref[...])
    u_ref[...] = _dotf(hb, wu_ref[...])
    s_ref[...] = _dotf(hb, ws_ref[...])
    tm = x.shape[0]
    kc = kv[:, :2 * LANE].reshape(tm // NSA_BLOCK, NSA_BLOCK, 2 * LANE) * wexp_ref[...][None]
    pool_ref[...] = jnp.sum(kc, axis=1)


def _proj(x, ln, wq, wkv, wdn, wz, wu, ws, wexp, tm=512):
    n = x.shape[0]
    const = lambda i: (0, 0)
    row = lambda i: (i, 0)
    outs = [(n, 1024), (n, 768), (n, 768), (n, 256), (n, 256), (n, LANE), (n // NSA_BLOCK, 2 * LANE)]
    return pl.pallas_call(
        _proj_kernel,
        grid=(n // tm,),
        in_specs=[pl.BlockSpec((tm, D_MODEL), row), pl.BlockSpec((1, D_MODEL), const)]
        + [pl.BlockSpec(w.shape, const) for w in (wq, wkv, wdn, wz, wu, ws, wexp)],
        out_specs=[pl.BlockSpec((tm, s[1]), row) for s in outs[:-1]]
        + [pl.BlockSpec((tm // NSA_BLOCK, 2 * LANE), row)],
        out_shape=[jax.ShapeDtypeStruct(s, F32) for s in outs],
        compiler_params=_cparams(("parallel",)),
        name="proj",
    )(x, ln, wq, wkv, wdn, wz, wu, ws, wexp)


def _topk_blocks(score, blk, nblk):
    sel = jnp.zeros(score.shape, dtype=jnp.bool_)
    for _ in range(NSA_TOPK):
        m = jnp.max(score, axis=0, keepdims=True)
        idx = jnp.min(jnp.where(score == m, blk, nblk), axis=0, keepdims=True)
        pick = blk == idx
        sel = sel | pick
        score = jnp.where(pick, -jnp.inf, score)
    return sel


def _cmp_heads(q_tile, kc, vct, t, blk, nblk_valid, write_oc):
    nblk = blk.shape[0]
    ok = ((blk + 1) * NSA_BLOCK - 1 <= t) & (blk < nblk_valid)
    okf = ok.astype(F32)
    dist = t.astype(F32) - (blk.astype(F32) * NSA_BLOCK + (NSA_BLOCK - 1) / 2.0)
    cur = t // NSA_BLOCK
    forced = (blk == 0) | (blk == cur) | (blk == cur - 1)
    sels = []
    for g in range(NSA_KV_HEADS):
        imp = jnp.zeros(blk.shape, F32)
        for r in range(NSA_GROUP):
            h = g * NSA_GROUP + r
            q = (q_tile(h) * HEAD_DIM ** -0.5).astype(BF16)
            s = _dot_nt(kc, q)
            s = jnp.where(ok, s - _slope(h) * dist, NEG)
            m = jnp.max(s, axis=0, keepdims=True)
            p = jnp.exp(s - m) * okf
            p = p / jnp.maximum(jnp.sum(p, axis=0, keepdims=True), 1e-30)
            imp = imp + p
            write_oc(h, _dotf(vct, p.astype(BF16)))
        score = jnp.where(blk > cur, -BIG, jnp.where(forced, BIG, imp))
        sel = _topk_blocks(score, blk, nblk) & (blk <= cur)
        sels.append(sel)
    return sels


def _cmp_kernel(q_ref, kcv_ref, oc_ref, sel_ref, any_ref, *, tq):
    qt = pl.program_id(1)
    kcv = kcv_ref[...]
    nblk = kcv.shape[0]
    kc = kcv[:, :LANE].astype(BF16)
    vct = kcv[:, LANE:].T.astype(BF16)
    blk = lax.broadcasted_iota(jnp.int32, (nblk, tq), 0)
    t = qt * tq + lax.broadcasted_iota(jnp.int32, (nblk, tq), 1)

    def write_oc(h, oct):
        oc_ref[:, h * LANE:(h + 1) * LANE] = oct.T

    sels = _cmp_heads(lambda h: q_ref[:, h * LANE:(h + 1) * LANE], kc, vct, t, blk, nblk, write_oc)
    for g in range(NSA_KV_HEADS):
        selt = sels[g].astype(F32).T
        sel_ref[:, g * LANE:(g + 1) * LANE] = selt
        any_ref[:, g * LANE:(g + 1) * LANE] = jnp.broadcast_to(jnp.max(selt, axis=0, keepdims=True), (8, LANE))


def _cmp_prompt(qh, pool, nb, lq, tq):
    nq = lq // tq
    nblk = lq // NSA_BLOCK
    assert nblk == LANE
    return pl.pallas_call(
        functools.partial(_cmp_kernel, tq=tq),
        grid=(nb, nq),
        in_specs=[pl.BlockSpec((tq, 1024), lambda b, i: (b * nq + i, 0)),
                  pl.BlockSpec((nblk, 2 * LANE), lambda b, i: (b, 0))],
        out_specs=[pl.BlockSpec((tq, 1024), lambda b, i: (b * nq + i, 0)),
                   pl.BlockSpec((tq, 2 * LANE), lambda b, i: (b * nq + i, 0)),
                   pl.BlockSpec((8, 2 * LANE), lambda b, i: (b * nq + i, 0))],
        out_shape=[jax.ShapeDtypeStruct((nb * lq, 1024), F32),
                   jax.ShapeDtypeStruct((nb * lq, 2 * LANE), F32),
                   jax.ShapeDtypeStruct((nb * nq * 8, 2 * LANE), F32)],
        compiler_params=_cparams(("parallel", "parallel")),
        name="nsa_cmp",
    )(qh, pool)


def _sel_kernel(act_ref, q_ref, sel_ref, k_ref, vt_ref, o_ref, lhs_s, m_s, l_s, acc_s, *, tq, nq):
    b = pl.program_id(0)
    g = pl.program_id(1)
    qt = pl.program_id(2)
    lanei = lax.broadcasted_iota(jnp.int32, (tq, LANE), 1)
    nio = lanei.astype(F32)
    selg = sel_ref[...] > 0.5
    for r in range(NSA_GROUP):
        slope = jnp.where(g == 0, _slope(r), _slope(NSA_GROUP + r))
        qt_r = q_ref[:, r * LANE:(r + 1) * LANE]
        qt_r = jnp.where(g == 0, qt_r, pltpu.roll(qt_r, HEAD_DIM, axis=1))
        qpart = jnp.where(lanei == HEAD_DIM, slope, qt_r * HEAD_DIM ** -0.5)
        bias = jnp.where(selg, (NSA_BLOCK * slope) * nio, NEG)
        lhs_s[r] = jnp.concatenate([bias.astype(BF16), qpart.astype(BF16)], axis=1)
    m_s[...] = jnp.full(m_s.shape, NEG, F32)
    l_s[...] = jnp.zeros(l_s.shape, F32)
    acc_s[...] = jnp.zeros(acc_s.shape, F32)

    def tile(kt, mask):
        kaug = k_ref[kt]
        vt = vt_ref[kt]
        heads = range(NSA_GROUP)
        ss = [_dot_nt(kaug, lhs_s[r]) for r in heads]
        if mask is not None:
            ss = [jnp.where(mask, s, NEG) for s in ss]
        m_old = [m_s[r] for r in heads]
        m_new = [jnp.maximum(m_old[r], jnp.max(ss[r], axis=0, keepdims=True)) for r in heads]
        ps = [jnp.exp(ss[r] - m_new[r]) for r in heads]
        pv = [_dotf(vt, ps[r].astype(BF16)) for r in heads]
        for r in heads:
            alpha = jnp.exp(m_old[r] - m_new[r])
            l_s[r] = alpha * l_s[r] + jnp.sum(ps[r], axis=0, keepdims=True)
            acc_s[r] = alpha * acc_s[r] + pv[r]
            m_s[r] = m_new[r]

    base = ((b * NSA_KV_HEADS + g) * nq + qt) * nq

    def body(kt, carry):
        @pl.when(act_ref[base + kt] != 0)
        def _():
            tile(kt, None)
        return carry

    lax.fori_loop(0, qt, body, 0)
    krow = lax.broadcasted_iota(jnp.int32, (tq, tq), 0)
    qcol = lax.broadcasted_iota(jnp.int32, (tq, tq), 1)
    tile(qt, krow <= qcol)
    zeros = jnp.zeros((HEAD_DIM, tq), F32)
    for r in range(NSA_GROUP):
        o = jnp.concatenate([acc_s[r] / l_s[r], zeros], axis=0).T
        o_ref[:, r * LANE:(r + 1) * LANE] = jnp.where(g == 0, o, pltpu.roll(o, HEAD_DIM, axis=1))


def _sel_prompt(qh, sel, anyblk, kv, nb, lq, tq):
    nq = lq // tq
    bpt = tq // NSA_BLOCK
    G = NSA_KV_HEADS
    act = anyblk[::8].reshape(nb, nq, G, nq, bpt).max(axis=-1) > 0.5
    act = act.transpose(0, 2, 1, 3).reshape(-1).astype(jnp.int32)
    pos = jnp.arange(lq)
    onehot = (pos[:, None] // NSA_BLOCK == jnp.arange(LANE)[None, :]).astype(BF16)
    jcol = jnp.zeros((lq, HEAD_DIM), BF16).at[:, 0].set((pos % NSA_BLOCK).astype(BF16))
    ksel = kv[:nb * lq, 2 * LANE:3 * LANE].reshape(nb, lq, G, HEAD_DIM).transpose(0, 2, 1, 3).astype(BF16)
    kaug = jnp.concatenate([jnp.broadcast_to(onehot, (nb, G, lq, LANE)), ksel,
                            jnp.broadcast_to(jcol, (nb, G, lq, HEAD_DIM))], axis=-1)
    kaug = kaug.reshape(nb, G, nq, tq, 2 * LANE)
    vt = kv[:nb * lq, 3 * LANE:4 * LANE].reshape(nb, nq, tq, G, HEAD_DIM).transpose(0, 3, 1, 4, 2).astype(BF16)
    qmap = lambda b, g, i, a: (b * nq + i, g)
    grid_spec = pltpu.PrefetchScalarGridSpec(
        num_scalar_prefetch=1,
        grid=(nb, G, nq),
        in_specs=[pl.BlockSpec((tq, NSA_GROUP * LANE), qmap),
                  pl.BlockSpec((tq, LANE), qmap),
                  pl.BlockSpec((None, None, nq, tq, 2 * LANE), lambda b, g, i, a: (b, g, 0, 0, 0)),
                  pl.BlockSpec((None, None, nq, HEAD_DIM, tq), lambda b, g, i, a: (b, g, 0, 0, 0))],
        out_specs=pl.BlockSpec((tq, NSA_GROUP * LANE), qmap),
        scratch_shapes=[pltpu.VMEM((NSA_GROUP, tq, 2 * LANE), BF16),
                        pltpu.VMEM((NSA_GROUP, 1, tq), F32),
                        pltpu.VMEM((NSA_GROUP, 1, tq), F32),
                        pltpu.VMEM((NSA_GROUP, HEAD_DIM, tq), F32)])
    return pl.pallas_call(
        functools.partial(_sel_kernel, tq=tq, nq=nq),
        grid_spec=grid_spec,
        out_shape=jax.ShapeDtypeStruct((nb * lq, 1024), F32),
        compiler_params=_cparams(("parallel", "parallel", "arbitrary")),
        name="nsa_sel",
    )(act, qh, sel, kaug, vt)


def _gate_mix(small, oc, os_, ow, h):
    gt = jax.nn.sigmoid(small[:, 3 * h:3 * h + 3])
    return gt[:, 0:1] * oc + gt[:, 1:2] * os_ + gt[:, 2:3] * ow


def _win_kernel(q_ref, k0_ref, k1_ref, k2_ref, v0_ref, v1_ref, v2_ref, oc_ref, os_ref, sm_ref, o_ref, *, tq):
    qt = pl.program_id(1)
    kcat = jnp.concatenate([k0_ref[...], k1_ref[...], k2_ref[...]], axis=0).astype(BF16)
    vcat = jnp.concatenate([v0_ref[...], v1_ref[...], v2_ref[...]], axis=0).astype(BF16)
    row = lax.broadcasted_iota(jnp.int32, (tq, 3 * tq), 0)
    col = lax.broadcasted_iota(jnp.int32, (tq, 3 * tq), 1)
    dist = row - col + 2 * tq
    valid = (dist >= 0) & (dist < WINDOW) & ((qt - 2) * tq + col >= 0)
    distf = dist.astype(F32)
    small = sm_ref[...]
    for h in range(NSA_HEADS):
        sl = slice(h * LANE, (h + 1) * LANE)
        q = (q_ref[:, sl] * HEAD_DIM ** -0.5).astype(BF16)
        s = _dot_nt(q, kcat)
        s = jnp.where(valid, s - _slope(h) * distf, NEG)
        m = jnp.max(s, axis=1, keepdims=True)
        p = jnp.exp(s - m)
        ow = _dotf(p.astype(BF16), vcat) / jnp.sum(p, axis=1, keepdims=True)
        o_ref[:, sl] = _gate_mix(small, oc_ref[:, sl], os_ref[:, sl], ow, h).astype(o_ref.dtype)


def _win_prompt(qh, kv, oc, osel, small, nb, lq, tq=256):
    assert 2 * tq >= WINDOW - 1
    nq = lq // tq
    row = lambda b, i: (b * nq + i, 0)

    def kmap(back, c):
        return lambda b, i: (b * nq + jnp.maximum(i - back, 0), c)

    return pl.pallas_call(
        functools.partial(_win_kernel, tq=tq),
        grid=(nb, nq),
        in_specs=[pl.BlockSpec((tq, 1024), row)]
        + [pl.BlockSpec((tq, LANE), kmap(back, 4)) for back in (2, 1, 0)]
        + [pl.BlockSpec((tq, LANE), kmap(back, 5)) for back in (2, 1, 0)]
        + [pl.BlockSpec((tq, 1024), row), pl.BlockSpec((tq, 1024), row), pl.BlockSpec((tq, LANE), row)],
        out_specs=pl.BlockSpec((tq, 1024), row),
        out_shape=jax.ShapeDtypeStruct((nb * lq, 1024), BF16),
        compiler_params=_cparams(("parallel", "parallel")),
        name="nsa_win",
    )(qh, kv, kv, kv, kv, kv, kv, oc, osel, small)


def _per_head(idx_h, fn):
    out = jnp.zeros(idx_h.shape, F32)
    for h in range(NSA_HEADS):
        out = jnp.where(idx_h == h, fn(h), out)
    return out


def _pad_rows(a, rows):
    return jnp.concatenate([a, jnp.zeros((rows - a.shape[0], a.shape[1]), a.dtype)], axis=0)


def _nsa_sample_kernel(pt_ref, q_ref, kvn_ref, sm_ref, *rest, past, lq, npages, page):
    pages = rest[:npages]
    win_ref, wt_ref, o_ref, ktc_s, vtc_s, rt_s, vt_s = rest[npages:]
    b = pl.program_id(0)
    nkeys = npages * page
    nrow = NSA_HEADS * lq
    nb_complete = (past + lq) // NSA_BLOCK
    new_blk = past // NSA_BLOCK
    assert past % NSA_BLOCK + lq <= NSA_BLOCK and nkeys == past and new_blk < LANE

    @pl.when(b == 0)
    def _():
        n_i = lax.broadcasted_iota(jnp.int32, (LANE, nkeys), 0)
        pos_i = lax.broadcasted_iota(jnp.int32, (LANE, nkeys), 1)
        rt_s[0:LANE, :] = (n_i == pos_i // NSA_BLOCK).astype(BF16)

    for i in range(npages):
        sl = slice(i * page, (i + 1) * page)
        pg = pages[i]
        ktc_s[:, sl] = pg[0:LANE, :].astype(BF16)
        vtc_s[:, sl] = pg[LANE:2 * LANE, :].astype(BF16)
        rt_s[LANE:2 * LANE, sl] = pg[2 * LANE:3 * LANE, :].astype(BF16)
        vt_s[:, sl] = pg[3 * LANE:4 * LANE, :].astype(BF16)

    q8 = q_ref[...]
    qf = jnp.concatenate([q8[:, h * LANE:(h + 1) * LANE] for h in range(NSA_HEADS)], axis=0) * HEAD_DIM ** -0.5
    qb = qf.astype(BF16)
    kvn = kvn_ref[...]
    lane_lo = lax.broadcasted_iota(jnp.int32, (LANE, LANE), 1) < HEAD_DIM

    ktc = ktc_s[...]
    vtc = vtc_s[...]
    kc = jnp.where(lane_lo, _dot_nt(wt_ref[0, 0], ktc), _dot_nt(wt_ref[0, 1], ktc)).astype(BF16)
    vc = jnp.where(lane_lo, _dot_nt(wt_ref[1, 0], vtc), _dot_nt(wt_ref[1, 1], vtc))
    vct = vc.T.astype(BF16)
    blk = lax.broadcasted_iota(jnp.int32, (LANE, nrow), 0)
    col = lax.broadcasted_iota(jnp.int32, (LANE, nrow), 1)
    t = past + col % lq
    slope_c = _per_head(col // lq, _slope)
    ok = ((blk + 1) * NSA_BLOCK - 1 <= t) & (blk < nb_complete)
    dist = t.astype(F32) - (blk.astype(F32) * NSA_BLOCK + (NSA_BLOCK - 1) / 2.0)
    s = jnp.where(ok, _dot_nt(kc, qb) - slope_c * dist, NEG)
    m = jnp.max(s, axis=0, keepdims=True)
    p = jnp.exp(s - m) * ok.astype(F32)
    p = p / jnp.maximum(jnp.sum(p, axis=0, keepdims=True), 1e-30)
    o_c = _dotf(vct, p.astype(BF16)).T
    ri = lax.broadcasted_iota(jnp.int32, (nrow, nrow), 0)
    ci = lax.broadcasted_iota(jnp.int32, (nrow, nrow), 1)
    gsum = ((ri // (NSA_GROUP * lq) == ci // (NSA_GROUP * lq)) & (ri % lq == ci % lq)).astype(F32)
    imp = _dot_hi(p, gsum)
    cur = t // NSA_BLOCK
    forced = (blk == 0) | (blk == cur) | (blk == cur - 1)
    score = jnp.where(blk > cur, -BIG, jnp.where(forced, BIG, imp))
    sel = (_topk_blocks(score, blk, LANE) & (blk <= cur)).astype(F32).T

    rowi = lax.broadcasted_iota(jnp.int32, (nrow, LANE), 0)
    lanei = lax.broadcasted_iota(jnp.int32, (nrow, LANE), 1)
    slope_r = _per_head(rowi[:, 0:1] // lq, _slope)
    tok_r = rowi % lq
    bias = jnp.where(sel > 0.5, (NSA_BLOCK * slope_r) * lanei.astype(F32), NEG)
    lhs = jnp.concatenate([bias.astype(BF16), qb], axis=1)
    jrow = (lax.broadcasted_iota(jnp.int32, (1, nkeys), 1) % NSA_BLOCK).astype(F32)
    s_p = _dotf(lhs, rt_s[...]) + slope_r * jrow
    new_ok = (lanei <= tok_r) & (lanei < lq)
    bias_new = jnp.sum(jnp.where(lanei == new_blk, bias, 0.0), axis=1, keepdims=True)
    jnew = (past % NSA_BLOCK + lanei).astype(F32)
    kn = _pad_rows(kvn[:, 2 * LANE:3 * LANE], LANE).astype(BF16)
    s_n = jnp.where(new_ok, _dot_nt(qb, kn) + bias_new + slope_r * jnew, NEG)
    m = jnp.maximum(jnp.max(s_p, axis=1, keepdims=True), jnp.max(s_n, axis=1, keepdims=True))
    p_p = jnp.exp(s_p - m)
    p_n = jnp.exp(s_n - m)
    vn = _pad_rows(kvn[:, 3 * LANE:4 * LANE], LANE).astype(BF16)
    o_s = (_dot_nt(p_p.astype(BF16), vt_s[...]) + _dotf(p_n.astype(BF16), vn)) / (
        jnp.sum(p_p, axis=1, keepdims=True) + jnp.sum(p_n, axis=1, keepdims=True))

    wlen = win_ref.shape[1]
    wi = lax.broadcasted_iota(jnp.int32, (nrow, wlen), 1)
    tok_w = lax.broadcasted_iota(jnp.int32, (nrow, wlen), 0) % lq
    dist_w = wlen + tok_w - wi
    s_w = _dotf(qb, win_ref[0:LANE, :].astype(BF16))
    s_w = jnp.where(dist_w < WINDOW, s_w - slope_r * dist_w.astype(F32), NEG)
    kwn = _pad_rows(kvn[:, 4 * LANE:5 * LANE], LANE).astype(BF16)
    s_wn = jnp.where(new_ok, _dot_nt(qb, kwn) - slope_r * (tok_r - lanei).astype(F32), NEG)
    m = jnp.maximum(jnp.max(s_w, axis=1, keepdims=True), jnp.max(s_wn, axis=1, keepdims=True))
    p_w = jnp.exp(s_w - m)
    p_wn = jnp.exp(s_wn - m)
    vwn = _pad_rows(kvn[:, 5 * LANE:6 * LANE], LANE).astype(BF16)
    o_w = (_dot_nt(p_w.astype(BF16), win_ref[LANE:2 * LANE, :].astype(BF16)) + _dotf(p_wn.astype(BF16), vwn)) / (
        jnp.sum(p_w, axis=1, keepdims=True) + jnp.sum(p_wn, axis=1, keepdims=True))

    small = sm_ref[...]
    gate = [jax.nn.sigmoid(jnp.concatenate([small[:, 3 * h + c:3 * h + c + 1] for h in range(NSA_HEADS)], axis=0))
            for c in range(3)]
    o = gate[0] * o_c + gate[1] * o_s + gate[2] * o_w
    for h in range(NSA_HEADS):
        o_ref[:, h * LANE:(h + 1) * LANE] = o[h * lq:(h + 1) * lq].astype(o_ref.dtype)


def _pool_weights_t(wck, wcv, past):
    pos = jnp.arange(past)
    onehot = (jnp.arange(LANE)[:, None] == (pos // NSA_BLOCK)[None, :]).astype(F32)
    w = jnp.stack([wck, wcv])[:, :, pos % NSA_BLOCK]
    return (w[:, :, None, :] * onehot[None, None]).astype(BF16)


def _nsa_sample(page_table, qh, kv, small, cache_t, win_t, wt, *, layer, row0, lq, past):
    nb, npages = page_table.shape
    page = cache_t.shape[-1]
    rb = row0 // lq
    tokmap = lambda b, pt: (rb + b, 0)

    def page_spec(i):
        return pl.BlockSpec((None, None, 4 * LANE, page), lambda b, pt: (layer, pt[b, i], 0, 0))

    grid_spec = pltpu.PrefetchScalarGridSpec(
        num_scalar_prefetch=1,
        grid=(nb,),
        in_specs=[pl.BlockSpec((lq, 1024), tokmap), pl.BlockSpec((lq, 768), tokmap),
                  pl.BlockSpec((lq, LANE), tokmap)]
        + [page_spec(i) for i in range(npages)]
        + [pl.BlockSpec((None, None, 2 * LANE, win_t.shape[-1]), lambda b, pt: (layer, b, 0, 0)),
           pl.BlockSpec(wt.shape, lambda b, pt: (0, 0, 0, 0))],
        out_specs=pl.BlockSpec((lq, 1024), lambda b, pt: (b, 0)),
        scratch_shapes=[pltpu.VMEM((LANE, past), BF16), pltpu.VMEM((LANE, past), BF16),
                        pltpu.VMEM((2 * LANE, past), BF16), pltpu.VMEM((LANE, past), BF16)])
    return pl.pallas_call(
        functools.partial(_nsa_sample_kernel, past=past, lq=lq, npages=npages, page=page),
        grid_spec=grid_spec,
        out_shape=jax.ShapeDtypeStruct((nb * lq, 1024), BF16),
        compiler_params=_cparams(("arbitrary",)),
        name="nsa_sample",
    )(page_table, qh, kv, small, *([cache_t] * npages), win_t, wt)


GDN_ROWS = 64


def _softplus(x):
    return jnp.maximum(x, 0.0) + jnp.log(1.0 + jnp.exp(-jnp.abs(x)))


def _unit_lower_inverse(a, seg):
    n = a.shape[0]
    eye = (lax.broadcasted_iota(jnp.int32, (n, n), 0) == lax.broadcasted_iota(jnp.int32, (n, n), 1)).astype(F32)
    inv = eye - a
    pw = a
    span = 2
    while span < seg:
        pw = _dot_hi(pw, pw)
        inv = inv + _dot_hi(inv, pw)
        span *= 2
    return inv


def _gdn_kernel(x_ref, hist_ref, sm_ref, z_ref, cw_ref, alog_ref, dtb_ref, nw_ref, s0_ref,
                o_ref, s_ref, *, seg, zero_first_hist):
    c = pl.program_id(1)
    rows = GDN_ROWS
    nseg = rows // seg

    @pl.when(c == 0)
    def _():
        s_ref[...] = s0_ref[...]

    x = x_ref[...]
    hist = hist_ref[...]
    if zero_first_hist:
        hist = jnp.where(c == 0, 0.0, hist)
    tpos = lax.broadcasted_iota(jnp.int32, x.shape, 0) % seg
    cw = cw_ref[...]
    y = x * cw[DN_CONV_W - 1:DN_CONV_W]
    for sh in range(1, DN_CONV_W):
        xs = jnp.where(tpos >= sh, pltpu.roll(x, sh, axis=0), pltpu.roll(hist, sh, axis=0))
        y = y + xs * cw[DN_CONV_W - 1 - sh:DN_CONV_W - sh]
    y = y * jax.nn.sigmoid(y)

    ri = lax.broadcasted_iota(jnp.int32, (rows, rows), 0)
    ci = lax.broadcasted_iota(jnp.int32, (rows, rows), 1)
    same = (ri // seg) == (ci // seg)
    tri = same & (ci <= ri)
    strict = same & (ci < ri)
    trif = tri.astype(F32)
    small = sm_ref[...]
    z = z_ref[...]
    outs = []
    for h in range(DN_HEADS):
        sl = slice(h * HEAD_DIM, (h + 1) * HEAD_DIM)
        yq = y[:, sl]
        yk = y[:, DN_WIDTH + h * HEAD_DIM:DN_WIDTH + (h + 1) * HEAD_DIM]
        v = y[:, 2 * DN_WIDTH + h * HEAD_DIM:2 * DN_WIDTH + (h + 1) * HEAD_DIM]
        q = yq * lax.rsqrt(jnp.sum(yq * yq, axis=-1, keepdims=True) + 1e-6) * HEAD_DIM ** -0.5
        k = yk * lax.rsqrt(jnp.sum(yk * yk, axis=-1, keepdims=True) + 1e-6)
        beta = jax.nn.sigmoid(small[:, 28 + h:29 + h])
        g = -jnp.exp(alog_ref[:, h:h + 1]) * _softplus(small[:, 24 + h:25 + h] + dtb_ref[:, h:h + 1])
        gcb = _dot_hi(trif, jnp.broadcast_to(g, (rows, rows)))
        diff = gcb - gcb.T
        decay = jnp.where(tri, jnp.exp(jnp.where(tri, diff, 0.0)), 0.0)
        kb = k * beta
        a_mat = jnp.where(strict, _dot_nt_hi(kb, k) * decay, 0.0)
        tinv = _unit_lower_inverse(a_mat, seg)
        egc = jnp.exp(gcb[:, :HEAD_DIM])
        sol = _dot_hi(tinv, jnp.concatenate([v * beta, kb * egc], axis=1))
        u = sol[:, :HEAD_DIM]
        w = sol[:, HEAD_DIM:]
        a_qk = jnp.where(tri, _dot_nt_hi(q, k) * decay, 0.0)
        qg = q * egc
        vn, oq, glast = [], [], []
        for s in range(nseg):
            rs = slice(s * seg, (s + 1) * seg)
            st = s_ref[s, h]
            vn.append(u[rs] - _dot_hi(w[rs], st))
            oq.append(_dot_hi(qg[rs], st))
            glast.append(gcb[(s + 1) * seg - 1:(s + 1) * seg, :HEAD_DIM])
        v_new = vn[0] if nseg == 1 else jnp.concatenate(vn, axis=0)
        o = (oq[0] if nseg == 1 else jnp.concatenate(oq, axis=0)) + _dot_hi(a_qk, v_new)
        gl_rows = glast[0] if nseg == 1 else jnp.concatenate(
            [jnp.broadcast_to(gl, (seg, HEAD_DIM)) for gl in glast], axis=0)
        kdt = (k * jnp.exp(gl_rows - gcb[:, :HEAD_DIM])).T
        for s in range(nseg):
            kds = kdt if nseg == 1 else jnp.where((ci // seg) == s, kdt, 0.0)
            s_ref[s, h] = s_ref[s, h] * jnp.exp(glast[s]) + _dot_hi(kds, v_new)
        o = o * lax.rsqrt(jnp.mean(o * o, axis=-1, keepdims=True) + NORM_EPS) * nw_ref[...]
        zh = z[:, sl]
        outs.append(o * (zh * jax.nn.sigmoid(zh)))
    o_ref[...] = jnp.concatenate(outs, axis=1).astype(o_ref.dtype)


def _gdn(dnqkv, hist, small, z, cw, alog, dtb, nw, s0, *, seg, nb, nchunk, row0, hist_map, zero_first_hist):
    nseg = GDN_ROWS // seg
    rmap = lambda b, c: (row0 + b * nchunk + c, 0)
    const = lambda b, c: (0, 0)
    smap = lambda b, c: (b, 0, 0, 0)
    sblock = (nseg, DN_HEADS, HEAD_DIM, HEAD_DIM)
    return pl.pallas_call(
        functools.partial(_gdn_kernel, seg=seg, zero_first_hist=zero_first_hist),
        grid=(nb, nchunk),
        in_specs=[pl.BlockSpec((GDN_ROWS, 768), rmap), pl.BlockSpec((GDN_ROWS, 768), hist_map),
                  pl.BlockSpec((GDN_ROWS, LANE), rmap), pl.BlockSpec((GDN_ROWS, 256), rmap),
                  pl.BlockSpec(cw.shape, const), pl.BlockSpec(alog.shape, const),
                  pl.BlockSpec(dtb.shape, const), pl.BlockSpec(nw.shape, const),
                  pl.BlockSpec(sblock, smap)],
        out_specs=[pl.BlockSpec((GDN_ROWS, 256), lambda b, c: (b * nchunk + c, 0)),
                   pl.BlockSpec(sblock, smap)],
        out_shape=[jax.ShapeDtypeStruct((nb * nchunk * GDN_ROWS, 256), BF16),
                   jax.ShapeDtypeStruct(s0.shape, F32)],
        compiler_params=_cparams(("parallel", "arbitrary")),
        name="gdn",
    )(dnqkv, hist, small, z, cw, alog, dtb, nw, s0)


def _split(a):
    hi = a.astype(BF16)
    return hi, (a - hi.astype(F32)).astype(BF16)


def _dot3s(a, b):
    return _dotf(a[0], b[0]) + (_dotf(a[0], b[1]) + _dotf(a[1], b[0]))


def _dot3(a, b):
    return _dot3s(_split(a), _split(b))


def _dot3_nt(a, b):
    a, b = _split(a), _split(b)
    return _dot_nt(a[0], b[0]) + (_dot_nt(a[0], b[1]) + _dot_nt(a[1], b[0]))


def _unit_lower_inverse3(a, seg):
    n = a.shape[0]
    eye = (lax.broadcasted_iota(jnp.int32, (n, n), 0) == lax.broadcasted_iota(jnp.int32, (n, n), 1)).astype(F32)
    inv = eye - a
    pw = a
    span = 2
    while span < seg:
        pws = _split(pw)
        pw = _dot3s(pws, pws)
        inv = inv + _dot3s(_split(inv), _split(pw))
        span *= 2
    return inv


GDN_PREP_CHUNKS = 4


def _gdn_prep_kernel(x_ref, hist_ref, sm_ref, cw_ref, alog_ref, dtb_ref,
                     u_ref, w_ref, qg_ref, kd_ref, aqk_ref, egl_ref):
    c = pl.program_id(1)
    x = x_ref[...]
    rows = x.shape[0]
    hist = jnp.where(c == 0, 0.0, hist_ref[...])
    row8 = lax.broadcasted_iota(jnp.int32, hist.shape, 0)
    cw = cw_ref[...]
    y = x * cw[DN_CONV_W - 1:DN_CONV_W]
    for sh in range(1, DN_CONV_W):
        xs = pltpu.roll(x, sh, axis=0)
        top = jnp.where(row8 < sh, pltpu.roll(hist, sh, axis=0), xs[0:8])
        xs = jnp.concatenate([top, xs[8:]], axis=0)
        y = y + xs * cw[DN_CONV_W - 1 - sh:DN_CONV_W - sh]
    y = y * jax.nn.sigmoid(y)

    C = GDN_ROWS
    ri = lax.broadcasted_iota(jnp.int32, (C, C), 0)
    ci = lax.broadcasted_iota(jnp.int32, (C, C), 1)
    tri = ci <= ri
    strict = ci < ri
    trif = tri.astype(F32)
    small = sm_ref[...]
    nch = rows // C
    units = [(ch, h) for ch in range(nch) for h in range(DN_HEADS)]
    each = lambda f, *ls: [f(*a) for a in zip(*ls)]

    def qkv_of(ch, h):
        rs = slice(ch * C, (ch + 1) * C)
        yq = y[rs, h * HEAD_DIM:(h + 1) * HEAD_DIM]
        yk = y[rs, DN_WIDTH + h * HEAD_DIM:DN_WIDTH + (h + 1) * HEAD_DIM]
        v = y[rs, 2 * DN_WIDTH + h * HEAD_DIM:2 * DN_WIDTH + (h + 1) * HEAD_DIM]
        q = yq * lax.rsqrt(jnp.sum(yq * yq, axis=-1, keepdims=True) + 1e-6) * HEAD_DIM ** -0.5
        k = yk * lax.rsqrt(jnp.sum(yk * yk, axis=-1, keepdims=True) + 1e-6)
        beta = jax.nn.sigmoid(small[rs, 28 + h:29 + h])
        g = -jnp.exp(alog_ref[:, h:h + 1]) * _softplus(small[rs, 24 + h:25 + h] + dtb_ref[:, h:h + 1])
        return q, k, v, beta, g

    qs_, ks_, vs_, betas, gs = zip(*[qkv_of(ch, h) for ch, h in units])
    trifs = _split(trif)
    gcbs = [_dot3s(trifs, _split(jnp.broadcast_to(g, (C, C)))) for g in gs]
    kbs = each(lambda k, b: k * b, ks_, betas)
    ksp = [_split(k) for k in ks_]
    kbsp = [_split(kb) for kb in kbs]
    qsp = [_split(q) for q in qs_]
    nt3 = lambda a, b: _dot_nt(a[0], b[0]) + (_dot_nt(a[0], b[1]) + _dot_nt(a[1], b[0]))
    kks = each(nt3, kbsp, ksp)
    qks = each(nt3, qsp, ksp)
    decays = [jnp.where(tri, jnp.exp(jnp.where(tri, gcb - gcb.T, 0.0)), 0.0) for gcb in gcbs]
    amats = each(lambda kk, d: jnp.where(strict, kk * d, 0.0), kks, decays)
    eye = (ri == ci).astype(F32)
    invs = [eye - a for a in amats]
    pws = amats
    span = 2
    while span < C:
        pwsp = [_split(p) for p in pws]
        pws = [_dot3s(p, p) for p in pwsp]
        pwsp = [_split(p) for p in pws]
        invs = each(lambda i, p: i + _dot3s(_split(i), p), invs, pwsp)
        span *= 2
    egcs = [jnp.exp(gcb) for gcb in gcbs]
    rhss = each(lambda v, b, kb, e: jnp.concatenate([v * b, kb * e], axis=1), vs_, betas, kbs, egcs)
    sols = each(_dot3, invs, rhss)
    glasts = [gcb[C - 1:C, :] for gcb in gcbs]
    qgs = each(lambda q, e: q * e, qs_, egcs)
    kds = each(lambda k, gl, gcb: k * jnp.exp(gl - gcb), ks_, glasts, gcbs)
    aqks = each(lambda qk, d: jnp.where(tri, qk * d, 0.0), qks, decays)
    for ch in range(nch):
        rs = slice(ch * C, (ch + 1) * C)
        un = range(ch * DN_HEADS, (ch + 1) * DN_HEADS)
        u_ref[rs, :] = jnp.concatenate([sols[i][:, :HEAD_DIM] for i in un], axis=1)
        w_ref[rs, :] = jnp.concatenate([sols[i][:, HEAD_DIM:] for i in un], axis=1)
        qg_ref[rs, :] = jnp.concatenate([qgs[i] for i in un], axis=1)
        kd_ref[rs, :] = jnp.concatenate([kds[i] for i in un], axis=1)
        aqk_ref[rs, :] = jnp.concatenate([aqks[i] for i in un], axis=1)
        egl_ref[ch * 8:(ch + 1) * 8, :] = jnp.concatenate(
            [jnp.broadcast_to(jnp.exp(glasts[i]), (8, HEAD_DIM)) for i in un], axis=1)


def _gdn_scan_kernel(u_ref, w_ref, qg_ref, kd_ref, aqk_ref, egl_ref, z_ref, nw_ref, s0_ref, o_ref, s_ref):
    c = pl.program_id(0)

    @pl.when(c == 0)
    def _():
        s_ref[...] = s0_ref[...]

    nb = u_ref.shape[0]
    units = [(b, slice(h * HEAD_DIM, (h + 1) * HEAD_DIM), h) for b in range(nb) for h in range(DN_HEADS)]
    sts = [s_ref[b, h] for b, _, h in units]
    stsp = [_split(s) for s in sts]
    kdts = [_split(kd_ref[b, :, sl].T) for b, sl, _ in units]
    vnews = [u_ref[b, :, sl] - _dot3s(_split(w_ref[b, :, sl]), st) for (b, sl, _), st in zip(units, stsp)]
    oqs = [_dot3s(_split(qg_ref[b, :, sl]), st) for (b, sl, _), st in zip(units, stsp)]
    vsp = [_split(v) for v in vnews]
    os_ = [oq + _dot3s(_split(aqk_ref[b, :, sl]), vs) for (b, sl, _), oq, vs in zip(units, oqs, vsp)]
    for (b, sl, h), st, kdt, vs in zip(units, sts, kdts, vsp):
        s_ref[b, h] = st * egl_ref[b, 0:1, sl] + _dot3s(kdt, vs)
    outs = []
    for (b, sl, _), o in zip(units, os_):
        o = o * lax.rsqrt(jnp.mean(o * o, axis=-1, keepdims=True) + NORM_EPS) * nw_ref[...]
        zh = z_ref[b, :, sl]
        outs.append(o * (zh * jax.nn.sigmoid(zh)))
    for b in range(nb):
        o_ref[b] = jnp.concatenate(outs[b * DN_HEADS:(b + 1) * DN_HEADS], axis=1).astype(o_ref.dtype)


def _gdn_prompt(dnqkv, small, z, cw, alog, dtb, nw, s0, *, nb, lq):
    cb = GDN_PREP_CHUNKS
    rows = cb * GDN_ROWS
    nstep = lq // rows
    nchunk = lq // GDN_ROWS
    rmap = lambda b, c: (b * nstep + c, 0)
    const = lambda b, c: (0, 0)
    wide = jax.ShapeDtypeStruct((nb * lq, DN_WIDTH), F32)
    u, w, qg, kd, aqk, egl = pl.pallas_call(
        _gdn_prep_kernel,
        grid=(nb, nstep),
        in_specs=[pl.BlockSpec((rows, 768), rmap),
                  pl.BlockSpec((8, 768), lambda b, c: (jnp.maximum((b * nstep + c) * (rows // 8) - 1, 0), 0)),
                  pl.BlockSpec((rows, LANE), rmap),
                  pl.BlockSpec(cw.shape, const), pl.BlockSpec(alog.shape, const), pl.BlockSpec(dtb.shape, const)],
        out_specs=[pl.BlockSpec((rows, DN_WIDTH), rmap)] * 5 + [pl.BlockSpec((cb * 8, DN_WIDTH), rmap)],
        out_shape=[wide] * 5 + [jax.ShapeDtypeStruct((nb * nchunk * 8, DN_WIDTH), F32)],
        compiler_params=_cparams(("parallel", "parallel")),
        name="gdn_prep",
    )(dnqkv, dnqkv, small, cw, alog, dtb)
    r3 = lambda a: a.reshape(nb, -1, DN_WIDTH)
    cmap = lambda c: (0, c, 0)
    full = lambda a: pl.BlockSpec(a.shape, lambda c: (0,) * a.ndim)
    o, s = pl.pallas_call(
        _gdn_scan_kernel,
        grid=(nchunk,),
        in_specs=[pl.BlockSpec((nb, GDN_ROWS, DN_WIDTH), cmap)] * 5
        + [pl.BlockSpec((nb, 8, DN_WIDTH), cmap), pl.BlockSpec((nb, GDN_ROWS, DN_WIDTH), cmap),
           full(nw), full(s0)],
        out_specs=[pl.BlockSpec((nb, GDN_ROWS, DN_WIDTH), cmap), full(s0)],
        out_shape=[jax.ShapeDtypeStruct((nb, lq, DN_WIDTH), BF16), jax.ShapeDtypeStruct(s0.shape, F32)],
        compiler_params=_cparams(("arbitrary",)),
        name="gdn_scan",
    )(r3(u), r3(w), r3(qg), r3(kd), r3(aqk), r3(egl), z[:nb * lq].reshape(nb, lq, DN_WIDTH), nw, s0)
    return o.reshape(nb * lq, DN_WIDTH), s


def _s5_discretize(lre_ref, lim_ref, lstep_ref):
    lr = lre_ref[...]
    li = lim_ref[...]
    dt = jnp.exp(lstep_ref[...])
    mag = jnp.exp(lr * dt)
    ar = mag * jnp.cos(li * dt)
    ai = mag * jnp.sin(li * dt)
    den = lr * lr + li * li
    fr = ((ar - 1.0) * lr + ai * li) / den
    fi = (ai * lr - (ar - 1.0) * li) / den
    return ar, ai, fr, fi


def _s5_kernel(*refs, R, S, with_y):
    if with_y:
        (u_ref, h0r_ref, h0i_ref, lre_ref, lim_ref, lstep_ref, bre_ref, bim_ref, cre_ref, cim_ref, d_ref,
         wglu_ref, y_ref, hr_ref, hi_ref, ar_s, ai_s, bbr_s, bbi_s, xr_s, xi_s) = refs
    else:
        (u_ref, h0r_ref, h0i_ref, lre_ref, lim_ref, lstep_ref, bre_ref, bim_ref,
         hr_ref, hi_ref, ar_s, ai_s, bbr_s, bbi_s, xr_s, xi_s) = refs
    i = pl.program_id(0)

    @pl.when(i == 0)
    def _():
        ar, ai, fr, fi = _s5_discretize(lre_ref, lim_ref, lstep_ref)
        ar_s[...] = ar
        ai_s[...] = ai
        bbr_s[...] = (fr * bre_ref[...] - fi * bim_ref[...]).astype(BF16)
        bbi_s[...] = (fr * bim_ref[...] + fi * bre_ref[...]).astype(BF16)
        hr_ref[...] = h0r_ref[...]
        hi_ref[...] = h0i_ref[...]

    u = u_ref[...]
    ub = u.astype(BF16)
    xr_s[...] = _dotf(ub, bbr_s[...])
    xi_s[...] = _dotf(ub, bbi_s[...])
    ar = jnp.broadcast_to(ar_s[...], (R, S5_LANES))
    ai = jnp.broadcast_to(ai_s[...], (R, S5_LANES))

    def advance(hr, hi, off):
        nr = ar * hr - ai * hi + xr_s[pl.ds(off, R), :]
        ni = ar * hi + ai * hr + xi_s[pl.ds(off, R), :]
        xr_s[pl.ds(off, R), :] = nr
        xi_s[pl.ds(off, R), :] = ni

    advance(hr_ref[...], hi_ref[...], 0)

    def step(s, carry):
        prev = pl.multiple_of((s - 1) * R, R)
        advance(xr_s[pl.ds(prev, R), :], xi_s[pl.ds(prev, R), :], pl.multiple_of(s * R, R))
        return carry

    lax.fori_loop(1, S, step, 0)
    hr_ref[...] = xr_s[pl.ds((S - 1) * R, R), :]
    hi_ref[...] = xi_s[pl.ds((S - 1) * R, R), :]
    if with_y:
        y = _dotf(xr_s[...].astype(BF16), cre_ref[...]) - _dotf(xi_s[...].astype(BF16), cim_ref[...])
        y = y + d_ref[...] * u
        gl = jax.nn.gelu(y)
        y_ref[...] = (gl * jax.nn.sigmoid(_dotf(gl.astype(BF16), wglu_ref[...]))).astype(y_ref.dtype)


def _s5_scan(u_rows, h0r, h0i, prm, *, R, S, with_y):
    n = u_rows.shape[0]
    rows = R * S
    const = lambda i: (0, 0)
    rmap = lambda i: (i, 0)
    ins = [u_rows, h0r, h0i, prm["lre"], prm["lim"], prm["lstep"], prm["bre"], prm["bim"]]
    if with_y:
        ins += [prm["cre"], prm["cim"], prm["d"], prm["wglu"]]
    in_specs = [pl.BlockSpec((rows, S5_WIDTH), rmap)] + [pl.BlockSpec(a.shape, const) for a in ins[1:]]
    st_spec = pl.BlockSpec((R, S5_LANES), const)
    st_shape = jax.ShapeDtypeStruct((R, S5_LANES), F32)
    out_specs = [st_spec, st_spec]
    out_shape = [st_shape, st_shape]
    if with_y:
        out_specs = [pl.BlockSpec((rows, S5_WIDTH), rmap)] + out_specs
        out_shape = [jax.ShapeDtypeStruct((n, S5_WIDTH), BF16)] + out_shape
    return pl.pallas_call(
        functools.partial(_s5_kernel, R=R, S=S, with_y=with_y),
        grid=(n // rows,),
        in_specs=in_specs,
        out_specs=out_specs,
        out_shape=out_shape,
        scratch_shapes=[pltpu.VMEM((1, S5_LANES), F32), pltpu.VMEM((1, S5_LANES), F32),
                        pltpu.VMEM((S5_WIDTH, S5_LANES), BF16), pltpu.VMEM((S5_WIDTH, S5_LANES), BF16),
                        pltpu.VMEM((rows, S5_LANES), F32), pltpu.VMEM((rows, S5_LANES), F32)],
        compiler_params=_cparams(("arbitrary",)),
        name="s5_scan",
    )(*ins)


def _s5_carry_kernel(er_ref, ei_ref, lre_ref, lim_ref, lstep_ref, ir_ref, ii_ref, fr_ref, fi_ref, *, nseg, nsteps):
    ar, ai, _, _ = _s5_discretize(lre_ref, lim_ref, lstep_ref)
    pr, pi_ = ar, ai
    n = 1
    while n < nsteps:
        pr, pi_ = pr * pr - pi_ * pi_, 2.0 * pr * pi_
        n *= 2
    for b in range(er_ref.shape[0] // nseg):
        cr = jnp.zeros((1, S5_LANES), F32)
        ci = jnp.zeros((1, S5_LANES), F32)
        for s in range(nseg):
            r = b * nseg + s
            ir_ref[r:r + 1, :] = cr
            ii_ref[r:r + 1, :] = ci
            er = er_ref[r:r + 1, :]
            ei = ei_ref[r:r + 1, :]
            cr, ci = pr * cr - pi_ * ci + er, pr * ci + pi_ * cr + ei
        fr_ref[b:b + 1, :] = cr
        fi_ref[b:b + 1, :] = ci


def _s5_carry(er, ei, prm, *, nseg, nsteps):
    assert nsteps & (nsteps - 1) == 0
    nb = er.shape[0] // nseg
    full = lambda a: pl.BlockSpec(a.shape, lambda: (0,) * a.ndim)
    ins = [er, ei, prm["lre"], prm["lim"], prm["lstep"]]
    outs = [jax.ShapeDtypeStruct(er.shape, F32)] * 2 + [jax.ShapeDtypeStruct((nb, S5_LANES), F32)] * 2
    return pl.pallas_call(
        functools.partial(_s5_carry_kernel, nseg=nseg, nsteps=nsteps),
        in_specs=[full(a) for a in ins],
        out_specs=[full(o) for o in outs],
        out_shape=outs,
        name="s5_carry",
    )(*ins)


S5_SEGS = 8


def _s5_prep(lre, lim, lstep, bre, bim, cre, cim, d, wglu):
    eye = jnp.eye(S5_GROUPS, dtype=F32)
    bexp = lambda b: jnp.einsum("gpc,gh->gchp", b, eye).reshape(S5_WIDTH, S5_LANES)
    cexp = lambda c: jnp.einsum("gcp,gh->gphc", c, eye).reshape(S5_LANES, S5_WIDTH).astype(BF16)
    return {"lre": lre.reshape(1, S5_LANES), "lim": lim.reshape(1, S5_LANES),
            "lstep": jnp.repeat(lstep, S5_STATE).reshape(1, S5_LANES),
            "bre": bexp(bre), "bim": bexp(bim), "cre": cexp(cre), "cim": cexp(cim),
            "d": d.reshape(1, S5_WIDTH), "wglu": wglu.astype(BF16)}


def _s5_prompt(u, prm, nb, lq, steps_per_tile=32):
    nsteps = lq // S5_SEGS
    R = nb * S5_SEGS
    u_rows = u.reshape(nb, S5_SEGS, nsteps, S5_WIDTH).transpose(2, 0, 1, 3).reshape(nsteps * R, S5_WIDTH)
    zero = jnp.zeros((R, S5_LANES), F32)
    er, ei = _s5_scan(u_rows, zero, zero, prm, R=R, S=steps_per_tile, with_y=False)
    ir, ii, fr, fi = _s5_carry(er, ei, prm, nseg=S5_SEGS, nsteps=nsteps)
    y_rows, _, _ = _s5_scan(u_rows, ir, ii, prm, R=R, S=steps_per_tile, with_y=True)
    y = y_rows.reshape(nsteps, nb, S5_SEGS, S5_WIDTH).transpose(1, 2, 0, 3).reshape(nb * lq, S5_WIDTH)
    return y, fr, fi


def _s5_sample(u, h0r, h0i, prm, nb, lq, steps_per_tile=4):
    u_rows = u.reshape(nb, lq, S5_WIDTH).transpose(1, 0, 2).reshape(lq * nb, S5_WIDTH)
    y_rows, fr, fi = _s5_scan(u_rows, h0r, h0i, prm, R=nb, S=steps_per_tile, with_y=True)
    y = y_rows.reshape(lq, nb, S5_WIDTH).transpose(1, 0, 2).reshape(nb * lq, S5_WIDTH)
    return y, fr, fi


def _sample_conv_hist(buf):
    nb = buf.shape[0]
    blk = buf.reshape(nb // 8, 8, DN_CONV_W - 1, 768)
    blk = jnp.roll(blk, -1, axis=1)
    blk = jnp.pad(blk, ((0, 0), (0, 0), (8 - (DN_CONV_W - 1), 0), (0, 0)))
    return blk.reshape(nb * 8, 768)


def _rms(x, g):
    return x * lax.rsqrt(jnp.mean(x * x, axis=-1, keepdims=True) + NORM_EPS) * g


def _mlp_kernel(x_ref, on_ref, od_ref, os_ref, wn_ref, wd_ref, ws_ref, ln2_ref, up_ref, dn_ref, lnf_ref,
                o_ref, h2_s, *, final_norm):
    j = pl.program_id(1)

    @pl.when(j == 0)
    def _():
        x1 = (x_ref[...] + _dotf(on_ref[...], wn_ref[...]) + _dotf(od_ref[...], wd_ref[...])
              + _dotf(os_ref[...], ws_ref[...]))
        o_ref[...] = x1
        h2_s[...] = _rms(x1, ln2_ref[...]).astype(BF16)

    a = jnp.maximum(_dotf(h2_s[...], up_ref[...]), 0.0)
    o_ref[...] += _dotf((a * a).astype(BF16), dn_ref[...])

    if final_norm:
        @pl.when(j == pl.num_programs(1) - 1)
        def _():
            o_ref[...] = _rms(o_ref[...], lnf_ref[...])


def _mlp(x, o_nsa, o_dn, o_s5, wn, wd, ws, ln2, up, dn, lnf, *, final_norm, tm=512, tf=1024):
    n = x.shape[0]
    row = lambda i, j: (i, 0)
    const = lambda i, j: (0, 0)
    return pl.pallas_call(
        functools.partial(_mlp_kernel, final_norm=final_norm),
        grid=(n // tm, D_FF // tf),
        in_specs=[pl.BlockSpec((tm, D_MODEL), row), pl.BlockSpec((tm, 1024), row),
                  pl.BlockSpec((tm, DN_WIDTH), row), pl.BlockSpec((tm, S5_WIDTH), row),
                  pl.BlockSpec(wn.shape, const), pl.BlockSpec(wd.shape, const), pl.BlockSpec(ws.shape, const),
                  pl.BlockSpec((1, D_MODEL), const),
                  pl.BlockSpec((D_MODEL, tf), lambda i, j: (0, j)),
                  pl.BlockSpec((tf, D_MODEL), lambda i, j: (j, 0)),
                  pl.BlockSpec((1, D_MODEL), const)],
        out_specs=pl.BlockSpec((tm, D_MODEL), row),
        out_shape=jax.ShapeDtypeStruct((n, D_MODEL), F32),
        scratch_shapes=[pltpu.VMEM((tm, D_MODEL), BF16)],
        compiler_params=_cparams(("parallel", "arbitrary")),
        name="out_mlp",
    )(x, o_nsa, o_dn, o_s5, wn, wd, ws, ln2, up, dn, lnf)


_NSA_W = NSA_HEADS * HEAD_DIM
_KV_W = 6 * NSA_KV_HEADS * HEAD_DIM
_GATE_W = NSA_HEADS * 3
_OFF_KV = _NSA_W
_OFF_GATE = _OFF_KV + _KV_W
_OFF_DN = _OFF_GATE + _GATE_W
_OFF_A = _OFF_DN + 3 * DN_WIDTH
_OFF_B = _OFF_A + DN_HEADS
_OFF_Z = _OFF_B + DN_HEADS
_OFF_U = _OFF_Z + DN_WIDTH


def _pad_heads(w, axis):
    w = jnp.moveaxis(w, axis, 0).reshape((NSA_HEADS, HEAD_DIM) + w.shape[:axis] + w.shape[axis + 1:])
    out = jnp.zeros((NSA_HEADS, LANE) + w.shape[2:], w.dtype)
    for h in range(NSA_HEADS):
        g = h // NSA_GROUP
        out = out.at[h, g * HEAD_DIM:(g + 1) * HEAD_DIM].set(w[h])
    out = out.reshape((NSA_HEADS * LANE,) + w.shape[2:])
    return jnp.moveaxis(out, 0, axis)


def _layer_weights(w_in, w_out, wck, wcv):
    wq = _pad_heads(w_in[:, :_NSA_W], 1).astype(BF16)
    wkv = w_in[:, _OFF_KV:_OFF_GATE].astype(BF16)
    wdn = w_in[:, _OFF_DN:_OFF_A].astype(BF16)
    wz = w_in[:, _OFF_Z:_OFF_U].astype(BF16)
    wu = w_in[:, _OFF_U:].astype(BF16)
    wsm = jnp.zeros((D_MODEL, LANE), F32)
    wsm = wsm.at[:, :_GATE_W].set(w_in[:, _OFF_GATE:_OFF_DN])
    wsm = wsm.at[:, _GATE_W:_GATE_W + 2 * DN_HEADS].set(w_in[:, _OFF_A:_OFF_Z]).astype(BF16)
    wexp = jnp.concatenate([jnp.repeat(wck.T, HEAD_DIM, axis=1), jnp.repeat(wcv.T, HEAD_DIM, axis=1)], axis=1)
    won = _pad_heads(w_out[:_NSA_W], 0).astype(BF16)
    wod = w_out[_NSA_W:_NSA_W + DN_WIDTH].astype(BF16)
    wos = w_out[_NSA_W + DN_WIDTH:].astype(BF16)
    return wq, wkv, wdn, wz, wu, wsm, wexp, won, wod, wos


def kernel(x_prompt, x_sample, cache_nsa_kv, cache_win_kv, state_dn_conv, state_dn, state_s5_re, state_s5_im,
           page_table, ln1, ln2, ln_f, w_in, w_out, nsa_wck, nsa_wcv, dn_conv_w, dn_a_log, dn_dt_bias,
           dn_norm_w, s5_lambda_re, s5_lambda_im, s5_log_step, s5_b_re, s5_b_im, s5_c_re, s5_c_im, s5_d,
           s5_w_glu, mlp_up, mlp_down):
    nbp, lp, _ = x_prompt.shape
    nbs, ls, _ = x_sample.shape
    depth = w_in.shape[0]
    rows_p = nbp * lp
    rows_s = nbs * ls
    n_phys, page = cache_nsa_kv.shape[1], cache_nsa_kv.shape[2]
    past = page_table.shape[1] * page
    wlen = cache_win_kv.shape[2]
    G, dh = NSA_KV_HEADS, HEAD_DIM
    assert lp % (S5_SEGS * 32) == 0 and nbs % 8 == 0 and ls == 8 and wlen == WINDOW and lp >= WINDOW

    x = jnp.concatenate([x_prompt.reshape(rows_p, D_MODEL), x_sample.reshape(rows_s, D_MODEL)], axis=0)
    cache_t = cache_nsa_kv.transpose(0, 1, 3, 4, 5, 2).reshape(depth, n_phys, 4 * G * dh, page)
    win_t = cache_win_kv.transpose(0, 1, 3, 4, 5, 2).reshape(depth, nbs, 2 * G * dh, wlen)
    nchunk = lp // GDN_ROWS
    outs_p, outs_s = [], []
    for l in range(depth):
        wq, wkv, wdn, wz, wu, wsm, wexp, won, wod, wos = _layer_weights(w_in[l], w_out[l], nsa_wck[l], nsa_wcv[l])
        qh, kv, dn, z, u, small, pool = _proj(x, ln1[l][None], wq, wkv, wdn, wz, wu, wsm, wexp)

        oc, sel, anyblk = _cmp_prompt(qh, pool, nbp, lp, NSA_TQ)
        osel = _sel_prompt(qh, sel, anyblk, kv, nbp, lp, NSA_TQ)
        on_p = _win_prompt(qh, kv, oc, osel, small, nbp, lp)
        wt = _pool_weights_t(nsa_wck[l], nsa_wcv[l], past)
        on_s = _nsa_sample(page_table, qh, kv, small, cache_t, win_t, wt, layer=l, row0=rows_p, lq=ls, past=past)

        gdn_w = (dn_conv_w[l].T, dn_a_log[l][None], dn_dt_bias[l][None], dn_norm_w[l][None])
        od_p, dn_p = _gdn_prompt(dn, small, z, *gdn_w, jnp.zeros((nbp, DN_HEADS, dh, dh), F32), nb=nbp, lq=lp)
        od_s, dn_s = _gdn(dn, _sample_conv_hist(state_dn_conv[l]), small, z, *gdn_w, state_dn[l],
                          seg=ls, nb=rows_s // GDN_ROWS, nchunk=1, row0=rows_p // GDN_ROWS,
                          hist_map=lambda b, c: (b, 0), zero_first_hist=False)

        prm = _s5_prep(s5_lambda_re[l], s5_lambda_im[l], s5_log_step[l], s5_b_re[l], s5_b_im[l],
                       s5_c_re[l], s5_c_im[l], s5_d[l], s5_w_glu[l])
        os_p, s5r_p, s5i_p = _s5_prompt(u[:rows_p], prm, nbp, lp)
        os_s, s5r_s, s5i_s = _s5_sample(u[rows_p:], state_s5_re[l].reshape(nbs, S5_LANES),
                                        state_s5_im[l].reshape(nbs, S5_LANES), prm, nbs, ls)

        x = _mlp(x, jnp.concatenate([on_p, on_s], axis=0), jnp.concatenate([od_p, od_s], axis=0),
                 jnp.concatenate([os_p, os_s], axis=0), won, wod, wos, ln2[l][None],
                 mlp_up[l].astype(BF16), mlp_down[l].astype(BF16), ln_f[None], final_norm=(l == depth - 1))

        kv_p = kv[:rows_p].reshape(nbp, lp, 6, G, dh)
        kv_s = kv[rows_p:].reshape(nbs, ls, 6, G, dh)
        win_s = jnp.concatenate([cache_win_kv[l], kv_s[:, :, 4:]], axis=1)[:, -WINDOW:]
        dn_rows_p = dn[:rows_p].reshape(nbp, lp, 3 * DN_WIDTH)
        dn_rows_s = dn[rows_p:].reshape(nbs, ls, 3 * DN_WIDTH)
        outs_p.append((kv_p[:, :, :4], kv_p[:, -WINDOW:, 4:], dn_rows_p[:, -(DN_CONV_W - 1):], dn_p,
                       s5r_p.reshape(nbp, S5_GROUPS, S5_STATE), s5i_p.reshape(nbp, S5_GROUPS, S5_STATE)))
        outs_s.append((kv_s[:, :, :4], win_s, dn_rows_s[:, -(DN_CONV_W - 1):], dn_s,
                       s5r_s.reshape(nbs, S5_GROUPS, S5_STATE), s5i_s.reshape(nbs, S5_GROUPS, S5_STATE)))

    y_prompt = x[:rows_p].reshape(nbp, lp, D_MODEL)
    y_sample = x[rows_p:].reshape(nbs, ls, D_MODEL)
    res = [y_prompt, y_sample]
    for i in range(6):
        res.append(jnp.stack([o[i] for o in outs_p], axis=0))
        res.append(jnp.stack([o[i] for o in outs_s], axis=0))
    return tuple(res)
```

```python
import functools
import math

import jax
import jax.numpy as jnp
import numpy as np
from jax import lax
from jax.experimental import pallas as pl
from jax.experimental.pallas import tpu as pltpu

F32 = jnp.float32
BF16 = jnp.bfloat16
HIGHEST = lax.Precision.HIGHEST

D_MODEL = 1024
HEAD_DIM = 64
NSA_HEADS = 8
NSA_KV_HEADS = 2
NSA_GROUP = 4
NSA_BLOCK = 64
NSA_TOPK = 16
WINDOW = 512
DN_HEADS = 4
DN_WIDTH = 256
DN_CONV_W = 4
S5_GROUPS = 16
S5_GROUP_CH = 16
S5_STATE = 64
S5_WIDTH = 256
S5_LANES = S5_GROUPS * S5_STATE
D_FF = 4096
NORM_EPS = 1e-6
BIG = 1e9
NEG = -1e30
LANE = 128
NSA_TQ = 256
VMEM_LIMIT = 48 * 1024 * 1024

_NT = (((1,), (1,)), ((), ()))


def _slope(h):
    return 2.0 ** (-(h + 1))


def _cparams(sem):
    return pltpu.CompilerParams(dimension_semantics=sem, vmem_limit_bytes=VMEM_LIMIT)


def _dotf(a, b):
    return jnp.dot(a, b, preferred_element_type=F32)


def _dot_nt(a, b):
    return lax.dot_general(a, b, _NT, preferred_element_type=F32)


def _dot_hi(a, b):
    return jnp.dot(a, b, preferred_element_type=F32, precision=HIGHEST)


def _dot_nt_hi(a, b):
    return lax.dot_general(a, b, _NT, preferred_element_type=F32, precision=HIGHEST)


def _proj_kernel(x_ref, g_ref, wq_ref, wkv_ref, wdn_ref, wz_ref, wu_ref, ws_ref, wexp_ref,
                 q_ref, kv_ref, dn_ref, z_ref, u_ref, s_ref, pool_ref, kvt_ref):
    x = x_ref[...]
    h = x * lax.rsqrt(jnp.mean(x * x, axis=-1, keepdims=True) + NORM_EPS) * g_ref[...]
    hb = h.astype(BF16)
    q_ref[...] = _dotf(hb, wq_ref[...])
    kv = _dotf(hb, wkv_ref[...])
    kv_ref[...] = kv
    kvt_ref[...] = kv.T
    dn_ref[...] = _dotf(hb, wdn_ref[...])
    z_ref[...] = _dotf(hb, wz_ref[...])
    u_ref[...] = _dotf(hb, wu_ref[...])
    s_ref[...] = _dotf(hb, ws_ref[...])
    tm = x.shape[0]
    kc = kv[:, :2 * LANE].reshape(tm // NSA_BLOCK, NSA_BLOCK, 2 * LANE) * wexp_ref[...][None]
    pool_ref[...] = jnp.sum(kc, axis=1)


def _proj(x, ln, wq, wkv, wdn, wz, wu, ws, wexp, nseq, tm=512):
    n = x.shape[0]
    lt = n // nseq // tm
    const = lambda i: (0, 0)
    row = lambda i: (i, 0)
    outs = [(n, 1024), (n, 768), (n, 768), (n, 256), (n, 256), (n, LANE), (n // NSA_BLOCK, 2 * LANE),
            (nseq, 768, n // nseq)]
    return pl.pallas_call(
        _proj_kernel,
        grid=(n // tm,),
        in_specs=[pl.BlockSpec((tm, D_MODEL), row), pl.BlockSpec((1, D_MODEL), const)]
        + [pl.BlockSpec(w.shape, const) for w in (wq, wkv, wdn, wz, wu, ws, wexp)],
        out_specs=[pl.BlockSpec((tm, s[1]), row) for s in outs[:-2]]
        + [pl.BlockSpec((tm // NSA_BLOCK, 2 * LANE), row),
           pl.BlockSpec((None, 768, tm), lambda i: (i // lt, 0, i % lt))],
        out_shape=[jax.ShapeDtypeStruct(s, F32) for s in outs],
        compiler_params=_cparams(("parallel",)),
        name="proj",
    )(x, ln, wq, wkv, wdn, wz, wu, ws, wexp)


def _topk_blocks(score, blk, nblk):
    sel = jnp.zeros(score.shape, dtype=jnp.bool_)
    for _ in range(NSA_TOPK):
        m = jnp.max(score, axis=0, keepdims=True)
        idx = jnp.min(jnp.where(score == m, blk, nblk), axis=0, keepdims=True)
        pick = blk == idx
        sel = sel | pick
        score = jnp.where(pick, -jnp.inf, score)
    return sel


def _cmp_heads(q_tile, kc, vct, t, blk, nblk_valid, write_oc):
    nblk = blk.shape[0]
    ok = ((blk + 1) * NSA_BLOCK - 1 <= t) & (blk < nblk_valid)
    okf = ok.astype(F32)
    dist = t.astype(F32) - (blk.astype(F32) * NSA_BLOCK + (NSA_BLOCK - 1) / 2.0)
    cur = t // NSA_BLOCK
    forced = (blk == 0) | (blk == cur) | (blk == cur - 1)
    sels = []
    for g in range(NSA_KV_HEADS):
        imp = jnp.zeros(blk.shape, F32)
        for r in range(NSA_GROUP):
            h = g * NSA_GROUP + r
            q = (q_tile(h) * HEAD_DIM ** -0.5).astype(BF16)
            s = _dot_nt(kc, q)
            s = jnp.where(ok, s - _slope(h) * dist, NEG)
            m = jnp.max(s, axis=0, keepdims=True)
            p = jnp.exp(s - m) * okf
            p = p / jnp.maximum(jnp.sum(p, axis=0, keepdims=True), 1e-30)
            imp = imp + p
            write_oc(h, _dotf(vct, p.astype(BF16)))
        score = jnp.where(blk > cur, -BIG, jnp.where(forced, BIG, imp))
        sel = _topk_blocks(score, blk, nblk) & (blk <= cur)
        sels.append(sel)
    return sels


def _cmp_kernel(q_ref, kcv_ref, oc_ref, sel_ref, any_ref, *, tq):
    qt = pl.program_id(1)
    kcv = kcv_ref[...]
    nblk = kcv.shape[0]
    kc = kcv[:, :LANE].astype(BF16)
    vct = kcv[:, LANE:].T.astype(BF16)
    blk = lax.broadcasted_iota(jnp.int32, (nblk, tq), 0)
    t = qt * tq + lax.broadcasted_iota(jnp.int32, (nblk, tq), 1)

    def write_oc(h, oct):
        oc_ref[:, h * LANE:(h + 1) * LANE] = oct.T

    sels = _cmp_heads(lambda h: q_ref[:, h * LANE:(h + 1) * LANE], kc, vct, t, blk, nblk, write_oc)
    for g in range(NSA_KV_HEADS):
        selt = sels[g].astype(F32).T
        sel_ref[:, g * LANE:(g + 1) * LANE] = selt
        any_ref[:, g * LANE:(g + 1) * LANE] = jnp.broadcast_to(jnp.max(selt, axis=0, keepdims=True), (8, LANE))


def _cmp_prompt(qh, pool, nb, lq, tq):
    nq = lq // tq
    nblk = lq // NSA_BLOCK
    assert nblk == LANE
    return pl.pallas_call(
        functools.partial(_cmp_kernel, tq=tq),
        grid=(nb, nq),
        in_specs=[pl.BlockSpec((tq, 1024), lambda b, i: (b * nq + i, 0)),
                  pl.BlockSpec((nblk, 2 * LANE), lambda b, i: (b, 0))],
        out_specs=[pl.BlockSpec((tq, 1024), lambda b, i: (b * nq + i, 0)),
                   pl.BlockSpec((tq, 2 * LANE), lambda b, i: (b * nq + i, 0)),
                   pl.BlockSpec((8, 2 * LANE), lambda b, i: (b * nq + i, 0))],
        out_shape=[jax.ShapeDtypeStruct((nb * lq, 1024), F32),
                   jax.ShapeDtypeStruct((nb * lq, 2 * LANE), F32),
                   jax.ShapeDtypeStruct((nb * nq * 8, 2 * LANE), F32)],
        compiler_params=_cparams(("parallel", "parallel")),
        name="nsa_cmp",
    )(qh, pool)


def _sel_kernel(act_ref, q_ref, sel_ref, k_ref, vt_ref, o_ref, lhs_s, m_s, l_s, acc_s, *, tq, nq):
    b = pl.program_id(0)
    g = pl.program_id(1)
    qt = pl.program_id(2)
    lanei = lax.broadcasted_iota(jnp.int32, (tq, LANE), 1)
    nio = lanei.astype(F32)
    selg = sel_ref[...] > 0.5
    for r in range(NSA_GROUP):
        slope = jnp.where(g == 0, _slope(r), _slope(NSA_GROUP + r))
        qt_r = q_ref[:, r * LANE:(r + 1) * LANE]
        qt_r = jnp.where(g == 0, qt_r, pltpu.roll(qt_r, HEAD_DIM, axis=1))
        qpart = jnp.where(lanei == HEAD_DIM, slope, qt_r * HEAD_DIM ** -0.5)
        bias = jnp.where(selg, (NSA_BLOCK * slope) * nio, NEG)
        lhs_s[r] = jnp.concatenate([bias.astype(BF16), qpart.astype(BF16)], axis=1)
    m_s[...] = jnp.full(m_s.shape, NEG, F32)
    l_s[...] = jnp.zeros(l_s.shape, F32)
    acc_s[...] = jnp.zeros(acc_s.shape, F32)

    def tile(kt, mask):
        kaug = k_ref[kt]
        vt = vt_ref[kt]
        heads = range(NSA_GROUP)
        ss = [_dot_nt(kaug, lhs_s[r]) for r in heads]
        if mask is not None:
            ss = [jnp.where(mask, s, NEG) for s in ss]
        m_old = [m_s[r] for r in heads]
        m_new = [jnp.maximum(m_old[r], jnp.max(ss[r], axis=0, keepdims=True)) for r in heads]
        ps = [jnp.exp(ss[r] - m_new[r]) for r in heads]
        pv = [_dotf(vt, ps[r].astype(BF16)) for r in heads]
        for r in heads:
            alpha = jnp.exp(m_old[r] - m_new[r])
            l_s[r] = alpha * l_s[r] + jnp.sum(ps[r], axis=0, keepdims=True)
            acc_s[r] = alpha * acc_s[r] + pv[r]
            m_s[r] = m_new[r]

    base = ((b * NSA_KV_HEADS + g) * nq + qt) * nq

    def body(kt, carry):
        @pl.when(act_ref[base + kt] != 0)
        def _():
            tile(kt, None)
        return carry

    lax.fori_loop(0, qt, body, 0)
    krow = lax.broadcasted_iota(jnp.int32, (tq, tq), 0)
    qcol = lax.broadcasted_iota(jnp.int32, (tq, tq), 1)
    tile(qt, krow <= qcol)
    zeros = jnp.zeros((HEAD_DIM, tq), F32)
    for r in range(NSA_GROUP):
        o = jnp.concatenate([acc_s[r] / l_s[r], zeros], axis=0).T
        o_ref[:, r * LANE:(r + 1) * LANE] = jnp.where(g == 0, o, pltpu.roll(o, HEAD_DIM, axis=1))


def _sel_prompt(qh, sel, anyblk, kv, nb, lq, tq):
    nq = lq // tq
    bpt = tq // NSA_BLOCK
    G = NSA_KV_HEADS
    act = anyblk[::8].reshape(nb, nq, G, nq, bpt).max(axis=-1) > 0.5
    act = act.transpose(0, 2, 1, 3).reshape(-1).astype(jnp.int32)
    pos = jnp.arange(lq)
    onehot = (pos[:, None] // NSA_BLOCK == jnp.arange(LANE)[None, :]).astype(BF16)
    jcol = jnp.zeros((lq, HEAD_DIM), BF16).at[:, 0].set((pos % NSA_BLOCK).astype(BF16))
    ksel = kv[:nb * lq, 2 * LANE:3 * LANE].reshape(nb, lq, G, HEAD_DIM).transpose(0, 2, 1, 3).astype(BF16)
    kaug = jnp.concatenate([jnp.broadcast_to(onehot, (nb, G, lq, LANE)), ksel,
                            jnp.broadcast_to(jcol, (nb, G, lq, HEAD_DIM))], axis=-1)
    kaug = kaug.reshape(nb, G, nq, tq, 2 * LANE)
    vt = kv[:nb * lq, 3 * LANE:4 * LANE].reshape(nb, nq, tq, G, HEAD_DIM).transpose(0, 3, 1, 4, 2).astype(BF16)
    qmap = lambda b, g, i, a: (b * nq + i, g)
    grid_spec = pltpu.PrefetchScalarGridSpec(
        num_scalar_prefetch=1,
        grid=(nb, G, nq),
        in_specs=[pl.BlockSpec((tq, NSA_GROUP * LANE), qmap),
                  pl.BlockSpec((tq, LANE), qmap),
                  pl.BlockSpec((None, None, nq, tq, 2 * LANE), lambda b, g, i, a: (b, g, 0, 0, 0)),
                  pl.BlockSpec((None, None, nq, HEAD_DIM, tq), lambda b, g, i, a: (b, g, 0, 0, 0))],
        out_specs=pl.BlockSpec((tq, NSA_GROUP * LANE), qmap),
        scratch_shapes=[pltpu.VMEM((NSA_GROUP, tq, 2 * LANE), BF16),
                        pltpu.VMEM((NSA_GROUP, 1, tq), F32),
                        pltpu.VMEM((NSA_GROUP, 1, tq), F32),
                        pltpu.VMEM((NSA_GROUP, HEAD_DIM, tq), F32)])
    return pl.pallas_call(
        functools.partial(_sel_kernel, tq=tq, nq=nq),
        grid_spec=grid_spec,
        out_shape=jax.ShapeDtypeStruct((nb * lq, 1024), F32),
        compiler_params=_cparams(("parallel", "parallel", "arbitrary")),
        name="nsa_sel",
    )(act, qh, sel, kaug, vt)


def _gate_mix(small, oc, os_, ow, h):
    gt = jax.nn.sigmoid(small[:, 3 * h:3 * h + 3])
    return gt[:, 0:1] * oc + gt[:, 1:2] * os_ + gt[:, 2:3] * ow


def _win_kernel(q_ref, k0_ref, k1_ref, k2_ref, v0_ref, v1_ref, v2_ref, oc_ref, os_ref, sm_ref, o_ref, *, tq):
    qt = pl.program_id(1)
    kcat = jnp.concatenate([k0_ref[...], k1_ref[...], k2_ref[...]], axis=0).astype(BF16)
    vcat = jnp.concatenate([v0_ref[...], v1_ref[...], v2_ref[...]], axis=0).astype(BF16)
    row = lax.broadcasted_iota(jnp.int32, (tq, 3 * tq), 0)
    col = lax.broadcasted_iota(jnp.int32, (tq, 3 * tq), 1)
    dist = row - col + 2 * tq
    valid = (dist >= 0) & (dist < WINDOW) & ((qt - 2) * tq + col >= 0)
    distf = dist.astype(F32)
    small = sm_ref[...]
    for h in range(NSA_HEADS):
        sl = slice(h * LANE, (h + 1) * LANE)
        q = (q_ref[:, sl] * HEAD_DIM ** -0.5).astype(BF16)
        s = _dot_nt(q, kcat)
        s = jnp.where(valid, s - _slope(h) * distf, NEG)
        m = jnp.max(s, axis=1, keepdims=True)
        p = jnp.exp(s - m)
        ow = _dotf(p.astype(BF16), vcat) / jnp.sum(p, axis=1, keepdims=True)
        o_ref[:, sl] = _gate_mix(small, oc_ref[:, sl], os_ref[:, sl], ow, h).astype(o_ref.dtype)


def _win_prompt(qh, kv, oc, osel, small, nb, lq, tq=256):
    assert 2 * tq >= WINDOW - 1
    nq = lq // tq
    row = lambda b, i: (b * nq + i, 0)

    def kmap(back, c):
        return lambda b, i: (b * nq + jnp.maximum(i - back, 0), c)

    return pl.pallas_call(
        functools.partial(_win_kernel, tq=tq),
        grid=(nb, nq),
        in_specs=[pl.BlockSpec((tq, 1024), row)]
        + [pl.BlockSpec((tq, LANE), kmap(back, 4)) for back in (2, 1, 0)]
        + [pl.BlockSpec((tq, LANE), kmap(back, 5)) for back in (2, 1, 0)]
        + [pl.BlockSpec((tq, 1024), row), pl.BlockSpec((tq, 1024), row), pl.BlockSpec((tq, LANE), row)],
        out_specs=pl.BlockSpec((tq, 1024), row),
        out_shape=jax.ShapeDtypeStruct((nb * lq, 1024), BF16),
        compiler_params=_cparams(("parallel", "parallel")),
        name="nsa_win",
    )(qh, kv, kv, kv, kv, kv, kv, oc, osel, small)


def _per_head(idx_h, fn):
    out = jnp.zeros(idx_h.shape, F32)
    for h in range(NSA_HEADS):
        out = jnp.where(idx_h == h, fn(h), out)
    return out


def _pad_rows(a, rows):
    return jnp.concatenate([a, jnp.zeros((rows - a.shape[0], a.shape[1]), a.dtype)], axis=0)


def _nsa_sample_kernel(pt_ref, q_ref, kvn_ref, sm_ref, *rest, past, lq, npages, page):
    pages = rest[:npages]
    win_ref, wt_ref, o_ref, nwin_ref, ktc_s, vtc_s, rt_s, vt_s = rest[npages:]
    b = pl.program_id(0)
    nkeys = npages * page
    nrow = NSA_HEADS * lq
    nb_complete = (past + lq) // NSA_BLOCK
    new_blk = past // NSA_BLOCK
    assert past % NSA_BLOCK + lq <= NSA_BLOCK and nkeys == past and new_blk < LANE

    @pl.when(b == 0)
    def _():
        n_i = lax.broadcasted_iota(jnp.int32, (LANE, nkeys), 0)
        pos_i = lax.broadcasted_iota(jnp.int32, (LANE, nkeys), 1)
        rt_s[0:LANE, :] = (n_i == pos_i // NSA_BLOCK).astype(BF16)

    for i in range(npages):
        sl = slice(i * page, (i + 1) * page)
        pg = pages[i]
        ktc_s[:, sl] = pg[0:LANE, :].astype(BF16)
        vtc_s[:, sl] = pg[LANE:2 * LANE, :].astype(BF16)
        rt_s[LANE:2 * LANE, sl] = pg[2 * LANE:3 * LANE, :].astype(BF16)
        vt_s[:, sl] = pg[3 * LANE:4 * LANE, :].astype(BF16)

    q8 = q_ref[...]
    qf = jnp.concatenate([q8[:, h * LANE:(h + 1) * LANE] for h in range(NSA_HEADS)], axis=0) * HEAD_DIM ** -0.5
    qb = qf.astype(BF16)
    kvn = kvn_ref[...]
    lane_lo = lax.broadcasted_iota(jnp.int32, (LANE, LANE), 1) < HEAD_DIM

    ktc = ktc_s[...]
    vtc = vtc_s[...]
    kc = jnp.where(lane_lo, _dot_nt(wt_ref[0, 0], ktc), _dot_nt(wt_ref[0, 1], ktc)).astype(BF16)
    vc = jnp.where(lane_lo, _dot_nt(wt_ref[1, 0], vtc), _dot_nt(wt_ref[1, 1], vtc))
    vct = vc.T.astype(BF16)
    blk = lax.broadcasted_iota(jnp.int32, (LANE, nrow), 0)
    col = lax.broadcasted_iota(jnp.int32, (LANE, nrow), 1)
    t = past + col % lq
    slope_c = _per_head(col // lq, _slope)
    ok = ((blk + 1) * NSA_BLOCK - 1 <= t) & (blk < nb_complete)
    dist = t.astype(F32) - (blk.astype(F32) * NSA_BLOCK + (NSA_BLOCK - 1) / 2.0)
    s = jnp.where(ok, _dot_nt(kc, qb) - slope_c * dist, NEG)
    m = jnp.max(s, axis=0, keepdims=True)
    p = jnp.exp(s - m) * ok.astype(F32)
    p = p / jnp.maximum(jnp.sum(p, axis=0, keepdims=True), 1e-30)
    o_c = _dotf(vct, p.astype(BF16)).T
    ri = lax.broadcasted_iota(jnp.int32, (nrow, nrow), 0)
    ci = lax.broadcasted_iota(jnp.int32, (nrow, nrow), 1)
    gsum = ((ri // (NSA_GROUP * lq) == ci // (NSA_GROUP * lq)) & (ri % lq == ci % lq)).astype(F32)
    imp = _dot_hi(p, gsum)
    cur = t // NSA_BLOCK
    forced = (blk == 0) | (blk == cur) | (blk == cur - 1)
    score = jnp.where(blk > cur, -BIG, jnp.where(forced, BIG, imp))
    sel = (_topk_blocks(score, blk, LANE) & (blk <= cur)).astype(F32).T

    rowi = lax.broadcasted_iota(jnp.int32, (nrow, LANE), 0)
    lanei = lax.broadcasted_iota(jnp.int32, (nrow, LANE), 1)
    slope_r = _per_head(rowi[:, 0:1] // lq, _slope)
    tok_r = rowi % lq
    bias = jnp.where(sel > 0.5, (NSA_BLOCK * slope_r) * lanei.astype(F32), NEG)
    lhs = jnp.concatenate([bias.astype(BF16), qb], axis=1)
    jrow = (lax.broadcasted_iota(jnp.int32, (1, nkeys), 1) % NSA_BLOCK).astype(F32)
    s_p = _dotf(lhs, rt_s[...]) + slope_r * jrow
    new_ok = (lanei <= tok_r) & (lanei < lq)
    bias_new = jnp.sum(jnp.where(lanei == new_blk, bias, 0.0), axis=1, keepdims=True)
    jnew = (past % NSA_BLOCK + lanei).astype(F32)
    kn = _pad_rows(kvn[:, 2 * LANE:3 * LANE], LANE).astype(BF16)
    s_n = jnp.where(new_ok, _dot_nt(qb, kn) + bias_new + slope_r * jnew, NEG)
    m = jnp.maximum(jnp.max(s_p, axis=1, keepdims=True), jnp.max(s_n, axis=1, keepdims=True))
    p_p = jnp.exp(s_p - m)
    p_n = jnp.exp(s_n - m)
    vn = _pad_rows(kvn[:, 3 * LANE:4 * LANE], LANE).astype(BF16)
    o_s = (_dot_nt(p_p.astype(BF16), vt_s[...]) + _dotf(p_n.astype(BF16), vn)) / (
        jnp.sum(p_p, axis=1, keepdims=True) + jnp.sum(p_n, axis=1, keepdims=True))

    wlen = win_ref.shape[1]
    wi = lax.broadcasted_iota(jnp.int32, (nrow, wlen), 1)
    tok_w = lax.broadcasted_iota(jnp.int32, (nrow, wlen), 0) % lq
    dist_w = wlen + tok_w - wi
    s_w = _dotf(qb, win_ref[0:LANE, :].astype(BF16))
    s_w = jnp.where(dist_w < WINDOW, s_w - slope_r * dist_w.astype(F32), NEG)
    kwn = _pad_rows(kvn[:, 4 * LANE:5 * LANE], LANE).astype(BF16)
    s_wn = jnp.where(new_ok, _dot_nt(qb, kwn) - slope_r * (tok_r - lanei).astype(F32), NEG)
    m = jnp.maximum(jnp.max(s_w, axis=1, keepdims=True), jnp.max(s_wn, axis=1, keepdims=True))
    p_w = jnp.exp(s_w - m)
    p_wn = jnp.exp(s_wn - m)
    vwn = _pad_rows(kvn[:, 5 * LANE:6 * LANE], LANE).astype(BF16)
    o_w = (_dot_nt(p_w.astype(BF16), win_ref[LANE:2 * LANE, :].astype(BF16)) + _dotf(p_wn.astype(BF16), vwn)) / (
        jnp.sum(p_w, axis=1, keepdims=True) + jnp.sum(p_wn, axis=1, keepdims=True))

    small = sm_ref[...]
    gate = [jax.nn.sigmoid(jnp.concatenate([small[:, 3 * h + c:3 * h + c + 1] for h in range(NSA_HEADS)], axis=0))
            for c in range(3)]
    o = gate[0] * o_c + gate[1] * o_s + gate[2] * o_w
    for h in range(NSA_HEADS):
        o_ref[:, h * LANE:(h + 1) * LANE] = o[h * lq:(h + 1) * lq].astype(o_ref.dtype)

    win = win_ref[...]
    new_t = _pad_rows(kvn[:, 4 * LANE:6 * LANE], LANE).T
    new_t = pltpu.roll(new_t, LANE - lq, axis=1)
    tail = jnp.concatenate([jnp.zeros((2 * LANE, wlen - LANE), F32), new_t], axis=1)
    lane_w = lax.broadcasted_iota(jnp.int32, win.shape, 1)
    nwin_ref[...] = jnp.where(lane_w < wlen - lq, pltpu.roll(win, wlen - lq, axis=1), tail)


def _pool_weights_t(wck, wcv, past):
    pos = jnp.arange(past)
    onehot = (jnp.arange(LANE)[:, None] == (pos // NSA_BLOCK)[None, :]).astype(F32)
    w = jnp.stack([wck, wcv])[:, :, pos % NSA_BLOCK]
    return (w[:, :, None, :] * onehot[None, None]).astype(BF16)


def _nsa_sample(page_table, qh, kv, small, cache_t, win_t, wt, *, layer, lq, past):
    nb, npages = page_table.shape
    page = cache_t.shape[-1]
    wlen = win_t.shape[-1]
    tokmap = lambda b, pt: (b, 0)

    def page_spec(i):
        return pl.BlockSpec((None, None, 4 * LANE, page), lambda b, pt: (layer, pt[b, i], 0, 0))

    grid_spec = pltpu.PrefetchScalarGridSpec(
        num_scalar_prefetch=1,
        grid=(nb,),
        in_specs=[pl.BlockSpec((lq, 1024), tokmap), pl.BlockSpec((lq, 768), tokmap),
                  pl.BlockSpec((lq, LANE), tokmap)]
        + [page_spec(i) for i in range(npages)]
        + [pl.BlockSpec((None, None, 2 * LANE, wlen), lambda b, pt: (layer, b, 0, 0)),
           pl.BlockSpec(wt.shape, lambda b, pt: (0, 0, 0, 0))],
        out_specs=[pl.BlockSpec((lq, 1024), tokmap),
                   pl.BlockSpec((None, 2 * LANE, wlen), lambda b, pt: (b, 0, 0))],
        scratch_shapes=[pltpu.VMEM((LANE, past), BF16), pltpu.VMEM((LANE, past), BF16),
                        pltpu.VMEM((2 * LANE, past), BF16), pltpu.VMEM((LANE, past), BF16)])
    return pl.pallas_call(
        functools.partial(_nsa_sample_kernel, past=past, lq=lq, npages=npages, page=page),
        grid_spec=grid_spec,
        out_shape=[jax.ShapeDtypeStruct((nb * lq, 1024), BF16),
                   jax.ShapeDtypeStruct((nb, 2 * LANE, wlen), F32)],
        compiler_params=_cparams(("arbitrary",)),
        name="nsa_sample",
    )(page_table, qh, kv, small, *([cache_t] * npages), win_t, wt)


GDN_ROWS = 64


def _softplus(x):
    return jnp.maximum(x, 0.0) + jnp.log(1.0 + jnp.exp(-jnp.abs(x)))


def _unit_lower_inverse(a, seg):
    n = a.shape[0]
    eye = (lax.broadcasted_iota(jnp.int32, (n, n), 0) == lax.broadcasted_iota(jnp.int32, (n, n), 1)).astype(F32)
    inv = eye - a
    pw = a
    span = 2
    while span < seg:
        pw = _dot_hi(pw, pw)
        inv = inv + _dot_hi(inv, pw)
        span *= 2
    return inv


def _gdn_kernel(x_ref, hist_ref, sm_ref, z_ref, cw_ref, alog_ref, dtb_ref, nw_ref, s0_ref,
                o_ref, s_ref, *, seg, zero_first_hist):
    c = pl.program_id(1)
    rows = GDN_ROWS
    nseg = rows // seg

    @pl.when(c == 0)
    def _():
        s_ref[...] = s0_ref[...]

    x = x_ref[...]
    hist = hist_ref[...]
    if zero_first_hist:
        hist = jnp.where(c == 0, 0.0, hist)
    tpos = lax.broadcasted_iota(jnp.int32, x.shape, 0) % seg
    cw = cw_ref[...]
    y = x * cw[DN_CONV_W - 1:DN_CONV_W]
    for sh in range(1, DN_CONV_W):
        xs = jnp.where(tpos >= sh, pltpu.roll(x, sh, axis=0), pltpu.roll(hist, sh, axis=0))
        y = y + xs * cw[DN_CONV_W - 1 - sh:DN_CONV_W - sh]
    y = y * jax.nn.sigmoid(y)

    ri = lax.broadcasted_iota(jnp.int32, (rows, rows), 0)
    ci = lax.broadcasted_iota(jnp.int32, (rows, rows), 1)
    same = (ri // seg) == (ci // seg)
    tri = same & (ci <= ri)
    strict = same & (ci < ri)
    trif = tri.astype(F32)
    small = sm_ref[...]
    z = z_ref[...]
    outs = []
    for h in range(DN_HEADS):
        sl = slice(h * HEAD_DIM, (h + 1) * HEAD_DIM)
        yq = y[:, sl]
        yk = y[:, DN_WIDTH + h * HEAD_DIM:DN_WIDTH + (h + 1) * HEAD_DIM]
        v = y[:, 2 * DN_WIDTH + h * HEAD_DIM:2 * DN_WIDTH + (h + 1) * HEAD_DIM]
        q = yq * lax.rsqrt(jnp.sum(yq * yq, axis=-1, keepdims=True) + 1e-6) * HEAD_DIM ** -0.5
        k = yk * lax.rsqrt(jnp.sum(yk * yk, axis=-1, keepdims=True) + 1e-6)
        beta = jax.nn.sigmoid(small[:, 28 + h:29 + h])
        g = -jnp.exp(alog_ref[:, h:h + 1]) * _softplus(small[:, 24 + h:25 + h] + dtb_ref[:, h:h + 1])
        gcb = _dot_hi(trif, jnp.broadcast_to(g, (rows, rows)))
        diff = gcb - gcb.T
        decay = jnp.where(tri, jnp.exp(jnp.where(tri, diff, 0.0)), 0.0)
        kb = k * beta
        a_mat = jnp.where(strict, _dot_nt_hi(kb, k) * decay, 0.0)
        tinv = _unit_lower_inverse(a_mat, seg)
        egc = jnp.exp(gcb[:, :HEAD_DIM])
        sol = _dot_hi(tinv, jnp.concatenate([v * beta, kb * egc], axis=1))
        u = sol[:, :HEAD_DIM]
        w = sol[:, HEAD_DIM:]
        a_qk = jnp.where(tri, _dot_nt_hi(q, k) * decay, 0.0)
        qg = q * egc
        vn, oq, glast = [], [], []
        for s in range(nseg):
            rs = slice(s * seg, (s + 1) * seg)
            st = s_ref[s, h]
            vn.append(u[rs] - _dot_hi(w[rs], st))
            oq.append(_dot_hi(qg[rs], st))
            glast.append(gcb[(s + 1) * seg - 1:(s + 1) * seg, :HEAD_DIM])
        v_new = vn[0] if nseg == 1 else jnp.concatenate(vn, axis=0)
        o = (oq[0] if nseg == 1 else jnp.concatenate(oq, axis=0)) + _dot_hi(a_qk, v_new)
        gl_rows = glast[0] if nseg == 1 else jnp.concatenate(
            [jnp.broadcast_to(gl, (seg, HEAD_DIM)) for gl in glast], axis=0)
        kdt = (k * jnp.exp(gl_rows - gcb[:, :HEAD_DIM])).T
        for s in range(nseg):
            kds = kdt if nseg == 1 else jnp.where((ci // seg) == s, kdt, 0.0)
            s_ref[s, h] = s_ref[s, h] * jnp.exp(glast[s]) + _dot_hi(kds, v_new)
        o = o * lax.rsqrt(jnp.mean(o * o, axis=-1, keepdims=True) + NORM_EPS) * nw_ref[...]
        zh = z[:, sl]
        outs.append(o * (zh * jax.nn.sigmoid(zh)))
    o_ref[...] = jnp.concatenate(outs, axis=1).astype(o_ref.dtype)


def _gdn(dnqkv, hist, small, z, cw, alog, dtb, nw, s0, *, seg, nb, nchunk, row0, hist_map, zero_first_hist):
    nseg = GDN_ROWS // seg
    rmap = lambda b, c: (row0 + b * nchunk + c, 0)
    const = lambda b, c: (0, 0)
    smap = lambda b, c: (b, 0, 0, 0)
    sblock = (nseg, DN_HEADS, HEAD_DIM, HEAD_DIM)
    return pl.pallas_call(
        functools.partial(_gdn_kernel, seg=seg, zero_first_hist=zero_first_hist),
        grid=(nb, nchunk),
        in_specs=[pl.BlockSpec((GDN_ROWS, 768), rmap), pl.BlockSpec((GDN_ROWS, 768), hist_map),
                  pl.BlockSpec((GDN_ROWS, LANE), rmap), pl.BlockSpec((GDN_ROWS, 256), rmap),
                  pl.BlockSpec(cw.shape, const), pl.BlockSpec(alog.shape, const),
                  pl.BlockSpec(dtb.shape, const), pl.BlockSpec(nw.shape, const),
                  pl.BlockSpec(sblock, smap)],
        out_specs=[pl.BlockSpec((GDN_ROWS, 256), lambda b, c: (b * nchunk + c, 0)),
                   pl.BlockSpec(sblock, smap)],
        out_shape=[jax.ShapeDtypeStruct((nb * nchunk * GDN_ROWS, 256), BF16),
                   jax.ShapeDtypeStruct(s0.shape, F32)],
        compiler_params=_cparams(("parallel", "arbitrary")),
        name="gdn",
    )(dnqkv, hist, small, z, cw, alog, dtb, nw, s0)


def _split(a):
    hi = a.astype(BF16)
    return hi, (a - hi.astype(F32)).astype(BF16)


def _dot3s(a, b):
    return _dotf(a[0], b[0]) + (_dotf(a[0], b[1]) + _dotf(a[1], b[0]))


def _dot3(a, b):
    return _dot3s(_split(a), _split(b))


def _dot3_nt(a, b):
    a, b = _split(a), _split(b)
    return _dot_nt(a[0], b[0]) + (_dot_nt(a[0], b[1]) + _dot_nt(a[1], b[0]))


def _unit_lower_inverse3(a, seg):
    n = a.shape[0]
    eye = (lax.broadcasted_iota(jnp.int32, (n, n), 0) == lax.broadcasted_iota(jnp.int32, (n, n), 1)).astype(F32)
    inv = eye - a
    pw = a
    span = 2
    while span < seg:
        pws = _split(pw)
        pw = _dot3s(pws, pws)
        inv = inv + _dot3s(_split(inv), _split(pw))
        span *= 2
    return inv


GDN_PREP_CHUNKS = 4


def _gdn_prep_kernel(x_ref, hist_ref, sm_ref, cw_ref, alog_ref, dtb_ref,
                     u_ref, w_ref, qg_ref, kd_ref, aqk_ref, egl_ref):
    c = pl.program_id(1)
    x = x_ref[...]
    rows = x.shape[0]
    hist = jnp.where(c == 0, 0.0, hist_ref[...])
    row8 = lax.broadcasted_iota(jnp.int32, hist.shape, 0)
    cw = cw_ref[...]
    y = x * cw[DN_CONV_W - 1:DN_CONV_W]
    for sh in range(1, DN_CONV_W):
        xs = pltpu.roll(x, sh, axis=0)
        top = jnp.where(row8 < sh, pltpu.roll(hist, sh, axis=0), xs[0:8])
        xs = jnp.concatenate([top, xs[8:]], axis=0)
        y = y + xs * cw[DN_CONV_W - 1 - sh:DN_CONV_W - sh]
    y = y * jax.nn.sigmoid(y)

    C = GDN_ROWS
    ri = lax.broadcasted_iota(jnp.int32, (C, C), 0)
    ci = lax.broadcasted_iota(jnp.int32, (C, C), 1)
    tri = ci <= ri
    strict = ci < ri
    trif = tri.astype(F32)
    small = sm_ref[...]
    nch = rows // C
    units = [(ch, h) for ch in range(nch) for h in range(DN_HEADS)]
    each = lambda f, *ls: [f(*a) for a in zip(*ls)]

    def qkv_of(ch, h):
        rs = slice(ch * C, (ch + 1) * C)
        yq = y[rs, h * HEAD_DIM:(h + 1) * HEAD_DIM]
        yk = y[rs, DN_WIDTH + h * HEAD_DIM:DN_WIDTH + (h + 1) * HEAD_DIM]
        v = y[rs, 2 * DN_WIDTH + h * HEAD_DIM:2 * DN_WIDTH + (h + 1) * HEAD_DIM]
        q = yq * lax.rsqrt(jnp.sum(yq * yq, axis=-1, keepdims=True) + 1e-6) * HEAD_DIM ** -0.5
        k = yk * lax.rsqrt(jnp.sum(yk * yk, axis=-1, keepdims=True) + 1e-6)
        beta = jax.nn.sigmoid(small[rs, 28 + h:29 + h])
        g = -jnp.exp(alog_ref[:, h:h + 1]) * _softplus(small[rs, 24 + h:25 + h] + dtb_ref[:, h:h + 1])
        return q, k, v, beta, g

    qs_, ks_, vs_, betas, gs = zip(*[qkv_of(ch, h) for ch, h in units])
    trifs = _split(trif)
    gcbs = [_dot3s(trifs, _split(jnp.broadcast_to(g, (C, C)))) for g in gs]
    kbs = each(lambda k, b: k * b, ks_, betas)
    ksp = [_split(k) for k in ks_]
    kbsp = [_split(kb) for kb in kbs]
    qsp = [_split(q) for q in qs_]
    nt3 = lambda a, b: _dot_nt(a[0], b[0]) + (_dot_nt(a[0], b[1]) + _dot_nt(a[1], b[0]))
    kks = each(nt3, kbsp, ksp)
    qks = each(nt3, qsp, ksp)
    decays = [jnp.where(tri, jnp.exp(jnp.where(tri, gcb - gcb.T, 0.0)), 0.0) for gcb in gcbs]
    amats = each(lambda kk, d: jnp.where(strict, kk * d, 0.0), kks, decays)
    eye = (ri == ci).astype(F32)
    invs = [eye - a for a in amats]
    pws = amats
    span = 2
    while span < C:
        pwsp = [_split(p) for p in pws]
        pws = [_dot3s(p, p) for p in pwsp]
        pwsp = [_split(p) for p in pws]
        invs = each(lambda i, p: i + _dot3s(_split(i), p), invs, pwsp)
        span *= 2
    egcs = [jnp.exp(gcb) for gcb in gcbs]
    rhss = each(lambda v, b, kb, e: jnp.concatenate([v * b, kb * e], axis=1), vs_, betas, kbs, egcs)
    sols = each(_dot3, invs, rhss)
    glasts = [gcb[C - 1:C, :] for gcb in gcbs]
    qgs = each(lambda q, e: q * e, qs_, egcs)
    kds = each(lambda k, gl, gcb: k * jnp.exp(gl - gcb), ks_, glasts, gcbs)
    aqks = each(lambda qk, d: jnp.where(tri, qk * d, 0.0), qks, decays)
    for ch in range(nch):
        rs = slice(ch * C, (ch + 1) * C)
        un = range(ch * DN_HEADS, (ch + 1) * DN_HEADS)
        u_ref[rs, :] = jnp.concatenate([sols[i][:, :HEAD_DIM] for i in un], axis=1)
        w_ref[rs, :] = jnp.concatenate([sols[i][:, HEAD_DIM:] for i in un], axis=1)
        qg_ref[rs, :] = jnp.concatenate([qgs[i] for i in un], axis=1)
        kd_ref[rs, :] = jnp.concatenate([kds[i] for i in un], axis=1)
        aqk_ref[rs, :] = jnp.concatenate([aqks[i] for i in un], axis=1)
        egl_ref[ch * 8:(ch + 1) * 8, :] = jnp.concatenate(
            [jnp.broadcast_to(jnp.exp(glasts[i]), (8, HEAD_DIM)) for i in un], axis=1)


def _gdn_scan_kernel(u_ref, w_ref, qg_ref, kd_ref, aqk_ref, egl_ref, z_ref, nw_ref, s0_ref, o_ref, s_ref):
    c = pl.program_id(0)

    @pl.when(c == 0)
    def _():
        s_ref[...] = s0_ref[...]

    nb = u_ref.shape[0]
    units = [(b, slice(h * HEAD_DIM, (h + 1) * HEAD_DIM), h) for b in range(nb) for h in range(DN_HEADS)]
    sts = [s_ref[b, h] for b, _, h in units]
    stsp = [_split(s) for s in sts]
    kdts = [_split(kd_ref[b, :, sl].T) for b, sl, _ in units]
    vnews = [u_ref[b, :, sl] - _dot3s(_split(w_ref[b, :, sl]), st) for (b, sl, _), st in zip(units, stsp)]
    oqs = [_dot3s(_split(qg_ref[b, :, sl]), st) for (b, sl, _), st in zip(units, stsp)]
    vsp = [_split(v) for v in vnews]
    os_ = [oq + _dot3s(_split(aqk_ref[b, :, sl]), vs) for (b, sl, _), oq, vs in zip(units, oqs, vsp)]
    for (b, sl, h), st, kdt, vs in zip(units, sts, kdts, vsp):
        s_ref[b, h] = st * egl_ref[b, 0:1, sl] + _dot3s(kdt, vs)
    outs = []
    for (b, sl, _), o in zip(units, os_):
        o = o * lax.rsqrt(jnp.mean(o * o, axis=-1, keepdims=True) + NORM_EPS) * nw_ref[...]
        zh = z_ref[b, :, sl]
        outs.append(o * (zh * jax.nn.sigmoid(zh)))
    for b in range(nb):
        o_ref[b] = jnp.concatenate(outs[b * DN_HEADS:(b + 1) * DN_HEADS], axis=1).astype(o_ref.dtype)


def _gdn_prompt(dnqkv, small, z, cw, alog, dtb, nw, s0, *, nb, lq):
    cb = GDN_PREP_CHUNKS
    rows = cb * GDN_ROWS
    nstep = lq // rows
    nchunk = lq // GDN_ROWS
    rmap = lambda b, c: (b * nstep + c, 0)
    const = lambda b, c: (0, 0)
    wide = jax.ShapeDtypeStruct((nb * lq, DN_WIDTH), F32)
    u, w, qg, kd, aqk, egl = pl.pallas_call(
        _gdn_prep_kernel,
        grid=(nb, nstep),
        in_specs=[pl.BlockSpec((rows, 768), rmap),
                  pl.BlockSpec((8, 768), lambda b, c: (jnp.maximum((b * nstep + c) * (rows // 8) - 1, 0), 0)),
                  pl.BlockSpec((rows, LANE), rmap),
                  pl.BlockSpec(cw.shape, const), pl.BlockSpec(alog.shape, const), pl.BlockSpec(dtb.shape, const)],
        out_specs=[pl.BlockSpec((rows, DN_WIDTH), rmap)] * 5 + [pl.BlockSpec((cb * 8, DN_WIDTH), rmap)],
        out_shape=[wide] * 5 + [jax.ShapeDtypeStruct((nb * nchunk * 8, DN_WIDTH), F32)],
        compiler_params=_cparams(("parallel", "parallel")),
        name="gdn_prep",
    )(dnqkv, dnqkv, small, cw, alog, dtb)
    r3 = lambda a: a.reshape(nb, -1, DN_WIDTH)
    cmap = lambda c: (0, c, 0)
    full = lambda a: pl.BlockSpec(a.shape, lambda c: (0,) * a.ndim)
    o, s = pl.pallas_call(
        _gdn_scan_kernel,
        grid=(nchunk,),
        in_specs=[pl.BlockSpec((nb, GDN_ROWS, DN_WIDTH), cmap)] * 5
        + [pl.BlockSpec((nb, 8, DN_WIDTH), cmap), pl.BlockSpec((nb, GDN_ROWS, DN_WIDTH), cmap),
           full(nw), full(s0)],
        out_specs=[pl.BlockSpec((nb, GDN_ROWS, DN_WIDTH), cmap), full(s0)],
        out_shape=[jax.ShapeDtypeStruct((nb, lq, DN_WIDTH), BF16), jax.ShapeDtypeStruct(s0.shape, F32)],
        compiler_params=_cparams(("arbitrary",)),
        name="gdn_scan",
    )(r3(u), r3(w), r3(qg), r3(kd), r3(aqk), r3(egl), z[:nb * lq].reshape(nb, lq, DN_WIDTH), nw, s0)
    return o.reshape(nb * lq, DN_WIDTH), s


def _s5_discretize(lre_ref, lim_ref, lstep_ref):
    lr = lre_ref[...]
    li = lim_ref[...]
    dt = jnp.exp(lstep_ref[...])
    mag = jnp.exp(lr * dt)
    ar = mag * jnp.cos(li * dt)
    ai = mag * jnp.sin(li * dt)
    den = lr * lr + li * li
    fr = ((ar - 1.0) * lr + ai * li) / den
    fi = (ai * lr - (ar - 1.0) * li) / den
    return ar, ai, fr, fi


def _s5_kernel(*refs, R, S, with_y):
    if with_y:
        (u_ref, h0r_ref, h0i_ref, lre_ref, lim_ref, lstep_ref, bre_ref, bim_ref, cre_ref, cim_ref, d_ref,
         wglu_ref, y_ref, hr_ref, hi_ref, ar_s, ai_s, bbr_s, bbi_s, xr_s, xi_s) = refs
    else:
        (u_ref, h0r_ref, h0i_ref, lre_ref, lim_ref, lstep_ref, bre_ref, bim_ref,
         hr_ref, hi_ref, ar_s, ai_s, bbr_s, bbi_s, xr_s, xi_s) = refs
    i = pl.program_id(0)

    @pl.when(i == 0)
    def _():
        ar, ai, fr, fi = _s5_discretize(lre_ref, lim_ref, lstep_ref)
        ar_s[...] = ar
        ai_s[...] = ai
        bbr_s[...] = (fr * bre_ref[...] - fi * bim_ref[...]).astype(BF16)
        bbi_s[...] = (fr * bim_ref[...] + fi * bre_ref[...]).astype(BF16)
        hr_ref[...] = h0r_ref[...]
        hi_ref[...] = h0i_ref[...]

    u = u_ref[...]
    ub = u.astype(BF16)
    xr_s[...] = _dotf(ub, bbr_s[...])
    xi_s[...] = _dotf(ub, bbi_s[...])
    ar = jnp.broadcast_to(ar_s[...], (R, S5_LANES))
    ai = jnp.broadcast_to(ai_s[...], (R, S5_LANES))

    def advance(hr, hi, off):
        nr = ar * hr - ai * hi + xr_s[pl.ds(off, R), :]
        ni = ar * hi + ai * hr + xi_s[pl.ds(off, R), :]
        xr_s[pl.ds(off, R), :] = nr
        xi_s[pl.ds(off, R), :] = ni

    advance(hr_ref[...], hi_ref[...], 0)

    def step(s, carry):
        prev = pl.multiple_of((s - 1) * R, R)
        advance(xr_s[pl.ds(prev, R), :], xi_s[pl.ds(prev, R), :], pl.multiple_of(s * R, R))
        return carry

    lax.fori_loop(1, S, step, 0)
    hr_ref[...] = xr_s[pl.ds((S - 1) * R, R), :]
    hi_ref[...] = xi_s[pl.ds((S - 1) * R, R), :]
    if with_y:
        y = _dotf(xr_s[...].astype(BF16), cre_ref[...]) - _dotf(xi_s[...].astype(BF16), cim_ref[...])
        y = y + d_ref[...] * u
        gl = jax.nn.gelu(y)
        y_ref[...] = (gl * jax.nn.sigmoid(_dotf(gl.astype(BF16), wglu_ref[...]))).astype(y_ref.dtype)


def _s5_scan(u_rows, h0r, h0i, prm, *, R, S, with_y):
    n = u_rows.shape[0]
    rows = R * S
    const = lambda i: (0, 0)
    rmap = lambda i: (i, 0)
    ins = [u_rows, h0r, h0i, prm["lre"], prm["lim"], prm["lstep"], prm["bre"], prm["bim"]]
    if with_y:
        ins += [prm["cre"], prm["cim"], prm["d"], prm["wglu"]]
    in_specs = [pl.BlockSpec((rows, S5_WIDTH), rmap)] + [pl.BlockSpec(a.shape, const) for a in ins[1:]]
    st_spec = pl.BlockSpec((R, S5_LANES), const)
    st_shape = jax.ShapeDtypeStruct((R, S5_LANES), F32)
    out_specs = [st_spec, st_spec]
    out_shape = [st_shape, st_shape]
    if with_y:
        out_specs = [pl.BlockSpec((rows, S5_WIDTH), rmap)] + out_specs
        out_shape = [jax.ShapeDtypeStruct((n, S5_WIDTH), BF16)] + out_shape
    return pl.pallas_call(
        functools.partial(_s5_kernel, R=R, S=S, with_y=with_y),
        grid=(n // rows,),
        in_specs=in_specs,
        out_specs=out_specs,
        out_shape=out_shape,
        scratch_shapes=[pltpu.VMEM((1, S5_LANES), F32), pltpu.VMEM((1, S5_LANES), F32),
                        pltpu.VMEM((S5_WIDTH, S5_LANES), BF16), pltpu.VMEM((S5_WIDTH, S5_LANES), BF16),
                        pltpu.VMEM((rows, S5_LANES), F32), pltpu.VMEM((rows, S5_LANES), F32)],
        compiler_params=_cparams(("arbitrary",)),
        name="s5_scan",
    )(*ins)


def _s5_carry_kernel(er_ref, ei_ref, lre_ref, lim_ref, lstep_ref, ir_ref, ii_ref, fr_ref, fi_ref, *, nseg, nsteps):
    ar, ai, _, _ = _s5_discretize(lre_ref, lim_ref, lstep_ref)
    pr, pi_ = ar, ai
    n = 1
    while n < nsteps:
        pr, pi_ = pr * pr - pi_ * pi_, 2.0 * pr * pi_
        n *= 2
    for b in range(er_ref.shape[0] // nseg):
        cr = jnp.zeros((1, S5_LANES), F32)
        ci = jnp.zeros((1, S5_LANES), F32)
        for s in range(nseg):
            r = b * nseg + s
            ir_ref[r:r + 1, :] = cr
            ii_ref[r:r + 1, :] = ci
            er = er_ref[r:r + 1, :]
            ei = ei_ref[r:r + 1, :]
            cr, ci = pr * cr - pi_ * ci + er, pr * ci + pi_ * cr + ei
        fr_ref[b:b + 1, :] = cr
        fi_ref[b:b + 1, :] = ci


def _s5_carry(er, ei, prm, *, nseg, nsteps):
    assert nsteps & (nsteps - 1) == 0
    nb = er.shape[0] // nseg
    full = lambda a: pl.BlockSpec(a.shape, lambda: (0,) * a.ndim)
    ins = [er, ei, prm["lre"], prm["lim"], prm["lstep"]]
    outs = [jax.ShapeDtypeStruct(er.shape, F32)] * 2 + [jax.ShapeDtypeStruct((nb, S5_LANES), F32)] * 2
    return pl.pallas_call(
        functools.partial(_s5_carry_kernel, nseg=nseg, nsteps=nsteps),
        in_specs=[full(a) for a in ins],
        out_specs=[full(o) for o in outs],
        out_shape=outs,
        name="s5_carry",
    )(*ins)


S5_SEGS = 8


def _s5_prep(lre, lim, lstep, bre, bim, cre, cim, d, wglu):
    eye = jnp.eye(S5_GROUPS, dtype=F32)
    bexp = lambda b: jnp.einsum("gpc,gh->gchp", b, eye).reshape(S5_WIDTH, S5_LANES)
    cexp = lambda c: jnp.einsum("gcp,gh->gphc", c, eye).reshape(S5_LANES, S5_WIDTH).astype(BF16)
    return {"lre": lre.reshape(1, S5_LANES), "lim": lim.reshape(1, S5_LANES),
            "lstep": jnp.repeat(lstep, S5_STATE).reshape(1, S5_LANES),
            "bre": bexp(bre), "bim": bexp(bim), "cre": cexp(cre), "cim": cexp(cim),
            "d": d.reshape(1, S5_WIDTH), "wglu": wglu.astype(BF16)}


def _s5_prompt(u, prm, nb, lq, steps_per_tile=32):
    nsteps = lq // S5_SEGS
    R = nb * S5_SEGS
    u_rows = u.reshape(nb, S5_SEGS, nsteps, S5_WIDTH).transpose(2, 0, 1, 3).reshape(nsteps * R, S5_WIDTH)
    zero = jnp.zeros((R, S5_LANES), F32)
    er, ei = _s5_scan(u_rows, zero, zero, prm, R=R, S=steps_per_tile, with_y=False)
    ir, ii, fr, fi = _s5_carry(er, ei, prm, nseg=S5_SEGS, nsteps=nsteps)
    y_rows, _, _ = _s5_scan(u_rows, ir, ii, prm, R=R, S=steps_per_tile, with_y=True)
    y = y_rows.reshape(nsteps, nb, S5_SEGS, S5_WIDTH).transpose(1, 2, 0, 3).reshape(nb * lq, S5_WIDTH)
    return y, fr, fi


def _s5_sample(u, h0r, h0i, prm, nb, lq, steps_per_tile=4):
    u_rows = u.reshape(nb, lq, S5_WIDTH).transpose(1, 0, 2).reshape(lq * nb, S5_WIDTH)
    y_rows, fr, fi = _s5_scan(u_rows, h0r, h0i, prm, R=nb, S=steps_per_tile, with_y=True)
    y = y_rows.reshape(lq, nb, S5_WIDTH).transpose(1, 0, 2).reshape(nb * lq, S5_WIDTH)
    return y, fr, fi


def _sample_conv_hist(buf):
    nb = buf.shape[0]
    blk = buf.reshape(nb // 8, 8, DN_CONV_W - 1, 768)
    blk = jnp.roll(blk, -1, axis=1)
    blk = jnp.pad(blk, ((0, 0), (0, 0), (8 - (DN_CONV_W - 1), 0), (0, 0)))
    return blk.reshape(nb * 8, 768)


def _rms(x, g):
    return x * lax.rsqrt(jnp.mean(x * x, axis=-1, keepdims=True) + NORM_EPS) * g


def _mlp_kernel(x_ref, on_ref, od_ref, os_ref, wn_ref, wd_ref, ws_ref, ln2_ref, up_ref, dn_ref, lnf_ref,
                o_ref, h2_s, *, final_norm):
    j = pl.program_id(1)

    @pl.when(j == 0)
    def _():
        x1 = (x_ref[...] + _dotf(on_ref[...], wn_ref[...]) + _dotf(od_ref[...], wd_ref[...])
              + _dotf(os_ref[...], ws_ref[...]))
        o_ref[...] = x1
        h2_s[...] = _rms(x1, ln2_ref[...]).astype(BF16)

    a = jnp.maximum(_dotf(h2_s[...], up_ref[...]), 0.0)
    o_ref[...] += _dotf((a * a).astype(BF16), dn_ref[...])

    if final_norm:
        @pl.when(j == pl.num_programs(1) - 1)
        def _():
            o_ref[...] = _rms(o_ref[...], lnf_ref[...])


def _mlp(x, o_nsa, o_dn, o_s5, wn, wd, ws, ln2, up, dn, lnf, *, final_norm, tm=512, tf=1024):
    n = x.shape[0]
    row = lambda i, j: (i, 0)
    const = lambda i, j: (0, 0)
    return pl.pallas_call(
        functools.partial(_mlp_kernel, final_norm=final_norm),
        grid=(n // tm, D_FF // tf),
        in_specs=[pl.BlockSpec((tm, D_MODEL), row), pl.BlockSpec((tm, 1024), row),
                  pl.BlockSpec((tm, DN_WIDTH), row), pl.BlockSpec((tm, S5_WIDTH), row),
                  pl.BlockSpec(wn.shape, const), pl.BlockSpec(wd.shape, const), pl.BlockSpec(ws.shape, const),
                  pl.BlockSpec((1, D_MODEL), const),
                  pl.BlockSpec((D_MODEL, tf), lambda i, j: (0, j)),
                  pl.BlockSpec((tf, D_MODEL), lambda i, j: (j, 0)),
                  pl.BlockSpec((1, D_MODEL), const)],
        out_specs=pl.BlockSpec((tm, D_MODEL), row),
        out_shape=jax.ShapeDtypeStruct((n, D_MODEL), F32),
        scratch_shapes=[pltpu.VMEM((tm, D_MODEL), BF16)],
        compiler_params=_cparams(("parallel", "arbitrary")),
        name="out_mlp",
    )(x, o_nsa, o_dn, o_s5, wn, wd, ws, ln2, up, dn, lnf)


_NSA_W = NSA_HEADS * HEAD_DIM
_KV_W = 6 * NSA_KV_HEADS * HEAD_DIM
_GATE_W = NSA_HEADS * 3
_OFF_KV = _NSA_W
_OFF_GATE = _OFF_KV + _KV_W
_OFF_DN = _OFF_GATE + _GATE_W
_OFF_A = _OFF_DN + 3 * DN_WIDTH
_OFF_B = _OFF_A + DN_HEADS
_OFF_Z = _OFF_B + DN_HEADS
_OFF_U = _OFF_Z + DN_WIDTH


def _pad_heads(w, axis):
    w = jnp.moveaxis(w, axis, 0).reshape((NSA_HEADS, HEAD_DIM) + w.shape[:axis] + w.shape[axis + 1:])
    out = jnp.zeros((NSA_HEADS, LANE) + w.shape[2:], w.dtype)
    for h in range(NSA_HEADS):
        g = h // NSA_GROUP
        out = out.at[h, g * HEAD_DIM:(g + 1) * HEAD_DIM].set(w[h])
    out = out.reshape((NSA_HEADS * LANE,) + w.shape[2:])
    return jnp.moveaxis(out, 0, axis)


def _layer_weights(w_in, w_out, wck, wcv):
    wq = _pad_heads(w_in[:, :_NSA_W], 1).astype(BF16)
    wkv = w_in[:, _OFF_KV:_OFF_GATE].astype(BF16)
    wdn = w_in[:, _OFF_DN:_OFF_A].astype(BF16)
    wz = w_in[:, _OFF_Z:_OFF_U].astype(BF16)
    wu = w_in[:, _OFF_U:].astype(BF16)
    wsm = jnp.zeros((D_MODEL, LANE), F32)
    wsm = wsm.at[:, :_GATE_W].set(w_in[:, _OFF_GATE:_OFF_DN])
    wsm = wsm.at[:, _GATE_W:_GATE_W + 2 * DN_HEADS].set(w_in[:, _OFF_A:_OFF_Z]).astype(BF16)
    wexp = jnp.concatenate([jnp.repeat(wck.T, HEAD_DIM, axis=1), jnp.repeat(wcv.T, HEAD_DIM, axis=1)], axis=1)
    won = _pad_heads(w_out[:_NSA_W], 0).astype(BF16)
    wod = w_out[_NSA_W:_NSA_W + DN_WIDTH].astype(BF16)
    wos = w_out[_NSA_W + DN_WIDTH:].astype(BF16)
    return wq, wkv, wdn, wz, wu, wsm, wexp, won, wod, wos


def kernel(x_prompt, x_sample, cache_nsa_kv, cache_win_kv, state_dn_conv, state_dn, state_s5_re, state_s5_im,
           page_table, ln1, ln2, ln_f, w_in, w_out, nsa_wck, nsa_wcv, dn_conv_w, dn_a_log, dn_dt_bias,
           dn_norm_w, s5_lambda_re, s5_lambda_im, s5_log_step, s5_b_re, s5_b_im, s5_c_re, s5_c_im, s5_d,
           s5_w_glu, mlp_up, mlp_down):
    nbp, lp, _ = x_prompt.shape
    nbs, ls, _ = x_sample.shape
    depth = w_in.shape[0]
    rows_p = nbp * lp
    rows_s = nbs * ls
    n_phys, page = cache_nsa_kv.shape[1], cache_nsa_kv.shape[2]
    past = page_table.shape[1] * page
    wlen = cache_win_kv.shape[2]
    G, dh = NSA_KV_HEADS, HEAD_DIM
    assert lp % (S5_SEGS * 32) == 0 and nbs % 8 == 0 and ls == 8 and wlen == WINDOW and lp >= WINDOW

    xp = x_prompt.reshape(rows_p, D_MODEL)
    xs = x_sample.reshape(rows_s, D_MODEL)
    cache_t = cache_nsa_kv.transpose(0, 1, 3, 4, 5, 2).reshape(depth, n_phys, 4 * G * dh, page)
    win_t = cache_win_kv.transpose(0, 1, 3, 4, 5, 2).reshape(depth, nbs, 2 * G * dh, wlen)
    outs_p, outs_s = [], []
    for l in range(depth):
        wq, wkv, wdn, wz, wu, wsm, wexp, won, wod, wos = _layer_weights(w_in[l], w_out[l], nsa_wck[l], nsa_wcv[l])
        pw = (ln1[l][None], wq, wkv, wdn, wz, wu, wsm, wexp)
        qh_p, kv_p, dnx_p, z_p, u_p, small_p, pool_p, kvt_p = _proj(xp, *pw, nseq=nbp)
        qh_s, kv_s, dnx_s, z_s, u_s, small_s, _, _ = _proj(xs, *pw, nseq=1)

        oc, sel, anyblk = _cmp_prompt(qh_p, pool_p, nbp, lp, NSA_TQ)
        osel = _sel_prompt(qh_p, sel, anyblk, kv_p, nbp, lp, NSA_TQ)
        on_p = _win_prompt(qh_p, kv_p, oc, osel, small_p, nbp, lp)
        wt = _pool_weights_t(nsa_wck[l], nsa_wcv[l], past)
        on_s, nwin_t = _nsa_sample(page_table, qh_s, kv_s, small_s, cache_t, win_t, wt, layer=l, lq=ls, past=past)

        gdn_w = (dn_conv_w[l].T, dn_a_log[l][None], dn_dt_bias[l][None], dn_norm_w[l][None])
        od_p, dn_p = _gdn_prompt(dnx_p, small_p, z_p, *gdn_w, jnp.zeros((nbp, DN_HEADS, dh, dh), F32), nb=nbp, lq=lp)
        od_s, dn_s = _gdn(dnx_s, _sample_conv_hist(state_dn_conv[l]), small_s, z_s, *gdn_w, state_dn[l],
                          seg=ls, nb=rows_s // GDN_ROWS, nchunk=1, row0=0,
                          hist_map=lambda b, c: (b, 0), zero_first_hist=False)

        prm = _s5_prep(s5_lambda_re[l], s5_lambda_im[l], s5_log_step[l], s5_b_re[l], s5_b_im[l],
                       s5_c_re[l], s5_c_im[l], s5_d[l], s5_w_glu[l])
        os_p, s5r_p, s5i_p = _s5_prompt(u_p, prm, nbp, lp)
        os_s, s5r_s, s5i_s = _s5_sample(u_s, state_s5_re[l].reshape(nbs, S5_LANES),
                                        state_s5_im[l].reshape(nbs, S5_LANES), prm, nbs, ls)

        mw = (won, wod, wos, ln2[l][None], mlp_up[l].astype(BF16), mlp_down[l].astype(BF16), ln_f[None])
        xp = _mlp(xp, on_p, od_p, os_p, *mw, final_norm=(l == depth - 1))
        xs = _mlp(xs, on_s, od_s, os_s, *mw, final_norm=(l == depth - 1))

        kvt6 = kvt_p.reshape(nbp, 6, G, dh, lp)
        kv_s6 = kv_s.reshape(nbs, ls, 6, G, dh)
        win_s = nwin_t.reshape(nbs, 2, G, dh, wlen).transpose(0, 4, 1, 2, 3)
        outs_p.append((kvt6[:, :4].transpose(0, 4, 1, 2, 3), kvt6[:, 4:, :, :, lp - WINDOW:].transpose(0, 4, 1, 2, 3),
                       dnx_p.reshape(nbp, lp, 3 * DN_WIDTH)[:, -(DN_CONV_W - 1):], dn_p,
                       s5r_p.reshape(nbp, S5_GROUPS, S5_STATE), s5i_p.reshape(nbp, S5_GROUPS, S5_STATE)))
        outs_s.append((kv_s6[:, :, :4], win_s, dnx_s.reshape(nbs, ls, 3 * DN_WIDTH)[:, -(DN_CONV_W - 1):], dn_s,
                       s5r_s.reshape(nbs, S5_GROUPS, S5_STATE), s5i_s.reshape(nbs, S5_GROUPS, S5_STATE)))

    res = [xp.reshape(nbp, lp, D_MODEL), xs.reshape(nbs, ls, D_MODEL)]
    for i in range(6):
        res.append(jnp.stack([o[i] for o in outs_p], axis=0))
        res.append(jnp.stack([o[i] for o in outs_s], axis=0))
    return tuple(res)
```

```python
import functools
import math

import jax
import jax.numpy as jnp
import numpy as np
from jax import lax
from jax.experimental import pallas as pl
from jax.experimental.pallas import tpu as pltpu

F32 = jnp.float32
BF16 = jnp.bfloat16
HIGHEST = lax.Precision.HIGHEST

D_MODEL = 1024
HEAD_DIM = 64
NSA_HEADS = 8
NSA_KV_HEADS = 2
NSA_GROUP = 4
NSA_BLOCK = 64
NSA_TOPK = 16
WINDOW = 512
DN_HEADS = 4
DN_WIDTH = 256
DN_CONV_W = 4
S5_GROUPS = 16
S5_GROUP_CH = 16
S5_STATE = 64
S5_WIDTH = 256
S5_LANES = S5_GROUPS * S5_STATE
D_FF = 4096
NORM_EPS = 1e-6
BIG = 1e9
NEG = -1e30
LANE = 128
NSA_TQ = 256
VMEM_LIMIT = 48 * 1024 * 1024

_NT = (((1,), (1,)), ((), ()))


def _slope(h):
    return 2.0 ** (-(h + 1))


def _cparams(sem):
    return pltpu.CompilerParams(dimension_semantics=sem, vmem_limit_bytes=VMEM_LIMIT)


def _dotf(a, b):
    return jnp.dot(a, b, preferred_element_type=F32)


def _dot_nt(a, b):
    return lax.dot_general(a, b, _NT, preferred_element_type=F32)


def _dot_hi(a, b):
    return jnp.dot(a, b, preferred_element_type=F32, precision=HIGHEST)


def _dot_nt_hi(a, b):
    return lax.dot_general(a, b, _NT, preferred_element_type=F32, precision=HIGHEST)


def _proj_kernel(x_ref, g_ref, wq_ref, wkv_ref, wdn_ref, wz_ref, wu_ref, ws_ref, wexp_ref,
                 q_ref, kv_ref, dn_ref, z_ref, u_ref, s_ref, pool_ref, kvt_ref, *sel_refs, lt, tk):
    x = x_ref[...]
    h = x * lax.rsqrt(jnp.mean(x * x, axis=-1, keepdims=True) + NORM_EPS) * g_ref[...]
    hb = h.astype(BF16)
    q_ref[...] = _dotf(hb, wq_ref[...])
    kv = _dotf(hb, wkv_ref[...])
    kv_ref[...] = kv
    kvt = kv.T
    kvt_ref[...] = kvt
    if sel_refs:
        kaug_ref, vt_ref = sel_refs
        rows = x.shape[0]
        pos = (pl.program_id(0) % lt) * rows + lax.broadcasted_iota(jnp.int32, (rows, LANE), 0)
        lanei = lax.broadcasted_iota(jnp.int32, (rows, LANE), 1)
        onehot = (lanei == pos // NSA_BLOCK).astype(BF16)
        jpart = jnp.where(lanei == HEAD_DIM, (pos % NSA_BLOCK).astype(F32), 0.0)
        ksl = kv[:, 2 * LANE:3 * LANE]
        for g in range(NSA_KV_HEADS):
            kg = ksl if g == 0 else pltpu.roll(ksl, HEAD_DIM, axis=1)
            kaug_ref[g] = jnp.concatenate([onehot, jnp.where(lanei < HEAD_DIM, kg, jpart).astype(BF16)], axis=1)
            vrow = 3 * LANE + g * HEAD_DIM
            for c in range(rows // tk):
                vt_ref[g, c] = kvt[vrow:vrow + HEAD_DIM, c * tk:(c + 1) * tk].astype(BF16)
    dn_ref[...] = _dotf(hb, wdn_ref[...])
    z_ref[...] = _dotf(hb, wz_ref[...])
    u_ref[...] = _dotf(hb, wu_ref[...])
    s_ref[...] = _dotf(hb, ws_ref[...])
    tm = x.shape[0]
    kc = kv[:, :2 * LANE].reshape(tm // NSA_BLOCK, NSA_BLOCK, 2 * LANE) * wexp_ref[...][None]
    pool_ref[...] = jnp.sum(kc, axis=1)


def _proj(x, ln, wq, wkv, wdn, wz, wu, ws, wexp, nseq, sel_tile=None, tm=512):
    n = x.shape[0]
    lseq = n // nseq
    lt = lseq // tm
    const = lambda i: (0, 0)
    row = lambda i: (i, 0)
    outs = [(n, 1024), (n, 768), (n, 768), (n, 256), (n, 256), (n, LANE), (n // NSA_BLOCK, 2 * LANE),
            (nseq, 768, lseq)]
    out_specs = ([pl.BlockSpec((tm, s[1]), row) for s in outs[:-2]]
                 + [pl.BlockSpec((tm // NSA_BLOCK, 2 * LANE), row),
                    pl.BlockSpec((None, 768, tm), lambda i: (i // lt, 0, i % lt))])
    out_shape = [jax.ShapeDtypeStruct(s, F32) for s in outs]
    G = NSA_KV_HEADS
    if sel_tile is not None:
        out_specs += [pl.BlockSpec((None, G, tm, 2 * LANE), lambda i: (i // lt, 0, i % lt, 0)),
                      pl.BlockSpec((None, G, tm // sel_tile, HEAD_DIM, sel_tile),
                                   lambda i: (i // lt, 0, i % lt, 0, 0))]
        out_shape += [jax.ShapeDtypeStruct((nseq, G, lseq, 2 * LANE), BF16),
                      jax.ShapeDtypeStruct((nseq, G, lseq // sel_tile, HEAD_DIM, sel_tile), BF16)]
    return pl.pallas_call(
        functools.partial(_proj_kernel, lt=lt, tk=sel_tile),
        grid=(n // tm,),
        in_specs=[pl.BlockSpec((tm, D_MODEL), row), pl.BlockSpec((1, D_MODEL), const)]
        + [pl.BlockSpec(w.shape, const) for w in (wq, wkv, wdn, wz, wu, ws, wexp)],
        out_specs=out_specs,
        out_shape=out_shape,
        compiler_params=_cparams(("parallel",)),
        name="proj",
    )(x, ln, wq, wkv, wdn, wz, wu, ws, wexp)


def _topk_blocks(score, blk, nblk, between=None):
    sel = jnp.zeros(score.shape, dtype=jnp.bool_)
    for i in range(NSA_TOPK):
        if between is not None:
            between(i)
        m = jnp.max(score, axis=0, keepdims=True)
        idx = jnp.min(jnp.where(score == m, blk, nblk), axis=0, keepdims=True)
        pick = blk == idx
        sel = sel | pick
        score = jnp.where(pick, -jnp.inf, score)
    return sel


def _cmp_heads(q_tile, kc, vct, t, blk, nblk_valid, write_oc):
    nblk = blk.shape[0]
    ok = ((blk + 1) * NSA_BLOCK - 1 <= t) & (blk < nblk_valid)
    okf = ok.astype(F32)
    dist = t.astype(F32) - (blk.astype(F32) * NSA_BLOCK + (NSA_BLOCK - 1) / 2.0)
    cur = t // NSA_BLOCK
    forced = (blk == 0) | (blk == cur) | (blk == cur - 1)
    sels = []
    for g in range(NSA_KV_HEADS):
        imp = jnp.zeros(blk.shape, F32)
        for r in range(NSA_GROUP):
            h = g * NSA_GROUP + r
            q = (q_tile(h) * HEAD_DIM ** -0.5).astype(BF16)
            s = _dot_nt(kc, q)
            s = jnp.where(ok, s - _slope(h) * dist, NEG)
            m = jnp.max(s, axis=0, keepdims=True)
            p = jnp.exp(s - m) * okf
            p = p / jnp.maximum(jnp.sum(p, axis=0, keepdims=True), 1e-30)
            imp = imp + p
            write_oc(h, _dotf(vct, p.astype(BF16)))
        score = jnp.where(blk > cur, -BIG, jnp.where(forced, BIG, imp))
        sel = _topk_blocks(score, blk, nblk) & (blk <= cur)
        sels.append(sel)
    return sels


def _cmp_kernel(q_ref, kcv_ref, oc_ref, sel_ref, any_ref, *, tq):
    qt = pl.program_id(1)
    kcv = kcv_ref[...]
    nblk = kcv.shape[0]
    kc = kcv[:, :LANE].astype(BF16)
    vct = kcv[:, LANE:].T.astype(BF16)
    blk = lax.broadcasted_iota(jnp.int32, (nblk, tq), 0)
    t = qt * tq + lax.broadcasted_iota(jnp.int32, (nblk, tq), 1)

    def write_oc(h, oct):
        oc_ref[:, h * LANE:(h + 1) * LANE] = oct.T

    sels = _cmp_heads(lambda h: q_ref[:, h * LANE:(h + 1) * LANE], kc, vct, t, blk, nblk, write_oc)
    for g in range(NSA_KV_HEADS):
        selt = sels[g].astype(F32).T
        sel_ref[:, g * LANE:(g + 1) * LANE] = selt
        any_ref[:, g * LANE:(g + 1) * LANE] = jnp.broadcast_to(jnp.max(selt, axis=0, keepdims=True), (8, LANE))


def _cmp_prompt(qh, pool, nb, lq, tq):
    nq = lq // tq
    nblk = lq // NSA_BLOCK
    assert nblk == LANE
    return pl.pallas_call(
        functools.partial(_cmp_kernel, tq=tq),
        grid=(nb, nq),
        in_specs=[pl.BlockSpec((tq, 1024), lambda b, i: (b * nq + i, 0)),
                  pl.BlockSpec((nblk, 2 * LANE), lambda b, i: (b, 0))],
        out_specs=[pl.BlockSpec((tq, 1024), lambda b, i: (b * nq + i, 0)),
                   pl.BlockSpec((tq, 2 * LANE), lambda b, i: (b * nq + i, 0)),
                   pl.BlockSpec((8, 2 * LANE), lambda b, i: (b * nq + i, 0))],
        out_shape=[jax.ShapeDtypeStruct((nb * lq, 1024), F32),
                   jax.ShapeDtypeStruct((nb * lq, 2 * LANE), F32),
                   jax.ShapeDtypeStruct((nb * nq * 8, 2 * LANE), F32)],
        compiler_params=_cparams(("parallel", "parallel")),
        name="nsa_cmp",
    )(qh, pool)


def _sel_kernel(cnt_ref, lst_ref, q_ref, sel_ref, k_ref, vt_ref, o_ref, lhs_s, m_s, l_s, acc_s, *, tq, nq):
    b = pl.program_id(0)
    g = pl.program_id(1)
    qt = pl.program_id(2)
    lanei = lax.broadcasted_iota(jnp.int32, (tq, LANE), 1)
    nio = lanei.astype(F32)
    selg = sel_ref[...] > 0.5
    for r in range(NSA_GROUP):
        slope = jnp.where(g == 0, _slope(r), _slope(NSA_GROUP + r))
        qt_r = q_ref[:, r * LANE:(r + 1) * LANE]
        qt_r = jnp.where(g == 0, qt_r, pltpu.roll(qt_r, HEAD_DIM, axis=1))
        qpart = jnp.where(lanei == HEAD_DIM, slope, qt_r * HEAD_DIM ** -0.5)
        bias = jnp.where(selg, (NSA_BLOCK * slope) * nio, NEG)
        lhs_s[r] = jnp.concatenate([bias.astype(BF16), qpart.astype(BF16)], axis=1)
    m_s[...] = jnp.full(m_s.shape, NEG, F32)
    l_s[...] = jnp.zeros(l_s.shape, F32)
    acc_s[...] = jnp.zeros(acc_s.shape, F32)

    def tiles(items):
        heads = range(NSA_GROUP)
        kaugs = [k_ref[kt] for kt, _ in items]
        vts = [vt_ref[kt] for kt, _ in items]
        ss = [[_dot_nt(ka, lhs_s[r]) for r in heads] for ka in kaugs]
        ss = [[s if mask is None else jnp.where(mask, s, NEG) for s in row] for row, (_, mask) in zip(ss, items)]
        m_old = [m_s[r] for r in heads]
        m_new = []
        for r in heads:
            m = m_old[r]
            for row in ss:
                m = jnp.maximum(m, jnp.max(row[r], axis=0, keepdims=True))
            m_new.append(m)
        ps = [[jnp.exp(row[r] - m_new[r]) for r in heads] for row in ss]
        pvs = [[_dotf(vt, row[r].astype(BF16)) for r in heads] for vt, row in zip(vts, ps)]
        for r in heads:
            alpha = jnp.exp(m_old[r] - m_new[r])
            l_s[r] = alpha * l_s[r] + sum(jnp.sum(row[r], axis=0, keepdims=True) for row in ps)
            acc_s[r] = alpha * acc_s[r] + sum(row[r] for row in pvs)
            m_s[r] = m_new[r]

    cbase = (b * NSA_KV_HEADS + g) * nq + qt
    n_act = cnt_ref[cbase]
    lbase = cbase * nq

    def body(i, carry):
        tiles([(lst_ref[lbase + 2 * i], None), (lst_ref[lbase + 2 * i + 1], None)])
        return carry

    lax.fori_loop(0, n_act // 2, body, 0)
    diag = (qt, lax.broadcasted_iota(jnp.int32, (tq, tq), 0) <= lax.broadcasted_iota(jnp.int32, (tq, tq), 1))

    @pl.when(n_act % 2 == 1)
    def _():
        tiles([(lst_ref[lbase + n_act - 1], None), diag])

    @pl.when(n_act % 2 == 0)
    def _():
        tiles([diag])

    zeros = jnp.zeros((HEAD_DIM, tq), F32)
    for r in range(NSA_GROUP):
        o = jnp.concatenate([acc_s[r] / l_s[r], zeros], axis=0).T
        o_ref[:, r * LANE:(r + 1) * LANE] = jnp.where(g == 0, o, pltpu.roll(o, HEAD_DIM, axis=1))


def _sel_prompt(qh, sel, anyblk, kaug, vt, nb, lq, tq):
    assert vt.shape[-1] == tq
    nq = lq // tq
    bpt = tq // NSA_BLOCK
    G = NSA_KV_HEADS
    act = anyblk[::8].reshape(nb, nq, G, nq, bpt).max(axis=-1) > 0.5
    act = act.transpose(0, 2, 1, 3) & (jnp.arange(nq)[None, :] < jnp.arange(nq)[:, None])
    cnt = act.sum(axis=-1).astype(jnp.int32).reshape(-1)
    lst = jnp.argsort(jnp.logical_not(act), axis=-1, stable=True).astype(jnp.int32).reshape(-1)
    kaug = kaug.reshape(nb, G, nq, tq, 2 * LANE)
    qmap = lambda b, g, i, c, l: (b * nq + i, g)
    whole = lambda b, g, i, c, l: (b, g, 0, 0, 0)
    grid_spec = pltpu.PrefetchScalarGridSpec(
        num_scalar_prefetch=2,
        grid=(nb, G, nq),
        in_specs=[pl.BlockSpec((tq, NSA_GROUP * LANE), qmap),
                  pl.BlockSpec((tq, LANE), qmap),
                  pl.BlockSpec((None, None, nq, tq, 2 * LANE), whole),
                  pl.BlockSpec((None, None, nq, HEAD_DIM, tq), whole)],
        out_specs=pl.BlockSpec((tq, NSA_GROUP * LANE), qmap),
        scratch_shapes=[pltpu.VMEM((NSA_GROUP, tq, 2 * LANE), BF16),
                        pltpu.VMEM((NSA_GROUP, 1, tq), F32),
                        pltpu.VMEM((NSA_GROUP, 1, tq), F32),
                        pltpu.VMEM((NSA_GROUP, HEAD_DIM, tq), F32)])
    return pl.pallas_call(
        functools.partial(_sel_kernel, tq=tq, nq=nq),
        grid_spec=grid_spec,
        out_shape=jax.ShapeDtypeStruct((nb * lq, 1024), F32),
        compiler_params=_cparams(("parallel", "parallel", "arbitrary")),
        name="nsa_sel",
    )(cnt, lst, qh, sel, kaug, vt)


def _gate_mix(small, oc, os_, ow, h):
    gt = jax.nn.sigmoid(small[:, 3 * h:3 * h + 3])
    return gt[:, 0:1] * oc + gt[:, 1:2] * os_ + gt[:, 2:3] * ow


def _win_kernel(q_ref, k0_ref, k1_ref, k2_ref, v0_ref, v1_ref, v2_ref, oc_ref, os_ref, sm_ref, o_ref, *, tq):
    qt = pl.program_id(1)
    kcat = jnp.concatenate([k0_ref[...], k1_ref[...], k2_ref[...]], axis=0).astype(BF16)
    vcat = jnp.concatenate([v0_ref[...], v1_ref[...], v2_ref[...]], axis=0).astype(BF16)
    row = lax.broadcasted_iota(jnp.int32, (tq, 3 * tq), 0)
    col = lax.broadcasted_iota(jnp.int32, (tq, 3 * tq), 1)
    dist = row - col + 2 * tq
    valid = (dist >= 0) & (dist < WINDOW) & ((qt - 2) * tq + col >= 0)
    distf = dist.astype(F32)
    small = sm_ref[...]
    for h in range(NSA_HEADS):
        sl = slice(h * LANE, (h + 1) * LANE)
        q = (q_ref[:, sl] * HEAD_DIM ** -0.5).astype(BF16)
        s = _dot_nt(q, kcat)
        s = jnp.where(valid, s - _slope(h) * distf, NEG)
        m = jnp.max(s, axis=1, keepdims=True)
        p = jnp.exp(s - m)
        ow = _dotf(p.astype(BF16), vcat) / jnp.sum(p, axis=1, keepdims=True)
        o_ref[:, sl] = _gate_mix(small, oc_ref[:, sl], os_ref[:, sl], ow, h).astype(o_ref.dtype)


def _win_prompt(qh, kv, oc, osel, small, nb, lq, tq=256):
    assert 2 * tq >= WINDOW - 1
    nq = lq // tq
    row = lambda b, i: (b * nq + i, 0)

    def kmap(back, c):
        return lambda b, i: (b * nq + jnp.maximum(i - back, 0), c)

    return pl.pallas_call(
        functools.partial(_win_kernel, tq=tq),
        grid=(nb, nq),
        in_specs=[pl.BlockSpec((tq, 1024), row)]
        + [pl.BlockSpec((tq, LANE), kmap(back, 4)) for back in (2, 1, 0)]
        + [pl.BlockSpec((tq, LANE), kmap(back, 5)) for back in (2, 1, 0)]
        + [pl.BlockSpec((tq, 1024), row), pl.BlockSpec((tq, 1024), row), pl.BlockSpec((tq, LANE), row)],
        out_specs=pl.BlockSpec((tq, 1024), row),
        out_shape=jax.ShapeDtypeStruct((nb * lq, 1024), BF16),
        compiler_params=_cparams(("parallel", "parallel")),
        name="nsa_win",
    )(qh, kv, kv, kv, kv, kv, kv, oc, osel, small)


def _per_head(idx_h, fn):
    out = jnp.zeros(idx_h.shape, F32)
    for h in range(NSA_HEADS):
        out = jnp.where(idx_h == h, fn(h), out)
    return out


def _pad_rows(a, rows):
    return jnp.concatenate([a, jnp.zeros((rows - a.shape[0], a.shape[1]), a.dtype)], axis=0)


def _nsa_sample_kernel(pt_ref, q_ref, kvn_ref, sm_ref, *rest, past, lq, npages, page):
    pages = rest[:npages]
    win_ref, wt_ref, o_ref, nwin_ref, ktc_s, vtc_s, rt_s, vt_s = rest[npages:]
    b = pl.program_id(0)
    nkeys = npages * page
    nrow = NSA_HEADS * lq
    nb_complete = (past + lq) // NSA_BLOCK
    new_blk = past // NSA_BLOCK
    assert past % NSA_BLOCK + lq <= NSA_BLOCK and nkeys == past and new_blk < LANE

    @pl.when(b == 0)
    def _():
        n_i = lax.broadcasted_iota(jnp.int32, (LANE, nkeys), 0)
        pos_i = lax.broadcasted_iota(jnp.int32, (LANE, nkeys), 1)
        rt_s[0:LANE, :] = (n_i == pos_i // NSA_BLOCK).astype(BF16)

    for i in range(npages):
        sl = slice(i * page, (i + 1) * page)
        ktc_s[:, sl] = pages[i][0:LANE, :].astype(BF16)
        vtc_s[:, sl] = pages[i][LANE:2 * LANE, :].astype(BF16)

    def stage_selected_page(i):
        sl = slice(i * page, (i + 1) * page)
        rt_s[LANE:2 * LANE, sl] = pages[i][2 * LANE:3 * LANE, :].astype(BF16)
        vt_s[:, sl] = pages[i][3 * LANE:4 * LANE, :].astype(BF16)

    q8 = q_ref[...]
    qf = jnp.concatenate([q8[:, h * LANE:(h + 1) * LANE] for h in range(NSA_HEADS)], axis=0) * HEAD_DIM ** -0.5
    qb = qf.astype(BF16)
    kvn = kvn_ref[...]
    lane_lo = lax.broadcasted_iota(jnp.int32, (LANE, LANE), 1) < HEAD_DIM

    ktc = ktc_s[...]
    vtc = vtc_s[...]
    kc = jnp.where(lane_lo, _dot_nt(wt_ref[0, 0], ktc), _dot_nt(wt_ref[0, 1], ktc)).astype(BF16)
    vc = jnp.where(lane_lo, _dot_nt(wt_ref[1, 0], vtc), _dot_nt(wt_ref[1, 1], vtc))
    vct = vc.T.astype(BF16)
    blk = lax.broadcasted_iota(jnp.int32, (LANE, nrow), 0)
    col = lax.broadcasted_iota(jnp.int32, (LANE, nrow), 1)
    t = past + col % lq
    slope_c = _per_head(col // lq, _slope)
    ok = ((blk + 1) * NSA_BLOCK - 1 <= t) & (blk < nb_complete)
    dist = t.astype(F32) - (blk.astype(F32) * NSA_BLOCK + (NSA_BLOCK - 1) / 2.0)
    s = jnp.where(ok, _dot_nt(kc, qb) - slope_c * dist, NEG)
    m = jnp.max(s, axis=0, keepdims=True)
    p = jnp.exp(s - m) * ok.astype(F32)
    p = p / jnp.maximum(jnp.sum(p, axis=0, keepdims=True), 1e-30)
    o_c = _dotf(vct, p.astype(BF16)).T
    ri = lax.broadcasted_iota(jnp.int32, (nrow, nrow), 0)
    ci = lax.broadcasted_iota(jnp.int32, (nrow, nrow), 1)
    gsum = ((ri // (NSA_GROUP * lq) == ci // (NSA_GROUP * lq)) & (ri % lq == ci % lq)).astype(F32)
    imp = _dot_hi(p, gsum)
    cur = t // NSA_BLOCK
    forced = (blk == 0) | (blk == cur) | (blk == cur - 1)
    score = jnp.where(blk > cur, -BIG, jnp.where(forced, BIG, imp))
    rowi = lax.broadcasted_iota(jnp.int32, (nrow, LANE), 0)
    lanei = lax.broadcasted_iota(jnp.int32, (nrow, LANE), 1)
    slope_r = _per_head(rowi[:, 0:1] // lq, _slope)
    tok_r = rowi % lq
    new_ok = (lanei <= tok_r) & (lanei < lq)

    wlen = win_ref.shape[1]
    wi = lax.broadcasted_iota(jnp.int32, (nrow, wlen), 1)
    tok_w = lax.broadcasted_iota(jnp.int32, (nrow, wlen), 0) % lq
    dist_w = wlen + tok_w - wi
    s_w = _dotf(qb, win_ref[0:LANE, :].astype(BF16))
    s_w = jnp.where(dist_w < WINDOW, s_w - slope_r * dist_w.astype(F32), NEG)
    kwn = _pad_rows(kvn[:, 4 * LANE:5 * LANE], LANE).astype(BF16)
    s_wn = jnp.where(new_ok, _dot_nt(qb, kwn) - slope_r * (tok_r - lanei).astype(F32), NEG)
    m = jnp.maximum(jnp.max(s_w, axis=1, keepdims=True), jnp.max(s_wn, axis=1, keepdims=True))
    p_w = jnp.exp(s_w - m)
    p_wn = jnp.exp(s_wn - m)
    vwn = _pad_rows(kvn[:, 5 * LANE:6 * LANE], LANE).astype(BF16)
    o_w = (_dot_nt(p_w.astype(BF16), win_ref[LANE:2 * LANE, :].astype(BF16)) + _dotf(p_wn.astype(BF16), vwn)) / (
        jnp.sum(p_w, axis=1, keepdims=True) + jnp.sum(p_wn, axis=1, keepdims=True))

    win = win_ref[...]
    new_t = _pad_rows(kvn[:, 4 * LANE:6 * LANE], LANE).T
    new_t = pltpu.roll(new_t, LANE - lq, axis=1)
    tail = jnp.concatenate([jnp.zeros((2 * LANE, wlen - LANE), F32), new_t], axis=1)
    lane_w = lax.broadcasted_iota(jnp.int32, win.shape, 1)
    nwin_ref[...] = jnp.where(lane_w < wlen - lq, pltpu.roll(win, wlen - lq, axis=1), tail)

    assert npages <= NSA_TOPK
    sel = _topk_blocks(score, blk, LANE, lambda i: stage_selected_page(i) if i < npages else None)
    sel = (sel & (blk <= cur)).astype(F32).T

    bias = jnp.where(sel > 0.5, (NSA_BLOCK * slope_r) * lanei.astype(F32), NEG)
    lhs = jnp.concatenate([bias.astype(BF16), qb], axis=1)
    jrow = (lax.broadcasted_iota(jnp.int32, (1, nkeys), 1) % NSA_BLOCK).astype(F32)
    s_p = _dotf(lhs, rt_s[...]) + slope_r * jrow
    bias_new = jnp.sum(jnp.where(lanei == new_blk, bias, 0.0), axis=1, keepdims=True)
    jnew = (past % NSA_BLOCK + lanei).astype(F32)
    kn = _pad_rows(kvn[:, 2 * LANE:3 * LANE], LANE).astype(BF16)
    s_n = jnp.where(new_ok, _dot_nt(qb, kn) + bias_new + slope_r * jnew, NEG)
    m = jnp.maximum(jnp.max(s_p, axis=1, keepdims=True), jnp.max(s_n, axis=1, keepdims=True))
    p_p = jnp.exp(s_p - m)
    p_n = jnp.exp(s_n - m)
    vn = _pad_rows(kvn[:, 3 * LANE:4 * LANE], LANE).astype(BF16)
    o_s = (_dot_nt(p_p.astype(BF16), vt_s[...]) + _dotf(p_n.astype(BF16), vn)) / (
        jnp.sum(p_p, axis=1, keepdims=True) + jnp.sum(p_n, axis=1, keepdims=True))

    small = sm_ref[...]
    gate = [jax.nn.sigmoid(jnp.concatenate([small[:, 3 * h + c:3 * h + c + 1] for h in range(NSA_HEADS)], axis=0))
            for c in range(3)]
    o = gate[0] * o_c + gate[1] * o_s + gate[2] * o_w
    for h in range(NSA_HEADS):
        o_ref[:, h * LANE:(h + 1) * LANE] = o[h * lq:(h + 1) * lq].astype(o_ref.dtype)


def _pool_weights_t(wck, wcv, past):
    pos = jnp.arange(past)
    onehot = (jnp.arange(LANE)[:, None] == (pos // NSA_BLOCK)[None, :]).astype(F32)
    w = jnp.stack([wck, wcv])[:, :, pos % NSA_BLOCK]
    return (w[:, :, None, :] * onehot[None, None]).astype(BF16)


def _nsa_sample(page_table, qh, kv, small, cache_t, win_t, wt, *, layer, lq, past):
    nb, npages = page_table.shape
    page = cache_t.shape[-1]
    wlen = win_t.shape[-1]
    tokmap = lambda b, pt: (b, 0)

    def page_spec(i):
        return pl.BlockSpec((None, None, 4 * LANE, page), lambda b, pt: (layer, pt[b, i], 0, 0))

    grid_spec = pltpu.PrefetchScalarGridSpec(
        num_scalar_prefetch=1,
        grid=(nb,),
        in_specs=[pl.BlockSpec((lq, 1024), tokmap), pl.BlockSpec((lq, 768), tokmap),
                  pl.BlockSpec((lq, LANE), tokmap)]
        + [page_spec(i) for i in range(npages)]
        + [pl.BlockSpec((None, None, 2 * LANE, wlen), lambda b, pt: (layer, b, 0, 0)),
           pl.BlockSpec(wt.shape, lambda b, pt: (0, 0, 0, 0))],
        out_specs=[pl.BlockSpec((lq, 1024), tokmap),
                   pl.BlockSpec((None, 2 * LANE, wlen), lambda b, pt: (b, 0, 0))],
        scratch_shapes=[pltpu.VMEM((LANE, past), BF16), pltpu.VMEM((LANE, past), BF16),
                        pltpu.VMEM((2 * LANE, past), BF16), pltpu.VMEM((LANE, past), BF16)])
    return pl.pallas_call(
        functools.partial(_nsa_sample_kernel, past=past, lq=lq, npages=npages, page=page),
        grid_spec=grid_spec,
        out_shape=[jax.ShapeDtypeStruct((nb * lq, 1024), BF16),
                   jax.ShapeDtypeStruct((nb, 2 * LANE, wlen), F32)],
        compiler_params=_cparams(("arbitrary",)),
        name="nsa_sample",
    )(page_table, qh, kv, small, *([cache_t] * npages), win_t, wt)


GDN_ROWS = 64


def _softplus(x):
    return jnp.maximum(x, 0.0) + jnp.log(1.0 + jnp.exp(-jnp.abs(x)))


def _unit_lower_inverse(a, seg):
    n = a.shape[0]
    eye = (lax.broadcasted_iota(jnp.int32, (n, n), 0) == lax.broadcasted_iota(jnp.int32, (n, n), 1)).astype(F32)
    inv = eye - a
    pw = a
    span = 2
    while span < seg:
        pw = _dot_hi(pw, pw)
        inv = inv + _dot_hi(inv, pw)
        span *= 2
    return inv


def _gdn_kernel(x_ref, hist_ref, sm_ref, z_ref, cw_ref, alog_ref, dtb_ref, nw_ref, s0_ref,
                o_ref, s_ref, *, seg, zero_first_hist):
    c = pl.program_id(1)
    rows = GDN_ROWS
    nseg = rows // seg

    @pl.when(c == 0)
    def _():
        s_ref[...] = s0_ref[...]

    x = x_ref[...]
    hist = hist_ref[...]
    if zero_first_hist:
        hist = jnp.where(c == 0, 0.0, hist)
    tpos = lax.broadcasted_iota(jnp.int32, x.shape, 0) % seg
    cw = cw_ref[...]
    y = x * cw[DN_CONV_W - 1:DN_CONV_W]
    for sh in range(1, DN_CONV_W):
        xs = jnp.where(tpos >= sh, pltpu.roll(x, sh, axis=0), pltpu.roll(hist, sh, axis=0))
        y = y + xs * cw[DN_CONV_W - 1 - sh:DN_CONV_W - sh]
    y = y * jax.nn.sigmoid(y)

    ri = lax.broadcasted_iota(jnp.int32, (rows, rows), 0)
    ci = lax.broadcasted_iota(jnp.int32, (rows, rows), 1)
    same = (ri // seg) == (ci // seg)
    tri = same & (ci <= ri)
    strict = same & (ci < ri)
    trif = tri.astype(F32)
    small = sm_ref[...]
    z = z_ref[...]
    heads = range(DN_HEADS)
    segs = range(nseg)
    hs = lambda h: slice(h * HEAD_DIM, (h + 1) * HEAD_DIM)
    yqs = [y[:, hs(h)] for h in heads]
    yks = [y[:, DN_WIDTH + h * HEAD_DIM:DN_WIDTH + (h + 1) * HEAD_DIM] for h in heads]
    vs = [y[:, 2 * DN_WIDTH + h * HEAD_DIM:2 * DN_WIDTH + (h + 1) * HEAD_DIM] for h in heads]
    qs = [a * lax.rsqrt(jnp.sum(a * a, axis=-1, keepdims=True) + 1e-6) * HEAD_DIM ** -0.5 for a in yqs]
    ks = [a * lax.rsqrt(jnp.sum(a * a, axis=-1, keepdims=True) + 1e-6) for a in yks]
    betas = [jax.nn.sigmoid(small[:, 28 + h:29 + h]) for h in heads]
    gs = [-jnp.exp(alog_ref[:, h:h + 1]) * _softplus(small[:, 24 + h:25 + h] + dtb_ref[:, h:h + 1]) for h in heads]
    trifs = _split(trif)
    gcbs = [_dot3s(trifs, _split(jnp.broadcast_to(g, (rows, rows)))) for g in gs]
    decays = [jnp.where(tri, jnp.exp(jnp.where(tri, gcb - gcb.T, 0.0)), 0.0) for gcb in gcbs]
    kbs = [ks[h] * betas[h] for h in heads]
    amats = [jnp.where(strict, _dot3_nt(kbs[h], ks[h]) * decays[h], 0.0) for h in heads]
    aqks = [jnp.where(tri, _dot3_nt(qs[h], ks[h]) * decays[h], 0.0) for h in heads]
    eye = (ri == ci).astype(F32)
    invs = [eye - a for a in amats]
    pws = amats
    span = 2
    while span < seg:
        pwsp = [_split(p) for p in pws]
        pws = [_dot3s(p, p) for p in pwsp]
        invs = [invs[h] + _dot3(invs[h], pws[h]) for h in heads]
        span *= 2
    egcs = [jnp.exp(gcb[:, :HEAD_DIM]) for gcb in gcbs]
    sols = [_dot3(invs[h], jnp.concatenate([vs[h] * betas[h], kbs[h] * egcs[h]], axis=1)) for h in heads]
    us = [s[:, :HEAD_DIM] for s in sols]
    ws = [s[:, HEAD_DIM:] for s in sols]
    qgs = [qs[h] * egcs[h] for h in heads]
    rsl = lambda s: slice(s * seg, (s + 1) * seg)
    sts = [[s_ref[s, h] for s in segs] for h in heads]
    stsp = [[_split(st) for st in row] for row in sts]
    vns = [jnp.concatenate([us[h][rsl(s)] - _dot3s(_split(ws[h][rsl(s)]), stsp[h][s]) for s in segs], axis=0)
           for h in heads]
    oqs = [jnp.concatenate([_dot3s(_split(qgs[h][rsl(s)]), stsp[h][s]) for s in segs], axis=0) for h in heads]
    vnsp = [_split(v) for v in vns]
    os_ = [oqs[h] + _dot3s(_split(aqks[h]), vnsp[h]) for h in heads]
    glasts = [[gcbs[h][(s + 1) * seg - 1:(s + 1) * seg, :HEAD_DIM] for s in segs] for h in heads]
    gl_rows = [jnp.concatenate([jnp.broadcast_to(gl, (seg, HEAD_DIM)) for gl in glasts[h]], axis=0) for h in heads]
    kdts = [(ks[h] * jnp.exp(gl_rows[h] - gcbs[h][:, :HEAD_DIM])).T for h in heads]
    for h in heads:
        for s in segs:
            kds = kdts[h] if nseg == 1 else jnp.where((ci // seg) == s, kdts[h], 0.0)
            s_ref[s, h] = sts[h][s] * jnp.exp(glasts[h][s]) + _dot3s(_split(kds), vnsp[h])
    outs = []
    for h in heads:
        o = os_[h] * lax.rsqrt(jnp.mean(os_[h] * os_[h], axis=-1, keepdims=True) + NORM_EPS) * nw_ref[...]
        zh = z[:, hs(h)]
        outs.append(o * (zh * jax.nn.sigmoid(zh)))
    o_ref[...] = jnp.concatenate(outs, axis=1).astype(o_ref.dtype)


def _gdn(dnqkv, hist, small, z, cw, alog, dtb, nw, s0, *, seg, nb, nchunk, row0, hist_map, zero_first_hist):
    nseg = GDN_ROWS // seg
    rmap = lambda b, c: (row0 + b * nchunk + c, 0)
    const = lambda b, c: (0, 0)
    smap = lambda b, c: (b, 0, 0, 0)
    sblock = (nseg, DN_HEADS, HEAD_DIM, HEAD_DIM)
    return pl.pallas_call(
        functools.partial(_gdn_kernel, seg=seg, zero_first_hist=zero_first_hist),
        grid=(nb, nchunk),
        in_specs=[pl.BlockSpec((GDN_ROWS, 768), rmap), pl.BlockSpec((GDN_ROWS, 768), hist_map),
                  pl.BlockSpec((GDN_ROWS, LANE), rmap), pl.BlockSpec((GDN_ROWS, 256), rmap),
                  pl.BlockSpec(cw.shape, const), pl.BlockSpec(alog.shape, const),
                  pl.BlockSpec(dtb.shape, const), pl.BlockSpec(nw.shape, const),
                  pl.BlockSpec(sblock, smap)],
        out_specs=[pl.BlockSpec((GDN_ROWS, 256), lambda b, c: (b * nchunk + c, 0)),
                   pl.BlockSpec(sblock, smap)],
        out_shape=[jax.ShapeDtypeStruct((nb * nchunk * GDN_ROWS, 256), BF16),
                   jax.ShapeDtypeStruct(s0.shape, F32)],
        compiler_params=_cparams(("parallel", "arbitrary")),
        name="gdn",
    )(dnqkv, hist, small, z, cw, alog, dtb, nw, s0)


def _split(a):
    hi = a.astype(BF16)
    return hi, (a - hi.astype(F32)).astype(BF16)


def _dot3s(a, b):
    return _dotf(a[0], b[0]) + (_dotf(a[0], b[1]) + _dotf(a[1], b[0]))


def _dot3(a, b):
    return _dot3s(_split(a), _split(b))


def _dot3_nt(a, b):
    a, b = _split(a), _split(b)
    return _dot_nt(a[0], b[0]) + (_dot_nt(a[0], b[1]) + _dot_nt(a[1], b[0]))


def _unit_lower_inverse3(a, seg):
    n = a.shape[0]
    eye = (lax.broadcasted_iota(jnp.int32, (n, n), 0) == lax.broadcasted_iota(jnp.int32, (n, n), 1)).astype(F32)
    inv = eye - a
    pw = a
    span = 2
    while span < seg:
        pws = _split(pw)
        pw = _dot3s(pws, pws)
        inv = inv + _dot3s(_split(inv), _split(pw))
        span *= 2
    return inv


GDN_PREP_CHUNKS = 4


def _gdn_prep_kernel(x_ref, hist_ref, sm_ref, cw_ref, alog_ref, dtb_ref,
                     u_ref, w_ref, qg_ref, kd_ref, aqk_ref, egl_ref):
    c = pl.program_id(1)
    x = x_ref[...]
    rows = x.shape[0]
    hist = jnp.where(c == 0, 0.0, hist_ref[...])
    row8 = lax.broadcasted_iota(jnp.int32, hist.shape, 0)
    cw = cw_ref[...]
    y = x * cw[DN_CONV_W - 1:DN_CONV_W]
    for sh in range(1, DN_CONV_W):
        xs = pltpu.roll(x, sh, axis=0)
        top = jnp.where(row8 < sh, pltpu.roll(hist, sh, axis=0), xs[0:8])
        xs = jnp.concatenate([top, xs[8:]], axis=0)
        y = y + xs * cw[DN_CONV_W - 1 - sh:DN_CONV_W - sh]
    y = y * jax.nn.sigmoid(y)

    C = GDN_ROWS
    ri = lax.broadcasted_iota(jnp.int32, (C, C), 0)
    ci = lax.broadcasted_iota(jnp.int32, (C, C), 1)
    tri = ci <= ri
    strict = ci < ri
    trif = tri.astype(F32)
    small = sm_ref[...]
    nch = rows // C
    units = [(ch, h) for ch in range(nch) for h in range(DN_HEADS)]
    each = lambda f, *ls: [f(*a) for a in zip(*ls)]

    def qkv_of(ch, h):
        rs = slice(ch * C, (ch + 1) * C)
        yq = y[rs, h * HEAD_DIM:(h + 1) * HEAD_DIM]
        yk = y[rs, DN_WIDTH + h * HEAD_DIM:DN_WIDTH + (h + 1) * HEAD_DIM]
        v = y[rs, 2 * DN_WIDTH + h * HEAD_DIM:2 * DN_WIDTH + (h + 1) * HEAD_DIM]
        q = yq * lax.rsqrt(jnp.sum(yq * yq, axis=-1, keepdims=True) + 1e-6) * HEAD_DIM ** -0.5
        k = yk * lax.rsqrt(jnp.sum(yk * yk, axis=-1, keepdims=True) + 1e-6)
        beta = jax.nn.sigmoid(small[rs, 28 + h:29 + h])
        g = -jnp.exp(alog_ref[:, h:h + 1]) * _softplus(small[rs, 24 + h:25 + h] + dtb_ref[:, h:h + 1])
        return q, k, v, beta, g

    qs_, ks_, vs_, betas, gs = zip(*[qkv_of(ch, h) for ch, h in units])
    trifs = _split(trif)
    gcbs = [_dot3s(trifs, _split(jnp.broadcast_to(g, (C, C)))) for g in gs]
    kbs = each(lambda k, b: k * b, ks_, betas)
    ksp = [_split(k) for k in ks_]
    kbsp = [_split(kb) for kb in kbs]
    qsp = [_split(q) for q in qs_]
    nt3 = lambda a, b: _dot_nt(a[0], b[0]) + (_dot_nt(a[0], b[1]) + _dot_nt(a[1], b[0]))
    kks = each(nt3, kbsp, ksp)
    qks = each(nt3, qsp, ksp)
    decays = [jnp.where(tri, jnp.exp(jnp.where(tri, gcb - gcb.T, 0.0)), 0.0) for gcb in gcbs]
    amats = each(lambda kk, d: jnp.where(strict, kk * d, 0.0), kks, decays)
    eye = (ri == ci).astype(F32)
    invs = [eye - a for a in amats]
    pws = amats
    span = 2
    while span < C:
        pwsp = [_split(p) for p in pws]
        pws = [_dot3s(p, p) for p in pwsp]
        pwsp = [_split(p) for p in pws]
        invs = each(lambda i, p: i + _dot3s(_split(i), p), invs, pwsp)
        span *= 2
    egcs = [jnp.exp(gcb) for gcb in gcbs]
    rhss = each(lambda v, b, kb, e: jnp.concatenate([v * b, kb * e], axis=1), vs_, betas, kbs, egcs)
    sols = each(_dot3, invs, rhss)
    glasts = [gcb[C - 1:C, :] for gcb in gcbs]
    qgs = each(lambda q, e: q * e, qs_, egcs)
    kds = each(lambda k, gl, gcb: k * jnp.exp(gl - gcb), ks_, glasts, gcbs)
    aqks = each(lambda qk, d: jnp.where(tri, qk * d, 0.0), qks, decays)
    for ch in range(nch):
        rs = slice(ch * C, (ch + 1) * C)
        un = range(ch * DN_HEADS, (ch + 1) * DN_HEADS)
        u_ref[rs, :] = jnp.concatenate([sols[i][:, :HEAD_DIM] for i in un], axis=1)
        w_ref[rs, :] = jnp.concatenate([sols[i][:, HEAD_DIM:] for i in un], axis=1)
        qg_ref[rs, :] = jnp.concatenate([qgs[i] for i in un], axis=1)
        kd_ref[rs, :] = jnp.concatenate([kds[i] for i in un], axis=1)
        aqk_ref[rs, :] = jnp.concatenate([aqks[i] for i in un], axis=1)
        egl_ref[ch * 8:(ch + 1) * 8, :] = jnp.concatenate(
            [jnp.broadcast_to(jnp.exp(glasts[i]), (8, HEAD_DIM)) for i in un], axis=1)


def _gdn_scan_kernel(u_ref, w_ref, qg_ref, kd_ref, aqk_ref, egl_ref, z_ref, nw_ref, s0_ref, o_ref, s_ref):
    c = pl.program_id(0)

    @pl.when(c == 0)
    def _():
        s_ref[...] = s0_ref[...]

    nb = u_ref.shape[0]
    units = [(b, slice(h * HEAD_DIM, (h + 1) * HEAD_DIM), h) for b in range(nb) for h in range(DN_HEADS)]
    sts = [s_ref[b, h] for b, _, h in units]
    stsp = [_split(s) for s in sts]
    kdts = [_split(kd_ref[b, :, sl].T) for b, sl, _ in units]
    vnews = [u_ref[b, :, sl] - _dot3s(_split(w_ref[b, :, sl]), st) for (b, sl, _), st in zip(units, stsp)]
    oqs = [_dot3s(_split(qg_ref[b, :, sl]), st) for (b, sl, _), st in zip(units, stsp)]
    vsp = [_split(v) for v in vnews]
    os_ = [oq + _dot3s(_split(aqk_ref[b, :, sl]), vs) for (b, sl, _), oq, vs in zip(units, oqs, vsp)]
    for (b, sl, h), st, kdt, vs in zip(units, sts, kdts, vsp):
        s_ref[b, h] = st * egl_ref[b, 0:1, sl] + _dot3s(kdt, vs)
    outs = []
    for (b, sl, _), o in zip(units, os_):
        o = o * lax.rsqrt(jnp.mean(o * o, axis=-1, keepdims=True) + NORM_EPS) * nw_ref[...]
        zh = z_ref[b, :, sl]
        outs.append(o * (zh * jax.nn.sigmoid(zh)))
    for b in range(nb):
        o_ref[b] = jnp.concatenate(outs[b * DN_HEADS:(b + 1) * DN_HEADS], axis=1).astype(o_ref.dtype)


def _gdn_prompt(dnqkv, small, z, cw, alog, dtb, nw, s0, *, nb, lq):
    cb = GDN_PREP_CHUNKS
    rows = cb * GDN_ROWS
    nstep = lq // rows
    nchunk = lq // GDN_ROWS
    rmap = lambda b, c: (b * nstep + c, 0)
    const = lambda b, c: (0, 0)
    wide = jax.ShapeDtypeStruct((nb * lq, DN_WIDTH), F32)
    u, w, qg, kd, aqk, egl = pl.pallas_call(
        _gdn_prep_kernel,
        grid=(nb, nstep),
        in_specs=[pl.BlockSpec((rows, 768), rmap),
                  pl.BlockSpec((8, 768), lambda b, c: (jnp.maximum((b * nstep + c) * (rows // 8) - 1, 0), 0)),
                  pl.BlockSpec((rows, LANE), rmap),
                  pl.BlockSpec(cw.shape, const), pl.BlockSpec(alog.shape, const), pl.BlockSpec(dtb.shape, const)],
        out_specs=[pl.BlockSpec((rows, DN_WIDTH), rmap)] * 5 + [pl.BlockSpec((cb * 8, DN_WIDTH), rmap)],
        out_shape=[wide] * 5 + [jax.ShapeDtypeStruct((nb * nchunk * 8, DN_WIDTH), F32)],
        compiler_params=_cparams(("parallel", "parallel")),
        name="gdn_prep",
    )(dnqkv, dnqkv, small, cw, alog, dtb)
    r3 = lambda a: a.reshape(nb, -1, DN_WIDTH)
    cmap = lambda c: (0, c, 0)
    full = lambda a: pl.BlockSpec(a.shape, lambda c: (0,) * a.ndim)
    o, s = pl.pallas_call(
        _gdn_scan_kernel,
        grid=(nchunk,),
        in_specs=[pl.BlockSpec((nb, GDN_ROWS, DN_WIDTH), cmap)] * 5
        + [pl.BlockSpec((nb, 8, DN_WIDTH), cmap), pl.BlockSpec((nb, GDN_ROWS, DN_WIDTH), cmap),
           full(nw), full(s0)],
        out_specs=[pl.BlockSpec((nb, GDN_ROWS, DN_WIDTH), cmap), full(s0)],
        out_shape=[jax.ShapeDtypeStruct((nb, lq, DN_WIDTH), BF16), jax.ShapeDtypeStruct(s0.shape, F32)],
        compiler_params=_cparams(("arbitrary",)),
        name="gdn_scan",
    )(r3(u), r3(w), r3(qg), r3(kd), r3(aqk), r3(egl), z[:nb * lq].reshape(nb, lq, DN_WIDTH), nw, s0)
    return o.reshape(nb * lq, DN_WIDTH), s


def _s5_discretize(lre_ref, lim_ref, lstep_ref):
    lr = lre_ref[...]
    li = lim_ref[...]
    dt = jnp.exp(lstep_ref[...])
    mag = jnp.exp(lr * dt)
    ar = mag * jnp.cos(li * dt)
    ai = mag * jnp.sin(li * dt)
    den = lr * lr + li * li
    fr = ((ar - 1.0) * lr + ai * li) / den
    fi = (ai * lr - (ar - 1.0) * li) / den
    return ar, ai, fr, fi


def _s5_kernel(*refs, R, S, with_y):
    if with_y:
        (u_ref, h0r_ref, h0i_ref, lre_ref, lim_ref, lstep_ref, bre_ref, bim_ref, cre_ref, cim_ref, d_ref,
         wglu_ref, y_ref, hr_ref, hi_ref, ar_s, ai_s, bbr_s, bbi_s, xr_s, xi_s) = refs
    else:
        (u_ref, h0r_ref, h0i_ref, lre_ref, lim_ref, lstep_ref, bre_ref, bim_ref,
         hr_ref, hi_ref, ar_s, ai_s, bbr_s, bbi_s, xr_s, xi_s) = refs
    i = pl.program_id(0)

    @pl.when(i == 0)
    def _():
        ar, ai, fr, fi = _s5_discretize(lre_ref, lim_ref, lstep_ref)
        ar_s[...] = ar
        ai_s[...] = ai
        bbr_s[...] = (fr * bre_ref[...] - fi * bim_ref[...]).astype(BF16)
        bbi_s[...] = (fr * bim_ref[...] + fi * bre_ref[...]).astype(BF16)
        hr_ref[...] = h0r_ref[...]
        hi_ref[...] = h0i_ref[...]

    u = u_ref[...]
    ub = u.astype(BF16)
    xr_s[...] = _dotf(ub, bbr_s[...])
    xi_s[...] = _dotf(ub, bbi_s[...])
    ar = jnp.broadcast_to(ar_s[...], (R, S5_LANES))
    ai = jnp.broadcast_to(ai_s[...], (R, S5_LANES))

    def advance(hr, hi, off):
        nr = ar * hr - ai * hi + xr_s[pl.ds(off, R), :]
        ni = ar * hi + ai * hr + xi_s[pl.ds(off, R), :]
        xr_s[pl.ds(off, R), :] = nr
        xi_s[pl.ds(off, R), :] = ni

    advance(hr_ref[...], hi_ref[...], 0)

    def step(s, carry):
        prev = pl.multiple_of((s - 1) * R, R)
        advance(xr_s[pl.ds(prev, R), :], xi_s[pl.ds(prev, R), :], pl.multiple_of(s * R, R))
        return carry

    lax.fori_loop(1, S, step, 0)
    hr_ref[...] = xr_s[pl.ds((S - 1) * R, R), :]
    hi_ref[...] = xi_s[pl.ds((S - 1) * R, R), :]
    if with_y:
        y = _dotf(xr_s[...].astype(BF16), cre_ref[...]) - _dotf(xi_s[...].astype(BF16), cim_ref[...])
        y = y + d_ref[...] * u
        gl = jax.nn.gelu(y)
        y_ref[...] = (gl * jax.nn.sigmoid(_dotf(gl.astype(BF16), wglu_ref[...]))).astype(y_ref.dtype)


def _s5_scan(u_rows, h0r, h0i, prm, *, R, S, with_y):
    n = u_rows.shape[0]
    rows = R * S
    const = lambda i: (0, 0)
    rmap = lambda i: (i, 0)
    ins = [u_rows, h0r, h0i, prm["lre"], prm["lim"], prm["lstep"], prm["bre"], prm["bim"]]
    if with_y:
        ins += [prm["cre"], prm["cim"], prm["d"], prm["wglu"]]
    in_specs = [pl.BlockSpec((rows, S5_WIDTH), rmap)] + [pl.BlockSpec(a.shape, const) for a in ins[1:]]
    st_spec = pl.BlockSpec((R, S5_LANES), const)
    st_shape = jax.ShapeDtypeStruct((R, S5_LANES), F32)
    out_specs = [st_spec, st_spec]
    out_shape = [st_shape, st_shape]
    if with_y:
        out_specs = [pl.BlockSpec((rows, S5_WIDTH), rmap)] + out_specs
        out_shape = [jax.ShapeDtypeStruct((n, S5_WIDTH), BF16)] + out_shape
    return pl.pallas_call(
        functools.partial(_s5_kernel, R=R, S=S, with_y=with_y),
        grid=(n // rows,),
        in_specs=in_specs,
        out_specs=out_specs,
        out_shape=out_shape,
        scratch_shapes=[pltpu.VMEM((1, S5_LANES), F32), pltpu.VMEM((1, S5_LANES), F32),
                        pltpu.VMEM((S5_WIDTH, S5_LANES), BF16), pltpu.VMEM((S5_WIDTH, S5_LANES), BF16),
                        pltpu.VMEM((rows, S5_LANES), F32), pltpu.VMEM((rows, S5_LANES), F32)],
        compiler_params=_cparams(("arbitrary",)),
        name="s5_scan",
    )(*ins)


def _s5_carry_kernel(er_ref, ei_ref, lre_ref, lim_ref, lstep_ref, ir_ref, ii_ref, fr_ref, fi_ref, *, nseg, nsteps):
    ar, ai, _, _ = _s5_discretize(lre_ref, lim_ref, lstep_ref)
    pr, pi_ = ar, ai
    n = 1
    while n < nsteps:
        pr, pi_ = pr * pr - pi_ * pi_, 2.0 * pr * pi_
        n *= 2
    for b in range(er_ref.shape[0] // nseg):
        cr = jnp.zeros((1, S5_LANES), F32)
        ci = jnp.zeros((1, S5_LANES), F32)
        for s in range(nseg):
            r = b * nseg + s
            ir_ref[r:r + 1, :] = cr
            ii_ref[r:r + 1, :] = ci
            er = er_ref[r:r + 1, :]
            ei = ei_ref[r:r + 1, :]
            cr, ci = pr * cr - pi_ * ci + er, pr * ci + pi_ * cr + ei
        fr_ref[b:b + 1, :] = cr
        fi_ref[b:b + 1, :] = ci


def _s5_carry(er, ei, prm, *, nseg, nsteps):
    assert nsteps & (nsteps - 1) == 0
    nb = er.shape[0] // nseg
    full = lambda a: pl.BlockSpec(a.shape, lambda: (0,) * a.ndim)
    ins = [er, ei, prm["lre"], prm["lim"], prm["lstep"]]
    outs = [jax.ShapeDtypeStruct(er.shape, F32)] * 2 + [jax.ShapeDtypeStruct((nb, S5_LANES), F32)] * 2
    return pl.pallas_call(
        functools.partial(_s5_carry_kernel, nseg=nseg, nsteps=nsteps),
        in_specs=[full(a) for a in ins],
        out_specs=[full(o) for o in outs],
        out_shape=outs,
        name="s5_carry",
    )(*ins)


S5_SEGS = 8


def _s5_prep(lre, lim, lstep, bre, bim, cre, cim, d, wglu):
    eye = jnp.eye(S5_GROUPS, dtype=F32)
    bexp = lambda b: jnp.einsum("gpc,gh->gchp", b, eye).reshape(S5_WIDTH, S5_LANES)
    cexp = lambda c: jnp.einsum("gcp,gh->gphc", c, eye).reshape(S5_LANES, S5_WIDTH).astype(BF16)
    return {"lre": lre.reshape(1, S5_LANES), "lim": lim.reshape(1, S5_LANES),
            "lstep": jnp.repeat(lstep, S5_STATE).reshape(1, S5_LANES),
            "bre": bexp(bre), "bim": bexp(bim), "cre": cexp(cre), "cim": cexp(cim),
            "d": d.reshape(1, S5_WIDTH), "wglu": wglu.astype(BF16)}


def _s5_prompt(u, prm, nb, lq, steps_per_tile=32):
    nsteps = lq // S5_SEGS
    R = nb * S5_SEGS
    u_rows = u.reshape(nb, S5_SEGS, nsteps, S5_WIDTH).transpose(2, 0, 1, 3).reshape(nsteps * R, S5_WIDTH)
    zero = jnp.zeros((R, S5_LANES), F32)
    er, ei = _s5_scan(u_rows, zero, zero, prm, R=R, S=steps_per_tile, with_y=False)
    ir, ii, fr, fi = _s5_carry(er, ei, prm, nseg=S5_SEGS, nsteps=nsteps)
    y_rows, _, _ = _s5_scan(u_rows, ir, ii, prm, R=R, S=steps_per_tile, with_y=True)
    y = y_rows.reshape(nsteps, nb, S5_SEGS, S5_WIDTH).transpose(1, 2, 0, 3).reshape(nb * lq, S5_WIDTH)
    return y, fr, fi


def _s5_sample(u, h0r, h0i, prm, nb, lq, steps_per_tile=4):
    u_rows = u.reshape(nb, lq, S5_WIDTH).transpose(1, 0, 2).reshape(lq * nb, S5_WIDTH)
    y_rows, fr, fi = _s5_scan(u_rows, h0r, h0i, prm, R=nb, S=steps_per_tile, with_y=True)
    y = y_rows.reshape(lq, nb, S5_WIDTH).transpose(1, 0, 2).reshape(nb * lq, S5_WIDTH)
    return y, fr, fi


def _sample_conv_hist(buf):
    nb = buf.shape[0]
    blk = buf.reshape(nb // 8, 8, DN_CONV_W - 1, 768)
    blk = jnp.roll(blk, -1, axis=1)
    blk = jnp.pad(blk, ((0, 0), (0, 0), (8 - (DN_CONV_W - 1), 0), (0, 0)))
    return blk.reshape(nb * 8, 768)


def _rms(x, g):
    return x * lax.rsqrt(jnp.mean(x * x, axis=-1, keepdims=True) + NORM_EPS) * g


def _mlp_kernel(x_ref, on_ref, od_ref, os_ref, wn_ref, wd_ref, ws_ref, ln2_ref, up_ref, dn_ref, lnf_ref,
                o_ref, h2_s, *, final_norm):
    j = pl.program_id(1)

    @pl.when(j == 0)
    def _():
        x1 = (x_ref[...] + _dotf(on_ref[...], wn_ref[...]) + _dotf(od_ref[...], wd_ref[...])
              + _dotf(os_ref[...], ws_ref[...]))
        o_ref[...] = x1
        h2_s[...] = _rms(x1, ln2_ref[...]).astype(BF16)

    a = jnp.maximum(_dotf(h2_s[...], up_ref[...]), 0.0)
    o_ref[...] += _dotf((a * a).astype(BF16), dn_ref[...])

    if final_norm:
        @pl.when(j == pl.num_programs(1) - 1)
        def _():
            o_ref[...] = _rms(o_ref[...], lnf_ref[...])


def _mlp(x, o_nsa, o_dn, o_s5, wn, wd, ws, ln2, up, dn, lnf, *, final_norm, tm=512, tf=1024):
    n = x.shape[0]
    row = lambda i, j: (i, 0)
    const = lambda i, j: (0, 0)
    return pl.pallas_call(
        functools.partial(_mlp_kernel, final_norm=final_norm),
        grid=(n // tm, D_FF // tf),
        in_specs=[pl.BlockSpec((tm, D_MODEL), row), pl.BlockSpec((tm, 1024), row),
                  pl.BlockSpec((tm, DN_WIDTH), row), pl.BlockSpec((tm, S5_WIDTH), row),
                  pl.BlockSpec(wn.shape, const), pl.BlockSpec(wd.shape, const), pl.BlockSpec(ws.shape, const),
                  pl.BlockSpec((1, D_MODEL), const),
                  pl.BlockSpec((D_MODEL, tf), lambda i, j: (0, j)),
                  pl.BlockSpec((tf, D_MODEL), lambda i, j: (j, 0)),
                  pl.BlockSpec((1, D_MODEL), const)],
        out_specs=pl.BlockSpec((tm, D_MODEL), row),
        out_shape=jax.ShapeDtypeStruct((n, D_MODEL), F32),
        scratch_shapes=[pltpu.VMEM((tm, D_MODEL), BF16)],
        compiler_params=_cparams(("parallel", "arbitrary")),
        name="out_mlp",
    )(x, o_nsa, o_dn, o_s5, wn, wd, ws, ln2, up, dn, lnf)


_NSA_W = NSA_HEADS * HEAD_DIM
_KV_W = 6 * NSA_KV_HEADS * HEAD_DIM
_GATE_W = NSA_HEADS * 3
_OFF_KV = _NSA_W
_OFF_GATE = _OFF_KV + _KV_W
_OFF_DN = _OFF_GATE + _GATE_W
_OFF_A = _OFF_DN + 3 * DN_WIDTH
_OFF_B = _OFF_A + DN_HEADS
_OFF_Z = _OFF_B + DN_HEADS
_OFF_U = _OFF_Z + DN_WIDTH


def _pad_heads(w, axis):
    w = jnp.moveaxis(w, axis, 0).reshape((NSA_HEADS, HEAD_DIM) + w.shape[:axis] + w.shape[axis + 1:])
    out = jnp.zeros((NSA_HEADS, LANE) + w.shape[2:], w.dtype)
    for h in range(NSA_HEADS):
        g = h // NSA_GROUP
        out = out.at[h, g * HEAD_DIM:(g + 1) * HEAD_DIM].set(w[h])
    out = out.reshape((NSA_HEADS * LANE,) + w.shape[2:])
    return jnp.moveaxis(out, 0, axis)


def _layer_weights(w_in, w_out, wck, wcv):
    wq = _pad_heads(w_in[:, :_NSA_W], 1).astype(BF16)
    wkv = w_in[:, _OFF_KV:_OFF_GATE].astype(BF16)
    wdn = w_in[:, _OFF_DN:_OFF_A].astype(BF16)
    wz = w_in[:, _OFF_Z:_OFF_U].astype(BF16)
    wu = w_in[:, _OFF_U:].astype(BF16)
    wsm = jnp.zeros((D_MODEL, LANE), F32)
    wsm = wsm.at[:, :_GATE_W].set(w_in[:, _OFF_GATE:_OFF_DN])
    wsm = wsm.at[:, _GATE_W:_GATE_W + 2 * DN_HEADS].set(w_in[:, _OFF_A:_OFF_Z]).astype(BF16)
    wexp = jnp.concatenate([jnp.repeat(wck.T, HEAD_DIM, axis=1), jnp.repeat(wcv.T, HEAD_DIM, axis=1)], axis=1)
    won = _pad_heads(w_out[:_NSA_W], 0).astype(BF16)
    wod = w_out[_NSA_W:_NSA_W + DN_WIDTH].astype(BF16)
    wos = w_out[_NSA_W + DN_WIDTH:].astype(BF16)
    return wq, wkv, wdn, wz, wu, wsm, wexp, won, wod, wos


def kernel(x_prompt, x_sample, cache_nsa_kv, cache_win_kv, state_dn_conv, state_dn, state_s5_re, state_s5_im,
           page_table, ln1, ln2, ln_f, w_in, w_out, nsa_wck, nsa_wcv, dn_conv_w, dn_a_log, dn_dt_bias,
           dn_norm_w, s5_lambda_re, s5_lambda_im, s5_log_step, s5_b_re, s5_b_im, s5_c_re, s5_c_im, s5_d,
           s5_w_glu, mlp_up, mlp_down):
    nbp, lp, _ = x_prompt.shape
    nbs, ls, _ = x_sample.shape
    depth = w_in.shape[0]
    rows_p = nbp * lp
    rows_s = nbs * ls
    n_phys, page = cache_nsa_kv.shape[1], cache_nsa_kv.shape[2]
    past = page_table.shape[1] * page
    wlen = cache_win_kv.shape[2]
    G, dh = NSA_KV_HEADS, HEAD_DIM
    assert lp % (S5_SEGS * 32) == 0 and nbs % 8 == 0 and ls == 8 and wlen == WINDOW and lp >= WINDOW

    xp = x_prompt.reshape(rows_p, D_MODEL)
    xs = x_sample.reshape(rows_s, D_MODEL)
    cache_t = cache_nsa_kv.transpose(0, 1, 3, 4, 5, 2).reshape(depth, n_phys, 4 * G * dh, page)
    win_t = cache_win_kv.transpose(0, 1, 3, 4, 5, 2).reshape(depth, nbs, 2 * G * dh, wlen)
    outs_p, outs_s = [], []
    for l in range(depth):
        wq, wkv, wdn, wz, wu, wsm, wexp, won, wod, wos = _layer_weights(w_in[l], w_out[l], nsa_wck[l], nsa_wcv[l])
        pw = (ln1[l][None], wq, wkv, wdn, wz, wu, wsm, wexp)
        qh_p, kv_p, dnx_p, z_p, u_p, small_p, pool_p, kvt_p, kaug, vt = _proj(xp, *pw, nseq=nbp, sel_tile=NSA_TQ)
        qh_s, kv_s, dnx_s, z_s, u_s, small_s, _, _ = _proj(xs, *pw, nseq=1)

        oc, sel, anyblk = _cmp_prompt(qh_p, pool_p, nbp, lp, NSA_TQ)
        osel = _sel_prompt(qh_p, sel, anyblk, kaug, vt, nbp, lp, NSA_TQ)
        on_p = _win_prompt(qh_p, kv_p, oc, osel, small_p, nbp, lp)
        wt = _pool_weights_t(nsa_wck[l], nsa_wcv[l], past)
        on_s, nwin_t = _nsa_sample(page_table, qh_s, kv_s, small_s, cache_t, win_t, wt, layer=l, lq=ls, past=past)

        gdn_w = (dn_conv_w[l].T, dn_a_log[l][None], dn_dt_bias[l][None], dn_norm_w[l][None])
        od_p, dn_p = _gdn_prompt(dnx_p, small_p, z_p, *gdn_w, jnp.zeros((nbp, DN_HEADS, dh, dh), F32), nb=nbp, lq=lp)
        od_s, dn_s = _gdn(dnx_s, _sample_conv_hist(state_dn_conv[l]), small_s, z_s, *gdn_w, state_dn[l],
                          seg=ls, nb=rows_s // GDN_ROWS, nchunk=1, row0=0,
                          hist_map=lambda b, c: (b, 0), zero_first_hist=False)

        prm = _s5_prep(s5_lambda_re[l], s5_lambda_im[l], s5_log_step[l], s5_b_re[l], s5_b_im[l],
                       s5_c_re[l], s5_c_im[l], s5_d[l], s5_w_glu[l])
        os_p, s5r_p, s5i_p = _s5_prompt(u_p, prm, nbp, lp)
        os_s, s5r_s, s5i_s = _s5_sample(u_s, state_s5_re[l].reshape(nbs, S5_LANES),
                                        state_s5_im[l].reshape(nbs, S5_LANES), prm, nbs, ls)

        mw = (won, wod, wos, ln2[l][None], mlp_up[l].astype(BF16), mlp_down[l].astype(BF16), ln_f[None])
        xp = _mlp(xp, on_p, od_p, os_p, *mw, final_norm=(l == depth - 1))
        xs = _mlp(xs, on_s, od_s, os_s, *mw, final_norm=(l == depth - 1))

        kvt6 = kvt_p.reshape(nbp, 6, G, dh, lp)
        kv_s6 = kv_s.reshape(nbs, ls, 6, G, dh)
        win_s = nwin_t.reshape(nbs, 2, G, dh, wlen).transpose(0, 4, 1, 2, 3)
        outs_p.append((kvt6[:, :4].transpose(0, 4, 1, 2, 3), kvt6[:, 4:, :, :, lp - WINDOW:].transpose(0, 4, 1, 2, 3),
                       dnx_p.reshape(nbp, lp, 3 * DN_WIDTH)[:, -(DN_CONV_W - 1):], dn_p,
                       s5r_p.reshape(nbp, S5_GROUPS, S5_STATE), s5i_p.reshape(nbp, S5_GROUPS, S5_STATE)))
        outs_s.append((kv_s6[:, :, :4], win_s, dnx_s.reshape(nbs, ls, 3 * DN_WIDTH)[:, -(DN_CONV_W - 1):], dn_s,
                       s5r_s.reshape(nbs, S5_GROUPS, S5_STATE), s5i_s.reshape(nbs, S5_GROUPS, S5_STATE)))

    res = [xp.reshape(nbp, lp, D_MODEL), xs.reshape(nbs, ls, D_MODEL)]
    for i in range(6):
        res.append(jnp.stack([o[i] for o in outs_p], axis=0))
        res.append(jnp.stack([o[i] for o in outs_s], axis=0))
    return tuple(res)
```

```python
import functools
import math

import jax
import jax.numpy as jnp
import numpy as np
from jax import lax
from jax.experimental import pallas as pl
from jax.experimental.pallas import tpu as pltpu

F32 = jnp.float32
BF16 = jnp.bfloat16
HIGHEST = lax.Precision.HIGHEST

D_MODEL = 1024
HEAD_DIM = 64
NSA_HEADS = 8
NSA_KV_HEADS = 2
NSA_GROUP = 4
NSA_BLOCK = 64
NSA_TOPK = 16
WINDOW = 512
DN_HEADS = 4
DN_WIDTH = 256
DN_CONV_W = 4
S5_GROUPS = 16
S5_GROUP_CH = 16
S5_STATE = 64
S5_WIDTH = 256
S5_LANES = S5_GROUPS * S5_STATE
D_FF = 4096
NORM_EPS = 1e-6
BIG = 1e9
NEG = -1e30
LANE = 128
NSA_TQ = 256
VMEM_LIMIT = 48 * 1024 * 1024

_NT = (((1,), (1,)), ((), ()))


def _slope(h):
    return 2.0 ** (-(h + 1))


def _cparams(sem):
    return pltpu.CompilerParams(dimension_semantics=sem, vmem_limit_bytes=VMEM_LIMIT)


def _dotf(a, b):
    return jnp.dot(a, b, preferred_element_type=F32)


def _dot_nt(a, b):
    return lax.dot_general(a, b, _NT, preferred_element_type=F32)


def _dot_hi(a, b):
    return jnp.dot(a, b, preferred_element_type=F32, precision=HIGHEST)


def _dot_nt_hi(a, b):
    return lax.dot_general(a, b, _NT, preferred_element_type=F32, precision=HIGHEST)


def _proj_kernel(x_ref, g_ref, wq_ref, wkv_ref, wdn_ref, wz_ref, wu_ref, ws_ref, wexp_ref, *refs, lt, tk, nprev):
    if nprev:
        prev_ref, refs = refs[0], refs[1:]
    q_ref, kv_ref, dn_ref, z_ref, u_ref, s_ref, pool_ref, kvtm_ref, kvtw_ref = refs[:9]
    sel_refs = refs[9:]
    x = x_ref[...]
    h = x * lax.rsqrt(jnp.mean(x * x, axis=-1, keepdims=True) + NORM_EPS) * g_ref[...]
    hb = h.astype(BF16)
    q_ref[...] = _dotf(hb, wq_ref[...])
    kv = _dotf(hb, wkv_ref[...])
    kv_ref[...] = kv
    kvt = kv.T
    if nprev:
        kvtm_ref[0:nprev] = prev_ref[...]
    kvtm_ref[nprev] = kvt[:4 * LANE]
    kvtw_ref[...] = kvt[4 * LANE:]
    if sel_refs:
        kaug_ref, vt_ref = sel_refs
        rows = x.shape[0]
        pos = (pl.program_id(0) % lt) * rows + lax.broadcasted_iota(jnp.int32, (rows, LANE), 0)
        lanei = lax.broadcasted_iota(jnp.int32, (rows, LANE), 1)
        onehot = (lanei == pos // NSA_BLOCK).astype(BF16)
        jpart = jnp.where(lanei == HEAD_DIM, (pos % NSA_BLOCK).astype(F32), 0.0)
        ksl = kv[:, 2 * LANE:3 * LANE]
        for g in range(NSA_KV_HEADS):
            kg = ksl if g == 0 else pltpu.roll(ksl, HEAD_DIM, axis=1)
            kaug_ref[g] = jnp.concatenate([onehot, jnp.where(lanei < HEAD_DIM, kg, jpart).astype(BF16)], axis=1)
            vrow = 3 * LANE + g * HEAD_DIM
            for c in range(rows // tk):
                vt_ref[g, c] = kvt[vrow:vrow + HEAD_DIM, c * tk:(c + 1) * tk].astype(BF16)
    dn_ref[...] = _dotf(hb, wdn_ref[...])
    z_ref[...] = _dotf(hb, wz_ref[...])
    u_ref[...] = _dotf(hb, wu_ref[...])
    s_ref[...] = _dotf(hb, ws_ref[...])
    tm = x.shape[0]
    kc = kv[:, :2 * LANE].reshape(tm // NSA_BLOCK, NSA_BLOCK, 2 * LANE) * wexp_ref[...][None]
    pool_ref[...] = jnp.sum(kc, axis=1)


def _proj(x, ln, wq, wkv, wdn, wz, wu, ws, wexp, nseq, prev_kvt=None, sel_tile=None, tm=512):
    n = x.shape[0]
    lseq = n // nseq
    lt = lseq // tm
    nprev = 0 if prev_kvt is None else prev_kvt.shape[0]
    const = lambda i: (0, 0)
    row = lambda i: (i, 0)
    outs = [(n, 1024), (n, 768), (n, 768), (n, 256), (n, 256), (n, LANE), (n // NSA_BLOCK, 2 * LANE),
            (nprev + 1, nseq, 4 * LANE, lseq), (nseq, 2 * LANE, lseq)]
    out_specs = ([pl.BlockSpec((tm, s[1]), row) for s in outs[:6]]
                 + [pl.BlockSpec((tm // NSA_BLOCK, 2 * LANE), row),
                    pl.BlockSpec((nprev + 1, None, 4 * LANE, tm), lambda i: (0, i // lt, 0, i % lt)),
                    pl.BlockSpec((None, 2 * LANE, tm), lambda i: (i // lt, 0, i % lt))])
    out_shape = [jax.ShapeDtypeStruct(s, F32) for s in outs]
    prev_specs = [pl.BlockSpec((nprev, None, 4 * LANE, tm), lambda i: (0, i // lt, 0, i % lt))] if nprev else []
    G = NSA_KV_HEADS
    if sel_tile is not None:
        out_specs += [pl.BlockSpec((None, G, tm, 2 * LANE), lambda i: (i // lt, 0, i % lt, 0)),
                      pl.BlockSpec((None, G, tm // sel_tile, HEAD_DIM, sel_tile),
                                   lambda i: (i // lt, 0, i % lt, 0, 0))]
        out_shape += [jax.ShapeDtypeStruct((nseq, G, lseq, 2 * LANE), BF16),
                      jax.ShapeDtypeStruct((nseq, G, lseq // sel_tile, HEAD_DIM, sel_tile), BF16)]
    return pl.pallas_call(
        functools.partial(_proj_kernel, lt=lt, tk=sel_tile, nprev=nprev),
        grid=(n // tm,),
        in_specs=[pl.BlockSpec((tm, D_MODEL), row), pl.BlockSpec((1, D_MODEL), const)]
        + [pl.BlockSpec(w.shape, const) for w in (wq, wkv, wdn, wz, wu, ws, wexp)] + prev_specs,
        out_specs=out_specs,
        out_shape=out_shape,
        compiler_params=_cparams(("parallel",)),
        name="proj",
    )(x, ln, wq, wkv, wdn, wz, wu, ws, wexp, *([prev_kvt] if nprev else []))


def _topk_round(sel, score, blk, nblk):
    m = jnp.max(score, axis=0, keepdims=True)
    idx = jnp.min(jnp.where(score == m, blk, nblk), axis=0, keepdims=True)
    pick = blk == idx
    return sel | pick, jnp.where(pick, -jnp.inf, score)


NSA_FORCED = 3
NSA_FREE_ROUNDS = NSA_TOPK - NSA_FORCED


def _free_scores(imp, forced, blk, cur):
    return jnp.where(blk > cur, -BIG, jnp.where(forced, -jnp.inf, imp))


def _topk_blocks(score, blk, nblk):
    sel = jnp.zeros(score.shape, dtype=jnp.bool_)
    for _ in range(NSA_FREE_ROUNDS):
        sel, score = _topk_round(sel, score, blk, nblk)
    return sel


def _cmp_heads(q_tile, kc, vct, t, blk, nblk_valid, write_oc):
    nblk = blk.shape[0]
    ok = ((blk + 1) * NSA_BLOCK - 1 <= t) & (blk < nblk_valid)
    okf = ok.astype(F32)
    dist = t.astype(F32) - (blk.astype(F32) * NSA_BLOCK + (NSA_BLOCK - 1) / 2.0)
    cur = t // NSA_BLOCK
    forced = (blk == 0) | (blk == cur) | (blk == cur - 1)
    sels = []
    for g in range(NSA_KV_HEADS):
        imp = jnp.zeros(blk.shape, F32)
        for r in range(NSA_GROUP):
            h = g * NSA_GROUP + r
            q = (q_tile(h) * HEAD_DIM ** -0.5).astype(BF16)
            s = _dot_nt(kc, q)
            s = jnp.where(ok, s - _slope(h) * dist, NEG)
            m = jnp.max(s, axis=0, keepdims=True)
            p = jnp.exp(s - m) * okf
            p = p / jnp.maximum(jnp.sum(p, axis=0, keepdims=True), 1e-30)
            imp = imp + p
            write_oc(h, _dotf(vct, p.astype(BF16)))
        sel = (forced | _topk_blocks(_free_scores(imp, forced, blk, cur), blk, nblk)) & (blk <= cur)
        sels.append(sel)
    return sels


def _cmp_kernel(q_ref, kcv_ref, oc_ref, sel_ref, any_ref, *, tq):
    qt = pl.program_id(1)
    kcv = kcv_ref[...]
    nblk = kcv.shape[0]
    kc = kcv[:, :LANE].astype(BF16)
    vct = kcv[:, LANE:].T.astype(BF16)
    blk = lax.broadcasted_iota(jnp.int32, (nblk, tq), 0)
    t = qt * tq + lax.broadcasted_iota(jnp.int32, (nblk, tq), 1)

    def write_oc(h, oct):
        oc_ref[:, h * LANE:(h + 1) * LANE] = oct.T

    sels = _cmp_heads(lambda h: q_ref[:, h * LANE:(h + 1) * LANE], kc, vct, t, blk, nblk, write_oc)
    for g in range(NSA_KV_HEADS):
        selt = sels[g].astype(F32).T
        sel_ref[:, g * LANE:(g + 1) * LANE] = selt
        any_ref[:, g * LANE:(g + 1) * LANE] = jnp.broadcast_to(jnp.max(selt, axis=0, keepdims=True), (8, LANE))


def _cmp_prompt(qh, pool, nb, lq, tq):
    nq = lq // tq
    nblk = lq // NSA_BLOCK
    assert nblk == LANE
    return pl.pallas_call(
        functools.partial(_cmp_kernel, tq=tq),
        grid=(nb, nq),
        in_specs=[pl.BlockSpec((tq, 1024), lambda b, i: (b * nq + i, 0)),
                  pl.BlockSpec((nblk, 2 * LANE), lambda b, i: (b, 0))],
        out_specs=[pl.BlockSpec((tq, 1024), lambda b, i: (b * nq + i, 0)),
                   pl.BlockSpec((tq, 2 * LANE), lambda b, i: (b * nq + i, 0)),
                   pl.BlockSpec((8, 2 * LANE), lambda b, i: (b * nq + i, 0))],
        out_shape=[jax.ShapeDtypeStruct((nb * lq, 1024), F32),
                   jax.ShapeDtypeStruct((nb * lq, 2 * LANE), F32),
                   jax.ShapeDtypeStruct((nb * nq * 8, 2 * LANE), F32)],
        compiler_params=_cparams(("parallel", "parallel")),
        name="nsa_cmp",
    )(qh, pool)


def _sel_kernel(cnt_ref, lst_ref, q_ref, sel_ref, k_ref, vt_ref, o_ref, lhs_s, m_s, l_s, acc_s, *, tq, nq):
    b = pl.program_id(0)
    g = pl.program_id(1)
    qt = pl.program_id(2)
    lanei = lax.broadcasted_iota(jnp.int32, (tq, LANE), 1)
    nio = lanei.astype(F32)
    selg = sel_ref[...] > 0.5
    for r in range(NSA_GROUP):
        slope = jnp.where(g == 0, _slope(r), _slope(NSA_GROUP + r))
        qt_r = q_ref[:, r * LANE:(r + 1) * LANE]
        qt_r = jnp.where(g == 0, qt_r, pltpu.roll(qt_r, HEAD_DIM, axis=1))
        qpart = jnp.where(lanei == HEAD_DIM, slope, qt_r * HEAD_DIM ** -0.5)
        bias = jnp.where(selg, (NSA_BLOCK * slope) * nio, NEG)
        lhs_s[r] = jnp.concatenate([bias.astype(BF16), qpart.astype(BF16)], axis=1)
    m_s[...] = jnp.full(m_s.shape, NEG, F32)
    l_s[...] = jnp.zeros(l_s.shape, F32)
    acc_s[...] = jnp.zeros(acc_s.shape, F32)

    def tiles(items):
        heads = range(NSA_GROUP)
        kaugs = [k_ref[kt] for kt, _ in items]
        vts = [vt_ref[kt] for kt, _ in items]
        ss = [[_dot_nt(ka, lhs_s[r]) for r in heads] for ka in kaugs]
        ss = [[s if mask is None else jnp.where(mask, s, NEG) for s in row] for row, (_, mask) in zip(ss, items)]
        m_old = [m_s[r] for r in heads]
        m_new = []
        for r in heads:
            m = m_old[r]
            for row in ss:
                m = jnp.maximum(m, jnp.max(row[r], axis=0, keepdims=True))
            m_new.append(m)
        ps = [[jnp.exp(row[r] - m_new[r]) for r in heads] for row in ss]
        pvs = [[_dotf(vt, row[r].astype(BF16)) for r in heads] for vt, row in zip(vts, ps)]
        for r in heads:
            alpha = jnp.exp(m_old[r] - m_new[r])
            l_s[r] = alpha * l_s[r] + sum(jnp.sum(row[r], axis=0, keepdims=True) for row in ps)
            acc_s[r] = alpha * acc_s[r] + sum(row[r] for row in pvs)
            m_s[r] = m_new[r]

    cbase = (b * NSA_KV_HEADS + g) * nq + qt
    n_act = cnt_ref[cbase]
    lbase = cbase * nq

    def body(i, carry):
        tiles([(lst_ref[lbase + 2 * i], None), (lst_ref[lbase + 2 * i + 1], None)])
        return carry

    lax.fori_loop(0, n_act // 2, body, 0)
    diag = (qt, lax.broadcasted_iota(jnp.int32, (tq, tq), 0) <= lax.broadcasted_iota(jnp.int32, (tq, tq), 1))

    @pl.when(n_act % 2 == 1)
    def _():
        tiles([(lst_ref[lbase + n_act - 1], None), diag])

    @pl.when(n_act % 2 == 0)
    def _():
        tiles([diag])

    zeros = jnp.zeros((HEAD_DIM, tq), F32)
    for r in range(NSA_GROUP):
        o = jnp.concatenate([acc_s[r] / l_s[r], zeros], axis=0).T
        o_ref[:, r * LANE:(r + 1) * LANE] = jnp.where(g == 0, o, pltpu.roll(o, HEAD_DIM, axis=1))


def _sel_prompt(qh, sel, anyblk, kaug, vt, nb, lq, tq):
    assert vt.shape[-1] == tq
    nq = lq // tq
    bpt = tq // NSA_BLOCK
    G = NSA_KV_HEADS
    act = anyblk[::8].reshape(nb, nq, G, nq, bpt).max(axis=-1) > 0.5
    act = act.transpose(0, 2, 1, 3) & (jnp.arange(nq)[None, :] < jnp.arange(nq)[:, None])
    cnt = act.sum(axis=-1).astype(jnp.int32).reshape(-1)
    lst = jnp.argsort(jnp.logical_not(act), axis=-1, stable=True).astype(jnp.int32).reshape(-1)
    kaug = kaug.reshape(nb, G, nq, tq, 2 * LANE)
    qmap = lambda b, g, i, c, l: (b * nq + i, g)
    whole = lambda b, g, i, c, l: (b, g, 0, 0, 0)
    grid_spec = pltpu.PrefetchScalarGridSpec(
        num_scalar_prefetch=2,
        grid=(nb, G, nq),
        in_specs=[pl.BlockSpec((tq, NSA_GROUP * LANE), qmap),
                  pl.BlockSpec((tq, LANE), qmap),
                  pl.BlockSpec((None, None, nq, tq, 2 * LANE), whole),
                  pl.BlockSpec((None, None, nq, HEAD_DIM, tq), whole)],
        out_specs=pl.BlockSpec((tq, NSA_GROUP * LANE), qmap),
        scratch_shapes=[pltpu.VMEM((NSA_GROUP, tq, 2 * LANE), BF16),
                        pltpu.VMEM((NSA_GROUP, 1, tq), F32),
                        pltpu.VMEM((NSA_GROUP, 1, tq), F32),
                        pltpu.VMEM((NSA_GROUP, HEAD_DIM, tq), F32)])
    return pl.pallas_call(
        functools.partial(_sel_kernel, tq=tq, nq=nq),
        grid_spec=grid_spec,
        out_shape=jax.ShapeDtypeStruct((nb * lq, 1024), F32),
        compiler_params=_cparams(("parallel", "parallel", "arbitrary")),
        name="nsa_sel",
    )(cnt, lst, qh, sel, kaug, vt)


def _gate_mix(small, oc, os_, ow, h):
    gt = jax.nn.sigmoid(small[:, 3 * h:3 * h + 3])
    return gt[:, 0:1] * oc + gt[:, 1:2] * os_ + gt[:, 2:3] * ow


def _win_kernel(q_ref, k0_ref, k1_ref, k2_ref, v0_ref, v1_ref, v2_ref, oc_ref, os_ref, sm_ref, o_ref, *, tq):
    qt = pl.program_id(1)
    kcat = jnp.concatenate([k0_ref[...], k1_ref[...], k2_ref[...]], axis=0).astype(BF16)
    vcat = jnp.concatenate([v0_ref[...], v1_ref[...], v2_ref[...]], axis=0).astype(BF16)
    row = lax.broadcasted_iota(jnp.int32, (tq, 3 * tq), 0)
    col = lax.broadcasted_iota(jnp.int32, (tq, 3 * tq), 1)
    dist = row - col + 2 * tq
    valid = (dist >= 0) & (dist < WINDOW) & ((qt - 2) * tq + col >= 0)
    distf = dist.astype(F32)
    small = sm_ref[...]
    for h in range(NSA_HEADS):
        sl = slice(h * LANE, (h + 1) * LANE)
        q = (q_ref[:, sl] * HEAD_DIM ** -0.5).astype(BF16)
        s = _dot_nt(q, kcat)
        s = jnp.where(valid, s - _slope(h) * distf, NEG)
        m = jnp.max(s, axis=1, keepdims=True)
        p = jnp.exp(s - m)
        ow = _dotf(p.astype(BF16), vcat) / jnp.sum(p, axis=1, keepdims=True)
        o_ref[:, sl] = _gate_mix(small, oc_ref[:, sl], os_ref[:, sl], ow, h).astype(o_ref.dtype)


def _win_prompt(qh, kv, oc, osel, small, nb, lq, tq=256):
    assert 2 * tq >= WINDOW - 1
    nq = lq // tq
    row = lambda b, i: (b * nq + i, 0)

    def kmap(back, c):
        return lambda b, i: (b * nq + jnp.maximum(i - back, 0), c)

    return pl.pallas_call(
        functools.partial(_win_kernel, tq=tq),
        grid=(nb, nq),
        in_specs=[pl.BlockSpec((tq, 1024), row)]
        + [pl.BlockSpec((tq, LANE), kmap(back, 4)) for back in (2, 1, 0)]
        + [pl.BlockSpec((tq, LANE), kmap(back, 5)) for back in (2, 1, 0)]
        + [pl.BlockSpec((tq, 1024), row), pl.BlockSpec((tq, 1024), row), pl.BlockSpec((tq, LANE), row)],
        out_specs=pl.BlockSpec((tq, 1024), row),
        out_shape=jax.ShapeDtypeStruct((nb * lq, 1024), BF16),
        compiler_params=_cparams(("parallel", "parallel")),
        name="nsa_win",
    )(qh, kv, kv, kv, kv, kv, kv, oc, osel, small)


def _per_head(idx_h, fn):
    out = jnp.zeros(idx_h.shape, F32)
    for h in range(NSA_HEADS):
        out = jnp.where(idx_h == h, fn(h), out)
    return out


def _pad_rows(a, rows):
    return jnp.concatenate([a, jnp.zeros((rows - a.shape[0], a.shape[1]), a.dtype)], axis=0)


NSA_SAMPLE_SEQS = 2


def _nsa_sample_kernel(pt_ref, q_ref, kvn_ref, sm_ref, *rest, past, lq, npages, page, nprev):
    nseq = NSA_SAMPLE_SEQS
    pages = rest[:nseq * npages]
    rest = rest[nseq * npages:]
    if nprev:
        win_ref, wt_ref, prev_ref, o_ref, nwin_all, ktc_s, vtc_s, rt_s, vt_s = rest
        nwin_all[0:nprev] = prev_ref[...]
    else:
        win_ref, wt_ref, o_ref, nwin_all, ktc_s, vtc_s, rt_s, vt_s = rest
    nwin_ref = nwin_all.at[nprev]

    @pl.when(pl.program_id(0) == 0)
    def _():
        n_i = lax.broadcasted_iota(jnp.int32, (LANE, past), 0)
        pos_i = lax.broadcasted_iota(jnp.int32, (LANE, past), 1)
        for j in range(nseq):
            rt_s[j, 0:LANE, :] = (n_i == pos_i // NSA_BLOCK).astype(BF16)

    def rows(ref, j):
        return ref.at[pl.ds(j * lq, lq)]

    seqs = [_nsa_sample_seq(rows(q_ref, j), rows(kvn_ref, j), rows(sm_ref, j), pages[j * npages:(j + 1) * npages],
                            win_ref.at[j], wt_ref, rows(o_ref, j), nwin_ref.at[j], ktc_s.at[j], vtc_s.at[j],
                            rt_s.at[j], vt_s.at[j], past=past, lq=lq, npages=npages, page=page)
            for j in range(nseq)]
    live = True
    while live:
        for s in seqs:
            live = next(s, None) is not None and live


def _nsa_sample_seq(q_ref, kvn_ref, sm_ref, pages, win_ref, wt_ref, o_ref, nwin_ref, ktc_s, vtc_s, rt_s, vt_s,
                    *, past, lq, npages, page):
    nkeys = npages * page
    nrow = NSA_HEADS * lq
    nb_complete = (past + lq) // NSA_BLOCK
    new_blk = past // NSA_BLOCK
    assert past % NSA_BLOCK + lq <= NSA_BLOCK and nkeys == past and new_blk < LANE

    for i in range(npages):
        sl = slice(i * page, (i + 1) * page)
        ktc_s[:, sl] = pages[i][0:LANE, :].astype(BF16)
        vtc_s[:, sl] = pages[i][LANE:2 * LANE, :].astype(BF16)
    yield True

    def stage_selected_page(i):
        sl = slice(i * page, (i + 1) * page)
        rt_s[LANE:2 * LANE, sl] = pages[i][2 * LANE:3 * LANE, :].astype(BF16)
        vt_s[:, sl] = pages[i][3 * LANE:4 * LANE, :].astype(BF16)

    q8 = q_ref[...]
    qf = jnp.concatenate([q8[:, h * LANE:(h + 1) * LANE] for h in range(NSA_HEADS)], axis=0) * HEAD_DIM ** -0.5
    qb = qf.astype(BF16)
    kvn = kvn_ref[...]
    lane_lo = lax.broadcasted_iota(jnp.int32, (LANE, LANE), 1) < HEAD_DIM

    ktc = ktc_s[...]
    vtc = vtc_s[...]
    kc = jnp.where(lane_lo, _dot_nt(wt_ref[0, 0], ktc), _dot_nt(wt_ref[0, 1], ktc)).astype(BF16)
    vc = jnp.where(lane_lo, _dot_nt(wt_ref[1, 0], vtc), _dot_nt(wt_ref[1, 1], vtc))
    vct = vc.T.astype(BF16)
    blk = lax.broadcasted_iota(jnp.int32, (LANE, nrow), 0)
    col = lax.broadcasted_iota(jnp.int32, (LANE, nrow), 1)
    t = past + col % lq
    slope_c = _per_head(col // lq, _slope)
    ok = ((blk + 1) * NSA_BLOCK - 1 <= t) & (blk < nb_complete)
    dist = t.astype(F32) - (blk.astype(F32) * NSA_BLOCK + (NSA_BLOCK - 1) / 2.0)
    s = jnp.where(ok, _dot_nt(kc, qb) - slope_c * dist, NEG)
    m = jnp.max(s, axis=0, keepdims=True)
    p = jnp.exp(s - m) * ok.astype(F32)
    p = p / jnp.maximum(jnp.sum(p, axis=0, keepdims=True), 1e-30)
    o_c = _dotf(vct, p.astype(BF16)).T
    yield True
    ri = lax.broadcasted_iota(jnp.int32, (nrow, nrow), 0)
    ci = lax.broadcasted_iota(jnp.int32, (nrow, nrow), 1)
    gsum = ((ri // (NSA_GROUP * lq) == ci // (NSA_GROUP * lq)) & (ri % lq == ci % lq)).astype(F32)
    imp = _dot_hi(p, gsum)
    cur = t // NSA_BLOCK
    forced = (blk == 0) | (blk == cur) | (blk == cur - 1)
    score = _free_scores(imp, forced, blk, cur)
    rowi = lax.broadcasted_iota(jnp.int32, (nrow, LANE), 0)
    lanei = lax.broadcasted_iota(jnp.int32, (nrow, LANE), 1)
    slope_r = _per_head(rowi[:, 0:1] // lq, _slope)
    tok_r = rowi % lq
    new_ok = (lanei <= tok_r) & (lanei < lq)

    wlen = win_ref.shape[1]
    wi = lax.broadcasted_iota(jnp.int32, (nrow, wlen), 1)
    tok_w = lax.broadcasted_iota(jnp.int32, (nrow, wlen), 0) % lq
    dist_w = wlen + tok_w - wi
    s_w = _dotf(qb, win_ref[0:LANE, :].astype(BF16))
    s_w = jnp.where(dist_w < WINDOW, s_w - slope_r * dist_w.astype(F32), NEG)
    kwn = _pad_rows(kvn[:, 4 * LANE:5 * LANE], LANE).astype(BF16)
    s_wn = jnp.where(new_ok, _dot_nt(qb, kwn) - slope_r * (tok_r - lanei).astype(F32), NEG)
    m = jnp.maximum(jnp.max(s_w, axis=1, keepdims=True), jnp.max(s_wn, axis=1, keepdims=True))
    p_w = jnp.exp(s_w - m)
    p_wn = jnp.exp(s_wn - m)
    vwn = _pad_rows(kvn[:, 5 * LANE:6 * LANE], LANE).astype(BF16)
    o_w = (_dot_nt(p_w.astype(BF16), win_ref[LANE:2 * LANE, :].astype(BF16)) + _dotf(p_wn.astype(BF16), vwn)) / (
        jnp.sum(p_w, axis=1, keepdims=True) + jnp.sum(p_wn, axis=1, keepdims=True))

    win = win_ref[...]
    new_t = _pad_rows(kvn[:, 4 * LANE:6 * LANE], LANE).T
    new_t = pltpu.roll(new_t, LANE - lq, axis=1)
    tail = jnp.concatenate([jnp.zeros((2 * LANE, wlen - LANE), F32), new_t], axis=1)
    lane_w = lax.broadcasted_iota(jnp.int32, win.shape, 1)
    nwin_ref[...] = jnp.where(lane_w < wlen - lq, pltpu.roll(win, wlen - lq, axis=1), tail)
    yield True

    picked = forced
    for i in range(max(NSA_FREE_ROUNDS, npages)):
        if i < npages:
            stage_selected_page(i)
        if i < NSA_FREE_ROUNDS:
            picked, score = _topk_round(picked, score, blk, LANE)
        yield True
    sel = (picked & (blk <= cur)).astype(F32).T

    bias = jnp.where(sel > 0.5, (NSA_BLOCK * slope_r) * lanei.astype(F32), NEG)
    lhs = jnp.concatenate([bias.astype(BF16), qb], axis=1)
    jrow = (lax.broadcasted_iota(jnp.int32, (1, nkeys), 1) % NSA_BLOCK).astype(F32)
    s_p = _dotf(lhs, rt_s[...]) + slope_r * jrow
    bias_new = jnp.sum(jnp.where(lanei == new_blk, bias, 0.0), axis=1, keepdims=True)
    jnew = (past % NSA_BLOCK + lanei).astype(F32)
    kn = _pad_rows(kvn[:, 2 * LANE:3 * LANE], LANE).astype(BF16)
    s_n = jnp.where(new_ok, _dot_nt(qb, kn) + bias_new + slope_r * jnew, NEG)
    yield True
    m = jnp.maximum(jnp.max(s_p, axis=1, keepdims=True), jnp.max(s_n, axis=1, keepdims=True))
    p_p = jnp.exp(s_p - m)
    p_n = jnp.exp(s_n - m)
    vn = _pad_rows(kvn[:, 3 * LANE:4 * LANE], LANE).astype(BF16)
    o_s = (_dot_nt(p_p.astype(BF16), vt_s[...]) + _dotf(p_n.astype(BF16), vn)) / (
        jnp.sum(p_p, axis=1, keepdims=True) + jnp.sum(p_n, axis=1, keepdims=True))

    small = sm_ref[...]
    gate = [jax.nn.sigmoid(jnp.concatenate([small[:, 3 * h + c:3 * h + c + 1] for h in range(NSA_HEADS)], axis=0))
            for c in range(3)]
    o = gate[0] * o_c + gate[1] * o_s + gate[2] * o_w
    for h in range(NSA_HEADS):
        o_ref[:, h * LANE:(h + 1) * LANE] = o[h * lq:(h + 1) * lq].astype(o_ref.dtype)


def _pool_weights_t(wck, wcv, past):
    pos = jnp.arange(past)
    onehot = (jnp.arange(LANE)[:, None] == (pos // NSA_BLOCK)[None, :]).astype(F32)
    w = jnp.stack([wck, wcv])[:, :, pos % NSA_BLOCK]
    return (w[:, :, None, :] * onehot[None, None]).astype(BF16)


def _nsa_sample(page_table, qh, kv, small, cache_t, win_t, wt, prev_win, *, layer, lq, past):
    nprev = 0 if prev_win is None else prev_win.shape[0]
    assert nprev == layer
    nb, npages = page_table.shape
    page = cache_t.shape[-1]
    wlen = win_t.shape[-1]
    ns = NSA_SAMPLE_SEQS
    assert nb % ns == 0
    tokmap = lambda b, pt: (b, 0)

    def page_spec(j, i):
        return pl.BlockSpec((None, None, 4 * LANE, page), lambda b, pt: (layer, pt[ns * b + j, i], 0, 0))

    grid_spec = pltpu.PrefetchScalarGridSpec(
        num_scalar_prefetch=1,
        grid=(nb // ns,),
        in_specs=[pl.BlockSpec((ns * lq, 1024), tokmap), pl.BlockSpec((ns * lq, 768), tokmap),
                  pl.BlockSpec((ns * lq, LANE), tokmap)]
        + [page_spec(j, i) for j in range(ns) for i in range(npages)]
        + [pl.BlockSpec((None, ns, 2 * LANE, wlen), lambda b, pt: (layer, b, 0, 0)),
           pl.BlockSpec(wt.shape, lambda b, pt: (0, 0, 0, 0))]
        + ([pl.BlockSpec((nprev, ns, 2 * LANE, wlen), lambda b, pt: (0, b, 0, 0))] if nprev else []),
        out_specs=[pl.BlockSpec((ns * lq, 1024), tokmap),
                   pl.BlockSpec((nprev + 1, ns, 2 * LANE, wlen), lambda b, pt: (0, b, 0, 0))],
        scratch_shapes=[pltpu.VMEM((ns, LANE, past), BF16), pltpu.VMEM((ns, LANE, past), BF16),
                        pltpu.VMEM((ns, 2 * LANE, past), BF16), pltpu.VMEM((ns, LANE, past), BF16)])
    return pl.pallas_call(
        functools.partial(_nsa_sample_kernel, past=past, lq=lq, npages=npages, page=page, nprev=nprev),
        grid_spec=grid_spec,
        out_shape=[jax.ShapeDtypeStruct((nb * lq, 1024), BF16),
                   jax.ShapeDtypeStruct((nprev + 1, nb, 2 * LANE, wlen), F32)],
        compiler_params=_cparams(("arbitrary",)),
        name="nsa_sample",
    )(page_table, qh, kv, small, *([cache_t] * (ns * npages)), win_t, wt, *([prev_win] if nprev else []))


GDN_ROWS = 64


def _softplus(x):
    return jnp.maximum(x, 0.0) + jnp.log(1.0 + jnp.exp(-jnp.abs(x)))


def _unit_lower_inverse(a, seg):
    n = a.shape[0]
    eye = (lax.broadcasted_iota(jnp.int32, (n, n), 0) == lax.broadcasted_iota(jnp.int32, (n, n), 1)).astype(F32)
    inv = eye - a
    pw = a
    span = 2
    while span < seg:
        pw = _dot_hi(pw, pw)
        inv = inv + _dot_hi(inv, pw)
        span *= 2
    return inv


def _gdn_kernel(x_ref, hist_ref, sm_ref, z_ref, cw_ref, alog_ref, dtb_ref, nw_ref, s0_ref,
                o_ref, s_ref, *, seg, zero_first_hist):
    c = pl.program_id(1)
    rows = GDN_ROWS
    nseg = rows // seg

    @pl.when(c == 0)
    def _():
        s_ref[...] = s0_ref[...]

    x = x_ref[...]
    hist = hist_ref[...]
    if zero_first_hist:
        hist = jnp.where(c == 0, 0.0, hist)
    tpos = lax.broadcasted_iota(jnp.int32, x.shape, 0) % seg
    cw = cw_ref[...]
    y = x * cw[DN_CONV_W - 1:DN_CONV_W]
    for sh in range(1, DN_CONV_W):
        xs = jnp.where(tpos >= sh, pltpu.roll(x, sh, axis=0), pltpu.roll(hist, sh, axis=0))
        y = y + xs * cw[DN_CONV_W - 1 - sh:DN_CONV_W - sh]
    y = y * jax.nn.sigmoid(y)

    ri = lax.broadcasted_iota(jnp.int32, (rows, rows), 0)
    ci = lax.broadcasted_iota(jnp.int32, (rows, rows), 1)
    same = (ri // seg) == (ci // seg)
    tri = same & (ci <= ri)
    strict = same & (ci < ri)
    trif = tri.astype(F32)
    small = sm_ref[...]
    z = z_ref[...]
    heads = range(DN_HEADS)
    segs = range(nseg)
    hs = lambda h: slice(h * HEAD_DIM, (h + 1) * HEAD_DIM)
    yqs = [y[:, hs(h)] for h in heads]
    yks = [y[:, DN_WIDTH + h * HEAD_DIM:DN_WIDTH + (h + 1) * HEAD_DIM] for h in heads]
    vs = [y[:, 2 * DN_WIDTH + h * HEAD_DIM:2 * DN_WIDTH + (h + 1) * HEAD_DIM] for h in heads]
    qs = [a * lax.rsqrt(jnp.sum(a * a, axis=-1, keepdims=True) + 1e-6) * HEAD_DIM ** -0.5 for a in yqs]
    ks = [a * lax.rsqrt(jnp.sum(a * a, axis=-1, keepdims=True) + 1e-6) for a in yks]
    betas = [jax.nn.sigmoid(small[:, 28 + h:29 + h]) for h in heads]
    gs = [-jnp.exp(alog_ref[:, h:h + 1]) * _softplus(small[:, 24 + h:25 + h] + dtb_ref[:, h:h + 1]) for h in heads]
    trifs = _split(trif)
    gcbs = [_dot3s(trifs, _split(jnp.broadcast_to(g, (rows, rows)))) for g in gs]
    decays = [jnp.where(tri, jnp.exp(jnp.where(tri, gcb - gcb.T, 0.0)), 0.0) for gcb in gcbs]
    kbs = [ks[h] * betas[h] for h in heads]
    amats = [jnp.where(strict, _dot3_nt(kbs[h], ks[h]) * decays[h], 0.0) for h in heads]
    aqks = [jnp.where(tri, _dot3_nt(qs[h], ks[h]) * decays[h], 0.0) for h in heads]
    eye = (ri == ci).astype(F32)
    invs = [eye - a for a in amats]
    pws = amats
    span = 2
    while span < seg:
        pwsp = [_split(p) for p in pws]
        pws = [_dot3s(p, p) for p in pwsp]
        invs = [invs[h] + _dot3(invs[h], pws[h]) for h in heads]
        span *= 2
    egcs = [jnp.exp(gcb[:, :HEAD_DIM]) for gcb in gcbs]
    sols = [_dot3(invs[h], jnp.concatenate([vs[h] * betas[h], kbs[h] * egcs[h]], axis=1)) for h in heads]
    us = [s[:, :HEAD_DIM] for s in sols]
    ws = [s[:, HEAD_DIM:] for s in sols]
    qgs = [qs[h] * egcs[h] for h in heads]
    rsl = lambda s: slice(s * seg, (s + 1) * seg)
    sts = [[s_ref[s, h] for s in segs] for h in heads]
    stsp = [[_split(st) for st in row] for row in sts]
    vns = [jnp.concatenate([us[h][rsl(s)] - _dot3s(_split(ws[h][rsl(s)]), stsp[h][s]) for s in segs], axis=0)
           for h in heads]
    oqs = [jnp.concatenate([_dot3s(_split(qgs[h][rsl(s)]), stsp[h][s]) for s in segs], axis=0) for h in heads]
    vnsp = [_split(v) for v in vns]
    os_ = [oqs[h] + _dot3s(_split(aqks[h]), vnsp[h]) for h in heads]
    glasts = [[gcbs[h][(s + 1) * seg - 1:(s + 1) * seg, :HEAD_DIM] for s in segs] for h in heads]
    gl_rows = [jnp.concatenate([jnp.broadcast_to(gl, (seg, HEAD_DIM)) for gl in glasts[h]], axis=0) for h in heads]
    kdts = [(ks[h] * jnp.exp(gl_rows[h] - gcbs[h][:, :HEAD_DIM])).T for h in heads]
    for h in heads:
        for s in segs:
            kds = kdts[h] if nseg == 1 else jnp.where((ci // seg) == s, kdts[h], 0.0)
            s_ref[s, h] = sts[h][s] * jnp.exp(glasts[h][s]) + _dot3s(_split(kds), vnsp[h])
    outs = []
    for h in heads:
        o = os_[h] * lax.rsqrt(jnp.mean(os_[h] * os_[h], axis=-1, keepdims=True) + NORM_EPS) * nw_ref[...]
        zh = z[:, hs(h)]
        outs.append(o * (zh * jax.nn.sigmoid(zh)))
    o_ref[...] = jnp.concatenate(outs, axis=1).astype(o_ref.dtype)


def _gdn(dnqkv, hist, small, z, cw, alog, dtb, nw, s0, *, seg, nb, nchunk, row0, hist_map, zero_first_hist):
    nseg = GDN_ROWS // seg
    rmap = lambda b, c: (row0 + b * nchunk + c, 0)
    const = lambda b, c: (0, 0)
    smap = lambda b, c: (b, 0, 0, 0)
    sblock = (nseg, DN_HEADS, HEAD_DIM, HEAD_DIM)
    return pl.pallas_call(
        functools.partial(_gdn_kernel, seg=seg, zero_first_hist=zero_first_hist),
        grid=(nb, nchunk),
        in_specs=[pl.BlockSpec((GDN_ROWS, 768), rmap), pl.BlockSpec((GDN_ROWS, 768), hist_map),
                  pl.BlockSpec((GDN_ROWS, LANE), rmap), pl.BlockSpec((GDN_ROWS, 256), rmap),
                  pl.BlockSpec(cw.shape, const), pl.BlockSpec(alog.shape, const),
                  pl.BlockSpec(dtb.shape, const), pl.BlockSpec(nw.shape, const),
                  pl.BlockSpec(sblock, smap)],
        out_specs=[pl.BlockSpec((GDN_ROWS, 256), lambda b, c: (b * nchunk + c, 0)),
                   pl.BlockSpec(sblock, smap)],
        out_shape=[jax.ShapeDtypeStruct((nb * nchunk * GDN_ROWS, 256), BF16),
                   jax.ShapeDtypeStruct(s0.shape, F32)],
        compiler_params=_cparams(("parallel", "arbitrary")),
        name="gdn",
    )(dnqkv, hist, small, z, cw, alog, dtb, nw, s0)


def _split(a):
    hi = a.astype(BF16)
    return hi, (a - hi.astype(F32)).astype(BF16)


def _dot3s(a, b):
    return _dotf(a[0], b[0]) + (_dotf(a[0], b[1]) + _dotf(a[1], b[0]))


def _dot3(a, b):
    return _dot3s(_split(a), _split(b))


def _dot3_nt(a, b):
    a, b = _split(a), _split(b)
    return _dot_nt(a[0], b[0]) + (_dot_nt(a[0], b[1]) + _dot_nt(a[1], b[0]))


def _unit_lower_inverse3(a, seg):
    n = a.shape[0]
    eye = (lax.broadcasted_iota(jnp.int32, (n, n), 0) == lax.broadcasted_iota(jnp.int32, (n, n), 1)).astype(F32)
    inv = eye - a
    pw = a
    span = 2
    while span < seg:
        pws = _split(pw)
        pw = _dot3s(pws, pws)
        inv = inv + _dot3s(_split(inv), _split(pw))
        span *= 2
    return inv


GDN_PREP_CHUNKS = 4


def _gdn_prep_kernel(x_ref, hist_ref, sm_ref, cw_ref, alog_ref, dtb_ref,
                     u_ref, w_ref, qg_ref, kd_ref, aqk_ref, egl_ref):
    c = pl.program_id(1)
    x = x_ref[...]
    rows = x.shape[0]
    hist = jnp.where(c == 0, 0.0, hist_ref[...])
    row8 = lax.broadcasted_iota(jnp.int32, hist.shape, 0)
    cw = cw_ref[...]
    y = x * cw[DN_CONV_W - 1:DN_CONV_W]
    for sh in range(1, DN_CONV_W):
        xs = pltpu.roll(x, sh, axis=0)
        top = jnp.where(row8 < sh, pltpu.roll(hist, sh, axis=0), xs[0:8])
        xs = jnp.concatenate([top, xs[8:]], axis=0)
        y = y + xs * cw[DN_CONV_W - 1 - sh:DN_CONV_W - sh]
    y = y * jax.nn.sigmoid(y)

    C = GDN_ROWS
    ri = lax.broadcasted_iota(jnp.int32, (C, C), 0)
    ci = lax.broadcasted_iota(jnp.int32, (C, C), 1)
    tri = ci <= ri
    strict = ci < ri
    trif = tri.astype(F32)
    small = sm_ref[...]
    nch = rows // C
    units = [(ch, h) for ch in range(nch) for h in range(DN_HEADS)]
    each = lambda f, *ls: [f(*a) for a in zip(*ls)]

    def qkv_of(ch, h):
        rs = slice(ch * C, (ch + 1) * C)
        yq = y[rs, h * HEAD_DIM:(h + 1) * HEAD_DIM]
        yk = y[rs, DN_WIDTH + h * HEAD_DIM:DN_WIDTH + (h + 1) * HEAD_DIM]
        v = y[rs, 2 * DN_WIDTH + h * HEAD_DIM:2 * DN_WIDTH + (h + 1) * HEAD_DIM]
        q = yq * lax.rsqrt(jnp.sum(yq * yq, axis=-1, keepdims=True) + 1e-6) * HEAD_DIM ** -0.5
        k = yk * lax.rsqrt(jnp.sum(yk * yk, axis=-1, keepdims=True) + 1e-6)
        beta = jax.nn.sigmoid(small[rs, 28 + h:29 + h])
        g = -jnp.exp(alog_ref[:, h:h + 1]) * _softplus(small[rs, 24 + h:25 + h] + dtb_ref[:, h:h + 1])
        return q, k, v, beta, g

    qs_, ks_, vs_, betas, gs = zip(*[qkv_of(ch, h) for ch, h in units])
    trifs = _split(trif)
    gcbs = [_dot3s(trifs, _split(jnp.broadcast_to(g, (C, C)))) for g in gs]
    kbs = each(lambda k, b: k * b, ks_, betas)
    ksp = [_split(k) for k in ks_]
    kbsp = [_split(kb) for kb in kbs]
    qsp = [_split(q) for q in qs_]
    nt3 = lambda a, b: _dot_nt(a[0], b[0]) + (_dot_nt(a[0], b[1]) + _dot_nt(a[1], b[0]))
    kks = each(nt3, kbsp, ksp)
    qks = each(nt3, qsp, ksp)
    decays = [jnp.where(tri, jnp.exp(jnp.where(tri, gcb - gcb.T, 0.0)), 0.0) for gcb in gcbs]
    amats = each(lambda kk, d: jnp.where(strict, kk * d, 0.0), kks, decays)
    eye = (ri == ci).astype(F32)
    invs = [eye - a for a in amats]
    pws = amats
    span = 2
    while span < C:
        pwsp = [_split(p) for p in pws]
        pws = [_dot3s(p, p) for p in pwsp]
        pwsp = [_split(p) for p in pws]
        invs = each(lambda i, p: i + _dot3s(_split(i), p), invs, pwsp)
        span *= 2
    egcs = [jnp.exp(gcb) for gcb in gcbs]
    rhss = each(lambda v, b, kb, e: jnp.concatenate([v * b, kb * e], axis=1), vs_, betas, kbs, egcs)
    sols = each(_dot3, invs, rhss)
    glasts = [gcb[C - 1:C, :] for gcb in gcbs]
    qgs = each(lambda q, e: q * e, qs_, egcs)
    kds = each(lambda k, gl, gcb: k * jnp.exp(gl - gcb), ks_, glasts, gcbs)
    aqks = each(lambda qk, d: jnp.where(tri, qk * d, 0.0), qks, decays)
    for ch in range(nch):
        rs = slice(ch * C, (ch + 1) * C)
        un = range(ch * DN_HEADS, (ch + 1) * DN_HEADS)
        u_ref[rs, :] = jnp.concatenate([sols[i][:, :HEAD_DIM] for i in un], axis=1)
        w_ref[rs, :] = jnp.concatenate([sols[i][:, HEAD_DIM:] for i in un], axis=1)
        qg_ref[rs, :] = jnp.concatenate([qgs[i] for i in un], axis=1)
        kd_ref[rs, :] = jnp.concatenate([kds[i] for i in un], axis=1)
        aqk_ref[rs, :] = jnp.concatenate([aqks[i] for i in un], axis=1)
        egl_ref[ch * 8:(ch + 1) * 8, :] = jnp.concatenate(
            [jnp.broadcast_to(jnp.exp(glasts[i]), (8, HEAD_DIM)) for i in un], axis=1)


def _gdn_scan_kernel(u_ref, w_ref, qg_ref, kd_ref, aqk_ref, egl_ref, z_ref, nw_ref, s0_ref, o_ref, s_ref):
    c = pl.program_id(0)

    @pl.when(c == 0)
    def _():
        s_ref[...] = s0_ref[...]

    nb = u_ref.shape[0]
    units = [(b, slice(h * HEAD_DIM, (h + 1) * HEAD_DIM), h) for b in range(nb) for h in range(DN_HEADS)]
    sts = [s_ref[b, h] for b, _, h in units]
    stsp = [_split(s) for s in sts]
    kdts = [_split(kd_ref[b, :, sl].T) for b, sl, _ in units]
    vnews = [u_ref[b, :, sl] - _dot3s(_split(w_ref[b, :, sl]), st) for (b, sl, _), st in zip(units, stsp)]
    oqs = [_dot3s(_split(qg_ref[b, :, sl]), st) for (b, sl, _), st in zip(units, stsp)]
    vsp = [_split(v) for v in vnews]
    os_ = [oq + _dot3s(_split(aqk_ref[b, :, sl]), vs) for (b, sl, _), oq, vs in zip(units, oqs, vsp)]
    for (b, sl, h), st, kdt, vs in zip(units, sts, kdts, vsp):
        s_ref[b, h] = st * egl_ref[b, 0:1, sl] + _dot3s(kdt, vs)
    outs = []
    for (b, sl, _), o in zip(units, os_):
        o = o * lax.rsqrt(jnp.mean(o * o, axis=-1, keepdims=True) + NORM_EPS) * nw_ref[...]
        zh = z_ref[b, :, sl]
        outs.append(o * (zh * jax.nn.sigmoid(zh)))
    for b in range(nb):
        o_ref[b] = jnp.concatenate(outs[b * DN_HEADS:(b + 1) * DN_HEADS], axis=1).astype(o_ref.dtype)


def _gdn_prompt(dnqkv, small, z, cw, alog, dtb, nw, s0, *, nb, lq):
    cb = GDN_PREP_CHUNKS
    rows = cb * GDN_ROWS
    nstep = lq // rows
    nchunk = lq // GDN_ROWS
    rmap = lambda b, c: (b * nstep + c, 0)
    const = lambda b, c: (0, 0)
    wide = jax.ShapeDtypeStruct((nb * lq, DN_WIDTH), F32)
    u, w, qg, kd, aqk, egl = pl.pallas_call(
        _gdn_prep_kernel,
        grid=(nb, nstep),
        in_specs=[pl.BlockSpec((rows, 768), rmap),
                  pl.BlockSpec((8, 768), lambda b, c: (jnp.maximum((b * nstep + c) * (rows // 8) - 1, 0), 0)),
                  pl.BlockSpec((rows, LANE), rmap),
                  pl.BlockSpec(cw.shape, const), pl.BlockSpec(alog.shape, const), pl.BlockSpec(dtb.shape, const)],
        out_specs=[pl.BlockSpec((rows, DN_WIDTH), rmap)] * 5 + [pl.BlockSpec((cb * 8, DN_WIDTH), rmap)],
        out_shape=[wide] * 5 + [jax.ShapeDtypeStruct((nb * nchunk * 8, DN_WIDTH), F32)],
        compiler_params=_cparams(("parallel", "parallel")),
        name="gdn_prep",
    )(dnqkv, dnqkv, small, cw, alog, dtb)
    r3 = lambda a: a.reshape(nb, -1, DN_WIDTH)
    cmap = lambda c: (0, c, 0)
    full = lambda a: pl.BlockSpec(a.shape, lambda c: (0,) * a.ndim)
    o, s = pl.pallas_call(
        _gdn_scan_kernel,
        grid=(nchunk,),
        in_specs=[pl.BlockSpec((nb, GDN_ROWS, DN_WIDTH), cmap)] * 5
        + [pl.BlockSpec((nb, 8, DN_WIDTH), cmap), pl.BlockSpec((nb, GDN_ROWS, DN_WIDTH), cmap),
           full(nw), full(s0)],
        out_specs=[pl.BlockSpec((nb, GDN_ROWS, DN_WIDTH), cmap), full(s0)],
        out_shape=[jax.ShapeDtypeStruct((nb, lq, DN_WIDTH), BF16), jax.ShapeDtypeStruct(s0.shape, F32)],
        compiler_params=_cparams(("arbitrary",)),
        name="gdn_scan",
    )(r3(u), r3(w), r3(qg), r3(kd), r3(aqk), r3(egl), z[:nb * lq].reshape(nb, lq, DN_WIDTH), nw, s0)
    return o.reshape(nb * lq, DN_WIDTH), s


def _s5_discretize(lre_ref, lim_ref, lstep_ref):
    lr = lre_ref[...]
    li = lim_ref[...]
    dt = jnp.exp(lstep_ref[...])
    mag = jnp.exp(lr * dt)
    ar = mag * jnp.cos(li * dt)
    ai = mag * jnp.sin(li * dt)
    den = lr * lr + li * li
    fr = ((ar - 1.0) * lr + ai * li) / den
    fi = (ai * lr - (ar - 1.0) * li) / den
    return ar, ai, fr, fi


def _s5_kernel(*refs, R, S, with_y):
    if with_y:
        (u_ref, h0r_ref, h0i_ref, lre_ref, lim_ref, lstep_ref, bre_ref, bim_ref, cre_ref, cim_ref, d_ref,
         wglu_ref, y_ref, hr_ref, hi_ref, ar_s, ai_s, bbr_s, bbi_s, xr_s, xi_s) = refs
    else:
        (u_ref, h0r_ref, h0i_ref, lre_ref, lim_ref, lstep_ref, bre_ref, bim_ref,
         hr_ref, hi_ref, ar_s, ai_s, bbr_s, bbi_s, xr_s, xi_s) = refs
    i = pl.program_id(0)

    @pl.when(i == 0)
    def _():
        ar, ai, fr, fi = _s5_discretize(lre_ref, lim_ref, lstep_ref)
        ar_s[...] = ar
        ai_s[...] = ai
        bbr_s[...] = (fr * bre_ref[...] - fi * bim_ref[...]).astype(BF16)
        bbi_s[...] = (fr * bim_ref[...] + fi * bre_ref[...]).astype(BF16)
        hr_ref[...] = h0r_ref[...]
        hi_ref[...] = h0i_ref[...]

    u = u_ref[...]
    ub = u.astype(BF16)
    xr_s[...] = _dotf(ub, bbr_s[...])
    xi_s[...] = _dotf(ub, bbi_s[...])
    ar = jnp.broadcast_to(ar_s[...], (R, S5_LANES))
    ai = jnp.broadcast_to(ai_s[...], (R, S5_LANES))

    def advance(hr, hi, off):
        nr = ar * hr - ai * hi + xr_s[pl.ds(off, R), :]
        ni = ar * hi + ai * hr + xi_s[pl.ds(off, R), :]
        xr_s[pl.ds(off, R), :] = nr
        xi_s[pl.ds(off, R), :] = ni

    advance(hr_ref[...], hi_ref[...], 0)

    def step(s, carry):
        prev = pl.multiple_of((s - 1) * R, R)
        advance(xr_s[pl.ds(prev, R), :], xi_s[pl.ds(prev, R), :], pl.multiple_of(s * R, R))
        return carry

    lax.fori_loop(1, S, step, 0)
    hr_ref[...] = xr_s[pl.ds((S - 1) * R, R), :]
    hi_ref[...] = xi_s[pl.ds((S - 1) * R, R), :]
    if with_y:
        y = _dotf(xr_s[...].astype(BF16), cre_ref[...]) - _dotf(xi_s[...].astype(BF16), cim_ref[...])
        y = y + d_ref[...] * u
        gl = jax.nn.gelu(y)
        y_ref[...] = (gl * jax.nn.sigmoid(_dotf(gl.astype(BF16), wglu_ref[...]))).astype(y_ref.dtype)


def _s5_scan(u_rows, h0r, h0i, prm, *, R, S, with_y):
    n = u_rows.shape[0]
    rows = R * S
    const = lambda i: (0, 0)
    rmap = lambda i: (i, 0)
    ins = [u_rows, h0r, h0i, prm["lre"], prm["lim"], prm["lstep"], prm["bre"], prm["bim"]]
    if with_y:
        ins += [prm["cre"], prm["cim"], prm["d"], prm["wglu"]]
    in_specs = [pl.BlockSpec((rows, S5_WIDTH), rmap)] + [pl.BlockSpec(a.shape, const) for a in ins[1:]]
    st_spec = pl.BlockSpec((R, S5_LANES), const)
    st_shape = jax.ShapeDtypeStruct((R, S5_LANES), F32)
    out_specs = [st_spec, st_spec]
    out_shape = [st_shape, st_shape]
    if with_y:
        out_specs = [pl.BlockSpec((rows, S5_WIDTH), rmap)] + out_specs
        out_shape = [jax.ShapeDtypeStruct((n, S5_WIDTH), BF16)] + out_shape
    return pl.pallas_call(
        functools.partial(_s5_kernel, R=R, S=S, with_y=with_y),
        grid=(n // rows,),
        in_specs=in_specs,
        out_specs=out_specs,
        out_shape=out_shape,
        scratch_shapes=[pltpu.VMEM((1, S5_LANES), F32), pltpu.VMEM((1, S5_LANES), F32),
                        pltpu.VMEM((S5_WIDTH, S5_LANES), BF16), pltpu.VMEM((S5_WIDTH, S5_LANES), BF16),
                        pltpu.VMEM((rows, S5_LANES), F32), pltpu.VMEM((rows, S5_LANES), F32)],
        compiler_params=_cparams(("arbitrary",)),
        name="s5_scan",
    )(*ins)


def _s5_carry_kernel(er_ref, ei_ref, lre_ref, lim_ref, lstep_ref, ir_ref, ii_ref, fr_ref, fi_ref, *, nseg, nsteps):
    ar, ai, _, _ = _s5_discretize(lre_ref, lim_ref, lstep_ref)
    pr, pi_ = ar, ai
    n = 1
    while n < nsteps:
        pr, pi_ = pr * pr - pi_ * pi_, 2.0 * pr * pi_
        n *= 2
    for b in range(er_ref.shape[0] // nseg):
        cr = jnp.zeros((1, S5_LANES), F32)
        ci = jnp.zeros((1, S5_LANES), F32)
        for s in range(nseg):
            r = b * nseg + s
            ir_ref[r:r + 1, :] = cr
            ii_ref[r:r + 1, :] = ci
            er = er_ref[r:r + 1, :]
            ei = ei_ref[r:r + 1, :]
            cr, ci = pr * cr - pi_ * ci + er, pr * ci + pi_ * cr + ei
        fr_ref[b:b + 1, :] = cr
        fi_ref[b:b + 1, :] = ci


def _s5_carry(er, ei, prm, *, nseg, nsteps):
    assert nsteps & (nsteps - 1) == 0
    nb = er.shape[0] // nseg
    full = lambda a: pl.BlockSpec(a.shape, lambda: (0,) * a.ndim)
    ins = [er, ei, prm["lre"], prm["lim"], prm["lstep"]]
    outs = [jax.ShapeDtypeStruct(er.shape, F32)] * 2 + [jax.ShapeDtypeStruct((nb, S5_LANES), F32)] * 2
    return pl.pallas_call(
        functools.partial(_s5_carry_kernel, nseg=nseg, nsteps=nsteps),
        in_specs=[full(a) for a in ins],
        out_specs=[full(o) for o in outs],
        out_shape=outs,
        name="s5_carry",
    )(*ins)


S5_SEGS = 8


def _s5_prep(lre, lim, lstep, bre, bim, cre, cim, d, wglu):
    eye = jnp.eye(S5_GROUPS, dtype=F32)
    bexp = lambda b: jnp.einsum("gpc,gh->gchp", b, eye).reshape(S5_WIDTH, S5_LANES)
    cexp = lambda c: jnp.einsum("gcp,gh->gphc", c, eye).reshape(S5_LANES, S5_WIDTH).astype(BF16)
    return {"lre": lre.reshape(1, S5_LANES), "lim": lim.reshape(1, S5_LANES),
            "lstep": jnp.repeat(lstep, S5_STATE).reshape(1, S5_LANES),
            "bre": bexp(bre), "bim": bexp(bim), "cre": cexp(cre), "cim": cexp(cim),
            "d": d.reshape(1, S5_WIDTH), "wglu": wglu.astype(BF16)}


def _s5_prompt(u, prm, nb, lq, steps_per_tile=32):
    nsteps = lq // S5_SEGS
    R = nb * S5_SEGS
    u_rows = u.reshape(nb, S5_SEGS, nsteps, S5_WIDTH).transpose(2, 0, 1, 3).reshape(nsteps * R, S5_WIDTH)
    zero = jnp.zeros((R, S5_LANES), F32)
    er, ei = _s5_scan(u_rows, zero, zero, prm, R=R, S=steps_per_tile, with_y=False)
    ir, ii, fr, fi = _s5_carry(er, ei, prm, nseg=S5_SEGS, nsteps=nsteps)
    y_rows, _, _ = _s5_scan(u_rows, ir, ii, prm, R=R, S=steps_per_tile, with_y=True)
    y = y_rows.reshape(nsteps, nb, S5_SEGS, S5_WIDTH).transpose(1, 2, 0, 3).reshape(nb * lq, S5_WIDTH)
    return y, fr, fi


def _s5_sample(u, h0r, h0i, prm, nb, lq, steps_per_tile=4):
    u_rows = u.reshape(nb, lq, S5_WIDTH).transpose(1, 0, 2).reshape(lq * nb, S5_WIDTH)
    y_rows, fr, fi = _s5_scan(u_rows, h0r, h0i, prm, R=nb, S=steps_per_tile, with_y=True)
    y = y_rows.reshape(lq, nb, S5_WIDTH).transpose(1, 0, 2).reshape(nb * lq, S5_WIDTH)
    return y, fr, fi


def _sample_conv_hist(buf):
    nb = buf.shape[0]
    blk = buf.reshape(nb // 8, 8, DN_CONV_W - 1, 768)
    blk = jnp.roll(blk, -1, axis=1)
    blk = jnp.pad(blk, ((0, 0), (0, 0), (8 - (DN_CONV_W - 1), 0), (0, 0)))
    return blk.reshape(nb * 8, 768)


def _rms(x, g):
    return x * lax.rsqrt(jnp.mean(x * x, axis=-1, keepdims=True) + NORM_EPS) * g


def _mlp_kernel(x_ref, on_ref, od_ref, os_ref, wn_ref, wd_ref, ws_ref, ln2_ref, up_ref, dn_ref, lnf_ref,
                o_ref, h2_s, *, final_norm):
    j = pl.program_id(1)

    @pl.when(j == 0)
    def _():
        x1 = (x_ref[...] + _dotf(on_ref[...], wn_ref[...]) + _dotf(od_ref[...], wd_ref[...])
              + _dotf(os_ref[...], ws_ref[...]))
        o_ref[...] = x1
        h2_s[...] = _rms(x1, ln2_ref[...]).astype(BF16)

    a = jnp.maximum(_dotf(h2_s[...], up_ref[...]), 0.0)
    o_ref[...] += _dotf((a * a).astype(BF16), dn_ref[...])

    if final_norm:
        @pl.when(j == pl.num_programs(1) - 1)
        def _():
            o_ref[...] = _rms(o_ref[...], lnf_ref[...])


def _mlp(x, o_nsa, o_dn, o_s5, wn, wd, ws, ln2, up, dn, lnf, *, final_norm, tm=512, tf=1024):
    n = x.shape[0]
    row = lambda i, j: (i, 0)
    const = lambda i, j: (0, 0)
    return pl.pallas_call(
        functools.partial(_mlp_kernel, final_norm=final_norm),
        grid=(n // tm, D_FF // tf),
        in_specs=[pl.BlockSpec((tm, D_MODEL), row), pl.BlockSpec((tm, 1024), row),
                  pl.BlockSpec((tm, DN_WIDTH), row), pl.BlockSpec((tm, S5_WIDTH), row),
                  pl.BlockSpec(wn.shape, const), pl.BlockSpec(wd.shape, const), pl.BlockSpec(ws.shape, const),
                  pl.BlockSpec((1, D_MODEL), const),
                  pl.BlockSpec((D_MODEL, tf), lambda i, j: (0, j)),
                  pl.BlockSpec((tf, D_MODEL), lambda i, j: (j, 0)),
                  pl.BlockSpec((1, D_MODEL), const)],
        out_specs=pl.BlockSpec((tm, D_MODEL), row),
        out_shape=jax.ShapeDtypeStruct((n, D_MODEL), F32),
        scratch_shapes=[pltpu.VMEM((tm, D_MODEL), BF16)],
        compiler_params=_cparams(("parallel", "arbitrary")),
        name="out_mlp",
    )(x, o_nsa, o_dn, o_s5, wn, wd, ws, ln2, up, dn, lnf)


_NSA_W = NSA_HEADS * HEAD_DIM
_KV_W = 6 * NSA_KV_HEADS * HEAD_DIM
_GATE_W = NSA_HEADS * 3
_OFF_KV = _NSA_W
_OFF_GATE = _OFF_KV + _KV_W
_OFF_DN = _OFF_GATE + _GATE_W
_OFF_A = _OFF_DN + 3 * DN_WIDTH
_OFF_B = _OFF_A + DN_HEADS
_OFF_Z = _OFF_B + DN_HEADS
_OFF_U = _OFF_Z + DN_WIDTH


def _pad_heads(w, axis):
    w = jnp.moveaxis(w, axis, 0).reshape((NSA_HEADS, HEAD_DIM) + w.shape[:axis] + w.shape[axis + 1:])
    out = jnp.zeros((NSA_HEADS, LANE) + w.shape[2:], w.dtype)
    for h in range(NSA_HEADS):
        g = h // NSA_GROUP
        out = out.at[h, g * HEAD_DIM:(g + 1) * HEAD_DIM].set(w[h])
    out = out.reshape((NSA_HEADS * LANE,) + w.shape[2:])
    return jnp.moveaxis(out, 0, axis)


def _layer_weights(w_in, w_out, wck, wcv):
    wq = _pad_heads(w_in[:, :_NSA_W], 1).astype(BF16)
    wkv = w_in[:, _OFF_KV:_OFF_GATE].astype(BF16)
    wdn = w_in[:, _OFF_DN:_OFF_A].astype(BF16)
    wz = w_in[:, _OFF_Z:_OFF_U].astype(BF16)
    wu = w_in[:, _OFF_U:].astype(BF16)
    wsm = jnp.zeros((D_MODEL, LANE), F32)
    wsm = wsm.at[:, :_GATE_W].set(w_in[:, _OFF_GATE:_OFF_DN])
    wsm = wsm.at[:, _GATE_W:_GATE_W + 2 * DN_HEADS].set(w_in[:, _OFF_A:_OFF_Z]).astype(BF16)
    wexp = jnp.concatenate([jnp.repeat(wck.T, HEAD_DIM, axis=1), jnp.repeat(wcv.T, HEAD_DIM, axis=1)], axis=1)
    won = _pad_heads(w_out[:_NSA_W], 0).astype(BF16)
    wod = w_out[_NSA_W:_NSA_W + DN_WIDTH].astype(BF16)
    wos = w_out[_NSA_W + DN_WIDTH:].astype(BF16)
    return wq, wkv, wdn, wz, wu, wsm, wexp, won, wod, wos


def kernel(x_prompt, x_sample, cache_nsa_kv, cache_win_kv, state_dn_conv, state_dn, state_s5_re, state_s5_im,
           page_table, ln1, ln2, ln_f, w_in, w_out, nsa_wck, nsa_wcv, dn_conv_w, dn_a_log, dn_dt_bias,
           dn_norm_w, s5_lambda_re, s5_lambda_im, s5_log_step, s5_b_re, s5_b_im, s5_c_re, s5_c_im, s5_d,
           s5_w_glu, mlp_up, mlp_down):
    nbp, lp, _ = x_prompt.shape
    nbs, ls, _ = x_sample.shape
    depth = w_in.shape[0]
    rows_p = nbp * lp
    rows_s = nbs * ls
    n_phys, page = cache_nsa_kv.shape[1], cache_nsa_kv.shape[2]
    past = page_table.shape[1] * page
    wlen = cache_win_kv.shape[2]
    G, dh = NSA_KV_HEADS, HEAD_DIM
    assert lp % (S5_SEGS * 32) == 0 and nbs % 8 == 0 and ls == 8 and wlen == WINDOW and lp >= WINDOW

    xp = x_prompt.reshape(rows_p, D_MODEL)
    xs = x_sample.reshape(rows_s, D_MODEL)
    cache_t = cache_nsa_kv.transpose(0, 1, 3, 4, 5, 2).reshape(depth, n_phys, 4 * G * dh, page)
    win_t = cache_win_kv.transpose(0, 1, 3, 4, 5, 2).reshape(depth, nbs, 2 * G * dh, wlen)
    outs_p, outs_s = [], []
    kvt_main = nwin_t = None
    for l in range(depth):
        wq, wkv, wdn, wz, wu, wsm, wexp, won, wod, wos = _layer_weights(w_in[l], w_out[l], nsa_wck[l], nsa_wcv[l])
        pw = (ln1[l][None], wq, wkv, wdn, wz, wu, wsm, wexp)
        qh_p, kv_p, dnx_p, z_p, u_p, small_p, pool_p, kvt_main, kvt_win, kaug, vt = _proj(
            xp, *pw, nseq=nbp, prev_kvt=kvt_main, sel_tile=NSA_TQ)
        qh_s, kv_s, dnx_s, z_s, u_s, small_s, _, _, _ = _proj(xs, *pw, nseq=1)

        oc, sel, anyblk = _cmp_prompt(qh_p, pool_p, nbp, lp, NSA_TQ)
        osel = _sel_prompt(qh_p, sel, anyblk, kaug, vt, nbp, lp, NSA_TQ)
        on_p = _win_prompt(qh_p, kv_p, oc, osel, small_p, nbp, lp)
        wt = _pool_weights_t(nsa_wck[l], nsa_wcv[l], past)
        on_s, nwin_t = _nsa_sample(page_table, qh_s, kv_s, small_s, cache_t, win_t, wt, nwin_t,
                                   layer=l, lq=ls, past=past)

        gdn_w = (dn_conv_w[l].T, dn_a_log[l][None], dn_dt_bias[l][None], dn_norm_w[l][None])
        od_p, dn_p = _gdn_prompt(dnx_p, small_p, z_p, *gdn_w, jnp.zeros((nbp, DN_HEADS, dh, dh), F32), nb=nbp, lq=lp)
        od_s, dn_s = _gdn(dnx_s, _sample_conv_hist(state_dn_conv[l]), small_s, z_s, *gdn_w, state_dn[l],
                          seg=ls, nb=rows_s // GDN_ROWS, nchunk=1, row0=0,
                          hist_map=lambda b, c: (b, 0), zero_first_hist=False)

        prm = _s5_prep(s5_lambda_re[l], s5_lambda_im[l], s5_log_step[l], s5_b_re[l], s5_b_im[l],
                       s5_c_re[l], s5_c_im[l], s5_d[l], s5_w_glu[l])
        os_p, s5r_p, s5i_p = _s5_prompt(u_p, prm, nbp, lp)
        os_s, s5r_s, s5i_s = _s5_sample(u_s, state_s5_re[l].reshape(nbs, S5_LANES),
                                        state_s5_im[l].reshape(nbs, S5_LANES), prm, nbs, ls)

        mw = (won, wod, wos, ln2[l][None], mlp_up[l].astype(BF16), mlp_down[l].astype(BF16), ln_f[None])
        xp = _mlp(xp, on_p, od_p, os_p, *mw, final_norm=(l == depth - 1))
        xs = _mlp(xs, on_s, od_s, os_s, *mw, final_norm=(l == depth - 1))

        kv_s6 = kv_s.reshape(nbs, ls, 6, G, dh)
        win_p = kvt_win[:, :, lp - WINDOW:].reshape(nbp, 2, G, dh, WINDOW).transpose(0, 4, 1, 2, 3)
        outs_p.append((win_p, dnx_p.reshape(nbp, lp, 3 * DN_WIDTH)[:, -(DN_CONV_W - 1):], dn_p,
                       s5r_p.reshape(nbp, S5_GROUPS, S5_STATE), s5i_p.reshape(nbp, S5_GROUPS, S5_STATE)))
        outs_s.append((kv_s6[:, :, :4], dnx_s.reshape(nbs, ls, 3 * DN_WIDTH)[:, -(DN_CONV_W - 1):], dn_s,
                       s5r_s.reshape(nbs, S5_GROUPS, S5_STATE), s5i_s.reshape(nbs, S5_GROUPS, S5_STATE)))

    stack = lambda outs, i: jnp.stack([o[i] for o in outs], axis=0)
    kv_prompt = kvt_main.reshape(depth, nbp, 4, G, dh, lp).transpose(0, 1, 5, 2, 3, 4)
    win_sample = nwin_t.reshape(depth, nbs, 2, G, dh, wlen).transpose(0, 1, 5, 2, 3, 4)
    return (xp.reshape(nbp, lp, D_MODEL), xs.reshape(nbs, ls, D_MODEL),
            kv_prompt, stack(outs_s, 0), stack(outs_p, 0), win_sample,
            stack(outs_p, 1), stack(outs_s, 1), stack(outs_p, 2), stack(outs_s, 2),
            stack(outs_p, 3), stack(outs_s, 3), stack(outs_p, 4), stack(outs_s, 4))
```

```python
import functools
import math

import jax
import jax.numpy as jnp
import numpy as np
from jax import lax
from jax.experimental import pallas as pl
from jax.experimental.pallas import tpu as pltpu

F32 = jnp.float32
BF16 = jnp.bfloat16
HIGHEST = lax.Precision.HIGHEST

D_MODEL = 1024
HEAD_DIM = 64
NSA_HEADS = 8
NSA_KV_HEADS = 2
NSA_GROUP = 4
NSA_BLOCK = 64
NSA_TOPK = 16
WINDOW = 512
DN_HEADS = 4
DN_WIDTH = 256
DN_CONV_W = 4
S5_GROUPS = 16
S5_GROUP_CH = 16
S5_STATE = 64
S5_WIDTH = 256
S5_LANES = S5_GROUPS * S5_STATE
D_FF = 4096
NORM_EPS = 1e-6
BIG = 1e9
NEG = -1e30
LANE = 128
NSA_TQ = 256
VMEM_LIMIT = 48 * 1024 * 1024

_NT = (((1,), (1,)), ((), ()))


def _slope(h):
    return 2.0 ** (-(h + 1))


def _cparams(sem):
    return pltpu.CompilerParams(dimension_semantics=sem, vmem_limit_bytes=VMEM_LIMIT)


def _dotf(a, b):
    return jnp.dot(a, b, preferred_element_type=F32)


def _dot_nt(a, b):
    return lax.dot_general(a, b, _NT, preferred_element_type=F32)


def _dot_hi(a, b):
    return jnp.dot(a, b, preferred_element_type=F32, precision=HIGHEST)


def _dot_nt_hi(a, b):
    return lax.dot_general(a, b, _NT, preferred_element_type=F32, precision=HIGHEST)


def _proj_kernel(x_ref, g_ref, wq_ref, wkv_ref, wdn_ref, wz_ref, wu_ref, ws_ref, wexp_ref, *refs, lt, tk, nprev):
    if nprev:
        prev_ref, refs = refs[0], refs[1:]
    q_ref, kv_ref, dn_ref, z_ref, u_ref, s_ref, pool_ref, kvtm_ref, kvtw_ref = refs[:9]
    sel_refs = refs[9:]
    x = x_ref[...]
    h = x * lax.rsqrt(jnp.mean(x * x, axis=-1, keepdims=True) + NORM_EPS) * g_ref[...]
    hb = h.astype(BF16)
    q_ref[...] = _dotf(hb, wq_ref[...])
    kv = _dotf(hb, wkv_ref[...])
    kv_ref[...] = kv
    kvt = kv.T
    if nprev:
        kvtm_ref[0:nprev] = prev_ref[...]
    kvtm_ref[nprev] = kvt[:4 * LANE]
    kvtw_ref[...] = kvt[4 * LANE:]
    if sel_refs:
        kaug_ref, vt_ref = sel_refs
        rows = x.shape[0]
        pos = (pl.program_id(0) % lt) * rows + lax.broadcasted_iota(jnp.int32, (rows, LANE), 0)
        lanei = lax.broadcasted_iota(jnp.int32, (rows, LANE), 1)
        onehot = (lanei == pos // NSA_BLOCK).astype(BF16)
        jpart = jnp.where(lanei == HEAD_DIM, (pos % NSA_BLOCK).astype(F32), 0.0)
        ksl = kv[:, 2 * LANE:3 * LANE]
        for g in range(NSA_KV_HEADS):
            kg = ksl if g == 0 else pltpu.roll(ksl, HEAD_DIM, axis=1)
            kaug_ref[g] = jnp.concatenate([onehot, jnp.where(lanei < HEAD_DIM, kg, jpart).astype(BF16)], axis=1)
            vrow = 3 * LANE + g * HEAD_DIM
            for c in range(rows // tk):
                vt_ref[g, c] = kvt[vrow:vrow + HEAD_DIM, c * tk:(c + 1) * tk].astype(BF16)
    dn_ref[...] = _dotf(hb, wdn_ref[...])
    z_ref[...] = _dotf(hb, wz_ref[...])
    u_ref[...] = _dotf(hb, wu_ref[...])
    s_ref[...] = _dotf(hb, ws_ref[...])
    tm = x.shape[0]
    kc = kv[:, :2 * LANE].reshape(tm // NSA_BLOCK, NSA_BLOCK, 2 * LANE) * wexp_ref[...][None]
    pool_ref[...] = jnp.sum(kc, axis=1)


def _proj(x, ln, wq, wkv, wdn, wz, wu, ws, wexp, nseq, prev_kvt=None, sel_tile=None, tm=512):
    n = x.shape[0]
    lseq = n // nseq
    lt = lseq // tm
    nprev = 0 if prev_kvt is None else prev_kvt.shape[0]
    const = lambda i: (0, 0)
    row = lambda i: (i, 0)
    outs = [(n, 1024), (n, 768), (n, 768), (n, 256), (n, 256), (n, LANE), (n // NSA_BLOCK, 2 * LANE),
            (nprev + 1, nseq, 4 * LANE, lseq), (nseq, 2 * LANE, lseq)]
    out_specs = ([pl.BlockSpec((tm, s[1]), row) for s in outs[:6]]
                 + [pl.BlockSpec((tm // NSA_BLOCK, 2 * LANE), row),
                    pl.BlockSpec((nprev + 1, None, 4 * LANE, tm), lambda i: (0, i // lt, 0, i % lt)),
                    pl.BlockSpec((None, 2 * LANE, tm), lambda i: (i // lt, 0, i % lt))])
    out_shape = [jax.ShapeDtypeStruct(s, F32) for s in outs]
    prev_specs = [pl.BlockSpec((nprev, None, 4 * LANE, tm), lambda i: (0, i // lt, 0, i % lt))] if nprev else []
    G = NSA_KV_HEADS
    if sel_tile is not None:
        out_specs += [pl.BlockSpec((None, G, tm, 2 * LANE), lambda i: (i // lt, 0, i % lt, 0)),
                      pl.BlockSpec((None, G, tm // sel_tile, HEAD_DIM, sel_tile),
                                   lambda i: (i // lt, 0, i % lt, 0, 0))]
        out_shape += [jax.ShapeDtypeStruct((nseq, G, lseq, 2 * LANE), BF16),
                      jax.ShapeDtypeStruct((nseq, G, lseq // sel_tile, HEAD_DIM, sel_tile), BF16)]
    return pl.pallas_call(
        functools.partial(_proj_kernel, lt=lt, tk=sel_tile, nprev=nprev),
        grid=(n // tm,),
        in_specs=[pl.BlockSpec((tm, D_MODEL), row), pl.BlockSpec((1, D_MODEL), const)]
        + [pl.BlockSpec(w.shape, const) for w in (wq, wkv, wdn, wz, wu, ws, wexp)] + prev_specs,
        out_specs=out_specs,
        out_shape=out_shape,
        compiler_params=_cparams(("parallel",)),
        name="proj",
    )(x, ln, wq, wkv, wdn, wz, wu, ws, wexp, *([prev_kvt] if nprev else []))


def _topk_round(sel, score, blk, nblk):
    m = jnp.max(score, axis=0, keepdims=True)
    idx = jnp.min(jnp.where(score == m, blk, nblk), axis=0, keepdims=True)
    pick = blk == idx
    return sel | pick, jnp.where(pick, -jnp.inf, score)


NSA_FORCED = 3
NSA_FREE_ROUNDS = NSA_TOPK - NSA_FORCED


def _free_scores(imp, forced, blk, cur):
    return jnp.where(blk > cur, -BIG, jnp.where(forced, -jnp.inf, imp))


def _topk_blocks(score, blk, nblk):
    sel = jnp.zeros(score.shape, dtype=jnp.bool_)
    for _ in range(NSA_FREE_ROUNDS):
        sel, score = _topk_round(sel, score, blk, nblk)
    return sel


def _cmp_heads(q_tile, kc, vct, t, blk, nblk_valid, write_oc):
    nblk = blk.shape[0]
    ok = ((blk + 1) * NSA_BLOCK - 1 <= t) & (blk < nblk_valid)
    okf = ok.astype(F32)
    dist = t.astype(F32) - (blk.astype(F32) * NSA_BLOCK + (NSA_BLOCK - 1) / 2.0)
    cur = t // NSA_BLOCK
    forced = (blk == 0) | (blk == cur) | (blk == cur - 1)
    sels = []
    for g in range(NSA_KV_HEADS):
        imp = jnp.zeros(blk.shape, F32)
        for r in range(NSA_GROUP):
            h = g * NSA_GROUP + r
            q = (q_tile(h) * HEAD_DIM ** -0.5).astype(BF16)
            s = _dot_nt(kc, q)
            s = jnp.where(ok, s - _slope(h) * dist, NEG)
            m = jnp.max(s, axis=0, keepdims=True)
            p = jnp.exp(s - m) * okf
            p = p / jnp.maximum(jnp.sum(p, axis=0, keepdims=True), 1e-30)
            imp = imp + p
            write_oc(h, _dotf(vct, p.astype(BF16)))
        sel = (forced | _topk_blocks(_free_scores(imp, forced, blk, cur), blk, nblk)) & (blk <= cur)
        sels.append(sel)
    return sels


def _cmp_kernel(q_ref, kcv_ref, oc_ref, sel_ref, any_ref, *, tq):
    qt = pl.program_id(1)
    kcv = kcv_ref[...]
    nblk = kcv.shape[0]
    kc = kcv[:, :LANE].astype(BF16)
    vct = kcv[:, LANE:].T.astype(BF16)
    blk = lax.broadcasted_iota(jnp.int32, (nblk, tq), 0)
    t = qt * tq + lax.broadcasted_iota(jnp.int32, (nblk, tq), 1)

    def write_oc(h, oct):
        oc_ref[:, h * LANE:(h + 1) * LANE] = oct.T

    sels = _cmp_heads(lambda h: q_ref[:, h * LANE:(h + 1) * LANE], kc, vct, t, blk, nblk, write_oc)
    for g in range(NSA_KV_HEADS):
        selt = sels[g].astype(F32).T
        sel_ref[:, g * LANE:(g + 1) * LANE] = selt
        any_ref[:, g * LANE:(g + 1) * LANE] = jnp.broadcast_to(jnp.max(selt, axis=0, keepdims=True), (8, LANE))


def _cmp_prompt(qh, pool, nb, lq, tq):
    nq = lq // tq
    nblk = lq // NSA_BLOCK
    assert nblk == LANE
    return pl.pallas_call(
        functools.partial(_cmp_kernel, tq=tq),
        grid=(nb, nq),
        in_specs=[pl.BlockSpec((tq, 1024), lambda b, i: (b * nq + i, 0)),
                  pl.BlockSpec((nblk, 2 * LANE), lambda b, i: (b, 0))],
        out_specs=[pl.BlockSpec((tq, 1024), lambda b, i: (b * nq + i, 0)),
                   pl.BlockSpec((tq, 2 * LANE), lambda b, i: (b * nq + i, 0)),
                   pl.BlockSpec((8, 2 * LANE), lambda b, i: (b * nq + i, 0))],
        out_shape=[jax.ShapeDtypeStruct((nb * lq, 1024), F32),
                   jax.ShapeDtypeStruct((nb * lq, 2 * LANE), F32),
                   jax.ShapeDtypeStruct((nb * nq * 8, 2 * LANE), F32)],
        compiler_params=_cparams(("parallel", "parallel")),
        name="nsa_cmp",
    )(qh, pool)


def _sel_kernel(act_ref, q_ref, sel_ref, k_ref, vt_ref, o_ref, lhs_s, m_s, l_s, acc_s, lst_s, *, tq, nq):
    b = pl.program_id(0)
    g = pl.program_id(1)
    qt = pl.program_id(2)
    lanei = lax.broadcasted_iota(jnp.int32, (tq, LANE), 1)
    nio = lanei.astype(F32)
    selg = sel_ref[...] > 0.5
    for r in range(NSA_GROUP):
        slope = jnp.where(g == 0, _slope(r), _slope(NSA_GROUP + r))
        qt_r = q_ref[:, r * LANE:(r + 1) * LANE]
        qt_r = jnp.where(g == 0, qt_r, pltpu.roll(qt_r, HEAD_DIM, axis=1))
        qpart = jnp.where(lanei == HEAD_DIM, slope, qt_r * HEAD_DIM ** -0.5)
        bias = jnp.where(selg, (NSA_BLOCK * slope) * nio, NEG)
        lhs_s[r] = jnp.concatenate([bias.astype(BF16), qpart.astype(BF16)], axis=1)
    m_s[...] = jnp.full(m_s.shape, NEG, F32)
    l_s[...] = jnp.zeros(l_s.shape, F32)
    acc_s[...] = jnp.zeros(acc_s.shape, F32)

    def tiles(items):
        heads = range(NSA_GROUP)
        kaugs = [k_ref[kt] for kt, _ in items]
        vts = [vt_ref[kt] for kt, _ in items]
        ss = [[_dot_nt(ka, lhs_s[r]) for r in heads] for ka in kaugs]
        ss = [[s if mask is None else jnp.where(mask, s, NEG) for s in row] for row, (_, mask) in zip(ss, items)]
        m_old = [m_s[r] for r in heads]
        m_new = []
        for r in heads:
            m = m_old[r]
            for row in ss:
                m = jnp.maximum(m, jnp.max(row[r], axis=0, keepdims=True))
            m_new.append(m)
        ps = [[jnp.exp(row[r] - m_new[r]) for r in heads] for row in ss]
        pvs = [[_dotf(vt, row[r].astype(BF16)) for r in heads] for vt, row in zip(vts, ps)]
        for r in heads:
            alpha = jnp.exp(m_old[r] - m_new[r])
            l_s[r] = alpha * l_s[r] + sum(jnp.sum(row[r], axis=0, keepdims=True) for row in ps)
            acc_s[r] = alpha * acc_s[r] + sum(row[r] for row in pvs)
            m_s[r] = m_new[r]

    abase = ((b * NSA_KV_HEADS + g) * nq + qt) * nq

    def compact(kt, n):
        lst_s[n] = kt
        return n + act_ref[abase + kt]

    n_act = lax.fori_loop(0, qt, compact, 0)

    def body(i, carry):
        tiles([(lst_s[2 * i], None), (lst_s[2 * i + 1], None)])
        return carry

    lax.fori_loop(0, n_act // 2, body, 0)
    diag = (qt, lax.broadcasted_iota(jnp.int32, (tq, tq), 0) <= lax.broadcasted_iota(jnp.int32, (tq, tq), 1))

    @pl.when(n_act % 2 == 1)
    def _():
        tiles([(lst_s[n_act - 1], None), diag])

    @pl.when(n_act % 2 == 0)
    def _():
        tiles([diag])

    zeros = jnp.zeros((HEAD_DIM, tq), F32)
    for r in range(NSA_GROUP):
        o = jnp.concatenate([acc_s[r] / l_s[r], zeros], axis=0).T
        o_ref[:, r * LANE:(r + 1) * LANE] = jnp.where(g == 0, o, pltpu.roll(o, HEAD_DIM, axis=1))


def _sel_prompt(qh, sel, anyblk, kaug, vt, nb, lq, tq):
    assert vt.shape[-1] == tq
    nq = lq // tq
    bpt = tq // NSA_BLOCK
    G = NSA_KV_HEADS
    act = anyblk[::8].reshape(nb, nq, G, nq, bpt).max(axis=-1) > 0.5
    act = act.transpose(0, 2, 1, 3).reshape(-1).astype(jnp.int32)
    kaug = kaug.reshape(nb, G, nq, tq, 2 * LANE)
    qmap = lambda b, g, i, a: (b * nq + i, g)
    whole = lambda b, g, i, a: (b, g, 0, 0, 0)
    grid_spec = pltpu.PrefetchScalarGridSpec(
        num_scalar_prefetch=1,
        grid=(nb, G, nq),
        in_specs=[pl.BlockSpec((tq, NSA_GROUP * LANE), qmap),
                  pl.BlockSpec((tq, LANE), qmap),
                  pl.BlockSpec((None, None, nq, tq, 2 * LANE), whole),
                  pl.BlockSpec((None, None, nq, HEAD_DIM, tq), whole)],
        out_specs=pl.BlockSpec((tq, NSA_GROUP * LANE), qmap),
        scratch_shapes=[pltpu.VMEM((NSA_GROUP, tq, 2 * LANE), BF16),
                        pltpu.VMEM((NSA_GROUP, 1, tq), F32),
                        pltpu.VMEM((NSA_GROUP, 1, tq), F32),
                        pltpu.VMEM((NSA_GROUP, HEAD_DIM, tq), F32),
                        pltpu.SMEM((nq,), jnp.int32)])
    return pl.pallas_call(
        functools.partial(_sel_kernel, tq=tq, nq=nq),
        grid_spec=grid_spec,
        out_shape=jax.ShapeDtypeStruct((nb * lq, 1024), F32),
        compiler_params=_cparams(("parallel", "parallel", "arbitrary")),
        name="nsa_sel",
    )(act, qh, sel, kaug, vt)


def _gate_mix(small, oc, os_, ow, h):
    gt = jax.nn.sigmoid(small[:, 3 * h:3 * h + 3])
    return gt[:, 0:1] * oc + gt[:, 1:2] * os_ + gt[:, 2:3] * ow


WIN_HEADS_PER_ROUND = 4
_WIN_SPLIT = 32


def _win_kernel(q_ref, k0_ref, k1_ref, k2_ref, v0_ref, v1_ref, v2_ref, oc_ref, os_ref, sm_ref, o_ref, *, tq):
    qt = pl.program_id(1)
    nkey = 3 * tq
    ki = lax.broadcasted_iota(jnp.int32, (nkey, LANE), 0)
    li = lax.broadcasted_iota(jnp.int32, (nkey, LANE), 1)
    penc = jnp.where(li == 0, ki // _WIN_SPLIT, jnp.where(li == 1, ki % _WIN_SPLIT, 0)).astype(BF16)
    kcat = jnp.concatenate([k0_ref[...], k1_ref[...], k2_ref[...]], axis=0).astype(BF16)
    kaug = jnp.concatenate([kcat, penc], axis=1)
    vts = [v.astype(BF16) for v in (v0_ref[...], v1_ref[...], v2_ref[...])]
    krow = lax.broadcasted_iota(jnp.int32, (tq, tq), 0)
    qcol = lax.broadcasted_iota(jnp.int32, (tq, tq), 1)
    assert WINDOW == 2 * tq
    valids = [(krow > qcol) & (qt >= 2), jnp.broadcast_to(qt >= 1, (tq, tq)), krow <= qcol]
    small = sm_ref[...]
    lq_i = lax.broadcasted_iota(jnp.int32, (tq, LANE), 1)
    for h0 in range(0, NSA_HEADS, WIN_HEADS_PER_ROUND):
        heads = range(h0, h0 + WIN_HEADS_PER_ROUND)
        qaug = []
        for h in heads:
            al = jnp.where(lq_i == 0, _WIN_SPLIT * _slope(h), jnp.where(lq_i == 1, _slope(h), 0.0))
            qaug.append(jnp.concatenate([(q_ref[:, h * LANE:(h + 1) * LANE] * HEAD_DIM ** -0.5).astype(BF16),
                                         al.astype(BF16)], axis=1))
        tiles3 = range(3)
        ss = [[jnp.where(valids[j], _dot_nt(kaug[j * tq:(j + 1) * tq], qa), NEG) for j in tiles3]
              for qa in qaug]
        ms = [functools.reduce(jnp.maximum, [jnp.max(s, axis=0, keepdims=True) for s in row]) for row in ss]
        ps = [[jnp.exp(s - m) for s in row] for row, m in zip(ss, ms)]
        ows = []
        for row in ps:
            num = sum(_dotf(vts[j], row[j].astype(BF16)) for j in tiles3)
            den = sum(jnp.sum(row[j], axis=0, keepdims=True) for j in tiles3)
            ows.append((num / den).T)
        for h, ow in zip(heads, ows):
            sl = slice(h * LANE, (h + 1) * LANE)
            o_ref[:, sl] = _gate_mix(small, oc_ref[:, sl], os_ref[:, sl], ow, h).astype(o_ref.dtype)


def _win_prompt(qh, kv, kvt_win, oc, osel, small, nb, lq, tq=256):
    assert 2 * tq >= WINDOW - 1 and 3 * tq % _WIN_SPLIT == 0 and 3 * tq // _WIN_SPLIT <= 256
    nq = lq // tq
    row = lambda b, i: (b * nq + i, 0)

    def kmap(back):
        return lambda b, i: (b * nq + jnp.maximum(i - back, 0), 4)

    def vmap(back):
        return lambda b, i: (b, 1, jnp.maximum(i - back, 0))

    return pl.pallas_call(
        functools.partial(_win_kernel, tq=tq),
        grid=(nb, nq),
        in_specs=[pl.BlockSpec((tq, 1024), row)]
        + [pl.BlockSpec((tq, LANE), kmap(back)) for back in (2, 1, 0)]
        + [pl.BlockSpec((None, LANE, tq), vmap(back)) for back in (2, 1, 0)]
        + [pl.BlockSpec((tq, 1024), row), pl.BlockSpec((tq, 1024), row), pl.BlockSpec((tq, LANE), row)],
        out_specs=pl.BlockSpec((tq, 1024), row),
        out_shape=jax.ShapeDtypeStruct((nb * lq, 1024), BF16),
        compiler_params=_cparams(("parallel", "parallel")),
        name="nsa_win",
    )(qh, kv, kv, kv, kvt_win, kvt_win, kvt_win, oc, osel, small)


def _per_head(idx_h, fn):
    out = jnp.zeros(idx_h.shape, F32)
    for h in range(NSA_HEADS):
        out = jnp.where(idx_h == h, fn(h), out)
    return out


def _pad_rows(a, rows):
    return jnp.concatenate([a, jnp.zeros((rows - a.shape[0], a.shape[1]), a.dtype)], axis=0)


NSA_SAMPLE_SEQS = 2


def _nsa_sample_kernel(pt_ref, q_ref, kvn_ref, sm_ref, *rest, past, lq, npages, page, nprev):
    nseq = NSA_SAMPLE_SEQS
    pages = rest[:nseq * npages]
    rest = rest[nseq * npages:]
    if nprev:
        win_ref, wt_ref, prev_ref, o_ref, nwin_all, ktc_s, vtc_s, rt_s, vt_s = rest
        nwin_all[0:nprev] = prev_ref[...]
    else:
        win_ref, wt_ref, o_ref, nwin_all, ktc_s, vtc_s, rt_s, vt_s = rest
    nwin_ref = nwin_all.at[nprev]

    @pl.when(pl.program_id(0) == 0)
    def _():
        n_i = lax.broadcasted_iota(jnp.int32, (LANE, past), 0)
        pos_i = lax.broadcasted_iota(jnp.int32, (LANE, past), 1)
        for j in range(nseq):
            rt_s[j, 0:LANE, :] = (n_i == pos_i // NSA_BLOCK).astype(BF16)

    def rows(ref, j):
        return ref.at[pl.ds(j * lq, lq)]

    seqs = [_nsa_sample_seq(rows(q_ref, j), rows(kvn_ref, j), rows(sm_ref, j), pages[j * npages:(j + 1) * npages],
                            win_ref.at[j], wt_ref, rows(o_ref, j), nwin_ref.at[j], ktc_s.at[j], vtc_s.at[j],
                            rt_s.at[j], vt_s.at[j], past=past, lq=lq, npages=npages, page=page)
            for j in range(nseq)]
    live = True
    while live:
        for s in seqs:
            live = next(s, None) is not None and live


def _nsa_sample_seq(q_ref, kvn_ref, sm_ref, pages, win_ref, wt_ref, o_ref, nwin_ref, ktc_s, vtc_s, rt_s, vt_s,
                    *, past, lq, npages, page):
    nkeys = npages * page
    nrow = NSA_HEADS * lq
    nb_complete = (past + lq) // NSA_BLOCK
    new_blk = past // NSA_BLOCK
    assert past % NSA_BLOCK + lq <= NSA_BLOCK and nkeys == past and new_blk < LANE

    for i in range(npages):
        sl = slice(i * page, (i + 1) * page)
        ktc_s[:, sl] = pages[i][0:LANE, :].astype(BF16)
        vtc_s[:, sl] = pages[i][LANE:2 * LANE, :].astype(BF16)
    yield True

    def stage_selected_page(i):
        sl = slice(i * page, (i + 1) * page)
        rt_s[LANE:2 * LANE, sl] = pages[i][2 * LANE:3 * LANE, :].astype(BF16)
        vt_s[:, sl] = pages[i][3 * LANE:4 * LANE, :].astype(BF16)

    q8 = q_ref[...]
    qf = jnp.concatenate([q8[:, h * LANE:(h + 1) * LANE] for h in range(NSA_HEADS)], axis=0) * HEAD_DIM ** -0.5
    qb = qf.astype(BF16)
    kvn = kvn_ref[...]
    lane_lo = lax.broadcasted_iota(jnp.int32, (LANE, LANE), 1) < HEAD_DIM

    ktc = ktc_s[...]
    vtc = vtc_s[...]
    kc = jnp.where(lane_lo, _dot_nt(wt_ref[0, 0], ktc), _dot_nt(wt_ref[0, 1], ktc)).astype(BF16)
    vc = jnp.where(lane_lo, _dot_nt(wt_ref[1, 0], vtc), _dot_nt(wt_ref[1, 1], vtc))
    vct = vc.T.astype(BF16)
    blk = lax.broadcasted_iota(jnp.int32, (LANE, nrow), 0)
    col = lax.broadcasted_iota(jnp.int32, (LANE, nrow), 1)
    t = past + col % lq
    slope_c = _per_head(col // lq, _slope)
    ok = ((blk + 1) * NSA_BLOCK - 1 <= t) & (blk < nb_complete)
    dist = t.astype(F32) - (blk.astype(F32) * NSA_BLOCK + (NSA_BLOCK - 1) / 2.0)
    s = jnp.where(ok, _dot_nt(kc, qb) - slope_c * dist, NEG)
    m = jnp.max(s, axis=0, keepdims=True)
    p = jnp.exp(s - m) * ok.astype(F32)
    p = p / jnp.maximum(jnp.sum(p, axis=0, keepdims=True), 1e-30)
    o_c = _dotf(vct, p.astype(BF16)).T
    yield True
    ri = lax.broadcasted_iota(jnp.int32, (nrow, nrow), 0)
    ci = lax.broadcasted_iota(jnp.int32, (nrow, nrow), 1)
    gsum = ((ri // (NSA_GROUP * lq) == ci // (NSA_GROUP * lq)) & (ri % lq == ci % lq)).astype(F32)
    imp = _dot_hi(p, gsum)
    cur = t // NSA_BLOCK
    forced = (blk == 0) | (blk == cur) | (blk == cur - 1)
    score = _free_scores(imp, forced, blk, cur)
    rowi = lax.broadcasted_iota(jnp.int32, (nrow, LANE), 0)
    lanei = lax.broadcasted_iota(jnp.int32, (nrow, LANE), 1)
    slope_r = _per_head(rowi[:, 0:1] // lq, _slope)
    tok_r = rowi % lq
    new_ok = (lanei <= tok_r) & (lanei < lq)

    wlen = win_ref.shape[1]
    wi = lax.broadcasted_iota(jnp.int32, (nrow, wlen), 1)
    tok_w = lax.broadcasted_iota(jnp.int32, (nrow, wlen), 0) % lq
    dist_w = wlen + tok_w - wi
    s_w = _dotf(qb, win_ref[0:LANE, :].astype(BF16))
    s_w = jnp.where(dist_w < WINDOW, s_w - slope_r * dist_w.astype(F32), NEG)
    kwn = _pad_rows(kvn[:, 4 * LANE:5 * LANE], LANE).astype(BF16)
    s_wn = jnp.where(new_ok, _dot_nt(qb, kwn) - slope_r * (tok_r - lanei).astype(F32), NEG)
    m = jnp.maximum(jnp.max(s_w, axis=1, keepdims=True), jnp.max(s_wn, axis=1, keepdims=True))
    p_w = jnp.exp(s_w - m)
    p_wn = jnp.exp(s_wn - m)
    vwn = _pad_rows(kvn[:, 5 * LANE:6 * LANE], LANE).astype(BF16)
    o_w = (_dot_nt(p_w.astype(BF16), win_ref[LANE:2 * LANE, :].astype(BF16)) + _dotf(p_wn.astype(BF16), vwn)) / (
        jnp.sum(p_w, axis=1, keepdims=True) + jnp.sum(p_wn, axis=1, keepdims=True))

    win = win_ref[...]
    new_t = _pad_rows(kvn[:, 4 * LANE:6 * LANE], LANE).T
    new_t = pltpu.roll(new_t, LANE - lq, axis=1)
    tail = jnp.concatenate([jnp.zeros((2 * LANE, wlen - LANE), F32), new_t], axis=1)
    lane_w = lax.broadcasted_iota(jnp.int32, win.shape, 1)
    nwin_ref[...] = jnp.where(lane_w < wlen - lq, pltpu.roll(win, wlen - lq, axis=1), tail)
    yield True

    picked = forced
    for i in range(max(NSA_FREE_ROUNDS, npages)):
        if i < npages:
            stage_selected_page(i)
        if i < NSA_FREE_ROUNDS:
            picked, score = _topk_round(picked, score, blk, LANE)
        yield True
    sel = (picked & (blk <= cur)).astype(F32).T

    bias = jnp.where(sel > 0.5, (NSA_BLOCK * slope_r) * lanei.astype(F32), NEG)
    lhs = jnp.concatenate([bias.astype(BF16), qb], axis=1)
    jrow = (lax.broadcasted_iota(jnp.int32, (1, nkeys), 1) % NSA_BLOCK).astype(F32)
    s_p = _dotf(lhs, rt_s[...]) + slope_r * jrow
    bias_new = jnp.sum(jnp.where(lanei == new_blk, bias, 0.0), axis=1, keepdims=True)
    jnew = (past % NSA_BLOCK + lanei).astype(F32)
    kn = _pad_rows(kvn[:, 2 * LANE:3 * LANE], LANE).astype(BF16)
    s_n = jnp.where(new_ok, _dot_nt(qb, kn) + bias_new + slope_r * jnew, NEG)
    yield True
    m = jnp.maximum(jnp.max(s_p, axis=1, keepdims=True), jnp.max(s_n, axis=1, keepdims=True))
    p_p = jnp.exp(s_p - m)
    p_n = jnp.exp(s_n - m)
    vn = _pad_rows(kvn[:, 3 * LANE:4 * LANE], LANE).astype(BF16)
    o_s = (_dot_nt(p_p.astype(BF16), vt_s[...]) + _dotf(p_n.astype(BF16), vn)) / (
        jnp.sum(p_p, axis=1, keepdims=True) + jnp.sum(p_n, axis=1, keepdims=True))

    small = sm_ref[...]
    gate = [jax.nn.sigmoid(jnp.concatenate([small[:, 3 * h + c:3 * h + c + 1] for h in range(NSA_HEADS)], axis=0))
            for c in range(3)]
    o = gate[0] * o_c + gate[1] * o_s + gate[2] * o_w
    for h in range(NSA_HEADS):
        o_ref[:, h * LANE:(h + 1) * LANE] = o[h * lq:(h + 1) * lq].astype(o_ref.dtype)


def _pool_weights_t(wck, wcv, past):
    pos = jnp.arange(past)
    onehot = (jnp.arange(LANE)[:, None] == (pos // NSA_BLOCK)[None, :]).astype(F32)
    w = jnp.stack([wck, wcv])[:, :, pos % NSA_BLOCK]
    return (w[:, :, None, :] * onehot[None, None]).astype(BF16)


def _nsa_sample(page_table, qh, kv, small, cache_t, win_t, wt, prev_win, *, layer, lq, past):
    nprev = 0 if prev_win is None else prev_win.shape[0]
    assert nprev == layer
    nb, npages = page_table.shape
    page = cache_t.shape[-1]
    wlen = win_t.shape[-1]
    ns = NSA_SAMPLE_SEQS
    assert nb % ns == 0
    tokmap = lambda b, pt: (b, 0)

    def page_spec(j, i):
        return pl.BlockSpec((None, None, 4 * LANE, page), lambda b, pt: (layer, pt[ns * b + j, i], 0, 0))

    grid_spec = pltpu.PrefetchScalarGridSpec(
        num_scalar_prefetch=1,
        grid=(nb // ns,),
        in_specs=[pl.BlockSpec((ns * lq, 1024), tokmap), pl.BlockSpec((ns * lq, 768), tokmap),
                  pl.BlockSpec((ns * lq, LANE), tokmap)]
        + [page_spec(j, i) for j in range(ns) for i in range(npages)]
        + [pl.BlockSpec((None, ns, 2 * LANE, wlen), lambda b, pt: (layer, b, 0, 0)),
           pl.BlockSpec(wt.shape, lambda b, pt: (0, 0, 0, 0))]
        + ([pl.BlockSpec((nprev, ns, 2 * LANE, wlen), lambda b, pt: (0, b, 0, 0))] if nprev else []),
        out_specs=[pl.BlockSpec((ns * lq, 1024), tokmap),
                   pl.BlockSpec((nprev + 1, ns, 2 * LANE, wlen), lambda b, pt: (0, b, 0, 0))],
        scratch_shapes=[pltpu.VMEM((ns, LANE, past), BF16), pltpu.VMEM((ns, LANE, past), BF16),
                        pltpu.VMEM((ns, 2 * LANE, past), BF16), pltpu.VMEM((ns, LANE, past), BF16)])
    return pl.pallas_call(
        functools.partial(_nsa_sample_kernel, past=past, lq=lq, npages=npages, page=page, nprev=nprev),
        grid_spec=grid_spec,
        out_shape=[jax.ShapeDtypeStruct((nb * lq, 1024), BF16),
                   jax.ShapeDtypeStruct((nprev + 1, nb, 2 * LANE, wlen), F32)],
        compiler_params=_cparams(("arbitrary",)),
        name="nsa_sample",
    )(page_table, qh, kv, small, *([cache_t] * (ns * npages)), win_t, wt, *([prev_win] if nprev else []))


GDN_ROWS = 64


def _softplus(x):
    return jnp.maximum(x, 0.0) + jnp.log(1.0 + jnp.exp(-jnp.abs(x)))


def _unit_lower_inverse(a, seg):
    n = a.shape[0]
    eye = (lax.broadcasted_iota(jnp.int32, (n, n), 0) == lax.broadcasted_iota(jnp.int32, (n, n), 1)).astype(F32)
    inv = eye - a
    pw = a
    span = 2
    while span < seg:
        pw = _dot_hi(pw, pw)
        inv = inv + _dot_hi(inv, pw)
        span *= 2
    return inv


def _gdn_kernel(x_ref, hist_ref, sm_ref, z_ref, cw_ref, alog_ref, dtb_ref, nw_ref, s0_ref,
                o_ref, s_ref, *, seg, zero_first_hist):
    c = pl.program_id(1)
    rows = GDN_ROWS
    nseg = rows // seg

    @pl.when(c == 0)
    def _():
        s_ref[...] = s0_ref[...]

    x = x_ref[...]
    hist = hist_ref[...]
    if zero_first_hist:
        hist = jnp.where(c == 0, 0.0, hist)
    tpos = lax.broadcasted_iota(jnp.int32, x.shape, 0) % seg
    cw = cw_ref[...]
    y = x * cw[DN_CONV_W - 1:DN_CONV_W]
    for sh in range(1, DN_CONV_W):
        xs = jnp.where(tpos >= sh, pltpu.roll(x, sh, axis=0), pltpu.roll(hist, sh, axis=0))
        y = y + xs * cw[DN_CONV_W - 1 - sh:DN_CONV_W - sh]
    y = y * jax.nn.sigmoid(y)

    ri = lax.broadcasted_iota(jnp.int32, (rows, rows), 0)
    ci = lax.broadcasted_iota(jnp.int32, (rows, rows), 1)
    same = (ri // seg) == (ci // seg)
    tri = same & (ci <= ri)
    strict = same & (ci < ri)
    trif = tri.astype(F32)
    small = sm_ref[...]
    z = z_ref[...]
    heads = range(DN_HEADS)
    segs = range(nseg)
    hs = lambda h: slice(h * HEAD_DIM, (h + 1) * HEAD_DIM)
    yqs = [y[:, hs(h)] for h in heads]
    yks = [y[:, DN_WIDTH + h * HEAD_DIM:DN_WIDTH + (h + 1) * HEAD_DIM] for h in heads]
    vs = [y[:, 2 * DN_WIDTH + h * HEAD_DIM:2 * DN_WIDTH + (h + 1) * HEAD_DIM] for h in heads]
    qs = [a * lax.rsqrt(jnp.sum(a * a, axis=-1, keepdims=True) + 1e-6) * HEAD_DIM ** -0.5 for a in yqs]
    ks = [a * lax.rsqrt(jnp.sum(a * a, axis=-1, keepdims=True) + 1e-6) for a in yks]
    betas = [jax.nn.sigmoid(small[:, 28 + h:29 + h]) for h in heads]
    gs = [-jnp.exp(alog_ref[:, h:h + 1]) * _softplus(small[:, 24 + h:25 + h] + dtb_ref[:, h:h + 1]) for h in heads]
    trifs = _split(trif)
    gcbs = [_dot3s(trifs, _split(jnp.broadcast_to(g, (rows, rows)))) for g in gs]
    decays = [jnp.where(tri, jnp.exp(jnp.where(tri, gcb - gcb.T, 0.0)), 0.0) for gcb in gcbs]
    kbs = [ks[h] * betas[h] for h in heads]
    amats = [jnp.where(strict, _dot3_nt(kbs[h], ks[h]) * decays[h], 0.0) for h in heads]
    aqks = [jnp.where(tri, _dot3_nt(qs[h], ks[h]) * decays[h], 0.0) for h in heads]
    eye = (ri == ci).astype(F32)
    invs = [eye - a for a in amats]
    pws = amats
    span = 2
    while span < seg:
        pwsp = [_split(p) for p in pws]
        pws = [_dot3s(p, p) for p in pwsp]
        invs = [invs[h] + _dot3(invs[h], pws[h]) for h in heads]
        span *= 2
    egcs = [jnp.exp(gcb[:, :HEAD_DIM]) for gcb in gcbs]
    sols = [_dot3(invs[h], jnp.concatenate([vs[h] * betas[h], kbs[h] * egcs[h]], axis=1)) for h in heads]
    us = [s[:, :HEAD_DIM] for s in sols]
    ws = [s[:, HEAD_DIM:] for s in sols]
    qgs = [qs[h] * egcs[h] for h in heads]
    rsl = lambda s: slice(s * seg, (s + 1) * seg)
    sts = [[s_ref[s, h] for s in segs] for h in heads]
    stsp = [[_split(st) for st in row] for row in sts]
    vns = [jnp.concatenate([us[h][rsl(s)] - _dot3s(_split(ws[h][rsl(s)]), stsp[h][s]) for s in segs], axis=0)
           for h in heads]
    oqs = [jnp.concatenate([_dot3s(_split(qgs[h][rsl(s)]), stsp[h][s]) for s in segs], axis=0) for h in heads]
    vnsp = [_split(v) for v in vns]
    os_ = [oqs[h] + _dot3s(_split(aqks[h]), vnsp[h]) for h in heads]
    glasts = [[gcbs[h][(s + 1) * seg - 1:(s + 1) * seg, :HEAD_DIM] for s in segs] for h in heads]
    gl_rows = [jnp.concatenate([jnp.broadcast_to(gl, (seg, HEAD_DIM)) for gl in glasts[h]], axis=0) for h in heads]
    kdts = [(ks[h] * jnp.exp(gl_rows[h] - gcbs[h][:, :HEAD_DIM])).T for h in heads]
    for h in heads:
        for s in segs:
            kds = kdts[h] if nseg == 1 else jnp.where((ci // seg) == s, kdts[h], 0.0)
            s_ref[s, h] = sts[h][s] * jnp.exp(glasts[h][s]) + _dot3s(_split(kds), vnsp[h])
    outs = []
    for h in heads:
        o = os_[h] * lax.rsqrt(jnp.mean(os_[h] * os_[h], axis=-1, keepdims=True) + NORM_EPS) * nw_ref[...]
        zh = z[:, hs(h)]
        outs.append(o * (zh * jax.nn.sigmoid(zh)))
    o_ref[...] = jnp.concatenate(outs, axis=1).astype(o_ref.dtype)


def _gdn(dnqkv, hist, small, z, cw, alog, dtb, nw, s0, *, seg, nb, nchunk, row0, hist_map, zero_first_hist):
    nseg = GDN_ROWS // seg
    rmap = lambda b, c: (row0 + b * nchunk + c, 0)
    const = lambda b, c: (0, 0)
    smap = lambda b, c: (b, 0, 0, 0)
    sblock = (nseg, DN_HEADS, HEAD_DIM, HEAD_DIM)
    return pl.pallas_call(
        functools.partial(_gdn_kernel, seg=seg, zero_first_hist=zero_first_hist),
        grid=(nb, nchunk),
        in_specs=[pl.BlockSpec((GDN_ROWS, 768), rmap), pl.BlockSpec((GDN_ROWS, 768), hist_map),
                  pl.BlockSpec((GDN_ROWS, LANE), rmap), pl.BlockSpec((GDN_ROWS, 256), rmap),
                  pl.BlockSpec(cw.shape, const), pl.BlockSpec(alog.shape, const),
                  pl.BlockSpec(dtb.shape, const), pl.BlockSpec(nw.shape, const),
                  pl.BlockSpec(sblock, smap)],
        out_specs=[pl.BlockSpec((GDN_ROWS, 256), lambda b, c: (b * nchunk + c, 0)),
                   pl.BlockSpec(sblock, smap)],
        out_shape=[jax.ShapeDtypeStruct((nb * nchunk * GDN_ROWS, 256), BF16),
                   jax.ShapeDtypeStruct(s0.shape, F32)],
        compiler_params=_cparams(("parallel", "arbitrary")),
        name="gdn",
    )(dnqkv, hist, small, z, cw, alog, dtb, nw, s0)


def _split(a):
    hi = a.astype(BF16)
    return hi, (a - hi.astype(F32)).astype(BF16)


def _dot3s(a, b):
    return _dotf(a[0], b[0]) + (_dotf(a[0], b[1]) + _dotf(a[1], b[0]))


def _dot3(a, b):
    return _dot3s(_split(a), _split(b))


def _dot3_nt(a, b):
    a, b = _split(a), _split(b)
    return _dot_nt(a[0], b[0]) + (_dot_nt(a[0], b[1]) + _dot_nt(a[1], b[0]))


def _unit_lower_inverse3(a, seg):
    n = a.shape[0]
    eye = (lax.broadcasted_iota(jnp.int32, (n, n), 0) == lax.broadcasted_iota(jnp.int32, (n, n), 1)).astype(F32)
    inv = eye - a
    pw = a
    span = 2
    while span < seg:
        pws = _split(pw)
        pw = _dot3s(pws, pws)
        inv = inv + _dot3s(_split(inv), _split(pw))
        span *= 2
    return inv


GDN_PREP_CHUNKS = 4


def _gdn_prep_kernel(x_ref, hist_ref, sm_ref, cw_ref, alog_ref, dtb_ref,
                     u_ref, w_ref, qg_ref, kd_ref, aqk_ref, egl_ref):
    c = pl.program_id(1)
    x = x_ref[...]
    rows = x.shape[0]
    hist = jnp.where(c == 0, 0.0, hist_ref[...])
    row8 = lax.broadcasted_iota(jnp.int32, hist.shape, 0)
    cw = cw_ref[...]
    y = x * cw[DN_CONV_W - 1:DN_CONV_W]
    for sh in range(1, DN_CONV_W):
        xs = pltpu.roll(x, sh, axis=0)
        top = jnp.where(row8 < sh, pltpu.roll(hist, sh, axis=0), xs[0:8])
        xs = jnp.concatenate([top, xs[8:]], axis=0)
        y = y + xs * cw[DN_CONV_W - 1 - sh:DN_CONV_W - sh]
    y = y * jax.nn.sigmoid(y)

    C = GDN_ROWS
    ri = lax.broadcasted_iota(jnp.int32, (C, C), 0)
    ci = lax.broadcasted_iota(jnp.int32, (C, C), 1)
    tri = ci <= ri
    strict = ci < ri
    trif = tri.astype(F32)
    small = sm_ref[...]
    nch = rows // C
    units = [(ch, h) for ch in range(nch) for h in range(DN_HEADS)]
    each = lambda f, *ls: [f(*a) for a in zip(*ls)]

    def qkv_of(ch, h):
        rs = slice(ch * C, (ch + 1) * C)
        yq = y[rs, h * HEAD_DIM:(h + 1) * HEAD_DIM]
        yk = y[rs, DN_WIDTH + h * HEAD_DIM:DN_WIDTH + (h + 1) * HEAD_DIM]
        v = y[rs, 2 * DN_WIDTH + h * HEAD_DIM:2 * DN_WIDTH + (h + 1) * HEAD_DIM]
        q = yq * lax.rsqrt(jnp.sum(yq * yq, axis=-1, keepdims=True) + 1e-6) * HEAD_DIM ** -0.5
        k = yk * lax.rsqrt(jnp.sum(yk * yk, axis=-1, keepdims=True) + 1e-6)
        beta = jax.nn.sigmoid(small[rs, 28 + h:29 + h])
        g = -jnp.exp(alog_ref[:, h:h + 1]) * _softplus(small[rs, 24 + h:25 + h] + dtb_ref[:, h:h + 1])
        return q, k, v, beta, g

    qs_, ks_, vs_, betas, gs = zip(*[qkv_of(ch, h) for ch, h in units])
    trifs = _split(trif)
    gcbs = [_dot3s(trifs, _split(jnp.broadcast_to(g, (C, C)))) for g in gs]
    kbs = each(lambda k, b: k * b, ks_, betas)
    ksp = [_split(k) for k in ks_]
    kbsp = [_split(kb) for kb in kbs]
    qsp = [_split(q) for q in qs_]
    nt3 = lambda a, b: _dot_nt(a[0], b[0]) + (_dot_nt(a[0], b[1]) + _dot_nt(a[1], b[0]))
    kks = each(nt3, kbsp, ksp)
    qks = each(nt3, qsp, ksp)
    decays = [jnp.where(tri, jnp.exp(jnp.where(tri, gcb - gcb.T, 0.0)), 0.0) for gcb in gcbs]
    amats = each(lambda kk, d: jnp.where(strict, kk * d, 0.0), kks, decays)
    eye = (ri == ci).astype(F32)
    invs = [eye - a for a in amats]
    pwsp = [_split(a) for a in amats]
    span = 2
    while span < C:
        pwsp = [_split(_dot3s(p, p)) for p in pwsp]
        invs = each(lambda i, p: i + _dot3s(_split(i), p), invs, pwsp)
        span *= 2
    egcs = [jnp.exp(gcb) for gcb in gcbs]
    rhss = each(lambda v, b, kb, e: jnp.concatenate([v * b, kb * e], axis=1), vs_, betas, kbs, egcs)
    sols = each(_dot3, invs, rhss)
    glasts = [gcb[C - 1:C, :] for gcb in gcbs]
    qgs = each(lambda q, e: q * e, qs_, egcs)
    kds = each(lambda k, gl, gcb: k * jnp.exp(gl - gcb), ks_, glasts, gcbs)
    aqks = each(lambda qk, d: jnp.where(tri, qk * d, 0.0), qks, decays)
    for ch in range(nch):
        rs = slice(ch * C, (ch + 1) * C)
        un = range(ch * DN_HEADS, (ch + 1) * DN_HEADS)
        u_ref[rs, :] = jnp.concatenate([sols[i][:, :HEAD_DIM] for i in un], axis=1)
        w_ref[rs, :] = jnp.concatenate([sols[i][:, HEAD_DIM:] for i in un], axis=1)
        qg_ref[rs, :] = jnp.concatenate([qgs[i] for i in un], axis=1)
        kd_ref[rs, :] = jnp.concatenate([kds[i] for i in un], axis=1)
        aqk_ref[rs, :] = jnp.concatenate([aqks[i] for i in un], axis=1)
        egl_ref[ch * 8:(ch + 1) * 8, :] = jnp.concatenate(
            [jnp.broadcast_to(jnp.exp(glasts[i]), (8, HEAD_DIM)) for i in un], axis=1)


def _gdn_scan_kernel(u_ref, w_ref, qg_ref, kd_ref, aqk_ref, egl_ref, z_ref, nw_ref, s0_ref, o_ref, s_ref):
    c = pl.program_id(0)

    @pl.when(c == 0)
    def _():
        s_ref[...] = s0_ref[...]

    nb = u_ref.shape[0]
    units = [(b, slice(h * HEAD_DIM, (h + 1) * HEAD_DIM), h) for b in range(nb) for h in range(DN_HEADS)]
    sts = [s_ref[b, h] for b, _, h in units]
    stsp = [_split(s) for s in sts]
    kdts = [_split(kd_ref[b, :, sl].T) for b, sl, _ in units]
    vnews = [u_ref[b, :, sl] - _dot3s(_split(w_ref[b, :, sl]), st) for (b, sl, _), st in zip(units, stsp)]
    oqs = [_dot3s(_split(qg_ref[b, :, sl]), st) for (b, sl, _), st in zip(units, stsp)]
    vsp = [_split(v) for v in vnews]
    os_ = [oq + _dot3s(_split(aqk_ref[b, :, sl]), vs) for (b, sl, _), oq, vs in zip(units, oqs, vsp)]
    for (b, sl, h), st, kdt, vs in zip(units, sts, kdts, vsp):
        s_ref[b, h] = st * egl_ref[b, 0:1, sl] + _dot3s(kdt, vs)
    outs = []
    for (b, sl, _), o in zip(units, os_):
        o = o * lax.rsqrt(jnp.mean(o * o, axis=-1, keepdims=True) + NORM_EPS) * nw_ref[...]
        zh = z_ref[b, :, sl]
        outs.append(o * (zh * jax.nn.sigmoid(zh)))
    for b in range(nb):
        o_ref[b] = jnp.concatenate(outs[b * DN_HEADS:(b + 1) * DN_HEADS], axis=1).astype(o_ref.dtype)


def _gdn_prompt(dnqkv, small, z, cw, alog, dtb, nw, s0, *, nb, lq):
    cb = GDN_PREP_CHUNKS
    rows = cb * GDN_ROWS
    nstep = lq // rows
    nchunk = lq // GDN_ROWS
    rmap = lambda b, c: (b * nstep + c, 0)
    const = lambda b, c: (0, 0)
    wide = jax.ShapeDtypeStruct((nb * lq, DN_WIDTH), F32)
    u, w, qg, kd, aqk, egl = pl.pallas_call(
        _gdn_prep_kernel,
        grid=(nb, nstep),
        in_specs=[pl.BlockSpec((rows, 768), rmap),
                  pl.BlockSpec((8, 768), lambda b, c: (jnp.maximum((b * nstep + c) * (rows // 8) - 1, 0), 0)),
                  pl.BlockSpec((rows, LANE), rmap),
                  pl.BlockSpec(cw.shape, const), pl.BlockSpec(alog.shape, const), pl.BlockSpec(dtb.shape, const)],
        out_specs=[pl.BlockSpec((rows, DN_WIDTH), rmap)] * 5 + [pl.BlockSpec((cb * 8, DN_WIDTH), rmap)],
        out_shape=[wide] * 5 + [jax.ShapeDtypeStruct((nb * nchunk * 8, DN_WIDTH), F32)],
        compiler_params=_cparams(("parallel", "parallel")),
        name="gdn_prep",
    )(dnqkv, dnqkv, small, cw, alog, dtb)
    r3 = lambda a: a.reshape(nb, -1, DN_WIDTH)
    cmap = lambda c: (0, c, 0)
    full = lambda a: pl.BlockSpec(a.shape, lambda c: (0,) * a.ndim)
    o, s = pl.pallas_call(
        _gdn_scan_kernel,
        grid=(nchunk,),
        in_specs=[pl.BlockSpec((nb, GDN_ROWS, DN_WIDTH), cmap)] * 5
        + [pl.BlockSpec((nb, 8, DN_WIDTH), cmap), pl.BlockSpec((nb, GDN_ROWS, DN_WIDTH), cmap),
           full(nw), full(s0)],
        out_specs=[pl.BlockSpec((nb, GDN_ROWS, DN_WIDTH), cmap), full(s0)],
        out_shape=[jax.ShapeDtypeStruct((nb, lq, DN_WIDTH), BF16), jax.ShapeDtypeStruct(s0.shape, F32)],
        compiler_params=_cparams(("arbitrary",)),
        name="gdn_scan",
    )(r3(u), r3(w), r3(qg), r3(kd), r3(aqk), r3(egl), z[:nb * lq].reshape(nb, lq, DN_WIDTH), nw, s0)
    return o.reshape(nb * lq, DN_WIDTH), s


def _s5_discretize(lre_ref, lim_ref, lstep_ref):
    lr = lre_ref[...]
    li = lim_ref[...]
    dt = jnp.exp(lstep_ref[...])
    mag = jnp.exp(lr * dt)
    ar = mag * jnp.cos(li * dt)
    ai = mag * jnp.sin(li * dt)
    den = lr * lr + li * li
    fr = ((ar - 1.0) * lr + ai * li) / den
    fi = (ai * lr - (ar - 1.0) * li) / den
    return ar, ai, fr, fi


def _s5_kernel(*refs, R, S, with_y):
    if with_y:
        (u_ref, h0r_ref, h0i_ref, lre_ref, lim_ref, lstep_ref, bre_ref, bim_ref, cre_ref, cim_ref, d_ref,
         wglu_ref, y_ref, hr_ref, hi_ref, ar_s, ai_s, bbr_s, bbi_s, xr_s, xi_s) = refs
    else:
        (u_ref, h0r_ref, h0i_ref, lre_ref, lim_ref, lstep_ref, bre_ref, bim_ref,
         hr_ref, hi_ref, ar_s, ai_s, bbr_s, bbi_s, xr_s, xi_s) = refs
    i = pl.program_id(0)

    @pl.when(i == 0)
    def _():
        ar, ai, fr, fi = _s5_discretize(lre_ref, lim_ref, lstep_ref)
        ar_s[...] = ar
        ai_s[...] = ai
        bbr_s[...] = (fr * bre_ref[...] - fi * bim_ref[...]).astype(BF16)
        bbi_s[...] = (fr * bim_ref[...] + fi * bre_ref[...]).astype(BF16)
        hr_ref[...] = h0r_ref[...]
        hi_ref[...] = h0i_ref[...]

    u = u_ref[...]
    ub = u.astype(BF16)
    xr_s[...] = _dotf(ub, bbr_s[...])
    xi_s[...] = _dotf(ub, bbi_s[...])
    ar = jnp.broadcast_to(ar_s[...], (R, S5_LANES))
    ai = jnp.broadcast_to(ai_s[...], (R, S5_LANES))

    if not with_y:
        def carry_step(s, h):
            off = pl.multiple_of(s * R, R)
            return (ar * h[0] - ai * h[1] + xr_s[pl.ds(off, R), :], ar * h[1] + ai * h[0] + xi_s[pl.ds(off, R), :])

        hr, hi = lax.fori_loop(0, S, carry_step, (hr_ref[...], hi_ref[...]))
        hr_ref[...] = hr
        hi_ref[...] = hi
        return

    def advance(hr, hi, off):
        nr = ar * hr - ai * hi + xr_s[pl.ds(off, R), :]
        ni = ar * hi + ai * hr + xi_s[pl.ds(off, R), :]
        xr_s[pl.ds(off, R), :] = nr
        xi_s[pl.ds(off, R), :] = ni

    advance(hr_ref[...], hi_ref[...], 0)

    def step(s, carry):
        prev = pl.multiple_of((s - 1) * R, R)
        advance(xr_s[pl.ds(prev, R), :], xi_s[pl.ds(prev, R), :], pl.multiple_of(s * R, R))
        return carry

    lax.fori_loop(1, S, step, 0)
    hr_ref[...] = xr_s[pl.ds((S - 1) * R, R), :]
    hi_ref[...] = xi_s[pl.ds((S - 1) * R, R), :]
    if with_y:
        y = _dotf(xr_s[...].astype(BF16), cre_ref[...]) - _dotf(xi_s[...].astype(BF16), cim_ref[...])
        y = y + d_ref[...] * u
        gl = jax.nn.gelu(y)
        y_ref[...] = (gl * jax.nn.sigmoid(_dotf(gl.astype(BF16), wglu_ref[...]))).astype(y_ref.dtype)


def _s5_scan(u_rows, h0r, h0i, prm, *, R, S, with_y):
    n = u_rows.shape[0]
    rows = R * S
    const = lambda i: (0, 0)
    rmap = lambda i: (i, 0)
    ins = [u_rows, h0r, h0i, prm["lre"], prm["lim"], prm["lstep"], prm["bre"], prm["bim"]]
    if with_y:
        ins += [prm["cre"], prm["cim"], prm["d"], prm["wglu"]]
    in_specs = [pl.BlockSpec((rows, S5_WIDTH), rmap)] + [pl.BlockSpec(a.shape, const) for a in ins[1:]]
    st_spec = pl.BlockSpec((R, S5_LANES), const)
    st_shape = jax.ShapeDtypeStruct((R, S5_LANES), F32)
    out_specs = [st_spec, st_spec]
    out_shape = [st_shape, st_shape]
    if with_y:
        out_specs = [pl.BlockSpec((rows, S5_WIDTH), rmap)] + out_specs
        out_shape = [jax.ShapeDtypeStruct((n, S5_WIDTH), BF16)] + out_shape
    return pl.pallas_call(
        functools.partial(_s5_kernel, R=R, S=S, with_y=with_y),
        grid=(n // rows,),
        in_specs=in_specs,
        out_specs=out_specs,
        out_shape=out_shape,
        scratch_shapes=[pltpu.VMEM((1, S5_LANES), F32), pltpu.VMEM((1, S5_LANES), F32),
                        pltpu.VMEM((S5_WIDTH, S5_LANES), BF16), pltpu.VMEM((S5_WIDTH, S5_LANES), BF16),
                        pltpu.VMEM((rows, S5_LANES), F32), pltpu.VMEM((rows, S5_LANES), F32)],
        compiler_params=_cparams(("arbitrary",)),
        name="s5_scan",
    )(*ins)


def _s5_carry_kernel(er_ref, ei_ref, lre_ref, lim_ref, lstep_ref, ir_ref, ii_ref, fr_ref, fi_ref, *, nseg, nsteps):
    ar, ai, _, _ = _s5_discretize(lre_ref, lim_ref, lstep_ref)
    pr, pi_ = ar, ai
    n = 1
    while n < nsteps:
        pr, pi_ = pr * pr - pi_ * pi_, 2.0 * pr * pi_
        n *= 2
    for b in range(er_ref.shape[0] // nseg):
        cr = jnp.zeros((1, S5_LANES), F32)
        ci = jnp.zeros((1, S5_LANES), F32)
        for s in range(nseg):
            r = b * nseg + s
            ir_ref[r:r + 1, :] = cr
            ii_ref[r:r + 1, :] = ci
            er = er_ref[r:r + 1, :]
            ei = ei_ref[r:r + 1, :]
            cr, ci = pr * cr - pi_ * ci + er, pr * ci + pi_ * cr + ei
        fr_ref[b:b + 1, :] = cr
        fi_ref[b:b + 1, :] = ci


def _s5_carry(er, ei, prm, *, nseg, nsteps):
    assert nsteps & (nsteps - 1) == 0
    nb = er.shape[0] // nseg
    full = lambda a: pl.BlockSpec(a.shape, lambda: (0,) * a.ndim)
    ins = [er, ei, prm["lre"], prm["lim"], prm["lstep"]]
    outs = [jax.ShapeDtypeStruct(er.shape, F32)] * 2 + [jax.ShapeDtypeStruct((nb, S5_LANES), F32)] * 2
    return pl.pallas_call(
        functools.partial(_s5_carry_kernel, nseg=nseg, nsteps=nsteps),
        in_specs=[full(a) for a in ins],
        out_specs=[full(o) for o in outs],
        out_shape=outs,
        name="s5_carry",
    )(*ins)


S5_SEGS = 8


def _s5_prep(lre, lim, lstep, bre, bim, cre, cim, d, wglu):
    eye = jnp.eye(S5_GROUPS, dtype=F32)
    bexp = lambda b: jnp.einsum("gpc,gh->gchp", b, eye).reshape(S5_WIDTH, S5_LANES)
    cexp = lambda c: jnp.einsum("gcp,gh->gphc", c, eye).reshape(S5_LANES, S5_WIDTH).astype(BF16)
    return {"lre": lre.reshape(1, S5_LANES), "lim": lim.reshape(1, S5_LANES),
            "lstep": jnp.repeat(lstep, S5_STATE).reshape(1, S5_LANES),
            "bre": bexp(bre), "bim": bexp(bim), "cre": cexp(cre), "cim": cexp(cim),
            "d": d.reshape(1, S5_WIDTH), "wglu": wglu.astype(BF16)}


def _s5_prompt(u, prm, nb, lq, steps_per_tile=32):
    nsteps = lq // S5_SEGS
    R = nb * S5_SEGS
    u_rows = u.reshape(nb, S5_SEGS, nsteps, S5_WIDTH).transpose(2, 0, 1, 3).reshape(nsteps * R, S5_WIDTH)
    zero = jnp.zeros((R, S5_LANES), F32)
    er, ei = _s5_scan(u_rows, zero, zero, prm, R=R, S=steps_per_tile, with_y=False)
    ir, ii, fr, fi = _s5_carry(er, ei, prm, nseg=S5_SEGS, nsteps=nsteps)
    y_rows, _, _ = _s5_scan(u_rows, ir, ii, prm, R=R, S=steps_per_tile, with_y=True)
    y = y_rows.reshape(nsteps, nb, S5_SEGS, S5_WIDTH).transpose(1, 2, 0, 3).reshape(nb * lq, S5_WIDTH)
    return y, fr, fi


def _s5_sample(u, h0r, h0i, prm, nb, lq, steps_per_tile=4):
    u_rows = u.reshape(nb, lq, S5_WIDTH).transpose(1, 0, 2).reshape(lq * nb, S5_WIDTH)
    y_rows, fr, fi = _s5_scan(u_rows, h0r, h0i, prm, R=nb, S=steps_per_tile, with_y=True)
    y = y_rows.reshape(lq, nb, S5_WIDTH).transpose(1, 0, 2).reshape(nb * lq, S5_WIDTH)
    return y, fr, fi


def _sample_conv_hist(buf):
    nb = buf.shape[0]
    blk = buf.reshape(nb // 8, 8, DN_CONV_W - 1, 768)
    blk = jnp.roll(blk, -1, axis=1)
    blk = jnp.pad(blk, ((0, 0), (0, 0), (8 - (DN_CONV_W - 1), 0), (0, 0)))
    return blk.reshape(nb * 8, 768)


def _rms(x, g):
    return x * lax.rsqrt(jnp.mean(x * x, axis=-1, keepdims=True) + NORM_EPS) * g


def _mlp_kernel(x_ref, on_ref, od_ref, os_ref, wn_ref, wd_ref, ws_ref, ln2_ref, up_ref, dn_ref, lnf_ref,
                o_ref, h2_s, *, final_norm):
    j = pl.program_id(1)

    @pl.when(j == 0)
    def _():
        x1 = (x_ref[...] + _dotf(on_ref[...], wn_ref[...]) + _dotf(od_ref[...], wd_ref[...])
              + _dotf(os_ref[...], ws_ref[...]))
        o_ref[...] = x1
        h2_s[...] = _rms(x1, ln2_ref[...]).astype(BF16)

    a = jnp.maximum(_dotf(h2_s[...], up_ref[...]), 0.0)
    o_ref[...] += _dotf((a * a).astype(BF16), dn_ref[...])

    if final_norm:
        @pl.when(j == pl.num_programs(1) - 1)
        def _():
            o_ref[...] = _rms(o_ref[...], lnf_ref[...])


def _mlp(x, o_nsa, o_dn, o_s5, wn, wd, ws, ln2, up, dn, lnf, *, final_norm, tm=512, tf=1024):
    n = x.shape[0]
    row = lambda i, j: (i, 0)
    const = lambda i, j: (0, 0)
    return pl.pallas_call(
        functools.partial(_mlp_kernel, final_norm=final_norm),
        grid=(n // tm, D_FF // tf),
        in_specs=[pl.BlockSpec((tm, D_MODEL), row), pl.BlockSpec((tm, 1024), row),
                  pl.BlockSpec((tm, DN_WIDTH), row), pl.BlockSpec((tm, S5_WIDTH), row),
                  pl.BlockSpec(wn.shape, const), pl.BlockSpec(wd.shape, const), pl.BlockSpec(ws.shape, const),
                  pl.BlockSpec((1, D_MODEL), const),
                  pl.BlockSpec((D_MODEL, tf), lambda i, j: (0, j)),
                  pl.BlockSpec((tf, D_MODEL), lambda i, j: (j, 0)),
                  pl.BlockSpec((1, D_MODEL), const)],
        out_specs=pl.BlockSpec((tm, D_MODEL), row),
        out_shape=jax.ShapeDtypeStruct((n, D_MODEL), F32),
        scratch_shapes=[pltpu.VMEM((tm, D_MODEL), BF16)],
        compiler_params=_cparams(("parallel", "arbitrary")),
        name="out_mlp",
    )(x, o_nsa, o_dn, o_s5, wn, wd, ws, ln2, up, dn, lnf)


_NSA_W = NSA_HEADS * HEAD_DIM
_KV_W = 6 * NSA_KV_HEADS * HEAD_DIM
_GATE_W = NSA_HEADS * 3
_OFF_KV = _NSA_W
_OFF_GATE = _OFF_KV + _KV_W
_OFF_DN = _OFF_GATE + _GATE_W
_OFF_A = _OFF_DN + 3 * DN_WIDTH
_OFF_B = _OFF_A + DN_HEADS
_OFF_Z = _OFF_B + DN_HEADS
_OFF_U = _OFF_Z + DN_WIDTH


def _pad_heads(w, axis):
    w = jnp.moveaxis(w, axis, 0).reshape((NSA_HEADS, HEAD_DIM) + w.shape[:axis] + w.shape[axis + 1:])
    out = jnp.zeros((NSA_HEADS, LANE) + w.shape[2:], w.dtype)
    for h in range(NSA_HEADS):
        g = h // NSA_GROUP
        out = out.at[h, g * HEAD_DIM:(g + 1) * HEAD_DIM].set(w[h])
    out = out.reshape((NSA_HEADS * LANE,) + w.shape[2:])
    return jnp.moveaxis(out, 0, axis)


def _layer_weights(w_in, w_out, wck, wcv):
    wq = _pad_heads(w_in[:, :_NSA_W], 1).astype(BF16)
    wkv = w_in[:, _OFF_KV:_OFF_GATE].astype(BF16)
    wdn = w_in[:, _OFF_DN:_OFF_A].astype(BF16)
    wz = w_in[:, _OFF_Z:_OFF_U].astype(BF16)
    wu = w_in[:, _OFF_U:].astype(BF16)
    wsm = jnp.zeros((D_MODEL, LANE), F32)
    wsm = wsm.at[:, :_GATE_W].set(w_in[:, _OFF_GATE:_OFF_DN])
    wsm = wsm.at[:, _GATE_W:_GATE_W + 2 * DN_HEADS].set(w_in[:, _OFF_A:_OFF_Z]).astype(BF16)
    wexp = jnp.concatenate([jnp.repeat(wck.T, HEAD_DIM, axis=1), jnp.repeat(wcv.T, HEAD_DIM, axis=1)], axis=1)
    won = _pad_heads(w_out[:_NSA_W], 0).astype(BF16)
    wod = w_out[_NSA_W:_NSA_W + DN_WIDTH].astype(BF16)
    wos = w_out[_NSA_W + DN_WIDTH:].astype(BF16)
    return wq, wkv, wdn, wz, wu, wsm, wexp, won, wod, wos


def kernel(x_prompt, x_sample, cache_nsa_kv, cache_win_kv, state_dn_conv, state_dn, state_s5_re, state_s5_im,
           page_table, ln1, ln2, ln_f, w_in, w_out, nsa_wck, nsa_wcv, dn_conv_w, dn_a_log, dn_dt_bias,
           dn_norm_w, s5_lambda_re, s5_lambda_im, s5_log_step, s5_b_re, s5_b_im, s5_c_re, s5_c_im, s5_d,
           s5_w_glu, mlp_up, mlp_down):
    nbp, lp, _ = x_prompt.shape
    nbs, ls, _ = x_sample.shape
    depth = w_in.shape[0]
    rows_p = nbp * lp
    rows_s = nbs * ls
    n_phys, page = cache_nsa_kv.shape[1], cache_nsa_kv.shape[2]
    past = page_table.shape[1] * page
    wlen = cache_win_kv.shape[2]
    G, dh = NSA_KV_HEADS, HEAD_DIM
    assert lp % (S5_SEGS * 32) == 0 and nbs % 8 == 0 and ls == 8 and wlen == WINDOW and lp >= WINDOW

    xp = x_prompt.reshape(rows_p, D_MODEL)
    xs = x_sample.reshape(rows_s, D_MODEL)
    cache_t = cache_nsa_kv.transpose(0, 1, 3, 4, 5, 2).reshape(depth, n_phys, 4 * G * dh, page)
    win_t = cache_win_kv.transpose(0, 1, 3, 4, 5, 2).reshape(depth, nbs, 2 * G * dh, wlen)
    outs_p, outs_s = [], []
    kvt_main = nwin_t = None
    for l in range(depth):
        wq, wkv, wdn, wz, wu, wsm, wexp, won, wod, wos = _layer_weights(w_in[l], w_out[l], nsa_wck[l], nsa_wcv[l])
        pw = (ln1[l][None], wq, wkv, wdn, wz, wu, wsm, wexp)
        qh_p, kv_p, dnx_p, z_p, u_p, small_p, pool_p, kvt_main, kvt_win, kaug, vt = _proj(
            xp, *pw, nseq=nbp, prev_kvt=kvt_main, sel_tile=NSA_TQ)
        qh_s, kv_s, dnx_s, z_s, u_s, small_s, _, _, _ = _proj(xs, *pw, nseq=1)

        oc, sel, anyblk = _cmp_prompt(qh_p, pool_p, nbp, lp, NSA_TQ)
        osel = _sel_prompt(qh_p, sel, anyblk, kaug, vt, nbp, lp, NSA_TQ)
        on_p = _win_prompt(qh_p, kv_p, kvt_win, oc, osel, small_p, nbp, lp)
        wt = _pool_weights_t(nsa_wck[l], nsa_wcv[l], past)
        on_s, nwin_t = _nsa_sample(page_table, qh_s, kv_s, small_s, cache_t, win_t, wt, nwin_t,
                                   layer=l, lq=ls, past=past)

        gdn_w = (dn_conv_w[l].T, dn_a_log[l][None], dn_dt_bias[l][None], dn_norm_w[l][None])
        od_p, dn_p = _gdn_prompt(dnx_p, small_p, z_p, *gdn_w, jnp.zeros((nbp, DN_HEADS, dh, dh), F32), nb=nbp, lq=lp)
        od_s, dn_s = _gdn(dnx_s, _sample_conv_hist(state_dn_conv[l]), small_s, z_s, *gdn_w, state_dn[l],
                          seg=ls, nb=rows_s // GDN_ROWS, nchunk=1, row0=0,
                          hist_map=lambda b, c: (b, 0), zero_first_hist=False)

        prm = _s5_prep(s5_lambda_re[l], s5_lambda_im[l], s5_log_step[l], s5_b_re[l], s5_b_im[l],
                       s5_c_re[l], s5_c_im[l], s5_d[l], s5_w_glu[l])
        os_p, s5r_p, s5i_p = _s5_prompt(u_p, prm, nbp, lp)
        os_s, s5r_s, s5i_s = _s5_sample(u_s, state_s5_re[l].reshape(nbs, S5_LANES),
                                        state_s5_im[l].reshape(nbs, S5_LANES), prm, nbs, ls)

        mw = (won, wod, wos, ln2[l][None], mlp_up[l].astype(BF16), mlp_down[l].astype(BF16), ln_f[None])
        xp = _mlp(xp, on_p, od_p, os_p, *mw, final_norm=(l == depth - 1))
        xs = _mlp(xs, on_s, od_s, os_s, *mw, final_norm=(l == depth - 1))

        kv_s6 = kv_s.reshape(nbs, ls, 6, G, dh)
        win_p = kvt_win[:, :, lp - WINDOW:].reshape(nbp, 2, G, dh, WINDOW).transpose(0, 4, 1, 2, 3)
        outs_p.append((win_p, dnx_p.reshape(nbp, lp, 3 * DN_WIDTH)[:, -(DN_CONV_W - 1):], dn_p,
                       s5r_p.reshape(nbp, S5_GROUPS, S5_STATE), s5i_p.reshape(nbp, S5_GROUPS, S5_STATE)))
        outs_s.append((kv_s6[:, :, :4], dnx_s.reshape(nbs, ls, 3 * DN_WIDTH)[:, -(DN_CONV_W - 1):], dn_s,
                       s5r_s.reshape(nbs, S5_GROUPS, S5_STATE), s5i_s.reshape(nbs, S5_GROUPS, S5_STATE)))

    stack = lambda outs, i: jnp.stack([o[i] for o in outs], axis=0)
    kv_prompt = kvt_main.reshape(depth, nbp, 4, G, dh, lp).transpose(0, 1, 5, 2, 3, 4)
    win_sample = nwin_t.reshape(depth, nbs, 2, G, dh, wlen).transpose(0, 1, 5, 2, 3, 4)
    return (xp.reshape(nbp, lp, D_MODEL), xs.reshape(nbs, ls, D_MODEL),
            kv_prompt, stack(outs_s, 0), stack(outs_p, 0), win_sample,
            stack(outs_p, 1), stack(outs_s, 1), stack(outs_p, 2), stack(outs_s, 2),
            stack(outs_p, 3), stack(outs_s, 3), stack(outs_p, 4), stack(outs_s, 4))
```

```python
import functools

import jax
import jax.numpy as jnp
from jax import lax
from jax.experimental import pallas as pl
from jax.experimental.pallas import tpu as pltpu

F32 = jnp.float32
BF16 = jnp.bfloat16
HIGHEST = lax.Precision.HIGHEST

D_MODEL = 1024
HEAD_DIM = 64
NSA_HEADS = 8
NSA_KV_HEADS = 2
NSA_GROUP = 4
NSA_BLOCK = 64
NSA_TOPK = 16
WINDOW = 512
DN_HEADS = 4
DN_WIDTH = 256
DN_CONV_W = 4
S5_GROUPS = 16
S5_GROUP_CH = 16
S5_STATE = 64
S5_WIDTH = 256
S5_LANES = S5_GROUPS * S5_STATE
D_FF = 4096
NORM_EPS = 1e-6
BIG = 1e9
NEG = -1e30
LANE = 128
NSA_TQ = 256
VMEM_LIMIT = 48 * 1024 * 1024

_NT = (((1,), (1,)), ((), ()))


def _slope(h):
    return 2.0 ** (-(h + 1))


def _cparams(sem):
    return pltpu.CompilerParams(dimension_semantics=sem, vmem_limit_bytes=VMEM_LIMIT)


def _dotf(a, b):
    return jnp.dot(a, b, preferred_element_type=F32)


def _dot_nt(a, b):
    return lax.dot_general(a, b, _NT, preferred_element_type=F32)


def _dot_hi(a, b):
    return jnp.dot(a, b, preferred_element_type=F32, precision=HIGHEST)


def _proj_kernel(x_ref, g_ref, wq_ref, wkv_ref, wdn_ref, wz_ref, wu_ref, ws_ref, wexp_ref, *refs, lt, tk, nprev):
    if nprev:
        prev_ref, refs = refs[0], refs[1:]
    q_ref, kv_ref, dn_ref, z_ref, u_ref, s_ref, pool_ref, kvtm_ref, kvtw_ref = refs[:9]
    sel_refs = refs[9:]
    x = x_ref[...]
    h = x * lax.rsqrt(jnp.mean(x * x, axis=-1, keepdims=True) + NORM_EPS) * g_ref[...]
    hb = h.astype(BF16)
    q_ref[...] = _dotf(hb, wq_ref[...])
    kv = _dotf(hb, wkv_ref[...])
    kv_ref[...] = kv
    kvt = kv.T
    if nprev:
        kvtm_ref[0:nprev] = prev_ref[...]
    kvtm_ref[nprev] = kvt[:4 * LANE]
    kvtw_ref[...] = kvt[4 * LANE:]
    if sel_refs:
        kaug_ref, vt_ref = sel_refs
        rows = x.shape[0]
        pos = (pl.program_id(0) % lt) * rows + lax.broadcasted_iota(jnp.int32, (rows, LANE), 0)
        lanei = lax.broadcasted_iota(jnp.int32, (rows, LANE), 1)
        onehot = (lanei == pos // NSA_BLOCK).astype(BF16)
        jpart = jnp.where(lanei == HEAD_DIM, (pos % NSA_BLOCK).astype(F32), 0.0)
        ksl = kv[:, 2 * LANE:3 * LANE]
        for g in range(NSA_KV_HEADS):
            kg = ksl if g == 0 else pltpu.roll(ksl, HEAD_DIM, axis=1)
            kaug_ref[g] = jnp.concatenate([onehot, jnp.where(lanei < HEAD_DIM, kg, jpart).astype(BF16)], axis=1)
            vrow = 3 * LANE + g * HEAD_DIM
            for c in range(rows // tk):
                vt_ref[g, c] = kvt[vrow:vrow + HEAD_DIM, c * tk:(c + 1) * tk].astype(BF16)
    dn_ref[...] = _dotf(hb, wdn_ref[...])
    z_ref[...] = _dotf(hb, wz_ref[...])
    u_ref[...] = _dotf(hb, wu_ref[...])
    s_ref[...] = _dotf(hb, ws_ref[...])
    tm = x.shape[0]
    kc = kv[:, :2 * LANE].reshape(tm // NSA_BLOCK, NSA_BLOCK, 2 * LANE) * wexp_ref[...][None]
    pool_ref[...] = jnp.sum(kc, axis=1)


def _proj(x, ln, wq, wkv, wdn, wz, wu, ws, wexp, nseq, prev_kvt=None, sel_tile=None, tm=512):
    n = x.shape[0]
    lseq = n // nseq
    lt = lseq // tm
    nprev = 0 if prev_kvt is None else prev_kvt.shape[0]
    const = lambda i: (0, 0)
    row = lambda i: (i, 0)
    outs = [(n, 1024), (n, 768), (n, 768), (n, 256), (n, 256), (n, LANE), (n // NSA_BLOCK, 2 * LANE),
            (nprev + 1, nseq, 4 * LANE, lseq), (nseq, 2 * LANE, lseq)]
    out_specs = ([pl.BlockSpec((tm, s[1]), row) for s in outs[:6]]
                 + [pl.BlockSpec((tm // NSA_BLOCK, 2 * LANE), row),
                    pl.BlockSpec((nprev + 1, None, 4 * LANE, tm), lambda i: (0, i // lt, 0, i % lt)),
                    pl.BlockSpec((None, 2 * LANE, tm), lambda i: (i // lt, 0, i % lt))])
    out_shape = [jax.ShapeDtypeStruct(s, F32) for s in outs]
    prev_specs = [pl.BlockSpec((nprev, None, 4 * LANE, tm), lambda i: (0, i // lt, 0, i % lt))] if nprev else []
    G = NSA_KV_HEADS
    if sel_tile is not None:
        out_specs += [pl.BlockSpec((None, G, tm, 2 * LANE), lambda i: (i // lt, 0, i % lt, 0)),
                      pl.BlockSpec((None, G, tm // sel_tile, HEAD_DIM, sel_tile),
                                   lambda i: (i // lt, 0, i % lt, 0, 0))]
        out_shape += [jax.ShapeDtypeStruct((nseq, G, lseq, 2 * LANE), BF16),
                      jax.ShapeDtypeStruct((nseq, G, lseq // sel_tile, HEAD_DIM, sel_tile), BF16)]
    return pl.pallas_call(
        functools.partial(_proj_kernel, lt=lt, tk=sel_tile, nprev=nprev),
        grid=(n // tm,),
        in_specs=[pl.BlockSpec((tm, D_MODEL), row), pl.BlockSpec((1, D_MODEL), const)]
        + [pl.BlockSpec(w.shape, const) for w in (wq, wkv, wdn, wz, wu, ws, wexp)] + prev_specs,
        out_specs=out_specs,
        out_shape=out_shape,
        compiler_params=_cparams(("parallel",)),
        name="proj",
    )(x, ln, wq, wkv, wdn, wz, wu, ws, wexp, *([prev_kvt] if nprev else []))


def _topk_round(sel, score, blk, nblk):
    m = jnp.max(score, axis=0, keepdims=True)
    idx = jnp.min(jnp.where(score == m, blk, nblk), axis=0, keepdims=True)
    pick = blk == idx
    return sel | pick, jnp.where(pick, -jnp.inf, score)


NSA_FORCED = 3
NSA_FREE_ROUNDS = NSA_TOPK - NSA_FORCED


def _free_scores(imp, forced, blk, cur):
    return jnp.where(blk > cur, -BIG, jnp.where(forced, -jnp.inf, imp))


def _topk_blocks(score, blk, nblk):
    sel = jnp.zeros(score.shape, dtype=jnp.bool_)
    for _ in range(NSA_FREE_ROUNDS):
        sel, score = _topk_round(sel, score, blk, nblk)
    return sel


def _cmp_heads(q_tile, kc, vct, t, blk, nblk_valid, write_oc):
    nblk = blk.shape[0]
    ok = ((blk + 1) * NSA_BLOCK - 1 <= t) & (blk < nblk_valid)
    okf = ok.astype(F32)
    dist = t.astype(F32) - (blk.astype(F32) * NSA_BLOCK + (NSA_BLOCK - 1) / 2.0)
    cur = t // NSA_BLOCK
    forced = (blk == 0) | (blk == cur) | (blk == cur - 1)
    sels = []
    for g in range(NSA_KV_HEADS):
        imp = jnp.zeros(blk.shape, F32)
        for r in range(NSA_GROUP):
            h = g * NSA_GROUP + r
            q = (q_tile(h) * HEAD_DIM ** -0.5).astype(BF16)
            s = _dot_nt(kc, q)
            s = jnp.where(ok, s - _slope(h) * dist, NEG)
            m = jnp.max(s, axis=0, keepdims=True)
            p = jnp.exp(s - m) * okf
            p = p / jnp.maximum(jnp.sum(p, axis=0, keepdims=True), 1e-30)
            imp = imp + p
            write_oc(h, _dotf(vct, p.astype(BF16)))
        sel = (forced | _topk_blocks(_free_scores(imp, forced, blk, cur), blk, nblk)) & (blk <= cur)
        sels.append(sel)
    return sels


def _cmp_kernel(q_ref, kcv_ref, oc_ref, sel_ref, any_ref, *, tq):
    qt = pl.program_id(1)
    kcv = kcv_ref[...]
    nblk = kcv.shape[0]
    kc = kcv[:, :LANE].astype(BF16)
    vct = kcv[:, LANE:].T.astype(BF16)
    blk = lax.broadcasted_iota(jnp.int32, (nblk, tq), 0)
    t = qt * tq + lax.broadcasted_iota(jnp.int32, (nblk, tq), 1)

    def write_oc(h, oct):
        oc_ref[:, h * LANE:(h + 1) * LANE] = oct.T

    sels = _cmp_heads(lambda h: q_ref[:, h * LANE:(h + 1) * LANE], kc, vct, t, blk, nblk, write_oc)
    for g in range(NSA_KV_HEADS):
        selt = sels[g].astype(F32).T
        sel_ref[:, g * LANE:(g + 1) * LANE] = selt
        any_ref[:, g * LANE:(g + 1) * LANE] = jnp.broadcast_to(jnp.max(selt, axis=0, keepdims=True), (8, LANE))


def _cmp_prompt(qh, pool, nb, lq, tq):
    nq = lq // tq
    nblk = lq // NSA_BLOCK
    assert nblk == LANE
    return pl.pallas_call(
        functools.partial(_cmp_kernel, tq=tq),
        grid=(nb, nq),
        in_specs=[pl.BlockSpec((tq, 1024), lambda b, i: (b * nq + i, 0)),
                  pl.BlockSpec((nblk, 2 * LANE), lambda b, i: (b, 0))],
        out_specs=[pl.BlockSpec((tq, 1024), lambda b, i: (b * nq + i, 0)),
                   pl.BlockSpec((tq, 2 * LANE), lambda b, i: (b * nq + i, 0)),
                   pl.BlockSpec((8, 2 * LANE), lambda b, i: (b * nq + i, 0))],
        out_shape=[jax.ShapeDtypeStruct((nb * lq, 1024), F32),
                   jax.ShapeDtypeStruct((nb * lq, 2 * LANE), F32),
                   jax.ShapeDtypeStruct((nb * nq * 8, 2 * LANE), F32)],
        compiler_params=_cparams(("parallel", "parallel")),
        name="nsa_cmp",
    )(qh, pool)


def _sel_kernel(act_ref, q_ref, sel_ref, k_ref, vt_ref, o_ref, lhs_s, m_s, l_s, acc_s, lst_s, *, tq, nq):
    b = pl.program_id(0)
    g = pl.program_id(1)
    qt = pl.program_id(2)
    lanei = lax.broadcasted_iota(jnp.int32, (tq, LANE), 1)
    nio = lanei.astype(F32)
    selg = sel_ref[...] > 0.5
    for r in range(NSA_GROUP):
        slope = jnp.where(g == 0, _slope(r), _slope(NSA_GROUP + r))
        qt_r = q_ref[:, r * LANE:(r + 1) * LANE]
        qt_r = jnp.where(g == 0, qt_r, pltpu.roll(qt_r, HEAD_DIM, axis=1))
        qpart = jnp.where(lanei == HEAD_DIM, slope, qt_r * HEAD_DIM ** -0.5)
        bias = jnp.where(selg, (NSA_BLOCK * slope) * nio, NEG)
        lhs_s[r] = jnp.concatenate([bias.astype(BF16), qpart.astype(BF16)], axis=1)
    m_s[...] = jnp.full(m_s.shape, NEG, F32)
    l_s[...] = jnp.zeros(l_s.shape, F32)
    acc_s[...] = jnp.zeros(acc_s.shape, F32)

    def tiles(items):
        heads = range(NSA_GROUP)
        kaugs = [k_ref[kt] for kt, _ in items]
        vts = [vt_ref[kt] for kt, _ in items]
        ss = [[_dot_nt(ka, lhs_s[r]) for r in heads] for ka in kaugs]
        ss = [[s if mask is None else jnp.where(mask, s, NEG) for s in row] for row, (_, mask) in zip(ss, items)]
        m_old = [m_s[r] for r in heads]
        m_new = []
        for r in heads:
            m = m_old[r]
            for row in ss:
                m = jnp.maximum(m, jnp.max(row[r], axis=0, keepdims=True))
            m_new.append(m)
        ps = [[jnp.exp(row[r] - m_new[r]) for r in heads] for row in ss]
        pvs = [[_dotf(vt, row[r].astype(BF16)) for r in heads] for vt, row in zip(vts, ps)]
        for r in heads:
            alpha = jnp.exp(m_old[r] - m_new[r])
            l_s[r] = alpha * l_s[r] + sum(jnp.sum(row[r], axis=0, keepdims=True) for row in ps)
            acc_s[r] = alpha * acc_s[r] + sum(row[r] for row in pvs)
            m_s[r] = m_new[r]

    abase = ((b * NSA_KV_HEADS + g) * nq + qt) * nq

    def compact(kt, n):
        lst_s[n] = kt
        return n + act_ref[abase + kt]

    n_act = lax.fori_loop(0, qt, compact, 0)

    def body(i, carry):
        tiles([(lst_s[2 * i], None), (lst_s[2 * i + 1], None)])
        return carry

    lax.fori_loop(0, n_act // 2, body, 0)
    diag = (qt, lax.broadcasted_iota(jnp.int32, (tq, tq), 0) <= lax.broadcasted_iota(jnp.int32, (tq, tq), 1))

    @pl.when(n_act % 2 == 1)
    def _():
        tiles([(lst_s[n_act - 1], None), diag])

    @pl.when(n_act % 2 == 0)
    def _():
        tiles([diag])

    zeros = jnp.zeros((HEAD_DIM, tq), F32)
    for r in range(NSA_GROUP):
        o = jnp.concatenate([acc_s[r] / l_s[r], zeros], axis=0).T
        o_ref[:, r * LANE:(r + 1) * LANE] = jnp.where(g == 0, o, pltpu.roll(o, HEAD_DIM, axis=1))


def _sel_prompt(qh, sel, anyblk, kaug, vt, nb, lq, tq):
    assert vt.shape[-1] == tq
    nq = lq // tq
    bpt = tq // NSA_BLOCK
    G = NSA_KV_HEADS
    act = anyblk[::8].reshape(nb, nq, G, nq, bpt).max(axis=-1) > 0.5
    act = act.transpose(0, 2, 1, 3).reshape(-1).astype(jnp.int32)
    kaug = kaug.reshape(nb, G, nq, tq, 2 * LANE)
    qmap = lambda b, g, i, a: (b * nq + i, g)
    whole = lambda b, g, i, a: (b, g, 0, 0, 0)
    grid_spec = pltpu.PrefetchScalarGridSpec(
        num_scalar_prefetch=1,
        grid=(nb, G, nq),
        in_specs=[pl.BlockSpec((tq, NSA_GROUP * LANE), qmap),
                  pl.BlockSpec((tq, LANE), qmap),
                  pl.BlockSpec((None, None, nq, tq, 2 * LANE), whole),
                  pl.BlockSpec((None, None, nq, HEAD_DIM, tq), whole)],
        out_specs=pl.BlockSpec((tq, NSA_GROUP * LANE), qmap),
        scratch_shapes=[pltpu.VMEM((NSA_GROUP, tq, 2 * LANE), BF16),
                        pltpu.VMEM((NSA_GROUP, 1, tq), F32),
                        pltpu.VMEM((NSA_GROUP, 1, tq), F32),
                        pltpu.VMEM((NSA_GROUP, HEAD_DIM, tq), F32),
                        pltpu.SMEM((nq,), jnp.int32)])
    return pl.pallas_call(
        functools.partial(_sel_kernel, tq=tq, nq=nq),
        grid_spec=grid_spec,
        out_shape=jax.ShapeDtypeStruct((nb * lq, 1024), F32),
        compiler_params=_cparams(("parallel", "parallel", "arbitrary")),
        name="nsa_sel",
    )(act, qh, sel, kaug, vt)


def _gate_mix(small, oc, os_, ow, h):
    gt = jax.nn.sigmoid(small[:, 3 * h:3 * h + 3])
    return gt[:, 0:1] * oc + gt[:, 1:2] * os_ + gt[:, 2:3] * ow


WIN_HEADS_PER_ROUND = 4
_WIN_SPLIT = 32


def _win_kernel(q_ref, k0_ref, k1_ref, k2_ref, v0_ref, v1_ref, v2_ref, oc_ref, os_ref, sm_ref, o_ref, *, tq):
    qt = pl.program_id(1)
    nkey = 3 * tq
    ki = lax.broadcasted_iota(jnp.int32, (nkey, LANE), 0)
    li = lax.broadcasted_iota(jnp.int32, (nkey, LANE), 1)
    penc = jnp.where(li == 0, ki // _WIN_SPLIT, jnp.where(li == 1, ki % _WIN_SPLIT, 0)).astype(BF16)
    kcat = jnp.concatenate([k0_ref[...], k1_ref[...], k2_ref[...]], axis=0).astype(BF16)
    kaug = jnp.concatenate([kcat, penc], axis=1)
    vts = [v.astype(BF16) for v in (v0_ref[...], v1_ref[...], v2_ref[...])]
    krow = lax.broadcasted_iota(jnp.int32, (tq, tq), 0)
    qcol = lax.broadcasted_iota(jnp.int32, (tq, tq), 1)
    assert WINDOW == 2 * tq
    valids = [(krow > qcol) & (qt >= 2), jnp.broadcast_to(qt >= 1, (tq, tq)), krow <= qcol]
    small = sm_ref[...]
    lq_i = lax.broadcasted_iota(jnp.int32, (tq, LANE), 1)
    for h0 in range(0, NSA_HEADS, WIN_HEADS_PER_ROUND):
        heads = range(h0, h0 + WIN_HEADS_PER_ROUND)
        qaug = []
        for h in heads:
            al = jnp.where(lq_i == 0, _WIN_SPLIT * _slope(h), jnp.where(lq_i == 1, _slope(h), 0.0))
            qaug.append(jnp.concatenate([(q_ref[:, h * LANE:(h + 1) * LANE] * HEAD_DIM ** -0.5).astype(BF16),
                                         al.astype(BF16)], axis=1))
        tiles3 = range(3)
        ss = [[jnp.where(valids[j], _dot_nt(kaug[j * tq:(j + 1) * tq], qa), NEG) for j in tiles3]
              for qa in qaug]
        ms = [functools.reduce(jnp.maximum, [jnp.max(s, axis=0, keepdims=True) for s in row]) for row in ss]
        ps = [[jnp.exp(s - m) for s in row] for row, m in zip(ss, ms)]
        ows = []
        for row in ps:
            num = sum(_dotf(vts[j], row[j].astype(BF16)) for j in tiles3)
            den = sum(jnp.sum(row[j], axis=0, keepdims=True) for j in tiles3)
            ows.append((num / den).T)
        for h, ow in zip(heads, ows):
            sl = slice(h * LANE, (h + 1) * LANE)
            o_ref[:, sl] = _gate_mix(small, oc_ref[:, sl], os_ref[:, sl], ow, h).astype(o_ref.dtype)


def _win_prompt(qh, kv, kvt_win, oc, osel, small, nb, lq, tq=256):
    assert 2 * tq >= WINDOW - 1 and 3 * tq % _WIN_SPLIT == 0 and 3 * tq // _WIN_SPLIT <= 256
    nq = lq // tq
    row = lambda b, i: (b * nq + i, 0)

    def kmap(back):
        return lambda b, i: (b * nq + jnp.maximum(i - back, 0), 4)

    def vmap(back):
        return lambda b, i: (b, 1, jnp.maximum(i - back, 0))

    return pl.pallas_call(
        functools.partial(_win_kernel, tq=tq),
        grid=(nb, nq),
        in_specs=[pl.BlockSpec((tq, 1024), row)]
        + [pl.BlockSpec((tq, LANE), kmap(back)) for back in (2, 1, 0)]
        + [pl.BlockSpec((None, LANE, tq), vmap(back)) for back in (2, 1, 0)]
        + [pl.BlockSpec((tq, 1024), row), pl.BlockSpec((tq, 1024), row), pl.BlockSpec((tq, LANE), row)],
        out_specs=pl.BlockSpec((tq, 1024), row),
        out_shape=jax.ShapeDtypeStruct((nb * lq, 1024), BF16),
        compiler_params=_cparams(("parallel", "parallel")),
        name="nsa_win",
    )(qh, kv, kv, kv, kvt_win, kvt_win, kvt_win, oc, osel, small)


def _per_head(idx_h, fn):
    out = jnp.zeros(idx_h.shape, F32)
    for h in range(NSA_HEADS):
        out = jnp.where(idx_h == h, fn(h), out)
    return out


def _pad_rows(a, rows):
    return jnp.concatenate([a, jnp.zeros((rows - a.shape[0], a.shape[1]), a.dtype)], axis=0)


NSA_SAMPLE_SEQS = 2


def _nsa_sample_kernel(pt_ref, q_ref, kvn_ref, sm_ref, *rest, past, lq, npages, page, nprev):
    nseq = NSA_SAMPLE_SEQS
    pages = rest[:nseq * npages]
    rest = rest[nseq * npages:]
    if nprev:
        win_ref, wt_ref, prev_ref, o_ref, nwin_all, ktc_s, vtc_s, rt_s, vt_s = rest
        nwin_all[0:nprev] = prev_ref[...]
    else:
        win_ref, wt_ref, o_ref, nwin_all, ktc_s, vtc_s, rt_s, vt_s = rest
    nwin_ref = nwin_all.at[nprev]

    @pl.when(pl.program_id(0) == 0)
    def _():
        n_i = lax.broadcasted_iota(jnp.int32, (LANE, past), 0)
        pos_i = lax.broadcasted_iota(jnp.int32, (LANE, past), 1)
        for j in range(nseq):
            rt_s[j, 0:LANE, :] = (n_i == pos_i // NSA_BLOCK).astype(BF16)

    def rows(ref, j):
        return ref.at[pl.ds(j * lq, lq)]

    seqs = [_nsa_sample_seq(rows(q_ref, j), rows(kvn_ref, j), rows(sm_ref, j), pages[j * npages:(j + 1) * npages],
                            win_ref.at[j], wt_ref, rows(o_ref, j), nwin_ref.at[j], ktc_s.at[j], vtc_s.at[j],
                            rt_s.at[j], vt_s.at[j], past=past, lq=lq, npages=npages, page=page)
            for j in range(nseq)]
    live = True
    while live:
        for s in seqs:
            live = next(s, None) is not None and live


def _nsa_sample_seq(q_ref, kvn_ref, sm_ref, pages, win_ref, wt_ref, o_ref, nwin_ref, ktc_s, vtc_s, rt_s, vt_s,
                    *, past, lq, npages, page):
    nkeys = npages * page
    nrow = NSA_HEADS * lq
    nb_complete = (past + lq) // NSA_BLOCK
    new_blk = past // NSA_BLOCK
    assert past % NSA_BLOCK + lq <= NSA_BLOCK and nkeys == past and new_blk < LANE

    for i in range(npages):
        sl = slice(i * page, (i + 1) * page)
        ktc_s[:, sl] = pages[i][0:LANE, :].astype(BF16)
        vtc_s[:, sl] = pages[i][LANE:2 * LANE, :].astype(BF16)
    yield True

    def stage_selected_page(i):
        sl = slice(i * page, (i + 1) * page)
        rt_s[LANE:2 * LANE, sl] = pages[i][2 * LANE:3 * LANE, :].astype(BF16)
        vt_s[:, sl] = pages[i][3 * LANE:4 * LANE, :].astype(BF16)

    q8 = q_ref[...]
    qf = jnp.concatenate([q8[:, h * LANE:(h + 1) * LANE] for h in range(NSA_HEADS)], axis=0) * HEAD_DIM ** -0.5
    qb = qf.astype(BF16)
    kvn = kvn_ref[...]
    lane_lo = lax.broadcasted_iota(jnp.int32, (LANE, LANE), 1) < HEAD_DIM

    ktc = ktc_s[...]
    vtc = vtc_s[...]
    kc = jnp.where(lane_lo, _dot_nt(wt_ref[0, 0], ktc), _dot_nt(wt_ref[0, 1], ktc)).astype(BF16)
    vc = jnp.where(lane_lo, _dot_nt(wt_ref[1, 0], vtc), _dot_nt(wt_ref[1, 1], vtc))
    vct = vc.T.astype(BF16)
    blk = lax.broadcasted_iota(jnp.int32, (LANE, nrow), 0)
    col = lax.broadcasted_iota(jnp.int32, (LANE, nrow), 1)
    t = past + col % lq
    slope_c = _per_head(col // lq, _slope)
    ok = ((blk + 1) * NSA_BLOCK - 1 <= t) & (blk < nb_complete)
    dist = t.astype(F32) - (blk.astype(F32) * NSA_BLOCK + (NSA_BLOCK - 1) / 2.0)
    s = jnp.where(ok, _dot_nt(kc, qb) - slope_c * dist, NEG)
    m = jnp.max(s, axis=0, keepdims=True)
    p = jnp.exp(s - m) * ok.astype(F32)
    p = p / jnp.maximum(jnp.sum(p, axis=0, keepdims=True), 1e-30)
    o_c = _dotf(vct, p.astype(BF16)).T
    yield True
    ri = lax.broadcasted_iota(jnp.int32, (nrow, nrow), 0)
    ci = lax.broadcasted_iota(jnp.int32, (nrow, nrow), 1)
    gsum = ((ri // (NSA_GROUP * lq) == ci // (NSA_GROUP * lq)) & (ri % lq == ci % lq)).astype(F32)
    imp = _dot_hi(p, gsum)
    cur = t // NSA_BLOCK
    forced = (blk == 0) | (blk == cur) | (blk == cur - 1)
    score = _free_scores(imp, forced, blk, cur)
    rowi = lax.broadcasted_iota(jnp.int32, (nrow, LANE), 0)
    lanei = lax.broadcasted_iota(jnp.int32, (nrow, LANE), 1)
    slope_r = _per_head(rowi[:, 0:1] // lq, _slope)
    tok_r = rowi % lq
    new_ok = (lanei <= tok_r) & (lanei < lq)

    wlen = win_ref.shape[1]
    wi = lax.broadcasted_iota(jnp.int32, (nrow, wlen), 1)
    tok_w = lax.broadcasted_iota(jnp.int32, (nrow, wlen), 0) % lq
    dist_w = wlen + tok_w - wi
    s_w = _dotf(qb, win_ref[0:LANE, :].astype(BF16))
    s_w = jnp.where(dist_w < WINDOW, s_w - slope_r * dist_w.astype(F32), NEG)
    kwn = _pad_rows(kvn[:, 4 * LANE:5 * LANE], LANE).astype(BF16)
    s_wn = jnp.where(new_ok, _dot_nt(qb, kwn) - slope_r * (tok_r - lanei).astype(F32), NEG)
    m = jnp.maximum(jnp.max(s_w, axis=1, keepdims=True), jnp.max(s_wn, axis=1, keepdims=True))
    p_w = jnp.exp(s_w - m)
    p_wn = jnp.exp(s_wn - m)
    vwn = _pad_rows(kvn[:, 5 * LANE:6 * LANE], LANE).astype(BF16)
    o_w = (_dot_nt(p_w.astype(BF16), win_ref[LANE:2 * LANE, :].astype(BF16)) + _dotf(p_wn.astype(BF16), vwn)) / (
        jnp.sum(p_w, axis=1, keepdims=True) + jnp.sum(p_wn, axis=1, keepdims=True))

    win = win_ref[...]
    new_t = _pad_rows(kvn[:, 4 * LANE:6 * LANE], LANE).T
    new_t = pltpu.roll(new_t, LANE - lq, axis=1)
    tail = jnp.concatenate([jnp.zeros((2 * LANE, wlen - LANE), F32), new_t], axis=1)
    lane_w = lax.broadcasted_iota(jnp.int32, win.shape, 1)
    nwin_ref[...] = jnp.where(lane_w < wlen - lq, pltpu.roll(win, wlen - lq, axis=1), tail)
    yield True

    picked = forced
    for i in range(max(NSA_FREE_ROUNDS, npages)):
        if i < npages:
            stage_selected_page(i)
        if i < NSA_FREE_ROUNDS:
            picked, score = _topk_round(picked, score, blk, LANE)
        yield True
    sel = (picked & (blk <= cur)).astype(F32).T

    bias = jnp.where(sel > 0.5, (NSA_BLOCK * slope_r) * lanei.astype(F32), NEG)
    lhs = jnp.concatenate([bias.astype(BF16), qb], axis=1)
    jrow = (lax.broadcasted_iota(jnp.int32, (1, nkeys), 1) % NSA_BLOCK).astype(F32)
    s_p = _dotf(lhs, rt_s[...]) + slope_r * jrow
    bias_new = jnp.sum(jnp.where(lanei == new_blk, bias, 0.0), axis=1, keepdims=True)
    jnew = (past % NSA_BLOCK + lanei).astype(F32)
    kn = _pad_rows(kvn[:, 2 * LANE:3 * LANE], LANE).astype(BF16)
    s_n = jnp.where(new_ok, _dot_nt(qb, kn) + bias_new + slope_r * jnew, NEG)
    yield True
    m = jnp.maximum(jnp.max(s_p, axis=1, keepdims=True), jnp.max(s_n, axis=1, keepdims=True))
    p_p = jnp.exp(s_p - m)
    p_n = jnp.exp(s_n - m)
    vn = _pad_rows(kvn[:, 3 * LANE:4 * LANE], LANE).astype(BF16)
    o_s = (_dot_nt(p_p.astype(BF16), vt_s[...]) + _dotf(p_n.astype(BF16), vn)) / (
        jnp.sum(p_p, axis=1, keepdims=True) + jnp.sum(p_n, axis=1, keepdims=True))

    small = sm_ref[...]
    gate = [jax.nn.sigmoid(jnp.concatenate([small[:, 3 * h + c:3 * h + c + 1] for h in range(NSA_HEADS)], axis=0))
            for c in range(3)]
    o = gate[0] * o_c + gate[1] * o_s + gate[2] * o_w
    for h in range(NSA_HEADS):
        o_ref[:, h * LANE:(h + 1) * LANE] = o[h * lq:(h + 1) * lq].astype(o_ref.dtype)


def _pool_weights_t(wck, wcv, past):
    pos = jnp.arange(past)
    onehot = (jnp.arange(LANE)[:, None] == (pos // NSA_BLOCK)[None, :]).astype(F32)
    w = jnp.stack([wck, wcv])[:, :, pos % NSA_BLOCK]
    return (w[:, :, None, :] * onehot[None, None]).astype(BF16)


def _nsa_sample(page_table, qh, kv, small, cache_t, win_t, wt, prev_win, *, layer, lq, past):
    nprev = 0 if prev_win is None else prev_win.shape[0]
    assert nprev == layer
    nb, npages = page_table.shape
    page = cache_t.shape[-1]
    wlen = win_t.shape[-1]
    ns = NSA_SAMPLE_SEQS
    assert nb % ns == 0
    tokmap = lambda b, pt: (b, 0)

    def page_spec(j, i):
        return pl.BlockSpec((None, None, 4 * LANE, page), lambda b, pt: (layer, pt[ns * b + j, i], 0, 0))

    grid_spec = pltpu.PrefetchScalarGridSpec(
        num_scalar_prefetch=1,
        grid=(nb // ns,),
        in_specs=[pl.BlockSpec((ns * lq, 1024), tokmap), pl.BlockSpec((ns * lq, 768), tokmap),
                  pl.BlockSpec((ns * lq, LANE), tokmap)]
        + [page_spec(j, i) for j in range(ns) for i in range(npages)]
        + [pl.BlockSpec((None, ns, 2 * LANE, wlen), lambda b, pt: (layer, b, 0, 0)),
           pl.BlockSpec(wt.shape, lambda b, pt: (0, 0, 0, 0))]
        + ([pl.BlockSpec((nprev, ns, 2 * LANE, wlen), lambda b, pt: (0, b, 0, 0))] if nprev else []),
        out_specs=[pl.BlockSpec((ns * lq, 1024), tokmap),
                   pl.BlockSpec((nprev + 1, ns, 2 * LANE, wlen), lambda b, pt: (0, b, 0, 0))],
        scratch_shapes=[pltpu.VMEM((ns, LANE, past), BF16), pltpu.VMEM((ns, LANE, past), BF16),
                        pltpu.VMEM((ns, 2 * LANE, past), BF16), pltpu.VMEM((ns, LANE, past), BF16)])
    return pl.pallas_call(
        functools.partial(_nsa_sample_kernel, past=past, lq=lq, npages=npages, page=page, nprev=nprev),
        grid_spec=grid_spec,
        out_shape=[jax.ShapeDtypeStruct((nb * lq, 1024), BF16),
                   jax.ShapeDtypeStruct((nprev + 1, nb, 2 * LANE, wlen), F32)],
        compiler_params=_cparams(("arbitrary",)),
        name="nsa_sample",
    )(page_table, qh, kv, small, *([cache_t] * (ns * npages)), win_t, wt, *([prev_win] if nprev else []))


GDN_ROWS = 64


def _softplus(x):
    return jnp.maximum(x, 0.0) + jnp.log(1.0 + jnp.exp(-jnp.abs(x)))


def _gdn_kernel(x_ref, hist_ref, sm_ref, z_ref, cw_ref, alog_ref, dtb_ref, nw_ref, s0_ref,
                o_ref, s_ref, *, seg, zero_first_hist):
    c = pl.program_id(1)
    rows = GDN_ROWS
    nseg = rows // seg

    @pl.when(c == 0)
    def _():
        s_ref[...] = s0_ref[...]

    x = x_ref[...]
    hist = hist_ref[...]
    if zero_first_hist:
        hist = jnp.where(c == 0, 0.0, hist)
    tpos = lax.broadcasted_iota(jnp.int32, x.shape, 0) % seg
    cw = cw_ref[...]
    y = x * cw[DN_CONV_W - 1:DN_CONV_W]
    for sh in range(1, DN_CONV_W):
        xs = jnp.where(tpos >= sh, pltpu.roll(x, sh, axis=0), pltpu.roll(hist, sh, axis=0))
        y = y + xs * cw[DN_CONV_W - 1 - sh:DN_CONV_W - sh]
    y = y * jax.nn.sigmoid(y)

    ri = lax.broadcasted_iota(jnp.int32, (rows, rows), 0)
    ci = lax.broadcasted_iota(jnp.int32, (rows, rows), 1)
    same = (ri // seg) == (ci // seg)
    tri = same & (ci <= ri)
    strict = same & (ci < ri)
    trif = tri.astype(F32)
    small = sm_ref[...]
    z = z_ref[...]
    heads = range(DN_HEADS)
    segs = range(nseg)
    hs = lambda h: slice(h * HEAD_DIM, (h + 1) * HEAD_DIM)
    yqs = [y[:, hs(h)] for h in heads]
    yks = [y[:, DN_WIDTH + h * HEAD_DIM:DN_WIDTH + (h + 1) * HEAD_DIM] for h in heads]
    vs = [y[:, 2 * DN_WIDTH + h * HEAD_DIM:2 * DN_WIDTH + (h + 1) * HEAD_DIM] for h in heads]
    qs = [a * lax.rsqrt(jnp.sum(a * a, axis=-1, keepdims=True) + 1e-6) * HEAD_DIM ** -0.5 for a in yqs]
    ks = [a * lax.rsqrt(jnp.sum(a * a, axis=-1, keepdims=True) + 1e-6) for a in yks]
    betas = [jax.nn.sigmoid(small[:, 28 + h:29 + h]) for h in heads]
    gs = [-jnp.exp(alog_ref[:, h:h + 1]) * _softplus(small[:, 24 + h:25 + h] + dtb_ref[:, h:h + 1]) for h in heads]
    trifs = _split(trif)
    gcbs = [_dot3s(trifs, _split(jnp.broadcast_to(g, (rows, rows)))) for g in gs]
    decays = [jnp.where(tri, jnp.exp(jnp.where(tri, gcb - gcb.T, 0.0)), 0.0) for gcb in gcbs]
    kbs = [ks[h] * betas[h] for h in heads]
    amats = [jnp.where(strict, _dot3_nt(kbs[h], ks[h]) * decays[h], 0.0) for h in heads]
    aqks = [jnp.where(tri, _dot3_nt(qs[h], ks[h]) * decays[h], 0.0) for h in heads]
    eye = (ri == ci).astype(F32)
    invs = [eye - a for a in amats]
    pws = amats
    span = 2
    while span < seg:
        pwsp = [_split(p) for p in pws]
        pws = [_dot3s(p, p) for p in pwsp]
        invs = [invs[h] + _dot3(invs[h], pws[h]) for h in heads]
        span *= 2
    egcs = [jnp.exp(gcb[:, :HEAD_DIM]) for gcb in gcbs]
    sols = [_dot3(invs[h], jnp.concatenate([vs[h] * betas[h], kbs[h] * egcs[h]], axis=1)) for h in heads]
    us = [s[:, :HEAD_DIM] for s in sols]
    ws = [s[:, HEAD_DIM:] for s in sols]
    qgs = [qs[h] * egcs[h] for h in heads]
    rsl = lambda s: slice(s * seg, (s + 1) * seg)
    sts = [[s_ref[s, h] for s in segs] for h in heads]
    stsp = [[_split(st) for st in row] for row in sts]
    vns = [jnp.concatenate([us[h][rsl(s)] - _dot3s(_split(ws[h][rsl(s)]), stsp[h][s]) for s in segs], axis=0)
           for h in heads]
    oqs = [jnp.concatenate([_dot3s(_split(qgs[h][rsl(s)]), stsp[h][s]) for s in segs], axis=0) for h in heads]
    vnsp = [_split(v) for v in vns]
    os_ = [oqs[h] + _dot3s(_split(aqks[h]), vnsp[h]) for h in heads]
    glasts = [[gcbs[h][(s + 1) * seg - 1:(s + 1) * seg, :HEAD_DIM] for s in segs] for h in heads]
    gl_rows = [jnp.concatenate([jnp.broadcast_to(gl, (seg, HEAD_DIM)) for gl in glasts[h]], axis=0) for h in heads]
    kdts = [(ks[h] * jnp.exp(gl_rows[h] - gcbs[h][:, :HEAD_DIM])).T for h in heads]
    for h in heads:
        for s in segs:
            kds = kdts[h] if nseg == 1 else jnp.where((ci // seg) == s, kdts[h], 0.0)
            s_ref[s, h] = sts[h][s] * jnp.exp(glasts[h][s]) + _dot3s(_split(kds), vnsp[h])
    outs = []
    for h in heads:
        o = os_[h] * lax.rsqrt(jnp.mean(os_[h] * os_[h], axis=-1, keepdims=True) + NORM_EPS) * nw_ref[...]
        zh = z[:, hs(h)]
        outs.append(o * (zh * jax.nn.sigmoid(zh)))
    o_ref[...] = jnp.concatenate(outs, axis=1).astype(o_ref.dtype)


def _gdn(dnqkv, hist, small, z, cw, alog, dtb, nw, s0, *, seg, nb, nchunk, row0, hist_map, zero_first_hist):
    nseg = GDN_ROWS // seg
    rmap = lambda b, c: (row0 + b * nchunk + c, 0)
    const = lambda b, c: (0, 0)
    smap = lambda b, c: (b, 0, 0, 0)
    sblock = (nseg, DN_HEADS, HEAD_DIM, HEAD_DIM)
    return pl.pallas_call(
        functools.partial(_gdn_kernel, seg=seg, zero_first_hist=zero_first_hist),
        grid=(nb, nchunk),
        in_specs=[pl.BlockSpec((GDN_ROWS, 768), rmap), pl.BlockSpec((GDN_ROWS, 768), hist_map),
                  pl.BlockSpec((GDN_ROWS, LANE), rmap), pl.BlockSpec((GDN_ROWS, 256), rmap),
                  pl.BlockSpec(cw.shape, const), pl.BlockSpec(alog.shape, const),
                  pl.BlockSpec(dtb.shape, const), pl.BlockSpec(nw.shape, const),
                  pl.BlockSpec(sblock, smap)],
        out_specs=[pl.BlockSpec((GDN_ROWS, 256), lambda b, c: (b * nchunk + c, 0)),
                   pl.BlockSpec(sblock, smap)],
        out_shape=[jax.ShapeDtypeStruct((nb * nchunk * GDN_ROWS, 256), BF16),
                   jax.ShapeDtypeStruct(s0.shape, F32)],
        compiler_params=_cparams(("parallel", "arbitrary")),
        name="gdn",
    )(dnqkv, hist, small, z, cw, alog, dtb, nw, s0)


def _split(a):
    hi = a.astype(BF16)
    return hi, (a - hi.astype(F32)).astype(BF16)


def _dot3s(a, b):
    return _dotf(a[0], b[0]) + (_dotf(a[0], b[1]) + _dotf(a[1], b[0]))


def _dot3(a, b):
    return _dot3s(_split(a), _split(b))


def _dot3_nt(a, b):
    a, b = _split(a), _split(b)
    return _dot_nt(a[0], b[0]) + (_dot_nt(a[0], b[1]) + _dot_nt(a[1], b[0]))


GDN_PREP_CHUNKS = 4
GDN_PROMPT_CHUNK = 64


def _gdn_prep_kernel(x_ref, hist_ref, sm_ref, cw_ref, alog_ref, dtb_ref,
                     u_ref, w_ref, qg_ref, kd_ref, aqk_ref, egl_ref):
    c = pl.program_id(1)
    x = x_ref[...]
    rows = x.shape[0]
    hist = jnp.where(c == 0, 0.0, hist_ref[...])
    row8 = lax.broadcasted_iota(jnp.int32, hist.shape, 0)
    cw = cw_ref[...]
    y = x * cw[DN_CONV_W - 1:DN_CONV_W]
    for sh in range(1, DN_CONV_W):
        xs = pltpu.roll(x, sh, axis=0)
        top = jnp.where(row8 < sh, pltpu.roll(hist, sh, axis=0), xs[0:8])
        xs = jnp.concatenate([top, xs[8:]], axis=0)
        y = y + xs * cw[DN_CONV_W - 1 - sh:DN_CONV_W - sh]
    y = y * jax.nn.sigmoid(y)

    C = GDN_PROMPT_CHUNK
    ri = lax.broadcasted_iota(jnp.int32, (C, C), 0)
    ci = lax.broadcasted_iota(jnp.int32, (C, C), 1)
    tri = ci <= ri
    strict = ci < ri
    trif = tri.astype(F32)
    small = sm_ref[...]
    nch = rows // C
    units = [(ch, h) for ch in range(nch) for h in range(DN_HEADS)]
    each = lambda f, *ls: [f(*a) for a in zip(*ls)]

    def qkv_of(ch, h):
        rs = slice(ch * C, (ch + 1) * C)
        yq = y[rs, h * HEAD_DIM:(h + 1) * HEAD_DIM]
        yk = y[rs, DN_WIDTH + h * HEAD_DIM:DN_WIDTH + (h + 1) * HEAD_DIM]
        v = y[rs, 2 * DN_WIDTH + h * HEAD_DIM:2 * DN_WIDTH + (h + 1) * HEAD_DIM]
        q = yq * lax.rsqrt(jnp.sum(yq * yq, axis=-1, keepdims=True) + 1e-6) * HEAD_DIM ** -0.5
        k = yk * lax.rsqrt(jnp.sum(yk * yk, axis=-1, keepdims=True) + 1e-6)
        beta = jax.nn.sigmoid(small[rs, 28 + h:29 + h])
        g = -jnp.exp(alog_ref[:, h:h + 1]) * _softplus(small[rs, 24 + h:25 + h] + dtb_ref[:, h:h + 1])
        return q, k, v, beta, g

    qs_, ks_, vs_, betas, gs = zip(*[qkv_of(ch, h) for ch, h in units])
    trifs = _split(trif)
    gcbs = [_dot3s(trifs, _split(jnp.broadcast_to(g, (C, C)))) for g in gs]
    kbs = each(lambda k, b: k * b, ks_, betas)
    ksp = [_split(k) for k in ks_]
    kbsp = [_split(kb) for kb in kbs]
    qsp = [_split(q) for q in qs_]
    nt3 = lambda a, b: _dot_nt(a[0], b[0]) + (_dot_nt(a[0], b[1]) + _dot_nt(a[1], b[0]))
    kks = each(nt3, kbsp, ksp)
    qks = each(nt3, qsp, ksp)
    decays = [jnp.where(tri, jnp.exp(jnp.where(tri, gcb - gcb.T, 0.0)), 0.0) for gcb in gcbs]
    amats = each(lambda kk, d: jnp.where(strict, kk * d, 0.0), kks, decays)
    eye = (ri == ci).astype(F32)
    blk_mask = lambda n: (ri // n) == (ci // n)
    base = 8
    aprev = [jnp.where(blk_mask(base), a, 0.0) for a in amats]
    invs = [eye - a for a in aprev]
    pwsp = [_split(a) for a in aprev]
    span = 2
    while span < base:
        pwsp = [_split(_dot3s(p, p)) for p in pwsp]
        invs = each(lambda i, p: i + _dot3s(_split(i), p), invs, pwsp)
        span *= 2
    size = base
    while size < C:
        size *= 2
        acur = amats if size == C else [jnp.where(blk_mask(size), a, 0.0) for a in amats]
        invsp = [_split(i) for i in invs]
        cross = each(lambda isp, ac, ap: _dot3s(isp, _split(ac - ap)), invsp, acur, aprev)
        invs = each(lambda i, isp, cr: i - _dot3s(_split(cr), isp), invs, invsp, cross)
        aprev = acur
    gcs = [gcb[:, :HEAD_DIM] for gcb in gcbs]
    egcs = [jnp.exp(gc) for gc in gcs]
    rhss = each(lambda v, b, kb, e: jnp.concatenate([v * b, kb * e], axis=1), vs_, betas, kbs, egcs)
    sols = each(_dot3, invs, rhss)
    glasts = [gc[C - 1:C, :] for gc in gcs]
    qgs = each(lambda q, e: q * e, qs_, egcs)
    kds = each(lambda k, gl, gc: k * jnp.exp(gl - gc), ks_, glasts, gcs)
    aqks = each(lambda qk, d: jnp.where(tri, qk * d, 0.0), qks, decays)
    for ch in range(nch):
        rs = slice(ch * C, (ch + 1) * C)
        un = range(ch * DN_HEADS, (ch + 1) * DN_HEADS)
        u_ref[rs, :] = jnp.concatenate([sols[i][:, :HEAD_DIM] for i in un], axis=1)
        w_ref[rs, :] = jnp.concatenate([sols[i][:, HEAD_DIM:] for i in un], axis=1)
        qg_ref[rs, :] = jnp.concatenate([qgs[i] for i in un], axis=1)
        kd_ref[rs, :] = jnp.concatenate([kds[i] for i in un], axis=1)
        aqk_ref[rs, :] = jnp.concatenate([aqks[i] for i in un], axis=1)
        egl_ref[ch * 8:(ch + 1) * 8, :] = jnp.concatenate(
            [jnp.broadcast_to(jnp.exp(glasts[i]), (8, HEAD_DIM)) for i in un], axis=1)


def _gdn_scan_kernel(u_ref, w_ref, qg_ref, kd_ref, aqk_ref, egl_ref, z_ref, nw_ref, s0_ref, o_ref, s_ref):
    c = pl.program_id(0)

    @pl.when(c == 0)
    def _():
        s_ref[...] = s0_ref[...]

    nb = u_ref.shape[0]
    units = [(b, slice(h * HEAD_DIM, (h + 1) * HEAD_DIM), h) for b in range(nb) for h in range(DN_HEADS)]
    sts = [s_ref[b, h] for b, _, h in units]
    stsp = [_split(s) for s in sts]
    kdts = [_split(kd_ref[b, :, sl].T) for b, sl, _ in units]
    vnews = [u_ref[b, :, sl] - _dot3s(_split(w_ref[b, :, sl]), st) for (b, sl, _), st in zip(units, stsp)]
    oqs = [_dot3s(_split(qg_ref[b, :, sl]), st) for (b, sl, _), st in zip(units, stsp)]
    vsp = [_split(v) for v in vnews]
    C = u_ref.shape[1]
    os_ = [oq + _dot3s(_split(aqk_ref[b, :, h * C:(h + 1) * C]), vs) for (b, _, h), oq, vs in zip(units, oqs, vsp)]
    for (b, sl, h), st, kdt, vs in zip(units, sts, kdts, vsp):
        s_ref[b, h] = st * egl_ref[b, 0:1, sl] + _dot3s(kdt, vs)
    outs = []
    for (b, sl, _), o in zip(units, os_):
        o = o * lax.rsqrt(jnp.mean(o * o, axis=-1, keepdims=True) + NORM_EPS) * nw_ref[...]
        zh = z_ref[b, :, sl]
        outs.append(o * (zh * jax.nn.sigmoid(zh)))
    for b in range(nb):
        o_ref[b] = jnp.concatenate(outs[b * DN_HEADS:(b + 1) * DN_HEADS], axis=1).astype(o_ref.dtype)


def _gdn_prompt(dnqkv, small, z, cw, alog, dtb, nw, s0, *, nb, lq):
    cb = GDN_PREP_CHUNKS
    C = GDN_PROMPT_CHUNK
    rows = cb * C
    nstep = lq // rows
    nchunk = lq // C
    rmap = lambda b, c: (b * nstep + c, 0)
    const = lambda b, c: (0, 0)
    wide = jax.ShapeDtypeStruct((nb * lq, DN_WIDTH), F32)
    aqk_shape = jax.ShapeDtypeStruct((nb * lq, DN_HEADS * C), F32)
    u, w, qg, kd, aqk, egl = pl.pallas_call(
        _gdn_prep_kernel,
        grid=(nb, nstep),
        in_specs=[pl.BlockSpec((rows, 768), rmap),
                  pl.BlockSpec((8, 768), lambda b, c: (jnp.maximum((b * nstep + c) * (rows // 8) - 1, 0), 0)),
                  pl.BlockSpec((rows, LANE), rmap),
                  pl.BlockSpec(cw.shape, const), pl.BlockSpec(alog.shape, const), pl.BlockSpec(dtb.shape, const)],
        out_specs=[pl.BlockSpec((rows, DN_WIDTH), rmap)] * 4 + [pl.BlockSpec((rows, DN_HEADS * C), rmap),
                                                                pl.BlockSpec((cb * 8, DN_WIDTH), rmap)],
        out_shape=[wide] * 4 + [aqk_shape, jax.ShapeDtypeStruct((nb * nchunk * 8, DN_WIDTH), F32)],
        compiler_params=_cparams(("parallel", "parallel")),
        name="gdn_prep",
    )(dnqkv, dnqkv, small, cw, alog, dtb)
    r3 = lambda a: a.reshape(nb, -1, a.shape[-1])
    cmap = lambda c: (0, c, 0)
    full = lambda a: pl.BlockSpec(a.shape, lambda c: (0,) * a.ndim)
    o, s = pl.pallas_call(
        _gdn_scan_kernel,
        grid=(nchunk,),
        in_specs=[pl.BlockSpec((nb, C, DN_WIDTH), cmap)] * 4
        + [pl.BlockSpec((nb, C, DN_HEADS * C), cmap),
           pl.BlockSpec((nb, 8, DN_WIDTH), cmap), pl.BlockSpec((nb, C, DN_WIDTH), cmap),
           full(nw), full(s0)],
        out_specs=[pl.BlockSpec((nb, C, DN_WIDTH), cmap), full(s0)],
        out_shape=[jax.ShapeDtypeStruct((nb, lq, DN_WIDTH), BF16), jax.ShapeDtypeStruct(s0.shape, F32)],
        compiler_params=_cparams(("arbitrary",)),
        name="gdn_scan",
    )(r3(u), r3(w), r3(qg), r3(kd), r3(aqk), r3(egl), z[:nb * lq].reshape(nb, lq, DN_WIDTH), nw, s0)
    return o.reshape(nb * lq, DN_WIDTH), s


def _s5_discretize(lre_ref, lim_ref, lstep_ref):
    lr = lre_ref[...]
    li = lim_ref[...]
    dt = jnp.exp(lstep_ref[...])
    mag = jnp.exp(lr * dt)
    ar = mag * jnp.cos(li * dt)
    ai = mag * jnp.sin(li * dt)
    den = lr * lr + li * li
    fr = ((ar - 1.0) * lr + ai * li) / den
    fi = (ai * lr - (ar - 1.0) * li) / den
    return ar, ai, fr, fi


def _s5_kernel(*refs, R, S, with_y):
    if with_y:
        (u_ref, h0r_ref, h0i_ref, lre_ref, lim_ref, lstep_ref, bre_ref, bim_ref, cre_ref, cim_ref, d_ref,
         wglu_ref, y_ref, hr_ref, hi_ref, ar_s, ai_s, bbr_s, bbi_s, xr_s, xi_s) = refs
    else:
        (u_ref, h0r_ref, h0i_ref, lre_ref, lim_ref, lstep_ref, bre_ref, bim_ref,
         hr_ref, hi_ref, ar_s, ai_s, bbr_s, bbi_s, xr_s, xi_s) = refs
    i = pl.program_id(0)

    @pl.when(i == 0)
    def _():
        ar, ai, fr, fi = _s5_discretize(lre_ref, lim_ref, lstep_ref)
        ar_s[...] = ar
        ai_s[...] = ai
        bbr_s[...] = (fr * bre_ref[...] - fi * bim_ref[...]).astype(BF16)
        bbi_s[...] = (fr * bim_ref[...] + fi * bre_ref[...]).astype(BF16)
        hr_ref[...] = h0r_ref[...]
        hi_ref[...] = h0i_ref[...]

    u = u_ref[...]
    ub = u.astype(BF16)
    xr_s[...] = _dotf(ub, bbr_s[...])
    xi_s[...] = _dotf(ub, bbi_s[...])
    ar = jnp.broadcast_to(ar_s[...], (R, S5_LANES))
    ai = jnp.broadcast_to(ai_s[...], (R, S5_LANES))

    if not with_y:
        def carry_step(s, h):
            off = pl.multiple_of(s * R, R)
            return (ar * h[0] - ai * h[1] + xr_s[pl.ds(off, R), :], ar * h[1] + ai * h[0] + xi_s[pl.ds(off, R), :])

        hr, hi = lax.fori_loop(0, S, carry_step, (hr_ref[...], hi_ref[...]))
        hr_ref[...] = hr
        hi_ref[...] = hi
        return

    def advance(hr, hi, off):
        nr = ar * hr - ai * hi + xr_s[pl.ds(off, R), :]
        ni = ar * hi + ai * hr + xi_s[pl.ds(off, R), :]
        xr_s[pl.ds(off, R), :] = nr
        xi_s[pl.ds(off, R), :] = ni

    advance(hr_ref[...], hi_ref[...], 0)

    def step(s, carry):
        prev = pl.multiple_of((s - 1) * R, R)
        advance(xr_s[pl.ds(prev, R), :], xi_s[pl.ds(prev, R), :], pl.multiple_of(s * R, R))
        return carry

    lax.fori_loop(1, S, step, 0)
    hr_ref[...] = xr_s[pl.ds((S - 1) * R, R), :]
    hi_ref[...] = xi_s[pl.ds((S - 1) * R, R), :]
    if with_y:
        y = _dotf(xr_s[...].astype(BF16), cre_ref[...]) - _dotf(xi_s[...].astype(BF16), cim_ref[...])
        y = y + d_ref[...] * u
        gl = jax.nn.gelu(y)
        y_ref[...] = (gl * jax.nn.sigmoid(_dotf(gl.astype(BF16), wglu_ref[...]))).astype(y_ref.dtype)


def _s5_scan(u_rows, h0r, h0i, prm, *, R, S, with_y):
    n = u_rows.shape[0]
    rows = R * S
    const = lambda i: (0, 0)
    rmap = lambda i: (i, 0)
    ins = [u_rows, h0r, h0i, prm["lre"], prm["lim"], prm["lstep"], prm["bre"], prm["bim"]]
    if with_y:
        ins += [prm["cre"], prm["cim"], prm["d"], prm["wglu"]]
    in_specs = [pl.BlockSpec((rows, S5_WIDTH), rmap)] + [pl.BlockSpec(a.shape, const) for a in ins[1:]]
    st_spec = pl.BlockSpec((R, S5_LANES), const)
    st_shape = jax.ShapeDtypeStruct((R, S5_LANES), F32)
    out_specs = [st_spec, st_spec]
    out_shape = [st_shape, st_shape]
    if with_y:
        out_specs = [pl.BlockSpec((rows, S5_WIDTH), rmap)] + out_specs
        out_shape = [jax.ShapeDtypeStruct((n, S5_WIDTH), BF16)] + out_shape
    return pl.pallas_call(
        functools.partial(_s5_kernel, R=R, S=S, with_y=with_y),
        grid=(n // rows,),
        in_specs=in_specs,
        out_specs=out_specs,
        out_shape=out_shape,
        scratch_shapes=[pltpu.VMEM((1, S5_LANES), F32), pltpu.VMEM((1, S5_LANES), F32),
                        pltpu.VMEM((S5_WIDTH, S5_LANES), BF16), pltpu.VMEM((S5_WIDTH, S5_LANES), BF16),
                        pltpu.VMEM((rows, S5_LANES), F32), pltpu.VMEM((rows, S5_LANES), F32)],
        compiler_params=_cparams(("arbitrary",)),
        name="s5_scan",
    )(*ins)


def _s5_carry_kernel(er_ref, ei_ref, lre_ref, lim_ref, lstep_ref, ir_ref, ii_ref, fr_ref, fi_ref, *, nseg, nsteps):
    ar, ai, _, _ = _s5_discretize(lre_ref, lim_ref, lstep_ref)
    pr, pi_ = ar, ai
    n = 1
    while n < nsteps:
        pr, pi_ = pr * pr - pi_ * pi_, 2.0 * pr * pi_
        n *= 2
    for b in range(er_ref.shape[0] // nseg):
        cr = jnp.zeros((1, S5_LANES), F32)
        ci = jnp.zeros((1, S5_LANES), F32)
        for s in range(nseg):
            r = b * nseg + s
            ir_ref[r:r + 1, :] = cr
            ii_ref[r:r + 1, :] = ci
            er = er_ref[r:r + 1, :]
            ei = ei_ref[r:r + 1, :]
            cr, ci = pr * cr - pi_ * ci + er, pr * ci + pi_ * cr + ei
        fr_ref[b:b + 1, :] = cr
        fi_ref[b:b + 1, :] = ci


def _s5_carry(er, ei, prm, *, nseg, nsteps):
    assert nsteps & (nsteps - 1) == 0
    nb = er.shape[0] // nseg
    full = lambda a: pl.BlockSpec(a.shape, lambda: (0,) * a.ndim)
    ins = [er, ei, prm["lre"], prm["lim"], prm["lstep"]]
    outs = [jax.ShapeDtypeStruct(er.shape, F32)] * 2 + [jax.ShapeDtypeStruct((nb, S5_LANES), F32)] * 2
    return pl.pallas_call(
        functools.partial(_s5_carry_kernel, nseg=nseg, nsteps=nsteps),
        in_specs=[full(a) for a in ins],
        out_specs=[full(o) for o in outs],
        out_shape=outs,
        name="s5_carry",
    )(*ins)


S5_SEGS = 8


def _s5_prep(lre, lim, lstep, bre, bim, cre, cim, d, wglu):
    eye = jnp.eye(S5_GROUPS, dtype=F32)
    bexp = lambda b: jnp.einsum("gpc,gh->gchp", b, eye).reshape(S5_WIDTH, S5_LANES)
    cexp = lambda c: jnp.einsum("gcp,gh->gphc", c, eye).reshape(S5_LANES, S5_WIDTH).astype(BF16)
    return {"lre": lre.reshape(1, S5_LANES), "lim": lim.reshape(1, S5_LANES),
            "lstep": jnp.repeat(lstep, S5_STATE).reshape(1, S5_LANES),
            "bre": bexp(bre), "bim": bexp(bim), "cre": cexp(cre), "cim": cexp(cim),
            "d": d.reshape(1, S5_WIDTH), "wglu": wglu.astype(BF16)}


def _s5_prompt(u, prm, nb, lq, steps_per_tile=32):
    nsteps = lq // S5_SEGS
    R = nb * S5_SEGS
    u_rows = u.reshape(nb, S5_SEGS, nsteps, S5_WIDTH).transpose(2, 0, 1, 3).reshape(nsteps * R, S5_WIDTH)
    zero = jnp.zeros((R, S5_LANES), F32)
    er, ei = _s5_scan(u_rows, zero, zero, prm, R=R, S=steps_per_tile, with_y=False)
    ir, ii, fr, fi = _s5_carry(er, ei, prm, nseg=S5_SEGS, nsteps=nsteps)
    y_rows, _, _ = _s5_scan(u_rows, ir, ii, prm, R=R, S=steps_per_tile, with_y=True)
    y = y_rows.reshape(nsteps, nb, S5_SEGS, S5_WIDTH).transpose(1, 2, 0, 3).reshape(nb * lq, S5_WIDTH)
    return y, fr, fi


def _s5_sample(u, h0r, h0i, prm, nb, lq, steps_per_tile=4):
    u_rows = u.reshape(nb, lq, S5_WIDTH).transpose(1, 0, 2).reshape(lq * nb, S5_WIDTH)
    y_rows, fr, fi = _s5_scan(u_rows, h0r, h0i, prm, R=nb, S=steps_per_tile, with_y=True)
    y = y_rows.reshape(lq, nb, S5_WIDTH).transpose(1, 0, 2).reshape(nb * lq, S5_WIDTH)
    return y, fr, fi


def _sample_conv_hist(buf):
    nb = buf.shape[0]
    blk = buf.reshape(nb // 8, 8, DN_CONV_W - 1, 768)
    blk = jnp.roll(blk, -1, axis=1)
    blk = jnp.pad(blk, ((0, 0), (0, 0), (8 - (DN_CONV_W - 1), 0), (0, 0)))
    return blk.reshape(nb * 8, 768)


def _rms(x, g):
    return x * lax.rsqrt(jnp.mean(x * x, axis=-1, keepdims=True) + NORM_EPS) * g


def _mlp_kernel(x_ref, on_ref, od_ref, os_ref, wn_ref, wd_ref, ws_ref, ln2_ref, up_ref, dn_ref, lnf_ref,
                o_ref, h2_s, *, final_norm):
    j = pl.program_id(1)

    @pl.when(j == 0)
    def _():
        x1 = (x_ref[...] + _dotf(on_ref[...], wn_ref[...]) + _dotf(od_ref[...], wd_ref[...])
              + _dotf(os_ref[...], ws_ref[...]))
        o_ref[...] = x1
        h2_s[...] = _rms(x1, ln2_ref[...]).astype(BF16)

    a = jnp.maximum(_dotf(h2_s[...], up_ref[...]), 0.0)
    o_ref[...] += _dotf((a * a).astype(BF16), dn_ref[...])

    if final_norm:
        @pl.when(j == pl.num_programs(1) - 1)
        def _():
            o_ref[...] = _rms(o_ref[...], lnf_ref[...])


def _mlp(x, o_nsa, o_dn, o_s5, wn, wd, ws, ln2, up, dn, lnf, *, final_norm, tm=512, tf=1024):
    n = x.shape[0]
    row = lambda i, j: (i, 0)
    const = lambda i, j: (0, 0)
    return pl.pallas_call(
        functools.partial(_mlp_kernel, final_norm=final_norm),
        grid=(n // tm, D_FF // tf),
        in_specs=[pl.BlockSpec((tm, D_MODEL), row), pl.BlockSpec((tm, 1024), row),
                  pl.BlockSpec((tm, DN_WIDTH), row), pl.BlockSpec((tm, S5_WIDTH), row),
                  pl.BlockSpec(wn.shape, const), pl.BlockSpec(wd.shape, const), pl.BlockSpec(ws.shape, const),
                  pl.BlockSpec((1, D_MODEL), const),
                  pl.BlockSpec((D_MODEL, tf), lambda i, j: (0, j)),
                  pl.BlockSpec((tf, D_MODEL), lambda i, j: (j, 0)),
                  pl.BlockSpec((1, D_MODEL), const)],
        out_specs=pl.BlockSpec((tm, D_MODEL), row),
        out_shape=jax.ShapeDtypeStruct((n, D_MODEL), F32),
        scratch_shapes=[pltpu.VMEM((tm, D_MODEL), BF16)],
        compiler_params=_cparams(("parallel", "arbitrary")),
        name="out_mlp",
    )(x, o_nsa, o_dn, o_s5, wn, wd, ws, ln2, up, dn, lnf)


_NSA_W = NSA_HEADS * HEAD_DIM
_KV_W = 6 * NSA_KV_HEADS * HEAD_DIM
_GATE_W = NSA_HEADS * 3
_OFF_KV = _NSA_W
_OFF_GATE = _OFF_KV + _KV_W
_OFF_DN = _OFF_GATE + _GATE_W
_OFF_A = _OFF_DN + 3 * DN_WIDTH
_OFF_B = _OFF_A + DN_HEADS
_OFF_Z = _OFF_B + DN_HEADS
_OFF_U = _OFF_Z + DN_WIDTH


def _pad_heads(w, axis):
    w = jnp.moveaxis(w, axis, 0).reshape((NSA_HEADS, HEAD_DIM) + w.shape[:axis] + w.shape[axis + 1:])
    out = jnp.zeros((NSA_HEADS, LANE) + w.shape[2:], w.dtype)
    for h in range(NSA_HEADS):
        g = h // NSA_GROUP
        out = out.at[h, g * HEAD_DIM:(g + 1) * HEAD_DIM].set(w[h])
    out = out.reshape((NSA_HEADS * LANE,) + w.shape[2:])
    return jnp.moveaxis(out, 0, axis)


def _layer_weights(w_in, w_out, wck, wcv):
    wq = _pad_heads(w_in[:, :_NSA_W], 1).astype(BF16)
    wkv = w_in[:, _OFF_KV:_OFF_GATE].astype(BF16)
    wdn = w_in[:, _OFF_DN:_OFF_A].astype(BF16)
    wz = w_in[:, _OFF_Z:_OFF_U].astype(BF16)
    wu = w_in[:, _OFF_U:].astype(BF16)
    wsm = jnp.zeros((D_MODEL, LANE), F32)
    wsm = wsm.at[:, :_GATE_W].set(w_in[:, _OFF_GATE:_OFF_DN])
    wsm = wsm.at[:, _GATE_W:_GATE_W + 2 * DN_HEADS].set(w_in[:, _OFF_A:_OFF_Z]).astype(BF16)
    wexp = jnp.concatenate([jnp.repeat(wck.T, HEAD_DIM, axis=1), jnp.repeat(wcv.T, HEAD_DIM, axis=1)], axis=1)
    won = _pad_heads(w_out[:_NSA_W], 0).astype(BF16)
    wod = w_out[_NSA_W:_NSA_W + DN_WIDTH].astype(BF16)
    wos = w_out[_NSA_W + DN_WIDTH:].astype(BF16)
    return wq, wkv, wdn, wz, wu, wsm, wexp, won, wod, wos


def kernel(x_prompt, x_sample, cache_nsa_kv, cache_win_kv, state_dn_conv, state_dn, state_s5_re, state_s5_im,
           page_table, ln1, ln2, ln_f, w_in, w_out, nsa_wck, nsa_wcv, dn_conv_w, dn_a_log, dn_dt_bias,
           dn_norm_w, s5_lambda_re, s5_lambda_im, s5_log_step, s5_b_re, s5_b_im, s5_c_re, s5_c_im, s5_d,
           s5_w_glu, mlp_up, mlp_down):
    nbp, lp, _ = x_prompt.shape
    nbs, ls, _ = x_sample.shape
    depth = w_in.shape[0]
    rows_p = nbp * lp
    rows_s = nbs * ls
    n_phys, page = cache_nsa_kv.shape[1], cache_nsa_kv.shape[2]
    past = page_table.shape[1] * page
    wlen = cache_win_kv.shape[2]
    G, dh = NSA_KV_HEADS, HEAD_DIM
    assert lp % (S5_SEGS * 32) == 0 and nbs % 8 == 0 and ls == 8 and wlen == WINDOW and lp >= WINDOW

    xp = x_prompt.reshape(rows_p, D_MODEL)
    xs = x_sample.reshape(rows_s, D_MODEL)
    cache_t = cache_nsa_kv.transpose(0, 1, 3, 4, 5, 2).reshape(depth, n_phys, 4 * G * dh, page)
    win_t = cache_win_kv.transpose(0, 1, 3, 4, 5, 2).reshape(depth, nbs, 2 * G * dh, wlen)
    outs_p, outs_s = [], []
    kvt_main = nwin_t = None
    for l in range(depth):
        wq, wkv, wdn, wz, wu, wsm, wexp, won, wod, wos = _layer_weights(w_in[l], w_out[l], nsa_wck[l], nsa_wcv[l])
        pw = (ln1[l][None], wq, wkv, wdn, wz, wu, wsm, wexp)
        qh_p, kv_p, dnx_p, z_p, u_p, small_p, pool_p, kvt_main, kvt_win, kaug, vt = _proj(
            xp, *pw, nseq=nbp, prev_kvt=kvt_main, sel_tile=NSA_TQ)
        qh_s, kv_s, dnx_s, z_s, u_s, small_s, _, _, _ = _proj(xs, *pw, nseq=1)

        oc, sel, anyblk = _cmp_prompt(qh_p, pool_p, nbp, lp, NSA_TQ)
        osel = _sel_prompt(qh_p, sel, anyblk, kaug, vt, nbp, lp, NSA_TQ)
        on_p = _win_prompt(qh_p, kv_p, kvt_win, oc, osel, small_p, nbp, lp)
        wt = _pool_weights_t(nsa_wck[l], nsa_wcv[l], past)
        on_s, nwin_t = _nsa_sample(page_table, qh_s, kv_s, small_s, cache_t, win_t, wt, nwin_t,
                                   layer=l, lq=ls, past=past)

        gdn_w = (dn_conv_w[l].T, dn_a_log[l][None], dn_dt_bias[l][None], dn_norm_w[l][None])
        od_p, dn_p = _gdn_prompt(dnx_p, small_p, z_p, *gdn_w, jnp.zeros((nbp, DN_HEADS, dh, dh), F32), nb=nbp, lq=lp)
        od_s, dn_s = _gdn(dnx_s, _sample_conv_hist(state_dn_conv[l]), small_s, z_s, *gdn_w, state_dn[l],
                          seg=ls, nb=rows_s // GDN_ROWS, nchunk=1, row0=0,
                          hist_map=lambda b, c: (b, 0), zero_first_hist=False)

        prm = _s5_prep(s5_lambda_re[l], s5_lambda_im[l], s5_log_step[l], s5_b_re[l], s5_b_im[l],
                       s5_c_re[l], s5_c_im[l], s5_d[l], s5_w_glu[l])
        os_p, s5r_p, s5i_p = _s5_prompt(u_p, prm, nbp, lp)
        os_s, s5r_s, s5i_s = _s5_sample(u_s, state_s5_re[l].reshape(nbs, S5_LANES),
                                        state_s5_im[l].reshape(nbs, S5_LANES), prm, nbs, ls)

        mw = (won, wod, wos, ln2[l][None], mlp_up[l].astype(BF16), mlp_down[l].astype(BF16), ln_f[None])
        xp = _mlp(xp, on_p, od_p, os_p, *mw, final_norm=(l == depth - 1))
        xs = _mlp(xs, on_s, od_s, os_s, *mw, final_norm=(l == depth - 1))

        kv_s6 = kv_s.reshape(nbs, ls, 6, G, dh)
        win_p = kvt_win[:, :, lp - WINDOW:].reshape(nbp, 2, G, dh, WINDOW).transpose(0, 4, 1, 2, 3)
        outs_p.append((win_p, dnx_p.reshape(nbp, lp, 3 * DN_WIDTH)[:, -(DN_CONV_W - 1):], dn_p,
                       s5r_p.reshape(nbp, S5_GROUPS, S5_STATE), s5i_p.reshape(nbp, S5_GROUPS, S5_STATE)))
        outs_s.append((kv_s6[:, :, :4], dnx_s.reshape(nbs, ls, 3 * DN_WIDTH)[:, -(DN_CONV_W - 1):], dn_s,
                       s5r_s.reshape(nbs, S5_GROUPS, S5_STATE), s5i_s.reshape(nbs, S5_GROUPS, S5_STATE)))

    stack = lambda outs, i: jnp.stack([o[i] for o in outs], axis=0)
    kv_prompt = kvt_main.reshape(depth, nbp, 4, G, dh, lp).transpose(0, 1, 5, 2, 3, 4)
    win_sample = nwin_t.reshape(depth, nbs, 2, G, dh, wlen).transpose(0, 1, 5, 2, 3, 4)
    return (xp.reshape(nbp, lp, D_MODEL), xs.reshape(nbs, ls, D_MODEL),
            kv_prompt, stack(outs_s, 0), stack(outs_p, 0), win_sample,
            stack(outs_p, 1), stack(outs_s, 1), stack(outs_p, 2), stack(outs_s, 2),
            stack(outs_p, 3), stack(outs_s, 3), stack(outs_p, 4), stack(outs_s, 4))
```

```python
import functools

import jax
import jax.numpy as jnp
from jax import lax
from jax.experimental import pallas as pl
from jax.experimental.pallas import tpu as pltpu

F32 = jnp.float32
BF16 = jnp.bfloat16
HIGHEST = lax.Precision.HIGHEST

D_MODEL = 1024
HEAD_DIM = 64
NSA_HEADS = 8
NSA_KV_HEADS = 2
NSA_GROUP = 4
NSA_BLOCK = 64
NSA_TOPK = 16
WINDOW = 512
DN_HEADS = 4
DN_WIDTH = 256
DN_CONV_W = 4
S5_GROUPS = 16
S5_GROUP_CH = 16
S5_STATE = 64
S5_WIDTH = 256
S5_LANES = S5_GROUPS * S5_STATE
D_FF = 4096
NORM_EPS = 1e-6
BIG = 1e9
NEG = -1e30
LANE = 128
NSA_TQ = 256
VMEM_LIMIT = 48 * 1024 * 1024

_NT = (((1,), (1,)), ((), ()))


def _slope(h):
    return 2.0 ** (-(h + 1))


def _cparams(sem):
    return pltpu.CompilerParams(dimension_semantics=sem, vmem_limit_bytes=VMEM_LIMIT)


def _dotf(a, b):
    return jnp.dot(a, b, preferred_element_type=F32)


def _dot_nt(a, b):
    return lax.dot_general(a, b, _NT, preferred_element_type=F32)


def _dot_hi(a, b):
    return jnp.dot(a, b, preferred_element_type=F32, precision=HIGHEST)


def _proj_kernel(x_ref, g_ref, wq_ref, wkv_ref, wdn_ref, wz_ref, wu_ref, ws_ref, wexp_ref, *refs, lt, tk, nprev):
    if nprev:
        prev_ref, refs = refs[0], refs[1:]
    q_ref, kv_ref, dn_ref, z_ref, u_ref, s_ref, pool_ref, kvtm_ref, kvtw_ref = refs[:9]
    sel_refs = refs[9:]
    x = x_ref[...]
    h = x * lax.rsqrt(jnp.mean(x * x, axis=-1, keepdims=True) + NORM_EPS) * g_ref[...]
    hb = h.astype(BF16)
    q_ref[...] = _dotf(hb, wq_ref[...])
    kv = _dotf(hb, wkv_ref[...])
    kv_ref[...] = kv
    kvt = kv.T
    if nprev:
        kvtm_ref[0:nprev] = prev_ref[...]
    kvtm_ref[nprev] = kvt[:4 * LANE]
    kvtw_ref[...] = kvt[4 * LANE:]
    if sel_refs:
        kaug_ref, vt_ref = sel_refs
        rows = x.shape[0]
        pos = (pl.program_id(0) % lt) * rows + lax.broadcasted_iota(jnp.int32, (rows, LANE), 0)
        lanei = lax.broadcasted_iota(jnp.int32, (rows, LANE), 1)
        onehot = (lanei == pos // NSA_BLOCK).astype(BF16)
        jpart = jnp.where(lanei == HEAD_DIM, (pos % NSA_BLOCK).astype(F32), 0.0)
        ksl = kv[:, 2 * LANE:3 * LANE]
        for g in range(NSA_KV_HEADS):
            kg = ksl if g == 0 else pltpu.roll(ksl, HEAD_DIM, axis=1)
            kaug_ref[g] = jnp.concatenate([onehot, jnp.where(lanei < HEAD_DIM, kg, jpart).astype(BF16)], axis=1)
            vrow = 3 * LANE + g * HEAD_DIM
            for c in range(rows // tk):
                vt_ref[g, c] = kvt[vrow:vrow + HEAD_DIM, c * tk:(c + 1) * tk].astype(BF16)
    dn_ref[...] = _dotf(hb, wdn_ref[...])
    z_ref[...] = _dotf(hb, wz_ref[...])
    u_ref[...] = _dotf(hb, wu_ref[...])
    s_ref[...] = _dotf(hb, ws_ref[...])
    tm = x.shape[0]
    kc = kv[:, :2 * LANE].reshape(tm // NSA_BLOCK, NSA_BLOCK, 2 * LANE) * wexp_ref[...][None]
    pool_ref[...] = jnp.sum(kc, axis=1)


def _proj(x, ln, wq, wkv, wdn, wz, wu, ws, wexp, nseq, prev_kvt=None, sel_tile=None, tm=512):
    n = x.shape[0]
    lseq = n // nseq
    lt = lseq // tm
    nprev = 0 if prev_kvt is None else prev_kvt.shape[0]
    const = lambda i: (0, 0)
    row = lambda i: (i, 0)
    outs = [(n, 1024), (n, 768), (n, 768), (n, 256), (n, 256), (n, LANE), (n // NSA_BLOCK, 2 * LANE),
            (nprev + 1, nseq, 4 * LANE, lseq), (nseq, 2 * LANE, lseq)]
    out_specs = ([pl.BlockSpec((tm, s[1]), row) for s in outs[:6]]
                 + [pl.BlockSpec((tm // NSA_BLOCK, 2 * LANE), row),
                    pl.BlockSpec((nprev + 1, None, 4 * LANE, tm), lambda i: (0, i // lt, 0, i % lt)),
                    pl.BlockSpec((None, 2 * LANE, tm), lambda i: (i // lt, 0, i % lt))])
    out_shape = [jax.ShapeDtypeStruct(s, F32) for s in outs]
    prev_specs = [pl.BlockSpec((nprev, None, 4 * LANE, tm), lambda i: (0, i // lt, 0, i % lt))] if nprev else []
    G = NSA_KV_HEADS
    if sel_tile is not None:
        out_specs += [pl.BlockSpec((None, G, tm, 2 * LANE), lambda i: (i // lt, 0, i % lt, 0)),
                      pl.BlockSpec((None, G, tm // sel_tile, HEAD_DIM, sel_tile),
                                   lambda i: (i // lt, 0, i % lt, 0, 0))]
        out_shape += [jax.ShapeDtypeStruct((nseq, G, lseq, 2 * LANE), BF16),
                      jax.ShapeDtypeStruct((nseq, G, lseq // sel_tile, HEAD_DIM, sel_tile), BF16)]
    return pl.pallas_call(
        functools.partial(_proj_kernel, lt=lt, tk=sel_tile, nprev=nprev),
        grid=(n // tm,),
        in_specs=[pl.BlockSpec((tm, D_MODEL), row), pl.BlockSpec((1, D_MODEL), const)]
        + [pl.BlockSpec(w.shape, const) for w in (wq, wkv, wdn, wz, wu, ws, wexp)] + prev_specs,
        out_specs=out_specs,
        out_shape=out_shape,
        compiler_params=_cparams(("parallel",)),
        name="proj",
    )(x, ln, wq, wkv, wdn, wz, wu, ws, wexp, *([prev_kvt] if nprev else []))


def _topk_round(sel, score, blk, nblk):
    m = jnp.max(score, axis=0, keepdims=True)
    idx = jnp.min(jnp.where(score == m, blk, nblk), axis=0, keepdims=True)
    pick = blk == idx
    return sel | pick, jnp.where(pick, -jnp.inf, score)


NSA_FORCED = 3
NSA_FREE_ROUNDS = NSA_TOPK - NSA_FORCED


def _free_scores(imp, forced, blk, cur):
    return jnp.where(blk > cur, -BIG, jnp.where(forced, -jnp.inf, imp))


def _topk_blocks(score, blk, nblk):
    sel = jnp.zeros(score.shape, dtype=jnp.bool_)
    for _ in range(NSA_FREE_ROUNDS):
        sel, score = _topk_round(sel, score, blk, nblk)
    return sel


def _cmp_heads(q_tile, kc, vct, t, blk, nblk_valid, write_oc):
    nblk = blk.shape[0]
    ok = ((blk + 1) * NSA_BLOCK - 1 <= t) & (blk < nblk_valid)
    okf = ok.astype(F32)
    dist = t.astype(F32) - (blk.astype(F32) * NSA_BLOCK + (NSA_BLOCK - 1) / 2.0)
    cur = t // NSA_BLOCK
    forced = (blk == 0) | (blk == cur) | (blk == cur - 1)
    sels = []
    for g in range(NSA_KV_HEADS):
        imp = jnp.zeros(blk.shape, F32)
        for r in range(NSA_GROUP):
            h = g * NSA_GROUP + r
            q = (q_tile(h) * HEAD_DIM ** -0.5).astype(BF16)
            s = _dot_nt(kc, q)
            s = jnp.where(ok, s - _slope(h) * dist, NEG)
            m = jnp.max(s, axis=0, keepdims=True)
            p = jnp.exp(s - m) * okf
            p = p / jnp.maximum(jnp.sum(p, axis=0, keepdims=True), 1e-30)
            imp = imp + p
            write_oc(h, _dotf(vct, p.astype(BF16)))
        sel = (forced | _topk_blocks(_free_scores(imp, forced, blk, cur), blk, nblk)) & (blk <= cur)
        sels.append(sel)
    return sels


def _cmp_kernel(q_ref, kcv_ref, oc_ref, sel_ref, any_ref, *, tq):
    qt = pl.program_id(1)
    kcv = kcv_ref[...]
    nblk = kcv.shape[0]
    kc = kcv[:, :LANE].astype(BF16)
    vct = kcv[:, LANE:].T.astype(BF16)
    blk = lax.broadcasted_iota(jnp.int32, (nblk, tq), 0)
    t = qt * tq + lax.broadcasted_iota(jnp.int32, (nblk, tq), 1)

    def write_oc(h, oct):
        oc_ref[:, h * LANE:(h + 1) * LANE] = oct.T

    sels = _cmp_heads(lambda h: q_ref[:, h * LANE:(h + 1) * LANE], kc, vct, t, blk, nblk, write_oc)
    for g in range(NSA_KV_HEADS):
        selt = sels[g].astype(F32).T
        sel_ref[:, g * LANE:(g + 1) * LANE] = selt
        any_ref[:, g * LANE:(g + 1) * LANE] = jnp.broadcast_to(jnp.max(selt, axis=0, keepdims=True), (8, LANE))


def _cmp_prompt(qh, pool, nb, lq, tq):
    nq = lq // tq
    nblk = lq // NSA_BLOCK
    assert nblk == LANE
    return pl.pallas_call(
        functools.partial(_cmp_kernel, tq=tq),
        grid=(nb, nq),
        in_specs=[pl.BlockSpec((tq, 1024), lambda b, i: (b * nq + i, 0)),
                  pl.BlockSpec((nblk, 2 * LANE), lambda b, i: (b, 0))],
        out_specs=[pl.BlockSpec((tq, 1024), lambda b, i: (b * nq + i, 0)),
                   pl.BlockSpec((tq, 2 * LANE), lambda b, i: (b * nq + i, 0)),
                   pl.BlockSpec((8, 2 * LANE), lambda b, i: (b * nq + i, 0))],
        out_shape=[jax.ShapeDtypeStruct((nb * lq, 1024), F32),
                   jax.ShapeDtypeStruct((nb * lq, 2 * LANE), F32),
                   jax.ShapeDtypeStruct((nb * nq * 8, 2 * LANE), F32)],
        compiler_params=_cparams(("parallel", "parallel")),
        name="nsa_cmp",
    )(qh, pool)


def _sel_kernel(act_ref, q_ref, sel_ref, k_ref, vt_ref, o_ref, lhs_s, m_s, l_s, acc_s, lst_s, *, tq, nq):
    b = pl.program_id(0)
    g = pl.program_id(1)
    qt = pl.program_id(2)
    lanei = lax.broadcasted_iota(jnp.int32, (tq, LANE), 1)
    nio = lanei.astype(F32)
    selg = sel_ref[...] > 0.5
    for r in range(NSA_GROUP):
        slope = jnp.where(g == 0, _slope(r), _slope(NSA_GROUP + r))
        qt_r = q_ref[:, r * LANE:(r + 1) * LANE]
        qt_r = jnp.where(g == 0, qt_r, pltpu.roll(qt_r, HEAD_DIM, axis=1))
        qpart = jnp.where(lanei == HEAD_DIM, slope, qt_r * HEAD_DIM ** -0.5)
        bias = jnp.where(selg, (NSA_BLOCK * slope) * nio, NEG)
        lhs_s[r] = jnp.concatenate([bias.astype(BF16), qpart.astype(BF16)], axis=1)
    m_s[...] = jnp.full(m_s.shape, NEG, F32)
    l_s[...] = jnp.zeros(l_s.shape, F32)
    acc_s[...] = jnp.zeros(acc_s.shape, F32)

    def tiles(items):
        heads = range(NSA_GROUP)
        kaugs = [k_ref[kt] for kt, _ in items]
        vts = [vt_ref[kt] for kt, _ in items]
        ss = [[_dot_nt(ka, lhs_s[r]) for r in heads] for ka in kaugs]
        ss = [[s if mask is None else jnp.where(mask, s, NEG) for s in row] for row, (_, mask) in zip(ss, items)]
        m_old = [m_s[r] for r in heads]
        m_new = []
        for r in heads:
            m = m_old[r]
            for row in ss:
                m = jnp.maximum(m, jnp.max(row[r], axis=0, keepdims=True))
            m_new.append(m)
        ps = [[jnp.exp(row[r] - m_new[r]) for r in heads] for row in ss]
        pvs = [[_dotf(vt, row[r].astype(BF16)) for r in heads] for vt, row in zip(vts, ps)]
        for r in heads:
            alpha = jnp.exp(m_old[r] - m_new[r])
            l_s[r] = alpha * l_s[r] + sum(jnp.sum(row[r], axis=0, keepdims=True) for row in ps)
            acc_s[r] = alpha * acc_s[r] + sum(row[r] for row in pvs)
            m_s[r] = m_new[r]

    abase = ((b * NSA_KV_HEADS + g) * nq + qt) * nq

    def compact(kt, n):
        lst_s[n] = kt
        return n + act_ref[abase + kt]

    n_act = lax.fori_loop(0, qt, compact, 0)

    def body(i, carry):
        tiles([(lst_s[2 * i], None), (lst_s[2 * i + 1], None)])
        return carry

    lax.fori_loop(0, n_act // 2, body, 0)
    diag = (qt, lax.broadcasted_iota(jnp.int32, (tq, tq), 0) <= lax.broadcasted_iota(jnp.int32, (tq, tq), 1))

    @pl.when(n_act % 2 == 1)
    def _():
        tiles([(lst_s[n_act - 1], None), diag])

    @pl.when(n_act % 2 == 0)
    def _():
        tiles([diag])

    zeros = jnp.zeros((HEAD_DIM, tq), F32)
    for r in range(NSA_GROUP):
        o = jnp.concatenate([acc_s[r] / l_s[r], zeros], axis=0).T
        o_ref[:, r * LANE:(r + 1) * LANE] = jnp.where(g == 0, o, pltpu.roll(o, HEAD_DIM, axis=1))


def _sel_prompt(qh, sel, anyblk, kaug, vt, nb, lq, tq):
    assert vt.shape[-1] == tq
    nq = lq // tq
    bpt = tq // NSA_BLOCK
    G = NSA_KV_HEADS
    act = anyblk[::8].reshape(nb, nq, G, nq, bpt).max(axis=-1) > 0.5
    act = act.transpose(0, 2, 1, 3).reshape(-1).astype(jnp.int32)
    kaug = kaug.reshape(nb, G, nq, tq, 2 * LANE)
    qmap = lambda b, g, i, a: (b * nq + i, g)
    whole = lambda b, g, i, a: (b, g, 0, 0, 0)
    grid_spec = pltpu.PrefetchScalarGridSpec(
        num_scalar_prefetch=1,
        grid=(nb, G, nq),
        in_specs=[pl.BlockSpec((tq, NSA_GROUP * LANE), qmap),
                  pl.BlockSpec((tq, LANE), qmap),
                  pl.BlockSpec((None, None, nq, tq, 2 * LANE), whole),
                  pl.BlockSpec((None, None, nq, HEAD_DIM, tq), whole)],
        out_specs=pl.BlockSpec((tq, NSA_GROUP * LANE), qmap),
        scratch_shapes=[pltpu.VMEM((NSA_GROUP, tq, 2 * LANE), BF16),
                        pltpu.VMEM((NSA_GROUP, 1, tq), F32),
                        pltpu.VMEM((NSA_GROUP, 1, tq), F32),
                        pltpu.VMEM((NSA_GROUP, HEAD_DIM, tq), F32),
                        pltpu.SMEM((nq,), jnp.int32)])
    return pl.pallas_call(
        functools.partial(_sel_kernel, tq=tq, nq=nq),
        grid_spec=grid_spec,
        out_shape=jax.ShapeDtypeStruct((nb * lq, 1024), F32),
        compiler_params=_cparams(("parallel", "parallel", "arbitrary")),
        name="nsa_sel",
    )(act, qh, sel, kaug, vt)


def _gate_mix(small, oc, os_, ow, h):
    gt = jax.nn.sigmoid(small[:, 3 * h:3 * h + 3])
    return gt[:, 0:1] * oc + gt[:, 1:2] * os_ + gt[:, 2:3] * ow


WIN_HEADS_PER_ROUND = 4
_WIN_SPLIT = 32


def _win_kernel(q_ref, k0_ref, k1_ref, k2_ref, v0_ref, v1_ref, v2_ref, oc_ref, os_ref, sm_ref, o_ref, *, tq):
    qt = pl.program_id(1)
    nkey = 3 * tq
    ki = lax.broadcasted_iota(jnp.int32, (nkey, LANE), 0)
    li = lax.broadcasted_iota(jnp.int32, (nkey, LANE), 1)
    penc = jnp.where(li == 0, ki // _WIN_SPLIT, jnp.where(li == 1, ki % _WIN_SPLIT, 0)).astype(BF16)
    kcat = jnp.concatenate([k0_ref[...], k1_ref[...], k2_ref[...]], axis=0).astype(BF16)
    kaug = jnp.concatenate([kcat, penc], axis=1)
    vts = [v.astype(BF16) for v in (v0_ref[...], v1_ref[...], v2_ref[...])]
    krow = lax.broadcasted_iota(jnp.int32, (tq, tq), 0)
    qcol = lax.broadcasted_iota(jnp.int32, (tq, tq), 1)
    assert WINDOW == 2 * tq
    valids = [(krow > qcol) & (qt >= 2), jnp.broadcast_to(qt >= 1, (tq, tq)), krow <= qcol]
    small = sm_ref[...]
    lq_i = lax.broadcasted_iota(jnp.int32, (tq, LANE), 1)
    for h0 in range(0, NSA_HEADS, WIN_HEADS_PER_ROUND):
        heads = range(h0, h0 + WIN_HEADS_PER_ROUND)
        qaug = []
        for h in heads:
            al = jnp.where(lq_i == 0, _WIN_SPLIT * _slope(h), jnp.where(lq_i == 1, _slope(h), 0.0))
            qaug.append(jnp.concatenate([(q_ref[:, h * LANE:(h + 1) * LANE] * HEAD_DIM ** -0.5).astype(BF16),
                                         al.astype(BF16)], axis=1))
        tiles3 = range(3)
        ss = [[jnp.where(valids[j], _dot_nt(kaug[j * tq:(j + 1) * tq], qa), NEG) for j in tiles3]
              for qa in qaug]
        ms = [functools.reduce(jnp.maximum, [jnp.max(s, axis=0, keepdims=True) for s in row]) for row in ss]
        ps = [[jnp.exp(s - m) for s in row] for row, m in zip(ss, ms)]
        ows = []
        for row in ps:
            num = sum(_dotf(vts[j], row[j].astype(BF16)) for j in tiles3)
            den = sum(jnp.sum(row[j], axis=0, keepdims=True) for j in tiles3)
            ows.append((num / den).T)
        for h, ow in zip(heads, ows):
            sl = slice(h * LANE, (h + 1) * LANE)
            o_ref[:, sl] = _gate_mix(small, oc_ref[:, sl], os_ref[:, sl], ow, h).astype(o_ref.dtype)


def _win_prompt(qh, kv, kvt_win, oc, osel, small, nb, lq, tq=256):
    assert 2 * tq >= WINDOW - 1 and 3 * tq % _WIN_SPLIT == 0 and 3 * tq // _WIN_SPLIT <= 256
    nq = lq // tq
    row = lambda b, i: (b * nq + i, 0)

    def kmap(back):
        return lambda b, i: (b * nq + jnp.maximum(i - back, 0), 4)

    def vmap(back):
        return lambda b, i: (b, 1, jnp.maximum(i - back, 0))

    return pl.pallas_call(
        functools.partial(_win_kernel, tq=tq),
        grid=(nb, nq),
        in_specs=[pl.BlockSpec((tq, 1024), row)]
        + [pl.BlockSpec((tq, LANE), kmap(back)) for back in (2, 1, 0)]
        + [pl.BlockSpec((None, LANE, tq), vmap(back)) for back in (2, 1, 0)]
        + [pl.BlockSpec((tq, 1024), row), pl.BlockSpec((tq, 1024), row), pl.BlockSpec((tq, LANE), row)],
        out_specs=pl.BlockSpec((tq, 1024), row),
        out_shape=jax.ShapeDtypeStruct((nb * lq, 1024), BF16),
        compiler_params=_cparams(("parallel", "parallel")),
        name="nsa_win",
    )(qh, kv, kv, kv, kvt_win, kvt_win, kvt_win, oc, osel, small)


def _per_head(idx_h, fn):
    out = jnp.zeros(idx_h.shape, F32)
    for h in range(NSA_HEADS):
        out = jnp.where(idx_h == h, fn(h), out)
    return out


def _pad_rows(a, rows):
    return jnp.concatenate([a, jnp.zeros((rows - a.shape[0], a.shape[1]), a.dtype)], axis=0)


NSA_SAMPLE_SEQS = 2


def _nsa_sample_kernel(pt_ref, q_ref, kvn_ref, sm_ref, *rest, past, lq, npages, page, nprev):
    nseq = NSA_SAMPLE_SEQS
    pages = rest[:nseq * npages]
    rest = rest[nseq * npages:]
    if nprev:
        win_ref, wt_ref, prev_ref, o_ref, nwin_all, ktc_s, vtc_s, rt_s, vt_s = rest
        nwin_all[0:nprev] = prev_ref[...]
    else:
        win_ref, wt_ref, o_ref, nwin_all, ktc_s, vtc_s, rt_s, vt_s = rest
    nwin_ref = nwin_all.at[nprev]

    @pl.when(pl.program_id(0) == 0)
    def _():
        n_i = lax.broadcasted_iota(jnp.int32, (LANE, past), 0)
        pos_i = lax.broadcasted_iota(jnp.int32, (LANE, past), 1)
        for j in range(nseq):
            rt_s[j, 0:LANE, :] = (n_i == pos_i // NSA_BLOCK).astype(BF16)

    def rows(ref, j):
        return ref.at[pl.ds(j * lq, lq)]

    seqs = [_nsa_sample_seq(rows(q_ref, j), rows(kvn_ref, j), rows(sm_ref, j), pages[j * npages:(j + 1) * npages],
                            win_ref.at[j], wt_ref, rows(o_ref, j), nwin_ref.at[j], ktc_s.at[j], vtc_s.at[j],
                            rt_s.at[j], vt_s.at[j], past=past, lq=lq, npages=npages, page=page)
            for j in range(nseq)]
    live = True
    while live:
        for s in seqs:
            live = next(s, None) is not None and live


def _nsa_sample_seq(q_ref, kvn_ref, sm_ref, pages, win_ref, wt_ref, o_ref, nwin_ref, ktc_s, vtc_s, rt_s, vt_s,
                    *, past, lq, npages, page):
    nkeys = npages * page
    nrow = NSA_HEADS * lq
    nb_complete = (past + lq) // NSA_BLOCK
    new_blk = past // NSA_BLOCK
    assert past % NSA_BLOCK + lq <= NSA_BLOCK and nkeys == past and new_blk < LANE

    for i in range(npages):
        sl = slice(i * page, (i + 1) * page)
        ktc_s[:, sl] = pages[i][0:LANE, :].astype(BF16)
        vtc_s[:, sl] = pages[i][LANE:2 * LANE, :].astype(BF16)
    yield True

    def stage_selected_page(i):
        sl = slice(i * page, (i + 1) * page)
        rt_s[LANE:2 * LANE, sl] = pages[i][2 * LANE:3 * LANE, :].astype(BF16)
        vt_s[:, sl] = pages[i][3 * LANE:4 * LANE, :].astype(BF16)

    q8 = q_ref[...]
    qf = jnp.concatenate([q8[:, h * LANE:(h + 1) * LANE] for h in range(NSA_HEADS)], axis=0) * HEAD_DIM ** -0.5
    qb = qf.astype(BF16)
    kvn = kvn_ref[...]
    lane_lo = lax.broadcasted_iota(jnp.int32, (LANE, LANE), 1) < HEAD_DIM

    ktc = ktc_s[...]
    vtc = vtc_s[...]
    kc = jnp.where(lane_lo, _dot_nt(wt_ref[0, 0], ktc), _dot_nt(wt_ref[0, 1], ktc)).astype(BF16)
    vc = jnp.where(lane_lo, _dot_nt(wt_ref[1, 0], vtc), _dot_nt(wt_ref[1, 1], vtc))
    vct = vc.T.astype(BF16)
    blk = lax.broadcasted_iota(jnp.int32, (LANE, nrow), 0)
    col = lax.broadcasted_iota(jnp.int32, (LANE, nrow), 1)
    t = past + col % lq
    slope_c = _per_head(col // lq, _slope)
    ok = ((blk + 1) * NSA_BLOCK - 1 <= t) & (blk < nb_complete)
    dist = t.astype(F32) - (blk.astype(F32) * NSA_BLOCK + (NSA_BLOCK - 1) / 2.0)
    s = jnp.where(ok, _dot_nt(kc, qb) - slope_c * dist, NEG)
    m = jnp.max(s, axis=0, keepdims=True)
    p = jnp.exp(s - m) * ok.astype(F32)
    p = p / jnp.maximum(jnp.sum(p, axis=0, keepdims=True), 1e-30)
    o_c = _dotf(vct, p.astype(BF16)).T
    yield True
    ri = lax.broadcasted_iota(jnp.int32, (nrow, nrow), 0)
    ci = lax.broadcasted_iota(jnp.int32, (nrow, nrow), 1)
    gsum = ((ri // (NSA_GROUP * lq) == ci // (NSA_GROUP * lq)) & (ri % lq == ci % lq)).astype(F32)
    imp = _dot_hi(p, gsum)
    cur = t // NSA_BLOCK
    forced = (blk == 0) | (blk == cur) | (blk == cur - 1)
    score = _free_scores(imp, forced, blk, cur)
    rowi = lax.broadcasted_iota(jnp.int32, (nrow, LANE), 0)
    lanei = lax.broadcasted_iota(jnp.int32, (nrow, LANE), 1)
    slope_r = _per_head(rowi[:, 0:1] // lq, _slope)
    tok_r = rowi % lq
    new_ok = (lanei <= tok_r) & (lanei < lq)

    wlen = win_ref.shape[1]
    wi = lax.broadcasted_iota(jnp.int32, (nrow, wlen), 1)
    tok_w = lax.broadcasted_iota(jnp.int32, (nrow, wlen), 0) % lq
    dist_w = wlen + tok_w - wi
    s_w = _dotf(qb, win_ref[0:LANE, :].astype(BF16))
    s_w = jnp.where(dist_w < WINDOW, s_w - slope_r * dist_w.astype(F32), NEG)
    kwn = _pad_rows(kvn[:, 4 * LANE:5 * LANE], LANE).astype(BF16)
    s_wn = jnp.where(new_ok, _dot_nt(qb, kwn) - slope_r * (tok_r - lanei).astype(F32), NEG)
    m = jnp.maximum(jnp.max(s_w, axis=1, keepdims=True), jnp.max(s_wn, axis=1, keepdims=True))
    p_w = jnp.exp(s_w - m)
    p_wn = jnp.exp(s_wn - m)
    vwn = _pad_rows(kvn[:, 5 * LANE:6 * LANE], LANE).astype(BF16)
    o_w = (_dot_nt(p_w.astype(BF16), win_ref[LANE:2 * LANE, :].astype(BF16)) + _dotf(p_wn.astype(BF16), vwn)) / (
        jnp.sum(p_w, axis=1, keepdims=True) + jnp.sum(p_wn, axis=1, keepdims=True))

    win = win_ref[...]
    new_t = _pad_rows(kvn[:, 4 * LANE:6 * LANE], LANE).T
    new_t = pltpu.roll(new_t, LANE - lq, axis=1)
    tail = jnp.concatenate([jnp.zeros((2 * LANE, wlen - LANE), F32), new_t], axis=1)
    lane_w = lax.broadcasted_iota(jnp.int32, win.shape, 1)
    nwin_ref[...] = jnp.where(lane_w < wlen - lq, pltpu.roll(win, wlen - lq, axis=1), tail)
    yield True

    picked = forced
    for i in range(max(NSA_FREE_ROUNDS, npages)):
        if i < npages:
            stage_selected_page(i)
        if i < NSA_FREE_ROUNDS:
            picked, score = _topk_round(picked, score, blk, LANE)
        yield True
    sel = (picked & (blk <= cur)).astype(F32).T

    bias = jnp.where(sel > 0.5, (NSA_BLOCK * slope_r) * lanei.astype(F32), NEG)
    lhs = jnp.concatenate([bias.astype(BF16), qb], axis=1)
    jrow = (lax.broadcasted_iota(jnp.int32, (1, nkeys), 1) % NSA_BLOCK).astype(F32)
    s_p = _dotf(lhs, rt_s[...]) + slope_r * jrow
    bias_new = jnp.sum(jnp.where(lanei == new_blk, bias, 0.0), axis=1, keepdims=True)
    jnew = (past % NSA_BLOCK + lanei).astype(F32)
    kn = _pad_rows(kvn[:, 2 * LANE:3 * LANE], LANE).astype(BF16)
    s_n = jnp.where(new_ok, _dot_nt(qb, kn) + bias_new + slope_r * jnew, NEG)
    yield True
    m = jnp.maximum(jnp.max(s_p, axis=1, keepdims=True), jnp.max(s_n, axis=1, keepdims=True))
    p_p = jnp.exp(s_p - m)
    p_n = jnp.exp(s_n - m)
    vn = _pad_rows(kvn[:, 3 * LANE:4 * LANE], LANE).astype(BF16)
    o_s = (_dot_nt(p_p.astype(BF16), vt_s[...]) + _dotf(p_n.astype(BF16), vn)) / (
        jnp.sum(p_p, axis=1, keepdims=True) + jnp.sum(p_n, axis=1, keepdims=True))

    small = sm_ref[...]
    gate = [jax.nn.sigmoid(jnp.concatenate([small[:, 3 * h + c:3 * h + c + 1] for h in range(NSA_HEADS)], axis=0))
            for c in range(3)]
    o = gate[0] * o_c + gate[1] * o_s + gate[2] * o_w
    for h in range(NSA_HEADS):
        o_ref[:, h * LANE:(h + 1) * LANE] = o[h * lq:(h + 1) * lq].astype(o_ref.dtype)


def _pool_weights_t(wck, wcv, past):
    pos = jnp.arange(past)
    onehot = (jnp.arange(LANE)[:, None] == (pos // NSA_BLOCK)[None, :]).astype(F32)
    w = jnp.stack([wck, wcv])[:, :, pos % NSA_BLOCK]
    return (w[:, :, None, :] * onehot[None, None]).astype(BF16)


def _nsa_sample(page_table, qh, kv, small, cache_t, win_t, wt, prev_win, *, layer, lq, past):
    nprev = 0 if prev_win is None else prev_win.shape[0]
    assert nprev == layer
    nb, npages = page_table.shape
    page = cache_t.shape[-1]
    wlen = win_t.shape[-1]
    ns = NSA_SAMPLE_SEQS
    assert nb % ns == 0
    tokmap = lambda b, pt: (b, 0)

    def page_spec(j, i):
        return pl.BlockSpec((None, None, 4 * LANE, page), lambda b, pt: (layer, pt[ns * b + j, i], 0, 0))

    grid_spec = pltpu.PrefetchScalarGridSpec(
        num_scalar_prefetch=1,
        grid=(nb // ns,),
        in_specs=[pl.BlockSpec((ns * lq, 1024), tokmap), pl.BlockSpec((ns * lq, 768), tokmap),
                  pl.BlockSpec((ns * lq, LANE), tokmap)]
        + [page_spec(j, i) for j in range(ns) for i in range(npages)]
        + [pl.BlockSpec((None, ns, 2 * LANE, wlen), lambda b, pt: (layer, b, 0, 0)),
           pl.BlockSpec(wt.shape, lambda b, pt: (0, 0, 0, 0))]
        + ([pl.BlockSpec((nprev, ns, 2 * LANE, wlen), lambda b, pt: (0, b, 0, 0))] if nprev else []),
        out_specs=[pl.BlockSpec((ns * lq, 1024), tokmap),
                   pl.BlockSpec((nprev + 1, ns, 2 * LANE, wlen), lambda b, pt: (0, b, 0, 0))],
        scratch_shapes=[pltpu.VMEM((ns, LANE, past), BF16), pltpu.VMEM((ns, LANE, past), BF16),
                        pltpu.VMEM((ns, 2 * LANE, past), BF16), pltpu.VMEM((ns, LANE, past), BF16)])
    return pl.pallas_call(
        functools.partial(_nsa_sample_kernel, past=past, lq=lq, npages=npages, page=page, nprev=nprev),
        grid_spec=grid_spec,
        out_shape=[jax.ShapeDtypeStruct((nb * lq, 1024), BF16),
                   jax.ShapeDtypeStruct((nprev + 1, nb, 2 * LANE, wlen), F32)],
        compiler_params=_cparams(("arbitrary",)),
        name="nsa_sample",
    )(page_table, qh, kv, small, *([cache_t] * (ns * npages)), win_t, wt, *([prev_win] if nprev else []))


GDN_ROWS = 64


def _softplus(x):
    return jnp.maximum(x, 0.0) + jnp.log(1.0 + jnp.exp(-jnp.abs(x)))


def _gdn_kernel(x_ref, hist_ref, sm_ref, z_ref, cw_ref, alog_ref, dtb_ref, nw_ref, s0_ref,
                o_ref, s_ref, *, seg, zero_first_hist):
    c = pl.program_id(1)
    rows = GDN_ROWS
    nseg = rows // seg

    @pl.when(c == 0)
    def _():
        s_ref[...] = s0_ref[...]

    x = x_ref[...]
    hist = hist_ref[...]
    if zero_first_hist:
        hist = jnp.where(c == 0, 0.0, hist)
    tpos = lax.broadcasted_iota(jnp.int32, x.shape, 0) % seg
    cw = cw_ref[...]
    y = x * cw[DN_CONV_W - 1:DN_CONV_W]
    for sh in range(1, DN_CONV_W):
        xs = jnp.where(tpos >= sh, pltpu.roll(x, sh, axis=0), pltpu.roll(hist, sh, axis=0))
        y = y + xs * cw[DN_CONV_W - 1 - sh:DN_CONV_W - sh]
    y = y * jax.nn.sigmoid(y)

    ri = lax.broadcasted_iota(jnp.int32, (rows, rows), 0)
    ci = lax.broadcasted_iota(jnp.int32, (rows, rows), 1)
    same = (ri // seg) == (ci // seg)
    tri = same & (ci <= ri)
    strict = same & (ci < ri)
    trif = tri.astype(F32)
    small = sm_ref[...]
    z = z_ref[...]
    heads = range(DN_HEADS)
    segs = range(nseg)
    hs = lambda h: slice(h * HEAD_DIM, (h + 1) * HEAD_DIM)
    yqs = [y[:, hs(h)] for h in heads]
    yks = [y[:, DN_WIDTH + h * HEAD_DIM:DN_WIDTH + (h + 1) * HEAD_DIM] for h in heads]
    vs = [y[:, 2 * DN_WIDTH + h * HEAD_DIM:2 * DN_WIDTH + (h + 1) * HEAD_DIM] for h in heads]
    qs = [a * lax.rsqrt(jnp.sum(a * a, axis=-1, keepdims=True) + 1e-6) * HEAD_DIM ** -0.5 for a in yqs]
    ks = [a * lax.rsqrt(jnp.sum(a * a, axis=-1, keepdims=True) + 1e-6) for a in yks]
    betas = [jax.nn.sigmoid(small[:, 28 + h:29 + h]) for h in heads]
    gs = [-jnp.exp(alog_ref[:, h:h + 1]) * _softplus(small[:, 24 + h:25 + h] + dtb_ref[:, h:h + 1]) for h in heads]
    trifs = _split(trif)
    gcbs = [_dot3s(trifs, _split(jnp.broadcast_to(g, (rows, rows)))) for g in gs]
    decays = [jnp.where(tri, jnp.exp(jnp.where(tri, gcb - gcb.T, 0.0)), 0.0) for gcb in gcbs]
    kbs = [ks[h] * betas[h] for h in heads]
    amats = [jnp.where(strict, _dot3_nt(kbs[h], ks[h]) * decays[h], 0.0) for h in heads]
    aqks = [jnp.where(tri, _dot3_nt(qs[h], ks[h]) * decays[h], 0.0) for h in heads]
    eye = (ri == ci).astype(F32)
    invs = [eye - a for a in amats]
    pws = amats
    span = 2
    while span < seg:
        pwsp = [_split(p) for p in pws]
        pws = [_dot3s(p, p) for p in pwsp]
        invs = [invs[h] + _dot3(invs[h], pws[h]) for h in heads]
        span *= 2
    egcs = [jnp.exp(gcb[:, :HEAD_DIM]) for gcb in gcbs]
    sols = [_dot3(invs[h], jnp.concatenate([vs[h] * betas[h], kbs[h] * egcs[h]], axis=1)) for h in heads]
    us = [s[:, :HEAD_DIM] for s in sols]
    ws = [s[:, HEAD_DIM:] for s in sols]
    qgs = [qs[h] * egcs[h] for h in heads]
    rsl = lambda s: slice(s * seg, (s + 1) * seg)
    sts = [[s_ref[s, h] for s in segs] for h in heads]
    stsp = [[_split(st) for st in row] for row in sts]
    vns = [jnp.concatenate([us[h][rsl(s)] - _dot3s(_split(ws[h][rsl(s)]), stsp[h][s]) for s in segs], axis=0)
           for h in heads]
    oqs = [jnp.concatenate([_dot3s(_split(qgs[h][rsl(s)]), stsp[h][s]) for s in segs], axis=0) for h in heads]
    vnsp = [_split(v) for v in vns]
    os_ = [oqs[h] + _dot3s(_split(aqks[h]), vnsp[h]) for h in heads]
    glasts = [[gcbs[h][(s + 1) * seg - 1:(s + 1) * seg, :HEAD_DIM] for s in segs] for h in heads]
    gl_rows = [jnp.concatenate([jnp.broadcast_to(gl, (seg, HEAD_DIM)) for gl in glasts[h]], axis=0) for h in heads]
    kdts = [(ks[h] * jnp.exp(gl_rows[h] - gcbs[h][:, :HEAD_DIM])).T for h in heads]
    for h in heads:
        for s in segs:
            kds = kdts[h] if nseg == 1 else jnp.where((ci // seg) == s, kdts[h], 0.0)
            s_ref[s, h] = sts[h][s] * jnp.exp(glasts[h][s]) + _dot3s(_split(kds), vnsp[h])
    outs = []
    for h in heads:
        o = os_[h] * lax.rsqrt(jnp.mean(os_[h] * os_[h], axis=-1, keepdims=True) + NORM_EPS) * nw_ref[...]
        zh = z[:, hs(h)]
        outs.append(o * (zh * jax.nn.sigmoid(zh)))
    o_ref[...] = jnp.concatenate(outs, axis=1).astype(o_ref.dtype)


def _gdn(dnqkv, hist, small, z, cw, alog, dtb, nw, s0, *, seg, nb, nchunk, row0, hist_map, zero_first_hist):
    nseg = GDN_ROWS // seg
    rmap = lambda b, c: (row0 + b * nchunk + c, 0)
    const = lambda b, c: (0, 0)
    smap = lambda b, c: (b, 0, 0, 0)
    sblock = (nseg, DN_HEADS, HEAD_DIM, HEAD_DIM)
    return pl.pallas_call(
        functools.partial(_gdn_kernel, seg=seg, zero_first_hist=zero_first_hist),
        grid=(nb, nchunk),
        in_specs=[pl.BlockSpec((GDN_ROWS, 768), rmap), pl.BlockSpec((GDN_ROWS, 768), hist_map),
                  pl.BlockSpec((GDN_ROWS, LANE), rmap), pl.BlockSpec((GDN_ROWS, 256), rmap),
                  pl.BlockSpec(cw.shape, const), pl.BlockSpec(alog.shape, const),
                  pl.BlockSpec(dtb.shape, const), pl.BlockSpec(nw.shape, const),
                  pl.BlockSpec(sblock, smap)],
        out_specs=[pl.BlockSpec((GDN_ROWS, 256), lambda b, c: (b * nchunk + c, 0)),
                   pl.BlockSpec(sblock, smap)],
        out_shape=[jax.ShapeDtypeStruct((nb * nchunk * GDN_ROWS, 256), BF16),
                   jax.ShapeDtypeStruct(s0.shape, F32)],
        compiler_params=_cparams(("parallel", "arbitrary")),
        name="gdn",
    )(dnqkv, hist, small, z, cw, alog, dtb, nw, s0)


def _split(a):
    hi = a.astype(BF16)
    return hi, (a - hi.astype(F32)).astype(BF16)


def _dot3s(a, b):
    return _dotf(a[0], b[0]) + (_dotf(a[0], b[1]) + _dotf(a[1], b[0]))


def _dot3(a, b):
    return _dot3s(_split(a), _split(b))


def _dot3_nt(a, b):
    a, b = _split(a), _split(b)
    return _dot_nt(a[0], b[0]) + (_dot_nt(a[0], b[1]) + _dot_nt(a[1], b[0]))


GDN_PREP_CHUNKS = 4
GDN_PROMPT_CHUNK = 64


def _gdn_prep_kernel(x_ref, hist_ref, sm_ref, cw_ref, alog_ref, dtb_ref,
                     u_ref, w_ref, qg_ref, kd_ref, aqk_ref, egl_ref):
    c = pl.program_id(1)
    x = x_ref[...]
    rows = x.shape[0]
    hist = jnp.where(c == 0, 0.0, hist_ref[...])
    row8 = lax.broadcasted_iota(jnp.int32, hist.shape, 0)
    cw = cw_ref[...]
    y = x * cw[DN_CONV_W - 1:DN_CONV_W]
    for sh in range(1, DN_CONV_W):
        xs = pltpu.roll(x, sh, axis=0)
        top = jnp.where(row8 < sh, pltpu.roll(hist, sh, axis=0), xs[0:8])
        xs = jnp.concatenate([top, xs[8:]], axis=0)
        y = y + xs * cw[DN_CONV_W - 1 - sh:DN_CONV_W - sh]
    y = y * jax.nn.sigmoid(y)

    C = GDN_PROMPT_CHUNK
    ri = lax.broadcasted_iota(jnp.int32, (C, C), 0)
    ci = lax.broadcasted_iota(jnp.int32, (C, C), 1)
    tri = ci <= ri
    strict = ci < ri
    trif = tri.astype(F32)
    small = sm_ref[...]
    nch = rows // C
    units = [(ch, h) for ch in range(nch) for h in range(DN_HEADS)]
    each = lambda f, *ls: [f(*a) for a in zip(*ls)]

    def qkv_of(ch, h):
        rs = slice(ch * C, (ch + 1) * C)
        yq = y[rs, h * HEAD_DIM:(h + 1) * HEAD_DIM]
        yk = y[rs, DN_WIDTH + h * HEAD_DIM:DN_WIDTH + (h + 1) * HEAD_DIM]
        v = y[rs, 2 * DN_WIDTH + h * HEAD_DIM:2 * DN_WIDTH + (h + 1) * HEAD_DIM]
        q = yq * lax.rsqrt(jnp.sum(yq * yq, axis=-1, keepdims=True) + 1e-6) * HEAD_DIM ** -0.5
        k = yk * lax.rsqrt(jnp.sum(yk * yk, axis=-1, keepdims=True) + 1e-6)
        beta = jax.nn.sigmoid(small[rs, 28 + h:29 + h])
        g = -jnp.exp(alog_ref[:, h:h + 1]) * _softplus(small[rs, 24 + h:25 + h] + dtb_ref[:, h:h + 1])
        return q, k, v, beta, g

    qs_, ks_, vs_, betas, gs = zip(*[qkv_of(ch, h) for ch, h in units])
    trifs = _split(trif)
    gcbs = [_dot3s(trifs, _split(jnp.broadcast_to(g, (C, C)))) for g in gs]
    kbs = each(lambda k, b: k * b, ks_, betas)
    ksp = [_split(k) for k in ks_]
    kbsp = [_split(kb) for kb in kbs]
    qsp = [_split(q) for q in qs_]
    nt3 = lambda a, b: _dot_nt(a[0], b[0]) + (_dot_nt(a[0], b[1]) + _dot_nt(a[1], b[0]))
    kks = each(nt3, kbsp, ksp)
    qks = each(nt3, qsp, ksp)
    decays = [jnp.where(tri, jnp.exp(jnp.where(tri, gcb - gcb.T, 0.0)), 0.0) for gcb in gcbs]
    amats = each(lambda kk, d: jnp.where(strict, kk * d, 0.0), kks, decays)
    eye = (ri == ci).astype(F32)
    blk_mask = lambda n: (ri // n) == (ci // n)
    base = 8
    aprev = [jnp.where(blk_mask(base), a, 0.0) for a in amats]
    invs = [eye - a for a in aprev]
    pwsp = [_split(a) for a in aprev]
    span = 2
    while span < base:
        pwsp = [_split(_dot3s(p, p)) for p in pwsp]
        invs = each(lambda i, p: i + _dot3s(_split(i), p), invs, pwsp)
        span *= 2
    size = base
    while size < C:
        size *= 2
        acur = amats if size == C else [jnp.where(blk_mask(size), a, 0.0) for a in amats]
        invsp = [_split(i) for i in invs]
        cross = each(lambda isp, ac, ap: _dot3s(isp, _split(ac - ap)), invsp, acur, aprev)
        invs = each(lambda i, isp, cr: i - _dot3s(_split(cr), isp), invs, invsp, cross)
        aprev = acur
    gcs = [gcb[:, :HEAD_DIM] for gcb in gcbs]
    egcs = [jnp.exp(gc) for gc in gcs]
    rhss = each(lambda v, b, kb, e: jnp.concatenate([v * b, kb * e], axis=1), vs_, betas, kbs, egcs)
    sols = each(_dot3, invs, rhss)
    glasts = [gc[C - 1:C, :] for gc in gcs]
    qgs = each(lambda q, e: q * e, qs_, egcs)
    kds = each(lambda k, gl, gc: k * jnp.exp(gl - gc), ks_, glasts, gcs)
    aqks = each(lambda qk, d: jnp.where(tri, qk * d, 0.0), qks, decays)
    for ch in range(nch):
        rs = slice(ch * C, (ch + 1) * C)
        un = range(ch * DN_HEADS, (ch + 1) * DN_HEADS)
        u_ref[rs, :] = jnp.concatenate([sols[i][:, :HEAD_DIM] for i in un], axis=1)
        w_ref[rs, :] = jnp.concatenate([sols[i][:, HEAD_DIM:] for i in un], axis=1)
        qg_ref[rs, :] = jnp.concatenate([qgs[i] for i in un], axis=1)
        kd_ref[rs, :] = jnp.concatenate([kds[i] for i in un], axis=1)
        aqk_ref[rs, :] = jnp.concatenate([aqks[i] for i in un], axis=1)
        egl_ref[ch * 8:(ch + 1) * 8, :] = jnp.concatenate(
            [jnp.broadcast_to(jnp.exp(glasts[i]), (8, HEAD_DIM)) for i in un], axis=1)


GDN_SCAN_CHUNKS = 4


def _gdn_scan_kernel(u_ref, w_ref, qg_ref, kd_ref, aqk_ref, egl_ref, z_ref, nw_ref, s0_ref, o_ref, s_ref, *, C):
    c = pl.program_id(0)

    @pl.when(c == 0)
    def _():
        s_ref[...] = s0_ref[...]

    nb = u_ref.shape[0]
    nsub = u_ref.shape[1] // C
    units = [(b, slice(h * HEAD_DIM, (h + 1) * HEAD_DIM), h) for b in range(nb) for h in range(DN_HEADS)]
    subs = [slice(j * C, (j + 1) * C) for j in range(nsub)]
    kdts = [[_split(kd_ref[b, rs, sl].T) for b, sl, _ in units] for rs in subs]
    wsp = [[_split(w_ref[b, rs, sl]) for b, sl, _ in units] for rs in subs]
    qgsp = [[_split(qg_ref[b, rs, sl]) for b, sl, _ in units] for rs in subs]
    aqsp = [[_split(aqk_ref[b, rs, h * C:(h + 1) * C]) for b, _, h in units] for rs in subs]
    sts = [s_ref[b, h] for b, _, h in units]
    for j, rs in enumerate(subs):
        stsp = [_split(s) for s in sts]
        vnews = [u_ref[b, rs, sl] - _dot3s(wp, st) for (b, sl, _), wp, st in zip(units, wsp[j], stsp)]
        oqs = [_dot3s(qp, st) for qp, st in zip(qgsp[j], stsp)]
        vsp = [_split(v) for v in vnews]
        os_ = [oq + _dot3s(ap, vs) for oq, ap, vs in zip(oqs, aqsp[j], vsp)]
        sts = [st * egl_ref[b, 8 * j:8 * j + 1, sl] + _dot3s(kdt, vs)
               for (b, sl, _), st, kdt, vs in zip(units, sts, kdts[j], vsp)]
        outs = []
        for (b, sl, _), o in zip(units, os_):
            o = o * lax.rsqrt(jnp.mean(o * o, axis=-1, keepdims=True) + NORM_EPS) * nw_ref[...]
            zh = z_ref[b, rs, sl]
            outs.append(o * (zh * jax.nn.sigmoid(zh)))
        for b in range(nb):
            o_ref[b, rs] = jnp.concatenate(outs[b * DN_HEADS:(b + 1) * DN_HEADS], axis=1).astype(o_ref.dtype)
    for (b, _, h), st in zip(units, sts):
        s_ref[b, h] = st


def _gdn_prompt(dnqkv, small, z, cw, alog, dtb, nw, s0, *, nb, lq):
    cb = GDN_PREP_CHUNKS
    C = GDN_PROMPT_CHUNK
    rows = cb * C
    nstep = lq // rows
    nchunk = lq // C
    rmap = lambda b, c: (b * nstep + c, 0)
    const = lambda b, c: (0, 0)
    wide = jax.ShapeDtypeStruct((nb * lq, DN_WIDTH), F32)
    aqk_shape = jax.ShapeDtypeStruct((nb * lq, DN_HEADS * C), F32)
    u, w, qg, kd, aqk, egl = pl.pallas_call(
        _gdn_prep_kernel,
        grid=(nb, nstep),
        in_specs=[pl.BlockSpec((rows, 768), rmap),
                  pl.BlockSpec((8, 768), lambda b, c: (jnp.maximum((b * nstep + c) * (rows // 8) - 1, 0), 0)),
                  pl.BlockSpec((rows, LANE), rmap),
                  pl.BlockSpec(cw.shape, const), pl.BlockSpec(alog.shape, const), pl.BlockSpec(dtb.shape, const)],
        out_specs=[pl.BlockSpec((rows, DN_WIDTH), rmap)] * 4 + [pl.BlockSpec((rows, DN_HEADS * C), rmap),
                                                                pl.BlockSpec((cb * 8, DN_WIDTH), rmap)],
        out_shape=[wide] * 4 + [aqk_shape, jax.ShapeDtypeStruct((nb * nchunk * 8, DN_WIDTH), F32)],
        compiler_params=_cparams(("parallel", "parallel")),
        name="gdn_prep",
    )(dnqkv, dnqkv, small, cw, alog, dtb)
    r3 = lambda a: a.reshape(nb, -1, a.shape[-1])
    cmap = lambda c: (0, c, 0)
    full = lambda a: pl.BlockSpec(a.shape, lambda c: (0,) * a.ndim)
    ns = GDN_SCAN_CHUNKS
    o, s = pl.pallas_call(
        functools.partial(_gdn_scan_kernel, C=C),
        grid=(nchunk // ns,),
        in_specs=[pl.BlockSpec((nb, ns * C, DN_WIDTH), cmap)] * 4
        + [pl.BlockSpec((nb, ns * C, DN_HEADS * C), cmap),
           pl.BlockSpec((nb, ns * 8, DN_WIDTH), cmap), pl.BlockSpec((nb, ns * C, DN_WIDTH), cmap),
           full(nw), full(s0)],
        out_specs=[pl.BlockSpec((nb, ns * C, DN_WIDTH), cmap), full(s0)],
        out_shape=[jax.ShapeDtypeStruct((nb, lq, DN_WIDTH), BF16), jax.ShapeDtypeStruct(s0.shape, F32)],
        compiler_params=_cparams(("arbitrary",)),
        name="gdn_scan",
    )(r3(u), r3(w), r3(qg), r3(kd), r3(aqk), r3(egl), z[:nb * lq].reshape(nb, lq, DN_WIDTH), nw, s0)
    return o.reshape(nb * lq, DN_WIDTH), s


def _s5_discretize(lre_ref, lim_ref, lstep_ref):
    lr = lre_ref[...]
    li = lim_ref[...]
    dt = jnp.exp(lstep_ref[...])
    mag = jnp.exp(lr * dt)
    ar = mag * jnp.cos(li * dt)
    ai = mag * jnp.sin(li * dt)
    den = lr * lr + li * li
    fr = ((ar - 1.0) * lr + ai * li) / den
    fi = (ai * lr - (ar - 1.0) * li) / den
    return ar, ai, fr, fi


def _s5_kernel(*refs, R, S, with_y):
    if with_y:
        (u_ref, h0r_ref, h0i_ref, lre_ref, lim_ref, lstep_ref, bre_ref, bim_ref, cre_ref, cim_ref, d_ref,
         wglu_ref, y_ref, hr_ref, hi_ref, ar_s, ai_s, bbr_s, bbi_s, xr_s, xi_s) = refs
    else:
        (u_ref, h0r_ref, h0i_ref, lre_ref, lim_ref, lstep_ref, bre_ref, bim_ref,
         hr_ref, hi_ref, ar_s, ai_s, bbr_s, bbi_s, xr_s, xi_s) = refs
    i = pl.program_id(0)

    @pl.when(i == 0)
    def _():
        ar, ai, fr, fi = _s5_discretize(lre_ref, lim_ref, lstep_ref)
        ar_s[...] = ar
        ai_s[...] = ai
        bbr_s[...] = (fr * bre_ref[...] - fi * bim_ref[...]).astype(BF16)
        bbi_s[...] = (fr * bim_ref[...] + fi * bre_ref[...]).astype(BF16)
        hr_ref[...] = h0r_ref[...]
        hi_ref[...] = h0i_ref[...]

    u = u_ref[...]
    ub = u.astype(BF16)
    xr_s[...] = _dotf(ub, bbr_s[...])
    xi_s[...] = _dotf(ub, bbi_s[...])
    ar = jnp.broadcast_to(ar_s[...], (R, S5_LANES))
    ai = jnp.broadcast_to(ai_s[...], (R, S5_LANES))

    if not with_y:
        def carry_step(s, h):
            off = pl.multiple_of(s * R, R)
            return (ar * h[0] - ai * h[1] + xr_s[pl.ds(off, R), :], ar * h[1] + ai * h[0] + xi_s[pl.ds(off, R), :])

        hr, hi = lax.fori_loop(0, S, carry_step, (hr_ref[...], hi_ref[...]))
        hr_ref[...] = hr
        hi_ref[...] = hi
        return

    def advance(hr, hi, off):
        nr = ar * hr - ai * hi + xr_s[pl.ds(off, R), :]
        ni = ar * hi + ai * hr + xi_s[pl.ds(off, R), :]
        xr_s[pl.ds(off, R), :] = nr
        xi_s[pl.ds(off, R), :] = ni

    advance(hr_ref[...], hi_ref[...], 0)

    def step(s, carry):
        prev = pl.multiple_of((s - 1) * R, R)
        advance(xr_s[pl.ds(prev, R), :], xi_s[pl.ds(prev, R), :], pl.multiple_of(s * R, R))
        return carry

    lax.fori_loop(1, S, step, 0)
    hr_ref[...] = xr_s[pl.ds((S - 1) * R, R), :]
    hi_ref[...] = xi_s[pl.ds((S - 1) * R, R), :]
    if with_y:
        y = _dotf(xr_s[...].astype(BF16), cre_ref[...]) - _dotf(xi_s[...].astype(BF16), cim_ref[...])
        y = y + d_ref[...] * u
        gl = jax.nn.gelu(y)
        y_ref[...] = (gl * jax.nn.sigmoid(_dotf(gl.astype(BF16), wglu_ref[...]))).astype(y_ref.dtype)


def _s5_scan(u_rows, h0r, h0i, prm, *, R, S, with_y):
    n = u_rows.shape[0]
    rows = R * S
    const = lambda i: (0, 0)
    rmap = lambda i: (i, 0)
    ins = [u_rows, h0r, h0i, prm["lre"], prm["lim"], prm["lstep"], prm["bre"], prm["bim"]]
    if with_y:
        ins += [prm["cre"], prm["cim"], prm["d"], prm["wglu"]]
    in_specs = [pl.BlockSpec((rows, S5_WIDTH), rmap)] + [pl.BlockSpec(a.shape, const) for a in ins[1:]]
    st_spec = pl.BlockSpec((R, S5_LANES), const)
    st_shape = jax.ShapeDtypeStruct((R, S5_LANES), F32)
    out_specs = [st_spec, st_spec]
    out_shape = [st_shape, st_shape]
    if with_y:
        out_specs = [pl.BlockSpec((rows, S5_WIDTH), rmap)] + out_specs
        out_shape = [jax.ShapeDtypeStruct((n, S5_WIDTH), BF16)] + out_shape
    return pl.pallas_call(
        functools.partial(_s5_kernel, R=R, S=S, with_y=with_y),
        grid=(n // rows,),
        in_specs=in_specs,
        out_specs=out_specs,
        out_shape=out_shape,
        scratch_shapes=[pltpu.VMEM((1, S5_LANES), F32), pltpu.VMEM((1, S5_LANES), F32),
                        pltpu.VMEM((S5_WIDTH, S5_LANES), BF16), pltpu.VMEM((S5_WIDTH, S5_LANES), BF16),
                        pltpu.VMEM((rows, S5_LANES), F32), pltpu.VMEM((rows, S5_LANES), F32)],
        compiler_params=_cparams(("arbitrary",)),
        name="s5_scan",
    )(*ins)


def _s5_carry_kernel(er_ref, ei_ref, lre_ref, lim_ref, lstep_ref, ir_ref, ii_ref, fr_ref, fi_ref, *, nseg, nsteps):
    ar, ai, _, _ = _s5_discretize(lre_ref, lim_ref, lstep_ref)
    pr, pi_ = ar, ai
    n = 1
    while n < nsteps:
        pr, pi_ = pr * pr - pi_ * pi_, 2.0 * pr * pi_
        n *= 2
    for b in range(er_ref.shape[0] // nseg):
        cr = jnp.zeros((1, S5_LANES), F32)
        ci = jnp.zeros((1, S5_LANES), F32)
        for s in range(nseg):
            r = b * nseg + s
            ir_ref[r:r + 1, :] = cr
            ii_ref[r:r + 1, :] = ci
            er = er_ref[r:r + 1, :]
            ei = ei_ref[r:r + 1, :]
            cr, ci = pr * cr - pi_ * ci + er, pr * ci + pi_ * cr + ei
        fr_ref[b:b + 1, :] = cr
        fi_ref[b:b + 1, :] = ci


def _s5_carry(er, ei, prm, *, nseg, nsteps):
    assert nsteps & (nsteps - 1) == 0
    nb = er.shape[0] // nseg
    full = lambda a: pl.BlockSpec(a.shape, lambda: (0,) * a.ndim)
    ins = [er, ei, prm["lre"], prm["lim"], prm["lstep"]]
    outs = [jax.ShapeDtypeStruct(er.shape, F32)] * 2 + [jax.ShapeDtypeStruct((nb, S5_LANES), F32)] * 2
    return pl.pallas_call(
        functools.partial(_s5_carry_kernel, nseg=nseg, nsteps=nsteps),
        in_specs=[full(a) for a in ins],
        out_specs=[full(o) for o in outs],
        out_shape=outs,
        name="s5_carry",
    )(*ins)


S5_SEGS = 8


def _s5_prep(lre, lim, lstep, bre, bim, cre, cim, d, wglu):
    eye = jnp.eye(S5_GROUPS, dtype=F32)
    bexp = lambda b: jnp.einsum("gpc,gh->gchp", b, eye).reshape(S5_WIDTH, S5_LANES)
    cexp = lambda c: jnp.einsum("gcp,gh->gphc", c, eye).reshape(S5_LANES, S5_WIDTH).astype(BF16)
    return {"lre": lre.reshape(1, S5_LANES), "lim": lim.reshape(1, S5_LANES),
            "lstep": jnp.repeat(lstep, S5_STATE).reshape(1, S5_LANES),
            "bre": bexp(bre), "bim": bexp(bim), "cre": cexp(cre), "cim": cexp(cim),
            "d": d.reshape(1, S5_WIDTH), "wglu": wglu.astype(BF16)}


def _s5_prompt(u, prm, nb, lq, steps_per_tile=32):
    nsteps = lq // S5_SEGS
    R = nb * S5_SEGS
    u_rows = u.reshape(nb, S5_SEGS, nsteps, S5_WIDTH).transpose(2, 0, 1, 3).reshape(nsteps * R, S5_WIDTH)
    zero = jnp.zeros((R, S5_LANES), F32)
    er, ei = _s5_scan(u_rows, zero, zero, prm, R=R, S=steps_per_tile, with_y=False)
    ir, ii, fr, fi = _s5_carry(er, ei, prm, nseg=S5_SEGS, nsteps=nsteps)
    y_rows, _, _ = _s5_scan(u_rows, ir, ii, prm, R=R, S=steps_per_tile, with_y=True)
    y = y_rows.reshape(nsteps, nb, S5_SEGS, S5_WIDTH).transpose(1, 2, 0, 3).reshape(nb * lq, S5_WIDTH)
    return y, fr, fi


def _s5_sample(u, h0r, h0i, prm, nb, lq, steps_per_tile=4):
    u_rows = u.reshape(nb, lq, S5_WIDTH).transpose(1, 0, 2).reshape(lq * nb, S5_WIDTH)
    y_rows, fr, fi = _s5_scan(u_rows, h0r, h0i, prm, R=nb, S=steps_per_tile, with_y=True)
    y = y_rows.reshape(lq, nb, S5_WIDTH).transpose(1, 0, 2).reshape(nb * lq, S5_WIDTH)
    return y, fr, fi


def _sample_conv_hist(buf):
    nb = buf.shape[0]
    blk = buf.reshape(nb // 8, 8, DN_CONV_W - 1, 768)
    blk = jnp.roll(blk, -1, axis=1)
    blk = jnp.pad(blk, ((0, 0), (0, 0), (8 - (DN_CONV_W - 1), 0), (0, 0)))
    return blk.reshape(nb * 8, 768)


def _rms(x, g):
    return x * lax.rsqrt(jnp.mean(x * x, axis=-1, keepdims=True) + NORM_EPS) * g


def _mlp_kernel(x_ref, on_ref, od_ref, os_ref, wn_ref, wd_ref, ws_ref, ln2_ref, up_ref, dn_ref, lnf_ref,
                o_ref, h2_s, *, final_norm):
    j = pl.program_id(1)

    @pl.when(j == 0)
    def _():
        x1 = (x_ref[...] + _dotf(on_ref[...], wn_ref[...]) + _dotf(od_ref[...], wd_ref[...])
              + _dotf(os_ref[...], ws_ref[...]))
        o_ref[...] = x1
        h2_s[...] = _rms(x1, ln2_ref[...]).astype(BF16)

    a = jnp.maximum(_dotf(h2_s[...], up_ref[...]), 0.0)
    o_ref[...] += _dotf((a * a).astype(BF16), dn_ref[...])

    if final_norm:
        @pl.when(j == pl.num_programs(1) - 1)
        def _():
            o_ref[...] = _rms(o_ref[...], lnf_ref[...])


def _mlp(x, o_nsa, o_dn, o_s5, wn, wd, ws, ln2, up, dn, lnf, *, final_norm, tm=512, tf=1024):
    n = x.shape[0]
    row = lambda i, j: (i, 0)
    const = lambda i, j: (0, 0)
    return pl.pallas_call(
        functools.partial(_mlp_kernel, final_norm=final_norm),
        grid=(n // tm, D_FF // tf),
        in_specs=[pl.BlockSpec((tm, D_MODEL), row), pl.BlockSpec((tm, 1024), row),
                  pl.BlockSpec((tm, DN_WIDTH), row), pl.BlockSpec((tm, S5_WIDTH), row),
                  pl.BlockSpec(wn.shape, const), pl.BlockSpec(wd.shape, const), pl.BlockSpec(ws.shape, const),
                  pl.BlockSpec((1, D_MODEL), const),
                  pl.BlockSpec((D_MODEL, tf), lambda i, j: (0, j)),
                  pl.BlockSpec((tf, D_MODEL), lambda i, j: (j, 0)),
                  pl.BlockSpec((1, D_MODEL), const)],
        out_specs=pl.BlockSpec((tm, D_MODEL), row),
        out_shape=jax.ShapeDtypeStruct((n, D_MODEL), F32),
        scratch_shapes=[pltpu.VMEM((tm, D_MODEL), BF16)],
        compiler_params=_cparams(("parallel", "arbitrary")),
        name="out_mlp",
    )(x, o_nsa, o_dn, o_s5, wn, wd, ws, ln2, up, dn, lnf)


_NSA_W = NSA_HEADS * HEAD_DIM
_KV_W = 6 * NSA_KV_HEADS * HEAD_DIM
_GATE_W = NSA_HEADS * 3
_OFF_KV = _NSA_W
_OFF_GATE = _OFF_KV + _KV_W
_OFF_DN = _OFF_GATE + _GATE_W
_OFF_A = _OFF_DN + 3 * DN_WIDTH
_OFF_B = _OFF_A + DN_HEADS
_OFF_Z = _OFF_B + DN_HEADS
_OFF_U = _OFF_Z + DN_WIDTH


def _pad_heads(w, axis):
    w = jnp.moveaxis(w, axis, 0).reshape((NSA_HEADS, HEAD_DIM) + w.shape[:axis] + w.shape[axis + 1:])
    out = jnp.zeros((NSA_HEADS, LANE) + w.shape[2:], w.dtype)
    for h in range(NSA_HEADS):
        g = h // NSA_GROUP
        out = out.at[h, g * HEAD_DIM:(g + 1) * HEAD_DIM].set(w[h])
    out = out.reshape((NSA_HEADS * LANE,) + w.shape[2:])
    return jnp.moveaxis(out, 0, axis)


def _layer_weights(w_in, w_out, wck, wcv):
    wq = _pad_heads(w_in[:, :_NSA_W], 1).astype(BF16)
    wkv = w_in[:, _OFF_KV:_OFF_GATE].astype(BF16)
    wdn = w_in[:, _OFF_DN:_OFF_A].astype(BF16)
    wz = w_in[:, _OFF_Z:_OFF_U].astype(BF16)
    wu = w_in[:, _OFF_U:].astype(BF16)
    wsm = jnp.zeros((D_MODEL, LANE), F32)
    wsm = wsm.at[:, :_GATE_W].set(w_in[:, _OFF_GATE:_OFF_DN])
    wsm = wsm.at[:, _GATE_W:_GATE_W + 2 * DN_HEADS].set(w_in[:, _OFF_A:_OFF_Z]).astype(BF16)
    wexp = jnp.concatenate([jnp.repeat(wck.T, HEAD_DIM, axis=1), jnp.repeat(wcv.T, HEAD_DIM, axis=1)], axis=1)
    won = _pad_heads(w_out[:_NSA_W], 0).astype(BF16)
    wod = w_out[_NSA_W:_NSA_W + DN_WIDTH].astype(BF16)
    wos = w_out[_NSA_W + DN_WIDTH:].astype(BF16)
    return wq, wkv, wdn, wz, wu, wsm, wexp, won, wod, wos


def kernel(x_prompt, x_sample, cache_nsa_kv, cache_win_kv, state_dn_conv, state_dn, state_s5_re, state_s5_im,
           page_table, ln1, ln2, ln_f, w_in, w_out, nsa_wck, nsa_wcv, dn_conv_w, dn_a_log, dn_dt_bias,
           dn_norm_w, s5_lambda_re, s5_lambda_im, s5_log_step, s5_b_re, s5_b_im, s5_c_re, s5_c_im, s5_d,
           s5_w_glu, mlp_up, mlp_down):
    nbp, lp, _ = x_prompt.shape
    nbs, ls, _ = x_sample.shape
    depth = w_in.shape[0]
    rows_p = nbp * lp
    rows_s = nbs * ls
    n_phys, page = cache_nsa_kv.shape[1], cache_nsa_kv.shape[2]
    past = page_table.shape[1] * page
    wlen = cache_win_kv.shape[2]
    G, dh = NSA_KV_HEADS, HEAD_DIM
    assert lp % (S5_SEGS * 32) == 0 and nbs % 8 == 0 and ls == 8 and wlen == WINDOW and lp >= WINDOW

    xp = x_prompt.reshape(rows_p, D_MODEL)
    xs = x_sample.reshape(rows_s, D_MODEL)
    cache_t = cache_nsa_kv.transpose(0, 1, 3, 4, 5, 2).reshape(depth, n_phys, 4 * G * dh, page)
    win_t = cache_win_kv.transpose(0, 1, 3, 4, 5, 2).reshape(depth, nbs, 2 * G * dh, wlen)
    outs_p, outs_s = [], []
    kvt_main = nwin_t = None
    for l in range(depth):
        wq, wkv, wdn, wz, wu, wsm, wexp, won, wod, wos = _layer_weights(w_in[l], w_out[l], nsa_wck[l], nsa_wcv[l])
        pw = (ln1[l][None], wq, wkv, wdn, wz, wu, wsm, wexp)
        qh_p, kv_p, dnx_p, z_p, u_p, small_p, pool_p, kvt_main, kvt_win, kaug, vt = _proj(
            xp, *pw, nseq=nbp, prev_kvt=kvt_main, sel_tile=NSA_TQ)
        qh_s, kv_s, dnx_s, z_s, u_s, small_s, _, _, _ = _proj(xs, *pw, nseq=1)

        oc, sel, anyblk = _cmp_prompt(qh_p, pool_p, nbp, lp, NSA_TQ)
        osel = _sel_prompt(qh_p, sel, anyblk, kaug, vt, nbp, lp, NSA_TQ)
        on_p = _win_prompt(qh_p, kv_p, kvt_win, oc, osel, small_p, nbp, lp)
        wt = _pool_weights_t(nsa_wck[l], nsa_wcv[l], past)
        on_s, nwin_t = _nsa_sample(page_table, qh_s, kv_s, small_s, cache_t, win_t, wt, nwin_t,
                                   layer=l, lq=ls, past=past)

        gdn_w = (dn_conv_w[l].T, dn_a_log[l][None], dn_dt_bias[l][None], dn_norm_w[l][None])
        od_p, dn_p = _gdn_prompt(dnx_p, small_p, z_p, *gdn_w, jnp.zeros((nbp, DN_HEADS, dh, dh), F32), nb=nbp, lq=lp)
        od_s, dn_s = _gdn(dnx_s, _sample_conv_hist(state_dn_conv[l]), small_s, z_s, *gdn_w, state_dn[l],
                          seg=ls, nb=rows_s // GDN_ROWS, nchunk=1, row0=0,
                          hist_map=lambda b, c: (b, 0), zero_first_hist=False)

        prm = _s5_prep(s5_lambda_re[l], s5_lambda_im[l], s5_log_step[l], s5_b_re[l], s5_b_im[l],
                       s5_c_re[l], s5_c_im[l], s5_d[l], s5_w_glu[l])
        os_p, s5r_p, s5i_p = _s5_prompt(u_p, prm, nbp, lp)
        os_s, s5r_s, s5i_s = _s5_sample(u_s, state_s5_re[l].reshape(nbs, S5_LANES),
                                        state_s5_im[l].reshape(nbs, S5_LANES), prm, nbs, ls)

        mw = (won, wod, wos, ln2[l][None], mlp_up[l].astype(BF16), mlp_down[l].astype(BF16), ln_f[None])
        xp = _mlp(xp, on_p, od_p, os_p, *mw, final_norm=(l == depth - 1))
        xs = _mlp(xs, on_s, od_s, os_s, *mw, final_norm=(l == depth - 1))

        kv_s6 = kv_s.reshape(nbs, ls, 6, G, dh)
        win_p = kvt_win[:, :, lp - WINDOW:].reshape(nbp, 2, G, dh, WINDOW).transpose(0, 4, 1, 2, 3)
        outs_p.append((win_p, dnx_p.reshape(nbp, lp, 3 * DN_WIDTH)[:, -(DN_CONV_W - 1):], dn_p,
                       s5r_p.reshape(nbp, S5_GROUPS, S5_STATE), s5i_p.reshape(nbp, S5_GROUPS, S5_STATE)))
        outs_s.append((kv_s6[:, :, :4], dnx_s.reshape(nbs, ls, 3 * DN_WIDTH)[:, -(DN_CONV_W - 1):], dn_s,
                       s5r_s.reshape(nbs, S5_GROUPS, S5_STATE), s5i_s.reshape(nbs, S5_GROUPS, S5_STATE)))

    stack = lambda outs, i: jnp.stack([o[i] for o in outs], axis=0)
    kv_prompt = kvt_main.reshape(depth, nbp, 4, G, dh, lp).transpose(0, 1, 5, 2, 3, 4)
    win_sample = nwin_t.reshape(depth, nbs, 2, G, dh, wlen).transpose(0, 1, 5, 2, 3, 4)
    return (xp.reshape(nbp, lp, D_MODEL), xs.reshape(nbs, ls, D_MODEL),
            kv_prompt, stack(outs_s, 0), stack(outs_p, 0), win_sample,
            stack(outs_p, 1), stack(outs_s, 1), stack(outs_p, 2), stack(outs_s, 2),
            stack(outs_p, 3), stack(outs_s, 3), stack(outs_p, 4), stack(outs_s, 4))
```

```python
import functools

import jax
import jax.numpy as jnp
from jax import lax
from jax.experimental import pallas as pl
from jax.experimental.pallas import tpu as pltpu

F32 = jnp.float32
BF16 = jnp.bfloat16
HIGHEST = lax.Precision.HIGHEST

D_MODEL = 1024
HEAD_DIM = 64
NSA_HEADS = 8
NSA_KV_HEADS = 2
NSA_GROUP = 4
NSA_BLOCK = 64
NSA_TOPK = 16
WINDOW = 512
DN_HEADS = 4
DN_WIDTH = 256
DN_CONV_W = 4
S5_GROUPS = 16
S5_GROUP_CH = 16
S5_STATE = 64
S5_WIDTH = 256
S5_LANES = S5_GROUPS * S5_STATE
D_FF = 4096
NORM_EPS = 1e-6
BIG = 1e9
NEG = -1e30
LANE = 128
NSA_TQ = 256
VMEM_LIMIT = 48 * 1024 * 1024

_NT = (((1,), (1,)), ((), ()))


def _slope(h):
    return 2.0 ** (-(h + 1))


def _cparams(sem):
    return pltpu.CompilerParams(dimension_semantics=sem, vmem_limit_bytes=VMEM_LIMIT)


def _dotf(a, b):
    return jnp.dot(a, b, preferred_element_type=F32)


def _dot_nt(a, b):
    return lax.dot_general(a, b, _NT, preferred_element_type=F32)


def _dot_hi(a, b):
    return jnp.dot(a, b, preferred_element_type=F32, precision=HIGHEST)


def _proj_kernel(x_ref, g_ref, wq_ref, wkv_ref, wdn_ref, wz_ref, wu_ref, ws_ref, wexp_ref, *refs, lt, tk, nprev):
    if nprev:
        prev_ref, refs = refs[0], refs[1:]
    q_ref, kv_ref, dn_ref, z_ref, u_ref, s_ref, pool_ref, kvtm_ref, kvtw_ref = refs[:9]
    sel_refs = refs[9:]
    x = x_ref[...]
    h = x * lax.rsqrt(jnp.mean(x * x, axis=-1, keepdims=True) + NORM_EPS) * g_ref[...]
    hb = h.astype(BF16)
    q_ref[...] = _dotf(hb, wq_ref[...])
    kv = _dotf(hb, wkv_ref[...])
    kv_ref[...] = kv
    kvt = kv.T
    if nprev:
        kvtm_ref[0:nprev] = prev_ref[...]
    kvtm_ref[nprev] = kvt[:4 * LANE]
    kvtw_ref[...] = kvt[4 * LANE:]
    if sel_refs:
        kaug_ref, vt_ref = sel_refs
        rows = x.shape[0]
        pos = (pl.program_id(0) % lt) * rows + lax.broadcasted_iota(jnp.int32, (rows, LANE), 0)
        lanei = lax.broadcasted_iota(jnp.int32, (rows, LANE), 1)
        onehot = (lanei == pos // NSA_BLOCK).astype(BF16)
        jpart = jnp.where(lanei == HEAD_DIM, (pos % NSA_BLOCK).astype(F32), 0.0)
        ksl = kv[:, 2 * LANE:3 * LANE]
        for g in range(NSA_KV_HEADS):
            kg = ksl if g == 0 else pltpu.roll(ksl, HEAD_DIM, axis=1)
            kaug_ref[g] = jnp.concatenate([onehot, jnp.where(lanei < HEAD_DIM, kg, jpart).astype(BF16)], axis=1)
            vrow = 3 * LANE + g * HEAD_DIM
            for c in range(rows // tk):
                vt_ref[g, c] = kvt[vrow:vrow + HEAD_DIM, c * tk:(c + 1) * tk].astype(BF16)
    dn_ref[...] = _dotf(hb, wdn_ref[...])
    z_ref[...] = _dotf(hb, wz_ref[...])
    u_ref[...] = _dotf(hb, wu_ref[...])
    s_ref[...] = _dotf(hb, ws_ref[...])
    tm = x.shape[0]
    kc = kv[:, :2 * LANE].reshape(tm // NSA_BLOCK, NSA_BLOCK, 2 * LANE) * wexp_ref[...][None]
    pool_ref[...] = jnp.sum(kc, axis=1)


def _proj(x, ln, wq, wkv, wdn, wz, wu, ws, wexp, nseq, prev_kvt=None, sel_tile=None, tm=512):
    n = x.shape[0]
    lseq = n // nseq
    lt = lseq // tm
    nprev = 0 if prev_kvt is None else prev_kvt.shape[0]
    const = lambda i: (0, 0)
    row = lambda i: (i, 0)
    outs = [(n, 1024), (n, 768), (n, 768), (n, 256), (n, 256), (n, LANE), (n // NSA_BLOCK, 2 * LANE),
            (nprev + 1, nseq, 4 * LANE, lseq), (nseq, 2 * LANE, lseq)]
    out_specs = ([pl.BlockSpec((tm, s[1]), row) for s in outs[:6]]
                 + [pl.BlockSpec((tm // NSA_BLOCK, 2 * LANE), row),
                    pl.BlockSpec((nprev + 1, None, 4 * LANE, tm), lambda i: (0, i // lt, 0, i % lt)),
                    pl.BlockSpec((None, 2 * LANE, tm), lambda i: (i // lt, 0, i % lt))])
    out_shape = [jax.ShapeDtypeStruct(s, F32) for s in outs]
    prev_specs = [pl.BlockSpec((nprev, None, 4 * LANE, tm), lambda i: (0, i // lt, 0, i % lt))] if nprev else []
    G = NSA_KV_HEADS
    if sel_tile is not None:
        out_specs += [pl.BlockSpec((None, G, tm, 2 * LANE), lambda i: (i // lt, 0, i % lt, 0)),
                      pl.BlockSpec((None, G, tm // sel_tile, HEAD_DIM, sel_tile),
                                   lambda i: (i // lt, 0, i % lt, 0, 0))]
        out_shape += [jax.ShapeDtypeStruct((nseq, G, lseq, 2 * LANE), BF16),
                      jax.ShapeDtypeStruct((nseq, G, lseq // sel_tile, HEAD_DIM, sel_tile), BF16)]
    return pl.pallas_call(
        functools.partial(_proj_kernel, lt=lt, tk=sel_tile, nprev=nprev),
        grid=(n // tm,),
        in_specs=[pl.BlockSpec((tm, D_MODEL), row), pl.BlockSpec((1, D_MODEL), const)]
        + [pl.BlockSpec(w.shape, const) for w in (wq, wkv, wdn, wz, wu, ws, wexp)] + prev_specs,
        out_specs=out_specs,
        out_shape=out_shape,
        compiler_params=_cparams(("parallel",)),
        name="proj",
    )(x, ln, wq, wkv, wdn, wz, wu, ws, wexp, *([prev_kvt] if nprev else []))


def _topk_round(sel, score, blk, nblk):
    m = jnp.max(score, axis=0, keepdims=True)
    idx = jnp.min(jnp.where(score == m, blk, nblk), axis=0, keepdims=True)
    pick = blk == idx
    return sel | pick, jnp.where(pick, -jnp.inf, score)


NSA_FORCED = 3
NSA_FREE_ROUNDS = NSA_TOPK - NSA_FORCED


def _free_scores(imp, forced, blk, cur):
    return jnp.where(blk > cur, -BIG, jnp.where(forced, -jnp.inf, imp))


def _topk_blocks(score, blk, nblk):
    sel = jnp.zeros(score.shape, dtype=jnp.bool_)
    for _ in range(NSA_FREE_ROUNDS):
        sel, score = _topk_round(sel, score, blk, nblk)
    return sel


def _cmp_heads(q_tile, kc, vct, t, blk, nblk_valid, write_oc):
    nblk = blk.shape[0]
    ok = ((blk + 1) * NSA_BLOCK - 1 <= t) & (blk < nblk_valid)
    okf = ok.astype(F32)
    dist = t.astype(F32) - (blk.astype(F32) * NSA_BLOCK + (NSA_BLOCK - 1) / 2.0)
    cur = t // NSA_BLOCK
    forced = (blk == 0) | (blk == cur) | (blk == cur - 1)
    sels = []
    for g in range(NSA_KV_HEADS):
        imp = jnp.zeros(blk.shape, F32)
        for r in range(NSA_GROUP):
            h = g * NSA_GROUP + r
            q = (q_tile(h) * HEAD_DIM ** -0.5).astype(BF16)
            s = _dot_nt(kc, q)
            s = jnp.where(ok, s - _slope(h) * dist, NEG)
            m = jnp.max(s, axis=0, keepdims=True)
            p = jnp.exp(s - m) * okf
            p = p / jnp.maximum(jnp.sum(p, axis=0, keepdims=True), 1e-30)
            imp = imp + p
            write_oc(h, _dotf(vct, p.astype(BF16)))
        sel = (forced | _topk_blocks(_free_scores(imp, forced, blk, cur), blk, nblk)) & (blk <= cur)
        sels.append(sel)
    return sels


def _cmp_kernel(q_ref, kcv_ref, oc_ref, sel_ref, any_ref, *, tq):
    qt = pl.program_id(1)
    kcv = kcv_ref[...]
    nblk = kcv.shape[0]
    kc = kcv[:, :LANE].astype(BF16)
    vct = kcv[:, LANE:].T.astype(BF16)
    blk = lax.broadcasted_iota(jnp.int32, (nblk, tq), 0)
    t = qt * tq + lax.broadcasted_iota(jnp.int32, (nblk, tq), 1)

    def write_oc(h, oct):
        oc_ref[:, h * LANE:(h + 1) * LANE] = oct.T

    sels = _cmp_heads(lambda h: q_ref[:, h * LANE:(h + 1) * LANE], kc, vct, t, blk, nblk, write_oc)
    for g in range(NSA_KV_HEADS):
        selt = sels[g].astype(F32).T
        sel_ref[:, g * LANE:(g + 1) * LANE] = selt
        any_ref[:, g * LANE:(g + 1) * LANE] = jnp.broadcast_to(jnp.max(selt, axis=0, keepdims=True), (8, LANE))


def _cmp_prompt(qh, pool, nb, lq, tq):
    nq = lq // tq
    nblk = lq // NSA_BLOCK
    assert nblk == LANE
    return pl.pallas_call(
        functools.partial(_cmp_kernel, tq=tq),
        grid=(nb, nq),
        in_specs=[pl.BlockSpec((tq, 1024), lambda b, i: (b * nq + i, 0)),
                  pl.BlockSpec((nblk, 2 * LANE), lambda b, i: (b, 0))],
        out_specs=[pl.BlockSpec((tq, 1024), lambda b, i: (b * nq + i, 0)),
                   pl.BlockSpec((tq, 2 * LANE), lambda b, i: (b * nq + i, 0)),
                   pl.BlockSpec((8, 2 * LANE), lambda b, i: (b * nq + i, 0))],
        out_shape=[jax.ShapeDtypeStruct((nb * lq, 1024), F32),
                   jax.ShapeDtypeStruct((nb * lq, 2 * LANE), F32),
                   jax.ShapeDtypeStruct((nb * nq * 8, 2 * LANE), F32)],
        compiler_params=_cparams(("parallel", "parallel")),
        name="nsa_cmp",
    )(qh, pool)


def _sel_kernel(cnt_ref, lst_ref, q_ref, sel_ref, k_ref, vt_ref, o_ref, lhs_s, m_s, l_s, acc_s, *, tq, nq):
    b = pl.program_id(0)
    g = pl.program_id(1)
    qt = pl.program_id(2)
    lanei = lax.broadcasted_iota(jnp.int32, (tq, LANE), 1)
    nio = lanei.astype(F32)
    selg = sel_ref[...] > 0.5
    for r in range(NSA_GROUP):
        slope = jnp.where(g == 0, _slope(r), _slope(NSA_GROUP + r))
        qt_r = q_ref[:, r * LANE:(r + 1) * LANE]
        qt_r = jnp.where(g == 0, qt_r, pltpu.roll(qt_r, HEAD_DIM, axis=1))
        qpart = jnp.where(lanei == HEAD_DIM, slope, qt_r * HEAD_DIM ** -0.5)
        bias = jnp.where(selg, (NSA_BLOCK * slope) * nio, NEG)
        lhs_s[r] = jnp.concatenate([bias.astype(BF16), qpart.astype(BF16)], axis=1)
    m_s[...] = jnp.full(m_s.shape, NEG, F32)
    l_s[...] = jnp.zeros(l_s.shape, F32)
    acc_s[...] = jnp.zeros(acc_s.shape, F32)

    def tiles(items):
        heads = range(NSA_GROUP)
        kaugs = [k_ref[kt] for kt, _ in items]
        vts = [vt_ref[kt] for kt, _ in items]
        ss = [[_dot_nt(ka, lhs_s[r]) for r in heads] for ka in kaugs]
        ss = [[s if mask is None else jnp.where(mask, s, NEG) for s in row] for row, (_, mask) in zip(ss, items)]
        m_old = [m_s[r] for r in heads]
        m_new = []
        for r in heads:
            m = m_old[r]
            for row in ss:
                m = jnp.maximum(m, jnp.max(row[r], axis=0, keepdims=True))
            m_new.append(m)
        ps = [[jnp.exp(row[r] - m_new[r]) for r in heads] for row in ss]
        pvs = [[_dotf(vt, row[r].astype(BF16)) for r in heads] for vt, row in zip(vts, ps)]
        for r in heads:
            alpha = jnp.exp(m_old[r] - m_new[r])
            l_s[r] = alpha * l_s[r] + sum(jnp.sum(row[r], axis=0, keepdims=True) for row in ps)
            acc_s[r] = alpha * acc_s[r] + sum(row[r] for row in pvs)
            m_s[r] = m_new[r]

    cbase = (b * NSA_KV_HEADS + g) * nq + qt
    n_act = cnt_ref[cbase]
    lbase = cbase * nq

    def body(i, carry):
        tiles([(lst_ref[lbase + 2 * i], None), (lst_ref[lbase + 2 * i + 1], None)])
        return carry

    lax.fori_loop(0, n_act // 2, body, 0)
    diag = (qt, lax.broadcasted_iota(jnp.int32, (tq, tq), 0) <= lax.broadcasted_iota(jnp.int32, (tq, tq), 1))

    @pl.when(n_act % 2 == 1)
    def _():
        tiles([(lst_ref[lbase + n_act - 1], None), diag])

    @pl.when(n_act % 2 == 0)
    def _():
        tiles([diag])

    zeros = jnp.zeros((HEAD_DIM, tq), F32)
    for r in range(NSA_GROUP):
        o = jnp.concatenate([acc_s[r] / l_s[r], zeros], axis=0).T
        o_ref[:, r * LANE:(r + 1) * LANE] = jnp.where(g == 0, o, pltpu.roll(o, HEAD_DIM, axis=1))


def _sel_prompt(qh, sel, anyblk, kaug, vt, nb, lq, tq):
    assert vt.shape[-1] == tq
    nq = lq // tq
    bpt = tq // NSA_BLOCK
    G = NSA_KV_HEADS
    act = anyblk[::8].reshape(nb, nq, G, nq, bpt).max(axis=-1) > 0.5
    tile_i = jnp.arange(nq)
    act = act.transpose(0, 2, 1, 3) & (tile_i[None, :] < tile_i[:, None])
    rank = jnp.cumsum(act, axis=-1) - 1
    hit = act[..., None, :] & (rank[..., None, :] == tile_i[:, None])
    lst = jnp.sum(jnp.where(hit, tile_i, 0), axis=-1).astype(jnp.int32).reshape(-1)
    cnt = jnp.sum(act, axis=-1).astype(jnp.int32).reshape(-1)
    kaug = kaug.reshape(nb, G, nq, tq, 2 * LANE)
    qmap = lambda b, g, i, c, l: (b * nq + i, g)
    whole = lambda b, g, i, c, l: (b, g, 0, 0, 0)
    grid_spec = pltpu.PrefetchScalarGridSpec(
        num_scalar_prefetch=2,
        grid=(nb, G, nq),
        in_specs=[pl.BlockSpec((tq, NSA_GROUP * LANE), qmap),
                  pl.BlockSpec((tq, LANE), qmap),
                  pl.BlockSpec((None, None, nq, tq, 2 * LANE), whole),
                  pl.BlockSpec((None, None, nq, HEAD_DIM, tq), whole)],
        out_specs=pl.BlockSpec((tq, NSA_GROUP * LANE), qmap),
        scratch_shapes=[pltpu.VMEM((NSA_GROUP, tq, 2 * LANE), BF16),
                        pltpu.VMEM((NSA_GROUP, 1, tq), F32),
                        pltpu.VMEM((NSA_GROUP, 1, tq), F32),
                        pltpu.VMEM((NSA_GROUP, HEAD_DIM, tq), F32)])
    return pl.pallas_call(
        functools.partial(_sel_kernel, tq=tq, nq=nq),
        grid_spec=grid_spec,
        out_shape=jax.ShapeDtypeStruct((nb * lq, 1024), F32),
        compiler_params=_cparams(("parallel", "parallel", "arbitrary")),
        name="nsa_sel",
    )(cnt, lst, qh, sel, kaug, vt)


def _gate_mix(small, oc, os_, ow, h):
    gt = jax.nn.sigmoid(small[:, 3 * h:3 * h + 3])
    return gt[:, 0:1] * oc + gt[:, 1:2] * os_ + gt[:, 2:3] * ow


WIN_HEADS_PER_ROUND = 4
_WIN_SPLIT = 32


def _win_kernel(q_ref, k0_ref, k1_ref, k2_ref, v0_ref, v1_ref, v2_ref, oc_ref, os_ref, sm_ref, o_ref, *, tq):
    qt = pl.program_id(1)
    nkey = 3 * tq
    ki = lax.broadcasted_iota(jnp.int32, (nkey, LANE), 0)
    li = lax.broadcasted_iota(jnp.int32, (nkey, LANE), 1)
    penc = jnp.where(li == 0, ki // _WIN_SPLIT, jnp.where(li == 1, ki % _WIN_SPLIT, 0)).astype(BF16)
    kcat = jnp.concatenate([k0_ref[...], k1_ref[...], k2_ref[...]], axis=0).astype(BF16)
    kaug = jnp.concatenate([kcat, penc], axis=1)
    vts = [v.astype(BF16) for v in (v0_ref[...], v1_ref[...], v2_ref[...])]
    krow = lax.broadcasted_iota(jnp.int32, (tq, tq), 0)
    qcol = lax.broadcasted_iota(jnp.int32, (tq, tq), 1)
    assert WINDOW == 2 * tq
    valids = [(krow > qcol) & (qt >= 2), jnp.broadcast_to(qt >= 1, (tq, tq)), krow <= qcol]
    small = sm_ref[...]
    lq_i = lax.broadcasted_iota(jnp.int32, (tq, LANE), 1)
    for h0 in range(0, NSA_HEADS, WIN_HEADS_PER_ROUND):
        heads = range(h0, h0 + WIN_HEADS_PER_ROUND)
        qaug = []
        for h in heads:
            al = jnp.where(lq_i == 0, _WIN_SPLIT * _slope(h), jnp.where(lq_i == 1, _slope(h), 0.0))
            qaug.append(jnp.concatenate([(q_ref[:, h * LANE:(h + 1) * LANE] * HEAD_DIM ** -0.5).astype(BF16),
                                         al.astype(BF16)], axis=1))
        tiles3 = range(3)
        ss = [[jnp.where(valids[j], _dot_nt(kaug[j * tq:(j + 1) * tq], qa), NEG) for j in tiles3]
              for qa in qaug]
        ms = [functools.reduce(jnp.maximum, [jnp.max(s, axis=0, keepdims=True) for s in row]) for row in ss]
        ps = [[jnp.exp(s - m) for s in row] for row, m in zip(ss, ms)]
        ows = []
        for row in ps:
            num = sum(_dotf(vts[j], row[j].astype(BF16)) for j in tiles3)
            den = sum(jnp.sum(row[j], axis=0, keepdims=True) for j in tiles3)
            ows.append((num / den).T)
        for h, ow in zip(heads, ows):
            sl = slice(h * LANE, (h + 1) * LANE)
            o_ref[:, sl] = _gate_mix(small, oc_ref[:, sl], os_ref[:, sl], ow, h).astype(o_ref.dtype)


def _win_prompt(qh, kv, kvt_win, oc, osel, small, nb, lq, tq=256):
    assert 2 * tq >= WINDOW - 1 and 3 * tq % _WIN_SPLIT == 0 and 3 * tq // _WIN_SPLIT <= 256
    nq = lq // tq
    row = lambda b, i: (b * nq + i, 0)

    def kmap(back):
        return lambda b, i: (b * nq + jnp.maximum(i - back, 0), 4)

    def vmap(back):
        return lambda b, i: (b, 1, jnp.maximum(i - back, 0))

    return pl.pallas_call(
        functools.partial(_win_kernel, tq=tq),
        grid=(nb, nq),
        in_specs=[pl.BlockSpec((tq, 1024), row)]
        + [pl.BlockSpec((tq, LANE), kmap(back)) for back in (2, 1, 0)]
        + [pl.BlockSpec((None, LANE, tq), vmap(back)) for back in (2, 1, 0)]
        + [pl.BlockSpec((tq, 1024), row), pl.BlockSpec((tq, 1024), row), pl.BlockSpec((tq, LANE), row)],
        out_specs=pl.BlockSpec((tq, 1024), row),
        out_shape=jax.ShapeDtypeStruct((nb * lq, 1024), BF16),
        compiler_params=_cparams(("parallel", "parallel")),
        name="nsa_win",
    )(qh, kv, kv, kv, kvt_win, kvt_win, kvt_win, oc, osel, small)


def _per_head(idx_h, fn):
    out = jnp.zeros(idx_h.shape, F32)
    for h in range(NSA_HEADS):
        out = jnp.where(idx_h == h, fn(h), out)
    return out


def _pad_rows(a, rows):
    return jnp.concatenate([a, jnp.zeros((rows - a.shape[0], a.shape[1]), a.dtype)], axis=0)


NSA_SAMPLE_SEQS = 2


def _nsa_sample_kernel(pt_ref, q_ref, kvn_ref, sm_ref, *rest, past, lq, npages, page, nprev):
    nseq = NSA_SAMPLE_SEQS
    pages = rest[:nseq * npages]
    rest = rest[nseq * npages:]
    if nprev:
        win_ref, wt_ref, prev_ref, o_ref, nwin_all, ktc_s, vtc_s, rt_s, vt_s = rest
        nwin_all[0:nprev] = prev_ref[...]
    else:
        win_ref, wt_ref, o_ref, nwin_all, ktc_s, vtc_s, rt_s, vt_s = rest
    nwin_ref = nwin_all.at[nprev]

    @pl.when(pl.program_id(0) == 0)
    def _():
        n_i = lax.broadcasted_iota(jnp.int32, (LANE, past), 0)
        pos_i = lax.broadcasted_iota(jnp.int32, (LANE, past), 1)
        for j in range(nseq):
            rt_s[j, 0:LANE, :] = (n_i == pos_i // NSA_BLOCK).astype(BF16)

    def rows(ref, j):
        return ref.at[pl.ds(j * lq, lq)]

    seqs = [_nsa_sample_seq(rows(q_ref, j), rows(kvn_ref, j), rows(sm_ref, j), pages[j * npages:(j + 1) * npages],
                            win_ref.at[j], wt_ref, rows(o_ref, j), nwin_ref.at[j], ktc_s.at[j], vtc_s.at[j],
                            rt_s.at[j], vt_s.at[j], past=past, lq=lq, npages=npages, page=page)
            for j in range(nseq)]
    live = True
    while live:
        for s in seqs:
            live = next(s, None) is not None and live


def _nsa_sample_seq(q_ref, kvn_ref, sm_ref, pages, win_ref, wt_ref, o_ref, nwin_ref, ktc_s, vtc_s, rt_s, vt_s,
                    *, past, lq, npages, page):
    nkeys = npages * page
    nrow = NSA_HEADS * lq
    nb_complete = (past + lq) // NSA_BLOCK
    new_blk = past // NSA_BLOCK
    assert past % NSA_BLOCK + lq <= NSA_BLOCK and nkeys == past and new_blk < LANE

    for i in range(npages):
        sl = slice(i * page, (i + 1) * page)
        ktc_s[:, sl] = pages[i][0:LANE, :].astype(BF16)
        vtc_s[:, sl] = pages[i][LANE:2 * LANE, :].astype(BF16)
    yield True

    def stage_selected_page(i):
        sl = slice(i * page, (i + 1) * page)
        rt_s[LANE:2 * LANE, sl] = pages[i][2 * LANE:3 * LANE, :].astype(BF16)
        vt_s[:, sl] = pages[i][3 * LANE:4 * LANE, :].astype(BF16)

    q8 = q_ref[...]
    qf = jnp.concatenate([q8[:, h * LANE:(h + 1) * LANE] for h in range(NSA_HEADS)], axis=0) * HEAD_DIM ** -0.5
    qb = qf.astype(BF16)
    kvn = kvn_ref[...]
    lane_lo = lax.broadcasted_iota(jnp.int32, (LANE, LANE), 1) < HEAD_DIM

    ktc = ktc_s[...]
    vtc = vtc_s[...]
    kc = jnp.where(lane_lo, _dot_nt(wt_ref[0, 0], ktc), _dot_nt(wt_ref[0, 1], ktc)).astype(BF16)
    vc = jnp.where(lane_lo, _dot_nt(wt_ref[1, 0], vtc), _dot_nt(wt_ref[1, 1], vtc))
    vct = vc.T.astype(BF16)
    blk = lax.broadcasted_iota(jnp.int32, (LANE, nrow), 0)
    col = lax.broadcasted_iota(jnp.int32, (LANE, nrow), 1)
    t = past + col % lq
    slope_c = _per_head(col // lq, _slope)
    ok = ((blk + 1) * NSA_BLOCK - 1 <= t) & (blk < nb_complete)
    dist = t.astype(F32) - (blk.astype(F32) * NSA_BLOCK + (NSA_BLOCK - 1) / 2.0)
    s = jnp.where(ok, _dot_nt(kc, qb) - slope_c * dist, NEG)
    m = jnp.max(s, axis=0, keepdims=True)
    p = jnp.exp(s - m) * ok.astype(F32)
    p = p / jnp.maximum(jnp.sum(p, axis=0, keepdims=True), 1e-30)
    o_c = _dotf(vct, p.astype(BF16)).T
    yield True
    ri = lax.broadcasted_iota(jnp.int32, (nrow, nrow), 0)
    ci = lax.broadcasted_iota(jnp.int32, (nrow, nrow), 1)
    gsum = ((ri // (NSA_GROUP * lq) == ci // (NSA_GROUP * lq)) & (ri % lq == ci % lq)).astype(F32)
    imp = _dot_hi(p, gsum)
    cur = t // NSA_BLOCK
    forced = (blk == 0) | (blk == cur) | (blk == cur - 1)
    score = _free_scores(imp, forced, blk, cur)
    rowi = lax.broadcasted_iota(jnp.int32, (nrow, LANE), 0)
    lanei = lax.broadcasted_iota(jnp.int32, (nrow, LANE), 1)
    slope_r = _per_head(rowi[:, 0:1] // lq, _slope)
    tok_r = rowi % lq
    new_ok = (lanei <= tok_r) & (lanei < lq)

    wlen = win_ref.shape[1]
    wi = lax.broadcasted_iota(jnp.int32, (nrow, wlen), 1)
    tok_w = lax.broadcasted_iota(jnp.int32, (nrow, wlen), 0) % lq
    dist_w = wlen + tok_w - wi
    s_w = _dotf(qb, win_ref[0:LANE, :].astype(BF16))
    s_w = jnp.where(dist_w < WINDOW, s_w - slope_r * dist_w.astype(F32), NEG)
    kwn = _pad_rows(kvn[:, 4 * LANE:5 * LANE], LANE).astype(BF16)
    s_wn = jnp.where(new_ok, _dot_nt(qb, kwn) - slope_r * (tok_r - lanei).astype(F32), NEG)
    m = jnp.maximum(jnp.max(s_w, axis=1, keepdims=True), jnp.max(s_wn, axis=1, keepdims=True))
    p_w = jnp.exp(s_w - m)
    p_wn = jnp.exp(s_wn - m)
    vwn = _pad_rows(kvn[:, 5 * LANE:6 * LANE], LANE).astype(BF16)
    o_w = (_dot_nt(p_w.astype(BF16), win_ref[LANE:2 * LANE, :].astype(BF16)) + _dotf(p_wn.astype(BF16), vwn)) / (
        jnp.sum(p_w, axis=1, keepdims=True) + jnp.sum(p_wn, axis=1, keepdims=True))

    win = win_ref[...]
    new_t = _pad_rows(kvn[:, 4 * LANE:6 * LANE], LANE).T
    new_t = pltpu.roll(new_t, LANE - lq, axis=1)
    tail = jnp.concatenate([jnp.zeros((2 * LANE, wlen - LANE), F32), new_t], axis=1)
    lane_w = lax.broadcasted_iota(jnp.int32, win.shape, 1)
    nwin_ref[...] = jnp.where(lane_w < wlen - lq, pltpu.roll(win, wlen - lq, axis=1), tail)
    yield True

    picked = forced
    for i in range(max(NSA_FREE_ROUNDS, npages)):
        if i < npages:
            stage_selected_page(i)
        if i < NSA_FREE_ROUNDS:
            picked, score = _topk_round(picked, score, blk, LANE)
        yield True
    sel = (picked & (blk <= cur)).astype(F32).T

    bias = jnp.where(sel > 0.5, (NSA_BLOCK * slope_r) * lanei.astype(F32), NEG)
    lhs = jnp.concatenate([bias.astype(BF16), qb], axis=1)
    jrow = (lax.broadcasted_iota(jnp.int32, (1, nkeys), 1) % NSA_BLOCK).astype(F32)
    s_p = _dotf(lhs, rt_s[...]) + slope_r * jrow
    bias_new = jnp.sum(jnp.where(lanei == new_blk, bias, 0.0), axis=1, keepdims=True)
    jnew = (past % NSA_BLOCK + lanei).astype(F32)
    kn = _pad_rows(kvn[:, 2 * LANE:3 * LANE], LANE).astype(BF16)
    s_n = jnp.where(new_ok, _dot_nt(qb, kn) + bias_new + slope_r * jnew, NEG)
    yield True
    m = jnp.maximum(jnp.max(s_p, axis=1, keepdims=True), jnp.max(s_n, axis=1, keepdims=True))
    p_p = jnp.exp(s_p - m)
    p_n = jnp.exp(s_n - m)
    vn = _pad_rows(kvn[:, 3 * LANE:4 * LANE], LANE).astype(BF16)
    o_s = (_dot_nt(p_p.astype(BF16), vt_s[...]) + _dotf(p_n.astype(BF16), vn)) / (
        jnp.sum(p_p, axis=1, keepdims=True) + jnp.sum(p_n, axis=1, keepdims=True))

    small = sm_ref[...]
    gate = [jax.nn.sigmoid(jnp.concatenate([small[:, 3 * h + c:3 * h + c + 1] for h in range(NSA_HEADS)], axis=0))
            for c in range(3)]
    o = gate[0] * o_c + gate[1] * o_s + gate[2] * o_w
    for h in range(NSA_HEADS):
        o_ref[:, h * LANE:(h + 1) * LANE] = o[h * lq:(h + 1) * lq].astype(o_ref.dtype)


def _pool_weights_t(wck, wcv, past):
    pos = jnp.arange(past)
    onehot = (jnp.arange(LANE)[:, None] == (pos // NSA_BLOCK)[None, :]).astype(F32)
    w = jnp.stack([wck, wcv])[:, :, pos % NSA_BLOCK]
    return (w[:, :, None, :] * onehot[None, None]).astype(BF16)


def _nsa_sample(page_table, qh, kv, small, cache_t, win_t, wt, prev_win, *, layer, lq, past):
    nprev = 0 if prev_win is None else prev_win.shape[0]
    assert nprev == layer
    nb, npages = page_table.shape
    page = cache_t.shape[-1]
    wlen = win_t.shape[-1]
    ns = NSA_SAMPLE_SEQS
    assert nb % ns == 0
    tokmap = lambda b, pt: (b, 0)

    def page_spec(j, i):
        return pl.BlockSpec((None, None, 4 * LANE, page), lambda b, pt: (layer, pt[ns * b + j, i], 0, 0))

    grid_spec = pltpu.PrefetchScalarGridSpec(
        num_scalar_prefetch=1,
        grid=(nb // ns,),
        in_specs=[pl.BlockSpec((ns * lq, 1024), tokmap), pl.BlockSpec((ns * lq, 768), tokmap),
                  pl.BlockSpec((ns * lq, LANE), tokmap)]
        + [page_spec(j, i) for j in range(ns) for i in range(npages)]
        + [pl.BlockSpec((None, ns, 2 * LANE, wlen), lambda b, pt: (layer, b, 0, 0)),
           pl.BlockSpec(wt.shape, lambda b, pt: (0, 0, 0, 0))]
        + ([pl.BlockSpec((nprev, ns, 2 * LANE, wlen), lambda b, pt: (0, b, 0, 0))] if nprev else []),
        out_specs=[pl.BlockSpec((ns * lq, 1024), tokmap),
                   pl.BlockSpec((nprev + 1, ns, 2 * LANE, wlen), lambda b, pt: (0, b, 0, 0))],
        scratch_shapes=[pltpu.VMEM((ns, LANE, past), BF16), pltpu.VMEM((ns, LANE, past), BF16),
                        pltpu.VMEM((ns, 2 * LANE, past), BF16), pltpu.VMEM((ns, LANE, past), BF16)])
    return pl.pallas_call(
        functools.partial(_nsa_sample_kernel, past=past, lq=lq, npages=npages, page=page, nprev=nprev),
        grid_spec=grid_spec,
        out_shape=[jax.ShapeDtypeStruct((nb * lq, 1024), BF16),
                   jax.ShapeDtypeStruct((nprev + 1, nb, 2 * LANE, wlen), F32)],
        compiler_params=_cparams(("arbitrary",)),
        name="nsa_sample",
    )(page_table, qh, kv, small, *([cache_t] * (ns * npages)), win_t, wt, *([prev_win] if nprev else []))


GDN_ROWS = 64


def _softplus(x):
    return jnp.maximum(x, 0.0) + jnp.log(1.0 + jnp.exp(-jnp.abs(x)))


def _gdn_kernel(x_ref, hist_ref, sm_ref, z_ref, cw_ref, alog_ref, dtb_ref, nw_ref, s0_ref,
                o_ref, s_ref, *, seg, zero_first_hist):
    c = pl.program_id(1)
    rows = GDN_ROWS
    nseg = rows // seg

    @pl.when(c == 0)
    def _():
        s_ref[...] = s0_ref[...]

    x = x_ref[...]
    hist = hist_ref[...]
    if zero_first_hist:
        hist = jnp.where(c == 0, 0.0, hist)
    tpos = lax.broadcasted_iota(jnp.int32, x.shape, 0) % seg
    cw = cw_ref[...]
    y = x * cw[DN_CONV_W - 1:DN_CONV_W]
    for sh in range(1, DN_CONV_W):
        xs = jnp.where(tpos >= sh, pltpu.roll(x, sh, axis=0), pltpu.roll(hist, sh, axis=0))
        y = y + xs * cw[DN_CONV_W - 1 - sh:DN_CONV_W - sh]
    y = y * jax.nn.sigmoid(y)

    ri = lax.broadcasted_iota(jnp.int32, (rows, rows), 0)
    ci = lax.broadcasted_iota(jnp.int32, (rows, rows), 1)
    same = (ri // seg) == (ci // seg)
    tri = same & (ci <= ri)
    strict = same & (ci < ri)
    trif = tri.astype(F32)
    small = sm_ref[...]
    z = z_ref[...]
    heads = range(DN_HEADS)
    segs = range(nseg)
    hs = lambda h: slice(h * HEAD_DIM, (h + 1) * HEAD_DIM)
    yqs = [y[:, hs(h)] for h in heads]
    yks = [y[:, DN_WIDTH + h * HEAD_DIM:DN_WIDTH + (h + 1) * HEAD_DIM] for h in heads]
    vs = [y[:, 2 * DN_WIDTH + h * HEAD_DIM:2 * DN_WIDTH + (h + 1) * HEAD_DIM] for h in heads]
    qs = [a * lax.rsqrt(jnp.sum(a * a, axis=-1, keepdims=True) + 1e-6) * HEAD_DIM ** -0.5 for a in yqs]
    ks = [a * lax.rsqrt(jnp.sum(a * a, axis=-1, keepdims=True) + 1e-6) for a in yks]
    betas = [jax.nn.sigmoid(small[:, 28 + h:29 + h]) for h in heads]
    gs = [-jnp.exp(alog_ref[:, h:h + 1]) * _softplus(small[:, 24 + h:25 + h] + dtb_ref[:, h:h + 1]) for h in heads]
    trifs = _split(trif)
    gcbs = [_dot3s(trifs, _split(jnp.broadcast_to(g, (rows, rows)))) for g in gs]
    decays = [jnp.where(tri, jnp.exp(jnp.where(tri, gcb - gcb.T, 0.0)), 0.0) for gcb in gcbs]
    kbs = [ks[h] * betas[h] for h in heads]
    amats = [jnp.where(strict, _dot3_nt(kbs[h], ks[h]) * decays[h], 0.0) for h in heads]
    aqks = [jnp.where(tri, _dot3_nt(qs[h], ks[h]) * decays[h], 0.0) for h in heads]
    eye = (ri == ci).astype(F32)
    invs = [eye - a for a in amats]
    pws = amats
    span = 2
    while span < seg:
        pwsp = [_split(p) for p in pws]
        pws = [_dot3s(p, p) for p in pwsp]
        invs = [invs[h] + _dot3(invs[h], pws[h]) for h in heads]
        span *= 2
    egcs = [jnp.exp(gcb[:, :HEAD_DIM]) for gcb in gcbs]
    sols = [_dot3(invs[h], jnp.concatenate([vs[h] * betas[h], kbs[h] * egcs[h]], axis=1)) for h in heads]
    us = [s[:, :HEAD_DIM] for s in sols]
    ws = [s[:, HEAD_DIM:] for s in sols]
    qgs = [qs[h] * egcs[h] for h in heads]
    rsl = lambda s: slice(s * seg, (s + 1) * seg)
    sts = [[s_ref[s, h] for s in segs] for h in heads]
    stsp = [[_split(st) for st in row] for row in sts]
    vns = [jnp.concatenate([us[h][rsl(s)] - _dot3s(_split(ws[h][rsl(s)]), stsp[h][s]) for s in segs], axis=0)
           for h in heads]
    oqs = [jnp.concatenate([_dot3s(_split(qgs[h][rsl(s)]), stsp[h][s]) for s in segs], axis=0) for h in heads]
    vnsp = [_split(v) for v in vns]
    os_ = [oqs[h] + _dot3s(_split(aqks[h]), vnsp[h]) for h in heads]
    glasts = [[gcbs[h][(s + 1) * seg - 1:(s + 1) * seg, :HEAD_DIM] for s in segs] for h in heads]
    gl_rows = [jnp.concatenate([jnp.broadcast_to(gl, (seg, HEAD_DIM)) for gl in glasts[h]], axis=0) for h in heads]
    kdts = [(ks[h] * jnp.exp(gl_rows[h] - gcbs[h][:, :HEAD_DIM])).T for h in heads]
    for h in heads:
        for s in segs:
            kds = kdts[h] if nseg == 1 else jnp.where((ci // seg) == s, kdts[h], 0.0)
            s_ref[s, h] = sts[h][s] * jnp.exp(glasts[h][s]) + _dot3s(_split(kds), vnsp[h])
    outs = []
    for h in heads:
        o = os_[h] * lax.rsqrt(jnp.mean(os_[h] * os_[h], axis=-1, keepdims=True) + NORM_EPS) * nw_ref[...]
        zh = z[:, hs(h)]
        outs.append(o * (zh * jax.nn.sigmoid(zh)))
    o_ref[...] = jnp.concatenate(outs, axis=1).astype(o_ref.dtype)


def _gdn(dnqkv, hist, small, z, cw, alog, dtb, nw, s0, *, seg, nb, nchunk, row0, hist_map, zero_first_hist):
    nseg = GDN_ROWS // seg
    rmap = lambda b, c: (row0 + b * nchunk + c, 0)
    const = lambda b, c: (0, 0)
    smap = lambda b, c: (b, 0, 0, 0)
    sblock = (nseg, DN_HEADS, HEAD_DIM, HEAD_DIM)
    return pl.pallas_call(
        functools.partial(_gdn_kernel, seg=seg, zero_first_hist=zero_first_hist),
        grid=(nb, nchunk),
        in_specs=[pl.BlockSpec((GDN_ROWS, 768), rmap), pl.BlockSpec((GDN_ROWS, 768), hist_map),
                  pl.BlockSpec((GDN_ROWS, LANE), rmap), pl.BlockSpec((GDN_ROWS, 256), rmap),
                  pl.BlockSpec(cw.shape, const), pl.BlockSpec(alog.shape, const),
                  pl.BlockSpec(dtb.shape, const), pl.BlockSpec(nw.shape, const),
                  pl.BlockSpec(sblock, smap)],
        out_specs=[pl.BlockSpec((GDN_ROWS, 256), lambda b, c: (b * nchunk + c, 0)),
                   pl.BlockSpec(sblock, smap)],
        out_shape=[jax.ShapeDtypeStruct((nb * nchunk * GDN_ROWS, 256), BF16),
                   jax.ShapeDtypeStruct(s0.shape, F32)],
        compiler_params=_cparams(("parallel", "arbitrary")),
        name="gdn",
    )(dnqkv, hist, small, z, cw, alog, dtb, nw, s0)


def _split(a):
    hi = a.astype(BF16)
    return hi, (a - hi.astype(F32)).astype(BF16)


def _dot3s(a, b):
    return _dotf(a[0], b[0]) + (_dotf(a[0], b[1]) + _dotf(a[1], b[0]))


def _dot3(a, b):
    return _dot3s(_split(a), _split(b))


def _dot3_nt(a, b):
    a, b = _split(a), _split(b)
    return _dot_nt(a[0], b[0]) + (_dot_nt(a[0], b[1]) + _dot_nt(a[1], b[0]))


GDN_PREP_CHUNKS = 4
GDN_PROMPT_CHUNK = 64


def _gdn_prep_kernel(x_ref, hist_ref, sm_ref, cw_ref, alog_ref, dtb_ref,
                     u_ref, w_ref, qg_ref, kd_ref, aqk_ref, egl_ref):
    c = pl.program_id(1)
    x = x_ref[...]
    rows = x.shape[0]
    hist = jnp.where(c == 0, 0.0, hist_ref[...])
    row8 = lax.broadcasted_iota(jnp.int32, hist.shape, 0)
    cw = cw_ref[...]
    y = x * cw[DN_CONV_W - 1:DN_CONV_W]
    for sh in range(1, DN_CONV_W):
        xs = pltpu.roll(x, sh, axis=0)
        top = jnp.where(row8 < sh, pltpu.roll(hist, sh, axis=0), xs[0:8])
        xs = jnp.concatenate([top, xs[8:]], axis=0)
        y = y + xs * cw[DN_CONV_W - 1 - sh:DN_CONV_W - sh]
    y = y * jax.nn.sigmoid(y)

    C = GDN_PROMPT_CHUNK
    ri = lax.broadcasted_iota(jnp.int32, (C, C), 0)
    ci = lax.broadcasted_iota(jnp.int32, (C, C), 1)
    tri = ci <= ri
    strict = ci < ri
    trif = tri.astype(F32)
    small = sm_ref[...]
    nch = rows // C
    units = [(ch, h) for ch in range(nch) for h in range(DN_HEADS)]
    each = lambda f, *ls: [f(*a) for a in zip(*ls)]

    def qkv_of(ch, h):
        rs = slice(ch * C, (ch + 1) * C)
        yq = y[rs, h * HEAD_DIM:(h + 1) * HEAD_DIM]
        yk = y[rs, DN_WIDTH + h * HEAD_DIM:DN_WIDTH + (h + 1) * HEAD_DIM]
        v = y[rs, 2 * DN_WIDTH + h * HEAD_DIM:2 * DN_WIDTH + (h + 1) * HEAD_DIM]
        q = yq * lax.rsqrt(jnp.sum(yq * yq, axis=-1, keepdims=True) + 1e-6) * HEAD_DIM ** -0.5
        k = yk * lax.rsqrt(jnp.sum(yk * yk, axis=-1, keepdims=True) + 1e-6)
        beta = jax.nn.sigmoid(small[rs, 28 + h:29 + h])
        g = -jnp.exp(alog_ref[:, h:h + 1]) * _softplus(small[rs, 24 + h:25 + h] + dtb_ref[:, h:h + 1])
        return q, k, v, beta, g

    qs_, ks_, vs_, betas, gs = zip(*[qkv_of(ch, h) for ch, h in units])
    trifs = _split(trif)
    gcbs = [_dot3s(trifs, _split(jnp.broadcast_to(g, (C, C)))) for g in gs]
    kbs = each(lambda k, b: k * b, ks_, betas)
    ksp = [_split(k) for k in ks_]
    kbsp = [_split(kb) for kb in kbs]
    qsp = [_split(q) for q in qs_]
    nt3 = lambda a, b: _dot_nt(a[0], b[0]) + (_dot_nt(a[0], b[1]) + _dot_nt(a[1], b[0]))
    kks = each(nt3, kbsp, ksp)
    qks = each(nt3, qsp, ksp)
    decays = [jnp.where(tri, jnp.exp(jnp.where(tri, gcb - gcb.T, 0.0)), 0.0) for gcb in gcbs]
    amats = each(lambda kk, d: jnp.where(strict, kk * d, 0.0), kks, decays)
    eye = (ri == ci).astype(F32)
    blk_mask = lambda n: (ri // n) == (ci // n)
    base = 8
    aprev = [jnp.where(blk_mask(base), a, 0.0) for a in amats]
    invs = [eye - a for a in aprev]
    pwsp = [_split(a) for a in aprev]
    span = 2
    while span < base:
        pwsp = [_split(_dot3s(p, p)) for p in pwsp]
        invs = each(lambda i, p: i + _dot3s(_split(i), p), invs, pwsp)
        span *= 2
    size = base
    while size < C:
        size *= 2
        acur = amats if size == C else [jnp.where(blk_mask(size), a, 0.0) for a in amats]
        invsp = [_split(i) for i in invs]
        cross = each(lambda isp, ac, ap: _dot3s(isp, _split(ac - ap)), invsp, acur, aprev)
        invs = each(lambda i, isp, cr: i - _dot3s(_split(cr), isp), invs, invsp, cross)
        aprev = acur
    gcs = [gcb[:, :HEAD_DIM] for gcb in gcbs]
    egcs = [jnp.exp(gc) for gc in gcs]
    rhss = each(lambda v, b, kb, e: jnp.concatenate([v * b, kb * e], axis=1), vs_, betas, kbs, egcs)
    sols = each(_dot3, invs, rhss)
    glasts = [gc[C - 1:C, :] for gc in gcs]
    qgs = each(lambda q, e: q * e, qs_, egcs)
    kds = each(lambda k, gl, gc: k * jnp.exp(gl - gc), ks_, glasts, gcs)
    aqks = each(lambda qk, d: jnp.where(tri, qk * d, 0.0), qks, decays)
    for ch in range(nch):
        rs = slice(ch * C, (ch + 1) * C)
        un = range(ch * DN_HEADS, (ch + 1) * DN_HEADS)
        u_ref[rs, :] = jnp.concatenate([sols[i][:, :HEAD_DIM] for i in un], axis=1)
        w_ref[rs, :] = jnp.concatenate([sols[i][:, HEAD_DIM:] for i in un], axis=1)
        qg_ref[rs, :] = jnp.concatenate([qgs[i] for i in un], axis=1)
        kd_ref[rs, :] = jnp.concatenate([kds[i] for i in un], axis=1)
        aqk_ref[rs, :] = jnp.concatenate([aqks[i] for i in un], axis=1)
        egl_ref[ch * 8:(ch + 1) * 8, :] = jnp.concatenate(
            [jnp.broadcast_to(jnp.exp(glasts[i]), (8, HEAD_DIM)) for i in un], axis=1)


GDN_SCAN_CHUNKS = 4


def _gdn_scan_kernel(u_ref, w_ref, qg_ref, kd_ref, aqk_ref, egl_ref, z_ref, nw_ref, s0_ref, o_ref, s_ref, *, C):
    c = pl.program_id(0)

    @pl.when(c == 0)
    def _():
        s_ref[...] = s0_ref[...]

    nb = u_ref.shape[0]
    nsub = u_ref.shape[1] // C
    units = [(b, slice(h * HEAD_DIM, (h + 1) * HEAD_DIM), h) for b in range(nb) for h in range(DN_HEADS)]
    subs = [slice(j * C, (j + 1) * C) for j in range(nsub)]
    kdts = [[_split(kd_ref[b, rs, sl].T) for b, sl, _ in units] for rs in subs]
    wsp = [[_split(w_ref[b, rs, sl]) for b, sl, _ in units] for rs in subs]
    qgsp = [[_split(qg_ref[b, rs, sl]) for b, sl, _ in units] for rs in subs]
    aqsp = [[_split(aqk_ref[b, rs, h * C:(h + 1) * C]) for b, _, h in units] for rs in subs]
    sts = [s_ref[b, h] for b, _, h in units]
    for j, rs in enumerate(subs):
        stsp = [_split(s) for s in sts]
        vnews = [u_ref[b, rs, sl] - _dot3s(wp, st) for (b, sl, _), wp, st in zip(units, wsp[j], stsp)]
        oqs = [_dot3s(qp, st) for qp, st in zip(qgsp[j], stsp)]
        vsp = [_split(v) for v in vnews]
        os_ = [oq + _dot3s(ap, vs) for oq, ap, vs in zip(oqs, aqsp[j], vsp)]
        sts = [st * egl_ref[b, 8 * j:8 * j + 1, sl] + _dot3s(kdt, vs)
               for (b, sl, _), st, kdt, vs in zip(units, sts, kdts[j], vsp)]
        outs = []
        for (b, sl, _), o in zip(units, os_):
            o = o * lax.rsqrt(jnp.mean(o * o, axis=-1, keepdims=True) + NORM_EPS) * nw_ref[...]
            zh = z_ref[b, rs, sl]
            outs.append(o * (zh * jax.nn.sigmoid(zh)))
        for b in range(nb):
            o_ref[b, rs] = jnp.concatenate(outs[b * DN_HEADS:(b + 1) * DN_HEADS], axis=1).astype(o_ref.dtype)
    for (b, _, h), st in zip(units, sts):
        s_ref[b, h] = st


def _gdn_prompt(dnqkv, small, z, cw, alog, dtb, nw, s0, *, nb, lq):
    cb = GDN_PREP_CHUNKS
    C = GDN_PROMPT_CHUNK
    rows = cb * C
    nstep = lq // rows
    nchunk = lq // C
    rmap = lambda b, c: (b * nstep + c, 0)
    const = lambda b, c: (0, 0)
    wide = jax.ShapeDtypeStruct((nb * lq, DN_WIDTH), F32)
    aqk_shape = jax.ShapeDtypeStruct((nb * lq, DN_HEADS * C), F32)
    u, w, qg, kd, aqk, egl = pl.pallas_call(
        _gdn_prep_kernel,
        grid=(nb, nstep),
        in_specs=[pl.BlockSpec((rows, 768), rmap),
                  pl.BlockSpec((8, 768), lambda b, c: (jnp.maximum((b * nstep + c) * (rows // 8) - 1, 0), 0)),
                  pl.BlockSpec((rows, LANE), rmap),
                  pl.BlockSpec(cw.shape, const), pl.BlockSpec(alog.shape, const), pl.BlockSpec(dtb.shape, const)],
        out_specs=[pl.BlockSpec((rows, DN_WIDTH), rmap)] * 4 + [pl.BlockSpec((rows, DN_HEADS * C), rmap),
                                                                pl.BlockSpec((cb * 8, DN_WIDTH), rmap)],
        out_shape=[wide] * 4 + [aqk_shape, jax.ShapeDtypeStruct((nb * nchunk * 8, DN_WIDTH), F32)],
        compiler_params=_cparams(("parallel", "parallel")),
        name="gdn_prep",
    )(dnqkv, dnqkv, small, cw, alog, dtb)
    r3 = lambda a: a.reshape(nb, -1, a.shape[-1])
    cmap = lambda c: (0, c, 0)
    full = lambda a: pl.BlockSpec(a.shape, lambda c: (0,) * a.ndim)
    ns = GDN_SCAN_CHUNKS
    o, s = pl.pallas_call(
        functools.partial(_gdn_scan_kernel, C=C),
        grid=(nchunk // ns,),
        in_specs=[pl.BlockSpec((nb, ns * C, DN_WIDTH), cmap)] * 4
        + [pl.BlockSpec((nb, ns * C, DN_HEADS * C), cmap),
           pl.BlockSpec((nb, ns * 8, DN_WIDTH), cmap), pl.BlockSpec((nb, ns * C, DN_WIDTH), cmap),
           full(nw), full(s0)],
        out_specs=[pl.BlockSpec((nb, ns * C, DN_WIDTH), cmap), full(s0)],
        out_shape=[jax.ShapeDtypeStruct((nb, lq, DN_WIDTH), BF16), jax.ShapeDtypeStruct(s0.shape, F32)],
        compiler_params=_cparams(("arbitrary",)),
        name="gdn_scan",
    )(r3(u), r3(w), r3(qg), r3(kd), r3(aqk), r3(egl), z[:nb * lq].reshape(nb, lq, DN_WIDTH), nw, s0)
    return o.reshape(nb * lq, DN_WIDTH), s


def _s5_discretize(lre_ref, lim_ref, lstep_ref):
    lr = lre_ref[...]
    li = lim_ref[...]
    dt = jnp.exp(lstep_ref[...])
    mag = jnp.exp(lr * dt)
    ar = mag * jnp.cos(li * dt)
    ai = mag * jnp.sin(li * dt)
    den = lr * lr + li * li
    fr = ((ar - 1.0) * lr + ai * li) / den
    fi = (ai * lr - (ar - 1.0) * li) / den
    return ar, ai, fr, fi


def _s5_kernel(*refs, R, S, with_y):
    if with_y:
        (u_ref, h0r_ref, h0i_ref, lre_ref, lim_ref, lstep_ref, bre_ref, bim_ref, cre_ref, cim_ref, d_ref,
         wglu_ref, y_ref, hr_ref, hi_ref, ar_s, ai_s, bbr_s, bbi_s, xr_s, xi_s) = refs
    else:
        (u_ref, h0r_ref, h0i_ref, lre_ref, lim_ref, lstep_ref, bre_ref, bim_ref,
         hr_ref, hi_ref, ar_s, ai_s, bbr_s, bbi_s, xr_s, xi_s) = refs
    i = pl.program_id(0)

    @pl.when(i == 0)
    def _():
        ar, ai, fr, fi = _s5_discretize(lre_ref, lim_ref, lstep_ref)
        ar_s[...] = ar
        ai_s[...] = ai
        bbr_s[...] = (fr * bre_ref[...] - fi * bim_ref[...]).astype(BF16)
        bbi_s[...] = (fr * bim_ref[...] + fi * bre_ref[...]).astype(BF16)
        hr_ref[...] = h0r_ref[...]
        hi_ref[...] = h0i_ref[...]

    u = u_ref[...]
    ub = u.astype(BF16)
    xr_s[...] = _dotf(ub, bbr_s[...])
    xi_s[...] = _dotf(ub, bbi_s[...])
    ar = jnp.broadcast_to(ar_s[...], (R, S5_LANES))
    ai = jnp.broadcast_to(ai_s[...], (R, S5_LANES))

    if not with_y:
        def carry_step(s, h):
            off = pl.multiple_of(s * R, R)
            return (ar * h[0] - ai * h[1] + xr_s[pl.ds(off, R), :], ar * h[1] + ai * h[0] + xi_s[pl.ds(off, R), :])

        hr, hi = lax.fori_loop(0, S, carry_step, (hr_ref[...], hi_ref[...]))
        hr_ref[...] = hr
        hi_ref[...] = hi
        return

    def advance(hr, hi, off):
        nr = ar * hr - ai * hi + xr_s[pl.ds(off, R), :]
        ni = ar * hi + ai * hr + xi_s[pl.ds(off, R), :]
        xr_s[pl.ds(off, R), :] = nr
        xi_s[pl.ds(off, R), :] = ni

    advance(hr_ref[...], hi_ref[...], 0)

    def step(s, carry):
        prev = pl.multiple_of((s - 1) * R, R)
        advance(xr_s[pl.ds(prev, R), :], xi_s[pl.ds(prev, R), :], pl.multiple_of(s * R, R))
        return carry

    lax.fori_loop(1, S, step, 0)
    hr_ref[...] = xr_s[pl.ds((S - 1) * R, R), :]
    hi_ref[...] = xi_s[pl.ds((S - 1) * R, R), :]
    if with_y:
        y = _dotf(xr_s[...].astype(BF16), cre_ref[...]) - _dotf(xi_s[...].astype(BF16), cim_ref[...])
        y = y + d_ref[...] * u
        gl = jax.nn.gelu(y)
        y_ref[...] = (gl * jax.nn.sigmoid(_dotf(gl.astype(BF16), wglu_ref[...]))).astype(y_ref.dtype)


def _s5_scan(u_rows, h0r, h0i, prm, *, R, S, with_y):
    n = u_rows.shape[0]
    rows = R * S
    const = lambda i: (0, 0)
    rmap = lambda i: (i, 0)
    ins = [u_rows, h0r, h0i, prm["lre"], prm["lim"], prm["lstep"], prm["bre"], prm["bim"]]
    if with_y:
        ins += [prm["cre"], prm["cim"], prm["d"], prm["wglu"]]
    in_specs = [pl.BlockSpec((rows, S5_WIDTH), rmap)] + [pl.BlockSpec(a.shape, const) for a in ins[1:]]
    st_spec = pl.BlockSpec((R, S5_LANES), const)
    st_shape = jax.ShapeDtypeStruct((R, S5_LANES), F32)
    out_specs = [st_spec, st_spec]
    out_shape = [st_shape, st_shape]
    if with_y:
        out_specs = [pl.BlockSpec((rows, S5_WIDTH), rmap)] + out_specs
        out_shape = [jax.ShapeDtypeStruct((n, S5_WIDTH), BF16)] + out_shape
    return pl.pallas_call(
        functools.partial(_s5_kernel, R=R, S=S, with_y=with_y),
        grid=(n // rows,),
        in_specs=in_specs,
        out_specs=out_specs,
        out_shape=out_shape,
        scratch_shapes=[pltpu.VMEM((1, S5_LANES), F32), pltpu.VMEM((1, S5_LANES), F32),
                        pltpu.VMEM((S5_WIDTH, S5_LANES), BF16), pltpu.VMEM((S5_WIDTH, S5_LANES), BF16),
                        pltpu.VMEM((rows, S5_LANES), F32), pltpu.VMEM((rows, S5_LANES), F32)],
        compiler_params=_cparams(("arbitrary",)),
        name="s5_scan",
    )(*ins)


def _s5_carry_kernel(er_ref, ei_ref, lre_ref, lim_ref, lstep_ref, ir_ref, ii_ref, fr_ref, fi_ref, *, nseg, nsteps):
    ar, ai, _, _ = _s5_discretize(lre_ref, lim_ref, lstep_ref)
    pr, pi_ = ar, ai
    n = 1
    while n < nsteps:
        pr, pi_ = pr * pr - pi_ * pi_, 2.0 * pr * pi_
        n *= 2
    for b in range(er_ref.shape[0] // nseg):
        cr = jnp.zeros((1, S5_LANES), F32)
        ci = jnp.zeros((1, S5_LANES), F32)
        for s in range(nseg):
            r = b * nseg + s
            ir_ref[r:r + 1, :] = cr
            ii_ref[r:r + 1, :] = ci
            er = er_ref[r:r + 1, :]
            ei = ei_ref[r:r + 1, :]
            cr, ci = pr * cr - pi_ * ci + er, pr * ci + pi_ * cr + ei
        fr_ref[b:b + 1, :] = cr
        fi_ref[b:b + 1, :] = ci


def _s5_carry(er, ei, prm, *, nseg, nsteps):
    assert nsteps & (nsteps - 1) == 0
    nb = er.shape[0] // nseg
    full = lambda a: pl.BlockSpec(a.shape, lambda: (0,) * a.ndim)
    ins = [er, ei, prm["lre"], prm["lim"], prm["lstep"]]
    outs = [jax.ShapeDtypeStruct(er.shape, F32)] * 2 + [jax.ShapeDtypeStruct((nb, S5_LANES), F32)] * 2
    return pl.pallas_call(
        functools.partial(_s5_carry_kernel, nseg=nseg, nsteps=nsteps),
        in_specs=[full(a) for a in ins],
        out_specs=[full(o) for o in outs],
        out_shape=outs,
        name="s5_carry",
    )(*ins)


S5_SEGS = 8


def _s5_prep(lre, lim, lstep, bre, bim, cre, cim, d, wglu):
    eye = jnp.eye(S5_GROUPS, dtype=F32)
    bexp = lambda b: jnp.einsum("gpc,gh->gchp", b, eye).reshape(S5_WIDTH, S5_LANES)
    cexp = lambda c: jnp.einsum("gcp,gh->gphc", c, eye).reshape(S5_LANES, S5_WIDTH).astype(BF16)
    return {"lre": lre.reshape(1, S5_LANES), "lim": lim.reshape(1, S5_LANES),
            "lstep": jnp.repeat(lstep, S5_STATE).reshape(1, S5_LANES),
            "bre": bexp(bre), "bim": bexp(bim), "cre": cexp(cre), "cim": cexp(cim),
            "d": d.reshape(1, S5_WIDTH), "wglu": wglu.astype(BF16)}


def _s5_prompt(u, prm, nb, lq, steps_per_tile=32):
    nsteps = lq // S5_SEGS
    R = nb * S5_SEGS
    u_rows = u.reshape(nb, S5_SEGS, nsteps, S5_WIDTH).transpose(2, 0, 1, 3).reshape(nsteps * R, S5_WIDTH)
    zero = jnp.zeros((R, S5_LANES), F32)
    er, ei = _s5_scan(u_rows, zero, zero, prm, R=R, S=steps_per_tile, with_y=False)
    ir, ii, fr, fi = _s5_carry(er, ei, prm, nseg=S5_SEGS, nsteps=nsteps)
    y_rows, _, _ = _s5_scan(u_rows, ir, ii, prm, R=R, S=steps_per_tile, with_y=True)
    y = y_rows.reshape(nsteps, nb, S5_SEGS, S5_WIDTH).transpose(1, 2, 0, 3).reshape(nb * lq, S5_WIDTH)
    return y, fr, fi


def _s5_sample(u, h0r, h0i, prm, nb, lq, steps_per_tile=4):
    u_rows = u.reshape(nb, lq, S5_WIDTH).transpose(1, 0, 2).reshape(lq * nb, S5_WIDTH)
    y_rows, fr, fi = _s5_scan(u_rows, h0r, h0i, prm, R=nb, S=steps_per_tile, with_y=True)
    y = y_rows.reshape(lq, nb, S5_WIDTH).transpose(1, 0, 2).reshape(nb * lq, S5_WIDTH)
    return y, fr, fi


def _sample_conv_hist(buf):
    nb = buf.shape[0]
    blk = buf.reshape(nb // 8, 8, DN_CONV_W - 1, 768)
    blk = jnp.roll(blk, -1, axis=1)
    blk = jnp.pad(blk, ((0, 0), (0, 0), (8 - (DN_CONV_W - 1), 0), (0, 0)))
    return blk.reshape(nb * 8, 768)


def _rms(x, g):
    return x * lax.rsqrt(jnp.mean(x * x, axis=-1, keepdims=True) + NORM_EPS) * g


def _mlp_kernel(x_ref, on_ref, od_ref, os_ref, wn_ref, wd_ref, ws_ref, ln2_ref, up_ref, dn_ref, lnf_ref,
                o_ref, h2_s, *, final_norm):
    j = pl.program_id(1)

    @pl.when(j == 0)
    def _():
        x1 = (x_ref[...] + _dotf(on_ref[...], wn_ref[...]) + _dotf(od_ref[...], wd_ref[...])
              + _dotf(os_ref[...], ws_ref[...]))
        o_ref[...] = x1
        h2_s[...] = _rms(x1, ln2_ref[...]).astype(BF16)

    a = jnp.maximum(_dotf(h2_s[...], up_ref[...]), 0.0)
    o_ref[...] += _dotf((a * a).astype(BF16), dn_ref[...])

    if final_norm:
        @pl.when(j == pl.num_programs(1) - 1)
        def _():
            o_ref[...] = _rms(o_ref[...], lnf_ref[...])


def _mlp(x, o_nsa, o_dn, o_s5, wn, wd, ws, ln2, up, dn, lnf, *, final_norm, tm=512, tf=1024):
    n = x.shape[0]
    row = lambda i, j: (i, 0)
    const = lambda i, j: (0, 0)
    return pl.pallas_call(
        functools.partial(_mlp_kernel, final_norm=final_norm),
        grid=(n // tm, D_FF // tf),
        in_specs=[pl.BlockSpec((tm, D_MODEL), row), pl.BlockSpec((tm, 1024), row),
                  pl.BlockSpec((tm, DN_WIDTH), row), pl.BlockSpec((tm, S5_WIDTH), row),
                  pl.BlockSpec(wn.shape, const), pl.BlockSpec(wd.shape, const), pl.BlockSpec(ws.shape, const),
                  pl.BlockSpec((1, D_MODEL), const),
                  pl.BlockSpec((D_MODEL, tf), lambda i, j: (0, j)),
                  pl.BlockSpec((tf, D_MODEL), lambda i, j: (j, 0)),
                  pl.BlockSpec((1, D_MODEL), const)],
        out_specs=pl.BlockSpec((tm, D_MODEL), row),
        out_shape=jax.ShapeDtypeStruct((n, D_MODEL), F32),
        scratch_shapes=[pltpu.VMEM((tm, D_MODEL), BF16)],
        compiler_params=_cparams(("parallel", "arbitrary")),
        name="out_mlp",
    )(x, o_nsa, o_dn, o_s5, wn, wd, ws, ln2, up, dn, lnf)


_NSA_W = NSA_HEADS * HEAD_DIM
_KV_W = 6 * NSA_KV_HEADS * HEAD_DIM
_GATE_W = NSA_HEADS * 3
_OFF_KV = _NSA_W
_OFF_GATE = _OFF_KV + _KV_W
_OFF_DN = _OFF_GATE + _GATE_W
_OFF_A = _OFF_DN + 3 * DN_WIDTH
_OFF_B = _OFF_A + DN_HEADS
_OFF_Z = _OFF_B + DN_HEADS
_OFF_U = _OFF_Z + DN_WIDTH


def _pad_heads(w, axis):
    w = jnp.moveaxis(w, axis, 0).reshape((NSA_HEADS, HEAD_DIM) + w.shape[:axis] + w.shape[axis + 1:])
    out = jnp.zeros((NSA_HEADS, LANE) + w.shape[2:], w.dtype)
    for h in range(NSA_HEADS):
        g = h // NSA_GROUP
        out = out.at[h, g * HEAD_DIM:(g + 1) * HEAD_DIM].set(w[h])
    out = out.reshape((NSA_HEADS * LANE,) + w.shape[2:])
    return jnp.moveaxis(out, 0, axis)


def _layer_weights(w_in, w_out, wck, wcv):
    wq = _pad_heads(w_in[:, :_NSA_W], 1).astype(BF16)
    wkv = w_in[:, _OFF_KV:_OFF_GATE].astype(BF16)
    wdn = w_in[:, _OFF_DN:_OFF_A].astype(BF16)
    wz = w_in[:, _OFF_Z:_OFF_U].astype(BF16)
    wu = w_in[:, _OFF_U:].astype(BF16)
    wsm = jnp.zeros((D_MODEL, LANE), F32)
    wsm = wsm.at[:, :_GATE_W].set(w_in[:, _OFF_GATE:_OFF_DN])
    wsm = wsm.at[:, _GATE_W:_GATE_W + 2 * DN_HEADS].set(w_in[:, _OFF_A:_OFF_Z]).astype(BF16)
    wexp = jnp.concatenate([jnp.repeat(wck.T, HEAD_DIM, axis=1), jnp.repeat(wcv.T, HEAD_DIM, axis=1)], axis=1)
    won = _pad_heads(w_out[:_NSA_W], 0).astype(BF16)
    wod = w_out[_NSA_W:_NSA_W + DN_WIDTH].astype(BF16)
    wos = w_out[_NSA_W + DN_WIDTH:].astype(BF16)
    return wq, wkv, wdn, wz, wu, wsm, wexp, won, wod, wos


def kernel(x_prompt, x_sample, cache_nsa_kv, cache_win_kv, state_dn_conv, state_dn, state_s5_re, state_s5_im,
           page_table, ln1, ln2, ln_f, w_in, w_out, nsa_wck, nsa_wcv, dn_conv_w, dn_a_log, dn_dt_bias,
           dn_norm_w, s5_lambda_re, s5_lambda_im, s5_log_step, s5_b_re, s5_b_im, s5_c_re, s5_c_im, s5_d,
           s5_w_glu, mlp_up, mlp_down):
    nbp, lp, _ = x_prompt.shape
    nbs, ls, _ = x_sample.shape
    depth = w_in.shape[0]
    rows_p = nbp * lp
    rows_s = nbs * ls
    n_phys, page = cache_nsa_kv.shape[1], cache_nsa_kv.shape[2]
    past = page_table.shape[1] * page
    wlen = cache_win_kv.shape[2]
    G, dh = NSA_KV_HEADS, HEAD_DIM
    assert lp % (S5_SEGS * 32) == 0 and nbs % 8 == 0 and ls == 8 and wlen == WINDOW and lp >= WINDOW

    xp = x_prompt.reshape(rows_p, D_MODEL)
    xs = x_sample.reshape(rows_s, D_MODEL)
    cache_t = cache_nsa_kv.transpose(0, 1, 3, 4, 5, 2).reshape(depth, n_phys, 4 * G * dh, page)
    win_t = cache_win_kv.transpose(0, 1, 3, 4, 5, 2).reshape(depth, nbs, 2 * G * dh, wlen)
    outs_p, outs_s = [], []
    kvt_main = nwin_t = None
    for l in range(depth):
        wq, wkv, wdn, wz, wu, wsm, wexp, won, wod, wos = _layer_weights(w_in[l], w_out[l], nsa_wck[l], nsa_wcv[l])
        pw = (ln1[l][None], wq, wkv, wdn, wz, wu, wsm, wexp)
        qh_p, kv_p, dnx_p, z_p, u_p, small_p, pool_p, kvt_main, kvt_win, kaug, vt = _proj(
            xp, *pw, nseq=nbp, prev_kvt=kvt_main, sel_tile=NSA_TQ)
        qh_s, kv_s, dnx_s, z_s, u_s, small_s, _, _, _ = _proj(xs, *pw, nseq=1)

        oc, sel, anyblk = _cmp_prompt(qh_p, pool_p, nbp, lp, NSA_TQ)
        osel = _sel_prompt(qh_p, sel, anyblk, kaug, vt, nbp, lp, NSA_TQ)
        on_p = _win_prompt(qh_p, kv_p, kvt_win, oc, osel, small_p, nbp, lp)
        wt = _pool_weights_t(nsa_wck[l], nsa_wcv[l], past)
        on_s, nwin_t = _nsa_sample(page_table, qh_s, kv_s, small_s, cache_t, win_t, wt, nwin_t,
                                   layer=l, lq=ls, past=past)

        gdn_w = (dn_conv_w[l].T, dn_a_log[l][None], dn_dt_bias[l][None], dn_norm_w[l][None])
        od_p, dn_p = _gdn_prompt(dnx_p, small_p, z_p, *gdn_w, jnp.zeros((nbp, DN_HEADS, dh, dh), F32), nb=nbp, lq=lp)
        od_s, dn_s = _gdn(dnx_s, _sample_conv_hist(state_dn_conv[l]), small_s, z_s, *gdn_w, state_dn[l],
                          seg=ls, nb=rows_s // GDN_ROWS, nchunk=1, row0=0,
                          hist_map=lambda b, c: (b, 0), zero_first_hist=False)

        prm = _s5_prep(s5_lambda_re[l], s5_lambda_im[l], s5_log_step[l], s5_b_re[l], s5_b_im[l],
                       s5_c_re[l], s5_c_im[l], s5_d[l], s5_w_glu[l])
        os_p, s5r_p, s5i_p = _s5_prompt(u_p, prm, nbp, lp)
        os_s, s5r_s, s5i_s = _s5_sample(u_s, state_s5_re[l].reshape(nbs, S5_LANES),
                                        state_s5_im[l].reshape(nbs, S5_LANES), prm, nbs, ls)

        mw = (won, wod, wos, ln2[l][None], mlp_up[l].astype(BF16), mlp_down[l].astype(BF16), ln_f[None])
        xp = _mlp(xp, on_p, od_p, os_p, *mw, final_norm=(l == depth - 1))
        xs = _mlp(xs, on_s, od_s, os_s, *mw, final_norm=(l == depth - 1))

        kv_s6 = kv_s.reshape(nbs, ls, 6, G, dh)
        win_p = kvt_win[:, :, lp - WINDOW:].reshape(nbp, 2, G, dh, WINDOW).transpose(0, 4, 1, 2, 3)
        outs_p.append((win_p, dnx_p.reshape(nbp, lp, 3 * DN_WIDTH)[:, -(DN_CONV_W - 1):], dn_p,
                       s5r_p.reshape(nbp, S5_GROUPS, S5_STATE), s5i_p.reshape(nbp, S5_GROUPS, S5_STATE)))
        outs_s.append((kv_s6[:, :, :4], dnx_s.reshape(nbs, ls, 3 * DN_WIDTH)[:, -(DN_CONV_W - 1):], dn_s,
                       s5r_s.reshape(nbs, S5_GROUPS, S5_STATE), s5i_s.reshape(nbs, S5_GROUPS, S5_STATE)))

    stack = lambda outs, i: jnp.stack([o[i] for o in outs], axis=0)
    kv_prompt = kvt_main.reshape(depth, nbp, 4, G, dh, lp).transpose(0, 1, 5, 2, 3, 4)
    win_sample = nwin_t.reshape(depth, nbs, 2, G, dh, wlen).transpose(0, 1, 5, 2, 3, 4)
    return (xp.reshape(nbp, lp, D_MODEL), xs.reshape(nbs, ls, D_MODEL),
            kv_prompt, stack(outs_s, 0), stack(outs_p, 0), win_sample,
            stack(outs_p, 1), stack(outs_s, 1), stack(outs_p, 2), stack(outs_s, 2),
            stack(outs_p, 3), stack(outs_s, 3), stack(outs_p, 4), stack(outs_s, 4))
```

```python
import functools

import jax
import jax.numpy as jnp
from jax import lax
from jax.experimental import pallas as pl
from jax.experimental.pallas import tpu as pltpu

F32 = jnp.float32
BF16 = jnp.bfloat16
HIGHEST = lax.Precision.HIGHEST

D_MODEL = 1024
HEAD_DIM = 64
NSA_HEADS = 8
NSA_KV_HEADS = 2
NSA_GROUP = 4
NSA_BLOCK = 64
NSA_TOPK = 16
WINDOW = 512
DN_HEADS = 4
DN_WIDTH = 256
DN_CONV_W = 4
S5_GROUPS = 16
S5_GROUP_CH = 16
S5_STATE = 64
S5_WIDTH = 256
S5_LANES = S5_GROUPS * S5_STATE
D_FF = 4096
NORM_EPS = 1e-6
BIG = 1e9
NEG = -1e30
LANE = 128
NSA_TQ = 256
VMEM_LIMIT = 48 * 1024 * 1024

_NT = (((1,), (1,)), ((), ()))


def _slope(h):
    return 2.0 ** (-(h + 1))


def _cparams(sem):
    return pltpu.CompilerParams(dimension_semantics=sem, vmem_limit_bytes=VMEM_LIMIT)


def _dotf(a, b):
    return jnp.dot(a, b, preferred_element_type=F32)


def _dot_nt(a, b):
    return lax.dot_general(a, b, _NT, preferred_element_type=F32)


def _dot_hi(a, b):
    return jnp.dot(a, b, preferred_element_type=F32, precision=HIGHEST)


def _proj_kernel(x_ref, g_ref, wq_ref, wkv_ref, wdn_ref, wz_ref, wu_ref, ws_ref, wexp_ref, *refs, lt, tk, nprev):
    if nprev:
        prev_ref, refs = refs[0], refs[1:]
    q_ref, kv_ref, dn_ref, z_ref, u_ref, s_ref, pool_ref, kvtm_ref, kvtw_ref = refs[:9]
    sel_refs = refs[9:]
    x = x_ref[...]
    h = x * lax.rsqrt(jnp.mean(x * x, axis=-1, keepdims=True) + NORM_EPS) * g_ref[...]
    hb = h.astype(BF16)
    q_ref[...] = _dotf(hb, wq_ref[...])
    kv = _dotf(hb, wkv_ref[...])
    kv_ref[...] = kv
    kvt = kv.T
    if nprev:
        kvtm_ref[0:nprev] = prev_ref[...]
    kvtm_ref[nprev] = kvt[:4 * LANE]
    kvtw_ref[...] = kvt[4 * LANE:]
    if sel_refs:
        kaug_ref, vt_ref = sel_refs
        rows = x.shape[0]
        pos = (pl.program_id(0) % lt) * rows + lax.broadcasted_iota(jnp.int32, (rows, LANE), 0)
        lanei = lax.broadcasted_iota(jnp.int32, (rows, LANE), 1)
        onehot = (lanei == pos // NSA_BLOCK).astype(BF16)
        jpart = jnp.where(lanei == HEAD_DIM, (pos % NSA_BLOCK).astype(F32), 0.0)
        ksl = kv[:, 2 * LANE:3 * LANE]
        for g in range(NSA_KV_HEADS):
            kg = ksl if g == 0 else pltpu.roll(ksl, HEAD_DIM, axis=1)
            kaug_ref[g] = jnp.concatenate([onehot, jnp.where(lanei < HEAD_DIM, kg, jpart).astype(BF16)], axis=1)
            vrow = 3 * LANE + g * HEAD_DIM
            for c in range(rows // tk):
                vt_ref[g, c] = kvt[vrow:vrow + HEAD_DIM, c * tk:(c + 1) * tk].astype(BF16)
    dn_ref[...] = _dotf(hb, wdn_ref[...])
    z_ref[...] = _dotf(hb, wz_ref[...])
    u_ref[...] = _dotf(hb, wu_ref[...])
    s_ref[...] = _dotf(hb, ws_ref[...])
    tm = x.shape[0]
    kc = kv[:, :2 * LANE].reshape(tm // NSA_BLOCK, NSA_BLOCK, 2 * LANE) * wexp_ref[...][None]
    pool_ref[...] = jnp.sum(kc, axis=1)


def _proj(x, ln, wq, wkv, wdn, wz, wu, ws, wexp, nseq, prev_kvt=None, sel_tile=None, tm=512):
    n = x.shape[0]
    lseq = n // nseq
    lt = lseq // tm
    nprev = 0 if prev_kvt is None else prev_kvt.shape[0]
    const = lambda i: (0, 0)
    row = lambda i: (i, 0)
    outs = [(n, 1024), (n, 768), (n, 768), (n, 256), (n, 256), (n, LANE), (n // NSA_BLOCK, 2 * LANE),
            (nprev + 1, nseq, 4 * LANE, lseq), (nseq, 2 * LANE, lseq)]
    out_specs = ([pl.BlockSpec((tm, s[1]), row) for s in outs[:6]]
                 + [pl.BlockSpec((tm // NSA_BLOCK, 2 * LANE), row),
                    pl.BlockSpec((nprev + 1, None, 4 * LANE, tm), lambda i: (0, i // lt, 0, i % lt)),
                    pl.BlockSpec((None, 2 * LANE, tm), lambda i: (i // lt, 0, i % lt))])
    out_shape = [jax.ShapeDtypeStruct(s, F32) for s in outs]
    prev_specs = [pl.BlockSpec((nprev, None, 4 * LANE, tm), lambda i: (0, i // lt, 0, i % lt))] if nprev else []
    G = NSA_KV_HEADS
    if sel_tile is not None:
        out_specs += [pl.BlockSpec((None, G, tm, 2 * LANE), lambda i: (i // lt, 0, i % lt, 0)),
                      pl.BlockSpec((None, G, tm // sel_tile, HEAD_DIM, sel_tile),
                                   lambda i: (i // lt, 0, i % lt, 0, 0))]
        out_shape += [jax.ShapeDtypeStruct((nseq, G, lseq, 2 * LANE), BF16),
                      jax.ShapeDtypeStruct((nseq, G, lseq // sel_tile, HEAD_DIM, sel_tile), BF16)]
    return pl.pallas_call(
        functools.partial(_proj_kernel, lt=lt, tk=sel_tile, nprev=nprev),
        grid=(n // tm,),
        in_specs=[pl.BlockSpec((tm, D_MODEL), row), pl.BlockSpec((1, D_MODEL), const)]
        + [pl.BlockSpec(w.shape, const) for w in (wq, wkv, wdn, wz, wu, ws, wexp)] + prev_specs,
        out_specs=out_specs,
        out_shape=out_shape,
        compiler_params=_cparams(("parallel",)),
        name="proj",
    )(x, ln, wq, wkv, wdn, wz, wu, ws, wexp, *([prev_kvt] if nprev else []))


def _topk_round(sel, score, blk, nblk):
    m = jnp.max(score, axis=0, keepdims=True)
    idx = jnp.min(jnp.where(score == m, blk, nblk), axis=0, keepdims=True)
    pick = blk == idx
    return sel | pick, jnp.where(pick, -jnp.inf, score)


NSA_FORCED = 3
NSA_FREE_ROUNDS = NSA_TOPK - NSA_FORCED


def _free_scores(imp, forced, blk, cur):
    return jnp.where(blk > cur, -BIG, jnp.where(forced, -jnp.inf, imp))


def _topk_blocks(score, blk, nblk):
    sel = jnp.zeros(score.shape, dtype=jnp.bool_)
    for _ in range(NSA_FREE_ROUNDS):
        sel, score = _topk_round(sel, score, blk, nblk)
    return sel


def _cmp_heads(q_tile, kc, vct, t, blk, nblk_valid, write_oc):
    nblk = blk.shape[0]
    ok = ((blk + 1) * NSA_BLOCK - 1 <= t) & (blk < nblk_valid)
    okf = ok.astype(F32)
    dist = t.astype(F32) - (blk.astype(F32) * NSA_BLOCK + (NSA_BLOCK - 1) / 2.0)
    cur = t // NSA_BLOCK
    forced = (blk == 0) | (blk == cur) | (blk == cur - 1)
    sels = []
    for g in range(NSA_KV_HEADS):
        imp = jnp.zeros(blk.shape, F32)
        for r in range(NSA_GROUP):
            h = g * NSA_GROUP + r
            q = (q_tile(h) * HEAD_DIM ** -0.5).astype(BF16)
            s = _dot_nt(kc, q)
            s = jnp.where(ok, s - _slope(h) * dist, NEG)
            m = jnp.max(s, axis=0, keepdims=True)
            p = jnp.exp(s - m) * okf
            p = p / jnp.maximum(jnp.sum(p, axis=0, keepdims=True), 1e-30)
            imp = imp + p
            write_oc(h, _dotf(vct, p.astype(BF16)))
        sel = (forced | _topk_blocks(_free_scores(imp, forced, blk, cur), blk, nblk)) & (blk <= cur)
        sels.append(sel)
    return sels


def _cmp_kernel(q_ref, kcv_ref, oc_ref, sel_ref, any_ref, *, tq):
    qt = pl.program_id(1)
    kcv = kcv_ref[...]
    nblk = kcv.shape[0]
    kc = kcv[:, :LANE].astype(BF16)
    vct = kcv[:, LANE:].T.astype(BF16)
    blk = lax.broadcasted_iota(jnp.int32, (nblk, tq), 0)
    t = qt * tq + lax.broadcasted_iota(jnp.int32, (nblk, tq), 1)

    def write_oc(h, oct):
        oc_ref[:, h * LANE:(h + 1) * LANE] = oct.T

    sels = _cmp_heads(lambda h: q_ref[:, h * LANE:(h + 1) * LANE], kc, vct, t, blk, nblk, write_oc)
    for g in range(NSA_KV_HEADS):
        selt = sels[g].astype(F32).T
        sel_ref[:, g * LANE:(g + 1) * LANE] = selt
        any_ref[:, g * LANE:(g + 1) * LANE] = jnp.broadcast_to(jnp.max(selt, axis=0, keepdims=True), (8, LANE))


def _cmp_prompt(qh, pool, nb, lq, tq):
    nq = lq // tq
    nblk = lq // NSA_BLOCK
    assert nblk == LANE
    return pl.pallas_call(
        functools.partial(_cmp_kernel, tq=tq),
        grid=(nb, nq),
        in_specs=[pl.BlockSpec((tq, 1024), lambda b, i: (b * nq + i, 0)),
                  pl.BlockSpec((nblk, 2 * LANE), lambda b, i: (b, 0))],
        out_specs=[pl.BlockSpec((tq, 1024), lambda b, i: (b * nq + i, 0)),
                   pl.BlockSpec((tq, 2 * LANE), lambda b, i: (b * nq + i, 0)),
                   pl.BlockSpec((8, 2 * LANE), lambda b, i: (b * nq + i, 0))],
        out_shape=[jax.ShapeDtypeStruct((nb * lq, 1024), F32),
                   jax.ShapeDtypeStruct((nb * lq, 2 * LANE), F32),
                   jax.ShapeDtypeStruct((nb * nq * 8, 2 * LANE), F32)],
        compiler_params=_cparams(("parallel", "parallel")),
        name="nsa_cmp",
    )(qh, pool)


def _sel_kernel(cnt_ref, lst_ref, q_ref, sel_ref, k_ref, vt_ref, o_ref, lhs_s, m_s, l_s, acc_s, *, tq, nq):
    b = pl.program_id(0)
    g = pl.program_id(1)
    qt = pl.program_id(2)
    lanei = lax.broadcasted_iota(jnp.int32, (tq, LANE), 1)
    nio = lanei.astype(F32)
    selg = sel_ref[...] > 0.5
    for r in range(NSA_GROUP):
        slope = jnp.where(g == 0, _slope(r), _slope(NSA_GROUP + r))
        qt_r = q_ref[:, r * LANE:(r + 1) * LANE]
        qt_r = jnp.where(g == 0, qt_r, pltpu.roll(qt_r, HEAD_DIM, axis=1))
        qpart = jnp.where(lanei == HEAD_DIM, slope, qt_r * HEAD_DIM ** -0.5)
        bias = jnp.where(selg, (NSA_BLOCK * slope) * nio, NEG)
        lhs_s[r] = jnp.concatenate([bias.astype(BF16), qpart.astype(BF16)], axis=1)
    m_s[...] = jnp.full(m_s.shape, NEG, F32)
    l_s[...] = jnp.zeros(l_s.shape, F32)
    acc_s[...] = jnp.zeros(acc_s.shape, F32)

    def tiles(items):
        heads = range(NSA_GROUP)
        kaugs = [k_ref[kt] for kt, _ in items]
        vts = [vt_ref[kt] for kt, _ in items]
        ss = [[_dot_nt(ka, lhs_s[r]) for r in heads] for ka in kaugs]
        ss = [[s if mask is None else jnp.where(mask, s, NEG) for s in row] for row, (_, mask) in zip(ss, items)]
        m_old = [m_s[r] for r in heads]
        m_new = []
        for r in heads:
            m = m_old[r]
            for row in ss:
                m = jnp.maximum(m, jnp.max(row[r], axis=0, keepdims=True))
            m_new.append(m)
        ps = [[jnp.exp(row[r] - m_new[r]) for r in heads] for row in ss]
        pvs = [[_dotf(vt, row[r].astype(BF16)) for r in heads] for vt, row in zip(vts, ps)]
        for r in heads:
            alpha = jnp.exp(m_old[r] - m_new[r])
            l_s[r] = alpha * l_s[r] + sum(jnp.sum(row[r], axis=0, keepdims=True) for row in ps)
            acc_s[r] = alpha * acc_s[r] + sum(row[r] for row in pvs)
            m_s[r] = m_new[r]

    cbase = (b * NSA_KV_HEADS + g) * nq + qt
    n_act = cnt_ref[cbase]
    lbase = cbase * nq

    def body(i, carry):
        tiles([(lst_ref[lbase + 2 * i], None), (lst_ref[lbase + 2 * i + 1], None)])
        return carry

    lax.fori_loop(0, n_act // 2, body, 0)
    diag = (qt, lax.broadcasted_iota(jnp.int32, (tq, tq), 0) <= lax.broadcasted_iota(jnp.int32, (tq, tq), 1))

    @pl.when(n_act % 2 == 1)
    def _():
        tiles([(lst_ref[lbase + n_act - 1], None), diag])

    @pl.when(n_act % 2 == 0)
    def _():
        tiles([diag])

    zeros = jnp.zeros((HEAD_DIM, tq), F32)
    for r in range(NSA_GROUP):
        o = jnp.concatenate([acc_s[r] / l_s[r], zeros], axis=0).T
        o_ref[:, r * LANE:(r + 1) * LANE] = jnp.where(g == 0, o, pltpu.roll(o, HEAD_DIM, axis=1))


def _sel_prompt(qh, sel, anyblk, kaug, vt, nb, lq, tq):
    assert vt.shape[-1] == tq
    nq = lq // tq
    bpt = tq // NSA_BLOCK
    G = NSA_KV_HEADS
    act = anyblk[::8].reshape(nb, nq, G, nq, bpt).max(axis=-1) > 0.5
    tile_i = jnp.arange(nq)
    act = act.transpose(0, 2, 1, 3) & (tile_i[None, :] < tile_i[:, None])
    rank = jnp.cumsum(act, axis=-1) - 1
    hit = act[..., None, :] & (rank[..., None, :] == tile_i[:, None])
    lst = jnp.sum(jnp.where(hit, tile_i, 0), axis=-1).astype(jnp.int32).reshape(-1)
    cnt = jnp.sum(act, axis=-1).astype(jnp.int32).reshape(-1)
    kaug = kaug.reshape(nb, G, nq, tq, 2 * LANE)
    qmap = lambda b, g, i, c, l: (b * nq + i, g)
    whole = lambda b, g, i, c, l: (b, g, 0, 0, 0)
    grid_spec = pltpu.PrefetchScalarGridSpec(
        num_scalar_prefetch=2,
        grid=(nb, G, nq),
        in_specs=[pl.BlockSpec((tq, NSA_GROUP * LANE), qmap),
                  pl.BlockSpec((tq, LANE), qmap),
                  pl.BlockSpec((None, None, nq, tq, 2 * LANE), whole),
                  pl.BlockSpec((None, None, nq, HEAD_DIM, tq), whole)],
        out_specs=pl.BlockSpec((tq, NSA_GROUP * LANE), qmap),
        scratch_shapes=[pltpu.VMEM((NSA_GROUP, tq, 2 * LANE), BF16),
                        pltpu.VMEM((NSA_GROUP, 1, tq), F32),
                        pltpu.VMEM((NSA_GROUP, 1, tq), F32),
                        pltpu.VMEM((NSA_GROUP, HEAD_DIM, tq), F32)])
    return pl.pallas_call(
        functools.partial(_sel_kernel, tq=tq, nq=nq),
        grid_spec=grid_spec,
        out_shape=jax.ShapeDtypeStruct((nb * lq, 1024), F32),
        compiler_params=_cparams(("parallel", "parallel", "arbitrary")),
        name="nsa_sel",
    )(cnt, lst, qh, sel, kaug, vt)


def _gate_mix(small, oc, os_, ow, h):
    gt = jax.nn.sigmoid(small[:, 3 * h:3 * h + 3])
    return gt[:, 0:1] * oc + gt[:, 1:2] * os_ + gt[:, 2:3] * ow


WIN_HEADS_PER_ROUND = 4
_WIN_SPLIT = 32


def _win_kernel(q_ref, k0_ref, k1_ref, k2_ref, v0_ref, v1_ref, v2_ref, oc_ref, os_ref, sm_ref, o_ref, *, tq):
    qt = pl.program_id(1)
    nkey = 3 * tq
    ki = lax.broadcasted_iota(jnp.int32, (nkey, LANE), 0)
    li = lax.broadcasted_iota(jnp.int32, (nkey, LANE), 1)
    penc = jnp.where(li == 0, ki // _WIN_SPLIT, jnp.where(li == 1, ki % _WIN_SPLIT, 0)).astype(BF16)
    kcat = jnp.concatenate([k0_ref[...], k1_ref[...], k2_ref[...]], axis=0).astype(BF16)
    kaug = jnp.concatenate([kcat, penc], axis=1)
    vts = [v.astype(BF16) for v in (v0_ref[...], v1_ref[...], v2_ref[...])]
    krow = lax.broadcasted_iota(jnp.int32, (tq, tq), 0)
    qcol = lax.broadcasted_iota(jnp.int32, (tq, tq), 1)
    assert WINDOW == 2 * tq
    valids = [(krow > qcol) & (qt >= 2), jnp.broadcast_to(qt >= 1, (tq, tq)), krow <= qcol]
    small = sm_ref[...]
    lq_i = lax.broadcasted_iota(jnp.int32, (tq, LANE), 1)
    for h0 in range(0, NSA_HEADS, WIN_HEADS_PER_ROUND):
        heads = range(h0, h0 + WIN_HEADS_PER_ROUND)
        qaug = []
        for h in heads:
            al = jnp.where(lq_i == 0, _WIN_SPLIT * _slope(h), jnp.where(lq_i == 1, _slope(h), 0.0))
            qaug.append(jnp.concatenate([(q_ref[:, h * LANE:(h + 1) * LANE] * HEAD_DIM ** -0.5).astype(BF16),
                                         al.astype(BF16)], axis=1))
        tiles3 = range(3)
        ss = [[jnp.where(valids[j], _dot_nt(kaug[j * tq:(j + 1) * tq], qa), NEG) for j in tiles3]
              for qa in qaug]
        ms = [functools.reduce(jnp.maximum, [jnp.max(s, axis=0, keepdims=True) for s in row]) for row in ss]
        ps = [[jnp.exp(s - m) for s in row] for row, m in zip(ss, ms)]
        ows = []
        for row in ps:
            num = sum(_dotf(vts[j], row[j].astype(BF16)) for j in tiles3)
            den = sum(jnp.sum(row[j], axis=0, keepdims=True) for j in tiles3)
            ows.append((num / den).T)
        for h, ow in zip(heads, ows):
            sl = slice(h * LANE, (h + 1) * LANE)
            o_ref[:, sl] = _gate_mix(small, oc_ref[:, sl], os_ref[:, sl], ow, h).astype(o_ref.dtype)


def _win_prompt(qh, kv, kvt_win, oc, osel, small, nb, lq, tq=256):
    assert 2 * tq >= WINDOW - 1 and 3 * tq % _WIN_SPLIT == 0 and 3 * tq // _WIN_SPLIT <= 256
    nq = lq // tq
    row = lambda b, i: (b * nq + i, 0)

    def kmap(back):
        return lambda b, i: (b * nq + jnp.maximum(i - back, 0), 4)

    def vmap(back):
        return lambda b, i: (b, 1, jnp.maximum(i - back, 0))

    return pl.pallas_call(
        functools.partial(_win_kernel, tq=tq),
        grid=(nb, nq),
        in_specs=[pl.BlockSpec((tq, 1024), row)]
        + [pl.BlockSpec((tq, LANE), kmap(back)) for back in (2, 1, 0)]
        + [pl.BlockSpec((None, LANE, tq), vmap(back)) for back in (2, 1, 0)]
        + [pl.BlockSpec((tq, 1024), row), pl.BlockSpec((tq, 1024), row), pl.BlockSpec((tq, LANE), row)],
        out_specs=pl.BlockSpec((tq, 1024), row),
        out_shape=jax.ShapeDtypeStruct((nb * lq, 1024), BF16),
        compiler_params=_cparams(("parallel", "parallel")),
        name="nsa_win",
    )(qh, kv, kv, kv, kvt_win, kvt_win, kvt_win, oc, osel, small)


def _per_head(idx_h, fn):
    out = jnp.zeros(idx_h.shape, F32)
    for h in range(NSA_HEADS):
        out = jnp.where(idx_h == h, fn(h), out)
    return out


def _pad_rows(a, rows):
    return jnp.concatenate([a, jnp.zeros((rows - a.shape[0], a.shape[1]), a.dtype)], axis=0)


NSA_SAMPLE_SEQS = 2


def _nsa_sample_kernel(pt_ref, q_ref, kvn_ref, sm_ref, *rest, past, lq, npages, page, nprev):
    nseq = NSA_SAMPLE_SEQS
    pages = rest[:nseq * npages]
    rest = rest[nseq * npages:]
    if nprev:
        win_ref, wt_ref, prev_ref, o_ref, nwin_all, ktc_s, vtc_s, rt_s, vt_s = rest
        nwin_all[0:nprev] = prev_ref[...]
    else:
        win_ref, wt_ref, o_ref, nwin_all, ktc_s, vtc_s, rt_s, vt_s = rest
    nwin_ref = nwin_all.at[nprev]

    @pl.when(pl.program_id(0) == 0)
    def _():
        n_i = lax.broadcasted_iota(jnp.int32, (LANE, past), 0)
        pos_i = lax.broadcasted_iota(jnp.int32, (LANE, past), 1)
        for j in range(nseq):
            rt_s[j, 0:LANE, :] = (n_i == pos_i // NSA_BLOCK).astype(BF16)

    def rows(ref, j):
        return ref.at[pl.ds(j * lq, lq)]

    seqs = [_nsa_sample_seq(rows(q_ref, j), rows(kvn_ref, j), rows(sm_ref, j), pages[j * npages:(j + 1) * npages],
                            win_ref.at[j], wt_ref, rows(o_ref, j), nwin_ref.at[j], ktc_s.at[j], vtc_s.at[j],
                            rt_s.at[j], vt_s.at[j], past=past, lq=lq, npages=npages, page=page)
            for j in range(nseq)]
    live = True
    while live:
        for s in seqs:
            live = next(s, None) is not None and live


def _nsa_sample_seq(q_ref, kvn_ref, sm_ref, pages, win_ref, wt_ref, o_ref, nwin_ref, ktc_s, vtc_s, rt_s, vt_s,
                    *, past, lq, npages, page):
    nkeys = npages * page
    nrow = NSA_HEADS * lq
    nb_complete = (past + lq) // NSA_BLOCK
    new_blk = past // NSA_BLOCK
    assert past % NSA_BLOCK + lq <= NSA_BLOCK and nkeys == past and new_blk < LANE

    for i in range(npages):
        sl = slice(i * page, (i + 1) * page)
        ktc_s[:, sl] = pages[i][0:LANE, :].astype(BF16)
        vtc_s[:, sl] = pages[i][LANE:2 * LANE, :].astype(BF16)
    yield True

    def stage_selected_page(i):
        sl = slice(i * page, (i + 1) * page)
        rt_s[LANE:2 * LANE, sl] = pages[i][2 * LANE:3 * LANE, :].astype(BF16)
        vt_s[:, sl] = pages[i][3 * LANE:4 * LANE, :].astype(BF16)

    q8 = q_ref[...]
    qf = jnp.concatenate([q8[:, h * LANE:(h + 1) * LANE] for h in range(NSA_HEADS)], axis=0) * HEAD_DIM ** -0.5
    qb = qf.astype(BF16)
    kvn = kvn_ref[...]
    lane_lo = lax.broadcasted_iota(jnp.int32, (LANE, LANE), 1) < HEAD_DIM

    ktc = ktc_s[...]
    vtc = vtc_s[...]
    kc = jnp.where(lane_lo, _dot_nt(wt_ref[0, 0], ktc), _dot_nt(wt_ref[0, 1], ktc)).astype(BF16)
    vc = jnp.where(lane_lo, _dot_nt(wt_ref[1, 0], vtc), _dot_nt(wt_ref[1, 1], vtc))
    vct = vc.T.astype(BF16)
    blk = lax.broadcasted_iota(jnp.int32, (LANE, nrow), 0)
    col = lax.broadcasted_iota(jnp.int32, (LANE, nrow), 1)
    t = past + col % lq
    slope_c = _per_head(col // lq, _slope)
    ok = ((blk + 1) * NSA_BLOCK - 1 <= t) & (blk < nb_complete)
    dist = t.astype(F32) - (blk.astype(F32) * NSA_BLOCK + (NSA_BLOCK - 1) / 2.0)
    s = jnp.where(ok, _dot_nt(kc, qb) - slope_c * dist, NEG)
    m = jnp.max(s, axis=0, keepdims=True)
    p = jnp.exp(s - m) * ok.astype(F32)
    p = p / jnp.maximum(jnp.sum(p, axis=0, keepdims=True), 1e-30)
    o_c = _dotf(vct, p.astype(BF16)).T
    yield True
    ri = lax.broadcasted_iota(jnp.int32, (nrow, nrow), 0)
    ci = lax.broadcasted_iota(jnp.int32, (nrow, nrow), 1)
    gsum = ((ri // (NSA_GROUP * lq) == ci // (NSA_GROUP * lq)) & (ri % lq == ci % lq)).astype(F32)
    imp = _dot_hi(p, gsum)
    cur = t // NSA_BLOCK
    forced = (blk == 0) | (blk == cur) | (blk == cur - 1)
    score = _free_scores(imp, forced, blk, cur)
    rowi = lax.broadcasted_iota(jnp.int32, (nrow, LANE), 0)
    lanei = lax.broadcasted_iota(jnp.int32, (nrow, LANE), 1)
    slope_r = _per_head(rowi[:, 0:1] // lq, _slope)
    tok_r = rowi % lq
    new_ok = (lanei <= tok_r) & (lanei < lq)

    wlen = win_ref.shape[1]
    wi = lax.broadcasted_iota(jnp.int32, (nrow, wlen), 1)
    tok_w = lax.broadcasted_iota(jnp.int32, (nrow, wlen), 0) % lq
    dist_w = wlen + tok_w - wi
    s_w = _dotf(qb, win_ref[0:LANE, :].astype(BF16))
    s_w = jnp.where(dist_w < WINDOW, s_w - slope_r * dist_w.astype(F32), NEG)
    kwn = _pad_rows(kvn[:, 4 * LANE:5 * LANE], LANE).astype(BF16)
    s_wn = jnp.where(new_ok, _dot_nt(qb, kwn) - slope_r * (tok_r - lanei).astype(F32), NEG)
    m = jnp.maximum(jnp.max(s_w, axis=1, keepdims=True), jnp.max(s_wn, axis=1, keepdims=True))
    p_w = jnp.exp(s_w - m)
    p_wn = jnp.exp(s_wn - m)
    vwn = _pad_rows(kvn[:, 5 * LANE:6 * LANE], LANE).astype(BF16)
    o_w = (_dot_nt(p_w.astype(BF16), win_ref[LANE:2 * LANE, :].astype(BF16)) + _dotf(p_wn.astype(BF16), vwn)) / (
        jnp.sum(p_w, axis=1, keepdims=True) + jnp.sum(p_wn, axis=1, keepdims=True))

    win = win_ref[...]
    new_t = _pad_rows(kvn[:, 4 * LANE:6 * LANE], LANE).T
    new_t = pltpu.roll(new_t, LANE - lq, axis=1)
    tail = jnp.concatenate([jnp.zeros((2 * LANE, wlen - LANE), F32), new_t], axis=1)
    lane_w = lax.broadcasted_iota(jnp.int32, win.shape, 1)
    nwin_ref[...] = jnp.where(lane_w < wlen - lq, pltpu.roll(win, wlen - lq, axis=1), tail)
    yield True

    picked = forced
    for i in range(max(NSA_FREE_ROUNDS, npages)):
        if i < npages:
            stage_selected_page(i)
        if i < NSA_FREE_ROUNDS:
            picked, score = _topk_round(picked, score, blk, LANE)
        yield True
    sel = (picked & (blk <= cur)).astype(F32).T

    bias = jnp.where(sel > 0.5, (NSA_BLOCK * slope_r) * lanei.astype(F32), NEG)
    lhs = jnp.concatenate([bias.astype(BF16), qb], axis=1)
    jrow = (lax.broadcasted_iota(jnp.int32, (1, nkeys), 1) % NSA_BLOCK).astype(F32)
    s_p = _dotf(lhs, rt_s[...]) + slope_r * jrow
    bias_new = jnp.sum(jnp.where(lanei == new_blk, bias, 0.0), axis=1, keepdims=True)
    jnew = (past % NSA_BLOCK + lanei).astype(F32)
    kn = _pad_rows(kvn[:, 2 * LANE:3 * LANE], LANE).astype(BF16)
    s_n = jnp.where(new_ok, _dot_nt(qb, kn) + bias_new + slope_r * jnew, NEG)
    yield True
    m = jnp.maximum(jnp.max(s_p, axis=1, keepdims=True), jnp.max(s_n, axis=1, keepdims=True))
    p_p = jnp.exp(s_p - m)
    p_n = jnp.exp(s_n - m)
    vn = _pad_rows(kvn[:, 3 * LANE:4 * LANE], LANE).astype(BF16)
    o_s = (_dot_nt(p_p.astype(BF16), vt_s[...]) + _dotf(p_n.astype(BF16), vn)) / (
        jnp.sum(p_p, axis=1, keepdims=True) + jnp.sum(p_n, axis=1, keepdims=True))

    small = sm_ref[...]
    gate = [jax.nn.sigmoid(jnp.concatenate([small[:, 3 * h + c:3 * h + c + 1] for h in range(NSA_HEADS)], axis=0))
            for c in range(3)]
    o = gate[0] * o_c + gate[1] * o_s + gate[2] * o_w
    for h in range(NSA_HEADS):
        o_ref[:, h * LANE:(h + 1) * LANE] = o[h * lq:(h + 1) * lq].astype(o_ref.dtype)


def _pool_weights_t(wck, wcv, past):
    pos = jnp.arange(past)
    onehot = (jnp.arange(LANE)[:, None] == (pos // NSA_BLOCK)[None, :]).astype(F32)
    w = jnp.stack([wck, wcv])[:, :, pos % NSA_BLOCK]
    return (w[:, :, None, :] * onehot[None, None]).astype(BF16)


def _nsa_sample(page_table, qh, kv, small, cache_t, win_t, wt, prev_win, *, layer, lq, past):
    nprev = 0 if prev_win is None else prev_win.shape[0]
    assert nprev == layer
    nb, npages = page_table.shape
    page = cache_t.shape[-1]
    wlen = win_t.shape[-1]
    ns = NSA_SAMPLE_SEQS
    assert nb % ns == 0
    tokmap = lambda b, pt: (b, 0)

    def page_spec(j, i):
        return pl.BlockSpec((None, None, 4 * LANE, page), lambda b, pt: (layer, pt[ns * b + j, i], 0, 0))

    grid_spec = pltpu.PrefetchScalarGridSpec(
        num_scalar_prefetch=1,
        grid=(nb // ns,),
        in_specs=[pl.BlockSpec((ns * lq, 1024), tokmap), pl.BlockSpec((ns * lq, 768), tokmap),
                  pl.BlockSpec((ns * lq, LANE), tokmap)]
        + [page_spec(j, i) for j in range(ns) for i in range(npages)]
        + [pl.BlockSpec((None, ns, 2 * LANE, wlen), lambda b, pt: (layer, b, 0, 0)),
           pl.BlockSpec(wt.shape, lambda b, pt: (0, 0, 0, 0))]
        + ([pl.BlockSpec((nprev, ns, 2 * LANE, wlen), lambda b, pt: (0, b, 0, 0))] if nprev else []),
        out_specs=[pl.BlockSpec((ns * lq, 1024), tokmap),
                   pl.BlockSpec((nprev + 1, ns, 2 * LANE, wlen), lambda b, pt: (0, b, 0, 0))],
        scratch_shapes=[pltpu.VMEM((ns, LANE, past), BF16), pltpu.VMEM((ns, LANE, past), BF16),
                        pltpu.VMEM((ns, 2 * LANE, past), BF16), pltpu.VMEM((ns, LANE, past), BF16)])
    return pl.pallas_call(
        functools.partial(_nsa_sample_kernel, past=past, lq=lq, npages=npages, page=page, nprev=nprev),
        grid_spec=grid_spec,
        out_shape=[jax.ShapeDtypeStruct((nb * lq, 1024), BF16),
                   jax.ShapeDtypeStruct((nprev + 1, nb, 2 * LANE, wlen), F32)],
        compiler_params=_cparams(("arbitrary",)),
        name="nsa_sample",
    )(page_table, qh, kv, small, *([cache_t] * (ns * npages)), win_t, wt, *([prev_win] if nprev else []))


GDN_ROWS = 64


def _softplus(x):
    return jnp.maximum(x, 0.0) + jnp.log(1.0 + jnp.exp(-jnp.abs(x)))


def _gdn_kernel(x_ref, hist_ref, sm_ref, z_ref, cw_ref, alog_ref, dtb_ref, nw_ref, s0_ref,
                o_ref, s_ref, *, seg, zero_first_hist):
    c = pl.program_id(1)
    rows = GDN_ROWS
    nseg = rows // seg

    @pl.when(c == 0)
    def _():
        s_ref[...] = s0_ref[...]

    x = x_ref[...]
    hist = hist_ref[...]
    if zero_first_hist:
        hist = jnp.where(c == 0, 0.0, hist)
    tpos = lax.broadcasted_iota(jnp.int32, x.shape, 0) % seg
    cw = cw_ref[...]
    y = x * cw[DN_CONV_W - 1:DN_CONV_W]
    for sh in range(1, DN_CONV_W):
        xs = jnp.where(tpos >= sh, pltpu.roll(x, sh, axis=0), pltpu.roll(hist, sh, axis=0))
        y = y + xs * cw[DN_CONV_W - 1 - sh:DN_CONV_W - sh]
    y = y * jax.nn.sigmoid(y)

    ri = lax.broadcasted_iota(jnp.int32, (rows, rows), 0)
    ci = lax.broadcasted_iota(jnp.int32, (rows, rows), 1)
    same = (ri // seg) == (ci // seg)
    tri = same & (ci <= ri)
    strict = same & (ci < ri)
    trif = tri.astype(F32)
    small = sm_ref[...]
    z = z_ref[...]
    heads = range(DN_HEADS)
    segs = range(nseg)
    hs = lambda h: slice(h * HEAD_DIM, (h + 1) * HEAD_DIM)
    yqs = [y[:, hs(h)] for h in heads]
    yks = [y[:, DN_WIDTH + h * HEAD_DIM:DN_WIDTH + (h + 1) * HEAD_DIM] for h in heads]
    vs = [y[:, 2 * DN_WIDTH + h * HEAD_DIM:2 * DN_WIDTH + (h + 1) * HEAD_DIM] for h in heads]
    qs = [a * lax.rsqrt(jnp.sum(a * a, axis=-1, keepdims=True) + 1e-6) * HEAD_DIM ** -0.5 for a in yqs]
    ks = [a * lax.rsqrt(jnp.sum(a * a, axis=-1, keepdims=True) + 1e-6) for a in yks]
    betas = [jax.nn.sigmoid(small[:, 28 + h:29 + h]) for h in heads]
    gs = [-jnp.exp(alog_ref[:, h:h + 1]) * _softplus(small[:, 24 + h:25 + h] + dtb_ref[:, h:h + 1]) for h in heads]
    trifs = _split(trif)
    gcbs = [_dot3s(trifs, _split(jnp.broadcast_to(g, (rows, rows)))) for g in gs]
    decays = [jnp.where(tri, jnp.exp(jnp.where(tri, gcb - gcb.T, 0.0)), 0.0) for gcb in gcbs]
    kbs = [ks[h] * betas[h] for h in heads]
    amats = [jnp.where(strict, _dot3_nt(kbs[h], ks[h]) * decays[h], 0.0) for h in heads]
    aqks = [jnp.where(tri, _dot3_nt(qs[h], ks[h]) * decays[h], 0.0) for h in heads]
    eye = (ri == ci).astype(F32)
    invs = [eye - a for a in amats]
    pws = amats
    span = 2
    while span < seg:
        pwsp = [_split(p) for p in pws]
        pws = [_dot3s(p, p) for p in pwsp]
        invs = [invs[h] + _dot3(invs[h], pws[h]) for h in heads]
        span *= 2
    egcs = [jnp.exp(gcb[:, :HEAD_DIM]) for gcb in gcbs]
    sols = [_dot3(invs[h], jnp.concatenate([vs[h] * betas[h], kbs[h] * egcs[h]], axis=1)) for h in heads]
    us = [s[:, :HEAD_DIM] for s in sols]
    ws = [s[:, HEAD_DIM:] for s in sols]
    qgs = [qs[h] * egcs[h] for h in heads]
    rsl = lambda s: slice(s * seg, (s + 1) * seg)
    sts = [[s_ref[s, h] for s in segs] for h in heads]
    stsp = [[_split(st) for st in row] for row in sts]
    vns = [jnp.concatenate([us[h][rsl(s)] - _dot3s(_split(ws[h][rsl(s)]), stsp[h][s]) for s in segs], axis=0)
           for h in heads]
    oqs = [jnp.concatenate([_dot3s(_split(qgs[h][rsl(s)]), stsp[h][s]) for s in segs], axis=0) for h in heads]
    vnsp = [_split(v) for v in vns]
    os_ = [oqs[h] + _dot3s(_split(aqks[h]), vnsp[h]) for h in heads]
    glasts = [[gcbs[h][(s + 1) * seg - 1:(s + 1) * seg, :HEAD_DIM] for s in segs] for h in heads]
    gl_rows = [jnp.concatenate([jnp.broadcast_to(gl, (seg, HEAD_DIM)) for gl in glasts[h]], axis=0) for h in heads]
    kdts = [(ks[h] * jnp.exp(gl_rows[h] - gcbs[h][:, :HEAD_DIM])).T for h in heads]
    for h in heads:
        for s in segs:
            kds = kdts[h] if nseg == 1 else jnp.where((ci // seg) == s, kdts[h], 0.0)
            s_ref[s, h] = sts[h][s] * jnp.exp(glasts[h][s]) + _dot3s(_split(kds), vnsp[h])
    outs = []
    for h in heads:
        o = os_[h] * lax.rsqrt(jnp.mean(os_[h] * os_[h], axis=-1, keepdims=True) + NORM_EPS) * nw_ref[...]
        zh = z[:, hs(h)]
        outs.append(o * (zh * jax.nn.sigmoid(zh)))
    o_ref[...] = jnp.concatenate(outs, axis=1).astype(o_ref.dtype)


def _gdn(dnqkv, hist, small, z, cw, alog, dtb, nw, s0, *, seg, nb, nchunk, row0, hist_map, zero_first_hist):
    nseg = GDN_ROWS // seg
    rmap = lambda b, c: (row0 + b * nchunk + c, 0)
    const = lambda b, c: (0, 0)
    smap = lambda b, c: (b, 0, 0, 0)
    sblock = (nseg, DN_HEADS, HEAD_DIM, HEAD_DIM)
    return pl.pallas_call(
        functools.partial(_gdn_kernel, seg=seg, zero_first_hist=zero_first_hist),
        grid=(nb, nchunk),
        in_specs=[pl.BlockSpec((GDN_ROWS, 768), rmap), pl.BlockSpec((GDN_ROWS, 768), hist_map),
                  pl.BlockSpec((GDN_ROWS, LANE), rmap), pl.BlockSpec((GDN_ROWS, 256), rmap),
                  pl.BlockSpec(cw.shape, const), pl.BlockSpec(alog.shape, const),
                  pl.BlockSpec(dtb.shape, const), pl.BlockSpec(nw.shape, const),
                  pl.BlockSpec(sblock, smap)],
        out_specs=[pl.BlockSpec((GDN_ROWS, 256), lambda b, c: (b * nchunk + c, 0)),
                   pl.BlockSpec(sblock, smap)],
        out_shape=[jax.ShapeDtypeStruct((nb * nchunk * GDN_ROWS, 256), BF16),
                   jax.ShapeDtypeStruct(s0.shape, F32)],
        compiler_params=_cparams(("parallel", "arbitrary")),
        name="gdn",
    )(dnqkv, hist, small, z, cw, alog, dtb, nw, s0)


def _split(a):
    hi = a.astype(BF16)
    return hi, (a - hi.astype(F32)).astype(BF16)


def _dot3s(a, b):
    return _dotf(a[0], b[0]) + (_dotf(a[0], b[1]) + _dotf(a[1], b[0]))


def _dot3(a, b):
    return _dot3s(_split(a), _split(b))


def _dot3_nt(a, b):
    a, b = _split(a), _split(b)
    return _dot_nt(a[0], b[0]) + (_dot_nt(a[0], b[1]) + _dot_nt(a[1], b[0]))


GDN_PREP_CHUNKS = 4
GDN_PROMPT_CHUNK = 64


def _gdn_prep_kernel(x_ref, hist_ref, sm_ref, cw_ref, alog_ref, dtb_ref,
                     u_ref, w_ref, qg_ref, kd_ref, aqk_ref, egl_ref):
    c = pl.program_id(1)
    x = x_ref[...]
    rows = x.shape[0]
    hist = jnp.where(c == 0, 0.0, hist_ref[...])
    row8 = lax.broadcasted_iota(jnp.int32, hist.shape, 0)
    cw = cw_ref[...]
    y = x * cw[DN_CONV_W - 1:DN_CONV_W]
    for sh in range(1, DN_CONV_W):
        xs = pltpu.roll(x, sh, axis=0)
        top = jnp.where(row8 < sh, pltpu.roll(hist, sh, axis=0), xs[0:8])
        xs = jnp.concatenate([top, xs[8:]], axis=0)
        y = y + xs * cw[DN_CONV_W - 1 - sh:DN_CONV_W - sh]
    y = y * jax.nn.sigmoid(y)

    C = GDN_PROMPT_CHUNK
    ri = lax.broadcasted_iota(jnp.int32, (C, C), 0)
    ci = lax.broadcasted_iota(jnp.int32, (C, C), 1)
    tri = ci <= ri
    strict = ci < ri
    trif = tri.astype(F32)
    small = sm_ref[...]
    nch = rows // C
    units = [(ch, h) for ch in range(nch) for h in range(DN_HEADS)]
    each = lambda f, *ls: [f(*a) for a in zip(*ls)]

    def qkv_of(ch, h):
        rs = slice(ch * C, (ch + 1) * C)
        yq = y[rs, h * HEAD_DIM:(h + 1) * HEAD_DIM]
        yk = y[rs, DN_WIDTH + h * HEAD_DIM:DN_WIDTH + (h + 1) * HEAD_DIM]
        v = y[rs, 2 * DN_WIDTH + h * HEAD_DIM:2 * DN_WIDTH + (h + 1) * HEAD_DIM]
        q = yq * lax.rsqrt(jnp.sum(yq * yq, axis=-1, keepdims=True) + 1e-6) * HEAD_DIM ** -0.5
        k = yk * lax.rsqrt(jnp.sum(yk * yk, axis=-1, keepdims=True) + 1e-6)
        beta = jax.nn.sigmoid(small[rs, 28 + h:29 + h])
        g = -jnp.exp(alog_ref[:, h:h + 1]) * _softplus(small[rs, 24 + h:25 + h] + dtb_ref[:, h:h + 1])
        return q, k, v, beta, g

    qs_, ks_, vs_, betas, gs = zip(*[qkv_of(ch, h) for ch, h in units])
    trifs = _split(trif)
    gcbs = [_dot3s(trifs, _split(jnp.broadcast_to(g, (C, C)))) for g in gs]
    kbs = each(lambda k, b: k * b, ks_, betas)
    ksp = [_split(k) for k in ks_]
    kbsp = [_split(kb) for kb in kbs]
    qsp = [_split(q) for q in qs_]
    nt3 = lambda a, b: _dot_nt(a[0], b[0]) + (_dot_nt(a[0], b[1]) + _dot_nt(a[1], b[0]))
    kks = each(nt3, kbsp, ksp)
    qks = each(nt3, qsp, ksp)
    decays = [jnp.where(tri, jnp.exp(jnp.where(tri, gcb - gcb.T, 0.0)), 0.0) for gcb in gcbs]
    amats = each(lambda kk, d: jnp.where(strict, kk * d, 0.0), kks, decays)
    eye = (ri == ci).astype(F32)
    blk_mask = lambda n: (ri // n) == (ci // n)
    base = 8
    aprev = [jnp.where(blk_mask(base), a, 0.0) for a in amats]
    invs = [eye - a for a in aprev]
    pwsp = [_split(a) for a in aprev]
    span = 2
    while span < base:
        pwsp = [_split(_dot3s(p, p)) for p in pwsp]
        invs = each(lambda i, p: i + _dot3s(_split(i), p), invs, pwsp)
        span *= 2
    size = base
    while size < C:
        size *= 2
        acur = amats if size == C else [jnp.where(blk_mask(size), a, 0.0) for a in amats]
        invsp = [_split(i) for i in invs]
        cross = each(lambda isp, ac, ap: _dot3s(isp, _split(ac - ap)), invsp, acur, aprev)
        invs = each(lambda i, isp, cr: i - _dot3s(_split(cr), isp), invs, invsp, cross)
        aprev = acur
    gcs = [gcb[:, :HEAD_DIM] for gcb in gcbs]
    egcs = [jnp.exp(gc) for gc in gcs]
    rhss = each(lambda v, b, kb, e: jnp.concatenate([v * b, kb * e], axis=1), vs_, betas, kbs, egcs)
    sols = each(_dot3, invs, rhss)
    glasts = [gc[C - 1:C, :] for gc in gcs]
    qgs = each(lambda q, e: q * e, qs_, egcs)
    kds = each(lambda k, gl, gc: k * jnp.exp(gl - gc), ks_, glasts, gcs)
    aqks = each(lambda qk, d: jnp.where(tri, qk * d, 0.0), qks, decays)
    for ch in range(nch):
        rs = slice(ch * C, (ch + 1) * C)
        un = range(ch * DN_HEADS, (ch + 1) * DN_HEADS)
        u_ref[rs, :] = jnp.concatenate([sols[i][:, :HEAD_DIM] for i in un], axis=1)
        w_ref[rs, :] = jnp.concatenate([sols[i][:, HEAD_DIM:] for i in un], axis=1)
        qg_ref[rs, :] = jnp.concatenate([qgs[i] for i in un], axis=1)
        kd_ref[rs, :] = jnp.concatenate([kds[i] for i in un], axis=1)
        aqk_ref[rs, :] = jnp.concatenate([aqks[i] for i in un], axis=1)
        egl_ref[ch * 8:(ch + 1) * 8, :] = jnp.concatenate(
            [jnp.broadcast_to(jnp.exp(glasts[i]), (8, HEAD_DIM)) for i in un], axis=1)


GDN_SCAN_CHUNKS = 4


def _gdn_scan_kernel(u_ref, w_ref, qg_ref, kd_ref, aqk_ref, egl_ref, z_ref, nw_ref, s0_ref, o_ref, s_ref, *, C):
    c = pl.program_id(0)

    @pl.when(c == 0)
    def _():
        s_ref[...] = s0_ref[...]

    nb = u_ref.shape[0]
    nsub = u_ref.shape[1] // C
    units = [(b, slice(h * HEAD_DIM, (h + 1) * HEAD_DIM), h) for b in range(nb) for h in range(DN_HEADS)]
    subs = [slice(j * C, (j + 1) * C) for j in range(nsub)]
    kdts = [[_split(kd_ref[b, rs, sl].T) for b, sl, _ in units] for rs in subs]
    wsp = [[_split(w_ref[b, rs, sl]) for b, sl, _ in units] for rs in subs]
    qgsp = [[_split(qg_ref[b, rs, sl]) for b, sl, _ in units] for rs in subs]
    aqsp = [[_split(aqk_ref[b, rs, h * C:(h + 1) * C]) for b, _, h in units] for rs in subs]
    sts = [s_ref[b, h] for b, _, h in units]
    for j, rs in enumerate(subs):
        stsp = [_split(s) for s in sts]
        vnews = [u_ref[b, rs, sl] - _dot3s(wp, st) for (b, sl, _), wp, st in zip(units, wsp[j], stsp)]
        oqs = [_dot3s(qp, st) for qp, st in zip(qgsp[j], stsp)]
        vsp = [_split(v) for v in vnews]
        os_ = [oq + _dot3s(ap, vs) for oq, ap, vs in zip(oqs, aqsp[j], vsp)]
        sts = [st * egl_ref[b, 8 * j:8 * j + 1, sl] + _dot3s(kdt, vs)
               for (b, sl, _), st, kdt, vs in zip(units, sts, kdts[j], vsp)]
        outs = []
        for (b, sl, _), o in zip(units, os_):
            o = o * lax.rsqrt(jnp.mean(o * o, axis=-1, keepdims=True) + NORM_EPS) * nw_ref[...]
            zh = z_ref[b, rs, sl]
            outs.append(o * (zh * jax.nn.sigmoid(zh)))
        for b in range(nb):
            o_ref[b, rs] = jnp.concatenate(outs[b * DN_HEADS:(b + 1) * DN_HEADS], axis=1).astype(o_ref.dtype)
    for (b, _, h), st in zip(units, sts):
        s_ref[b, h] = st


def _gdn_prompt(dnqkv, small, z, cw, alog, dtb, nw, s0, *, nb, lq):
    cb = GDN_PREP_CHUNKS
    C = GDN_PROMPT_CHUNK
    rows = cb * C
    nstep = lq // rows
    nchunk = lq // C
    rmap = lambda b, c: (b * nstep + c, 0)
    const = lambda b, c: (0, 0)
    wide = jax.ShapeDtypeStruct((nb * lq, DN_WIDTH), F32)
    aqk_shape = jax.ShapeDtypeStruct((nb * lq, DN_HEADS * C), F32)
    u, w, qg, kd, aqk, egl = pl.pallas_call(
        _gdn_prep_kernel,
        grid=(nb, nstep),
        in_specs=[pl.BlockSpec((rows, 768), rmap),
                  pl.BlockSpec((8, 768), lambda b, c: (jnp.maximum((b * nstep + c) * (rows // 8) - 1, 0), 0)),
                  pl.BlockSpec((rows, LANE), rmap),
                  pl.BlockSpec(cw.shape, const), pl.BlockSpec(alog.shape, const), pl.BlockSpec(dtb.shape, const)],
        out_specs=[pl.BlockSpec((rows, DN_WIDTH), rmap)] * 4 + [pl.BlockSpec((rows, DN_HEADS * C), rmap),
                                                                pl.BlockSpec((cb * 8, DN_WIDTH), rmap)],
        out_shape=[wide] * 4 + [aqk_shape, jax.ShapeDtypeStruct((nb * nchunk * 8, DN_WIDTH), F32)],
        compiler_params=_cparams(("parallel", "parallel")),
        name="gdn_prep",
    )(dnqkv, dnqkv, small, cw, alog, dtb)
    r3 = lambda a: a.reshape(nb, -1, a.shape[-1])
    cmap = lambda c: (0, c, 0)
    full = lambda a: pl.BlockSpec(a.shape, lambda c: (0,) * a.ndim)
    ns = GDN_SCAN_CHUNKS
    o, s = pl.pallas_call(
        functools.partial(_gdn_scan_kernel, C=C),
        grid=(nchunk // ns,),
        in_specs=[pl.BlockSpec((nb, ns * C, DN_WIDTH), cmap)] * 4
        + [pl.BlockSpec((nb, ns * C, DN_HEADS * C), cmap),
           pl.BlockSpec((nb, ns * 8, DN_WIDTH), cmap), pl.BlockSpec((nb, ns * C, DN_WIDTH), cmap),
           full(nw), full(s0)],
        out_specs=[pl.BlockSpec((nb, ns * C, DN_WIDTH), cmap), full(s0)],
        out_shape=[jax.ShapeDtypeStruct((nb, lq, DN_WIDTH), BF16), jax.ShapeDtypeStruct(s0.shape, F32)],
        compiler_params=_cparams(("arbitrary",)),
        name="gdn_scan",
    )(r3(u), r3(w), r3(qg), r3(kd), r3(aqk), r3(egl), z[:nb * lq].reshape(nb, lq, DN_WIDTH), nw, s0)
    return o.reshape(nb * lq, DN_WIDTH), s


def _s5_discretize(lre_ref, lim_ref, lstep_ref):
    lr = lre_ref[...]
    li = lim_ref[...]
    dt = jnp.exp(lstep_ref[...])
    mag = jnp.exp(lr * dt)
    ar = mag * jnp.cos(li * dt)
    ai = mag * jnp.sin(li * dt)
    den = lr * lr + li * li
    fr = ((ar - 1.0) * lr + ai * li) / den
    fi = (ai * lr - (ar - 1.0) * li) / den
    return ar, ai, fr, fi


def _s5_kernel(*refs, R, S, with_y):
    if with_y:
        (u_ref, h0r_ref, h0i_ref, lre_ref, lim_ref, lstep_ref, bre_ref, bim_ref, cre_ref, cim_ref, d_ref,
         wglu_ref, y_ref, hr_ref, hi_ref, ar_s, ai_s, bbr_s, bbi_s, xr_s, xi_s) = refs
    else:
        (u_ref, h0r_ref, h0i_ref, lre_ref, lim_ref, lstep_ref, bre_ref, bim_ref,
         hr_ref, hi_ref, ar_s, ai_s, bbr_s, bbi_s, xr_s, xi_s) = refs
    i = pl.program_id(0)

    @pl.when(i == 0)
    def _():
        ar, ai, fr, fi = _s5_discretize(lre_ref, lim_ref, lstep_ref)
        ar_s[...] = ar
        ai_s[...] = ai
        bbr_s[...] = (fr * bre_ref[...] - fi * bim_ref[...]).astype(BF16)
        bbi_s[...] = (fr * bim_ref[...] + fi * bre_ref[...]).astype(BF16)
        hr_ref[...] = h0r_ref[...]
        hi_ref[...] = h0i_ref[...]

    u = u_ref[...]
    ub = u.astype(BF16)
    xr_s[...] = _dotf(ub, bbr_s[...])
    xi_s[...] = _dotf(ub, bbi_s[...])
    ar = jnp.broadcast_to(ar_s[...], (R, S5_LANES))
    ai = jnp.broadcast_to(ai_s[...], (R, S5_LANES))

    if not with_y:
        def carry_step(s, h):
            off = pl.multiple_of(s * R, R)
            return (ar * h[0] - ai * h[1] + xr_s[pl.ds(off, R), :], ar * h[1] + ai * h[0] + xi_s[pl.ds(off, R), :])

        hr, hi = lax.fori_loop(0, S, carry_step, (hr_ref[...], hi_ref[...]))
        hr_ref[...] = hr
        hi_ref[...] = hi
        return

    def advance(hr, hi, off):
        nr = ar * hr - ai * hi + xr_s[pl.ds(off, R), :]
        ni = ar * hi + ai * hr + xi_s[pl.ds(off, R), :]
        xr_s[pl.ds(off, R), :] = nr
        xi_s[pl.ds(off, R), :] = ni

    advance(hr_ref[...], hi_ref[...], 0)

    def step(s, carry):
        prev = pl.multiple_of((s - 1) * R, R)
        advance(xr_s[pl.ds(prev, R), :], xi_s[pl.ds(prev, R), :], pl.multiple_of(s * R, R))
        return carry

    lax.fori_loop(1, S, step, 0)
    hr_ref[...] = xr_s[pl.ds((S - 1) * R, R), :]
    hi_ref[...] = xi_s[pl.ds((S - 1) * R, R), :]
    if with_y:
        y = _dotf(xr_s[...].astype(BF16), cre_ref[...]) - _dotf(xi_s[...].astype(BF16), cim_ref[...])
        y = y + d_ref[...] * u
        gl = jax.nn.gelu(y)
        y_ref[...] = (gl * jax.nn.sigmoid(_dotf(gl.astype(BF16), wglu_ref[...]))).astype(y_ref.dtype)


def _s5_scan(u_rows, h0r, h0i, prm, *, R, S, with_y):
    n = u_rows.shape[0]
    rows = R * S
    const = lambda i: (0, 0)
    rmap = lambda i: (i, 0)
    ins = [u_rows, h0r, h0i, prm["lre"], prm["lim"], prm["lstep"], prm["bre"], prm["bim"]]
    if with_y:
        ins += [prm["cre"], prm["cim"], prm["d"], prm["wglu"]]
    in_specs = [pl.BlockSpec((rows, S5_WIDTH), rmap)] + [pl.BlockSpec(a.shape, const) for a in ins[1:]]
    st_spec = pl.BlockSpec((R, S5_LANES), const)
    st_shape = jax.ShapeDtypeStruct((R, S5_LANES), F32)
    out_specs = [st_spec, st_spec]
    out_shape = [st_shape, st_shape]
    if with_y:
        out_specs = [pl.BlockSpec((rows, S5_WIDTH), rmap)] + out_specs
        out_shape = [jax.ShapeDtypeStruct((n, S5_WIDTH), BF16)] + out_shape
    return pl.pallas_call(
        functools.partial(_s5_kernel, R=R, S=S, with_y=with_y),
        grid=(n // rows,),
        in_specs=in_specs,
        out_specs=out_specs,
        out_shape=out_shape,
        scratch_shapes=[pltpu.VMEM((1, S5_LANES), F32), pltpu.VMEM((1, S5_LANES), F32),
                        pltpu.VMEM((S5_WIDTH, S5_LANES), BF16), pltpu.VMEM((S5_WIDTH, S5_LANES), BF16),
                        pltpu.VMEM((rows, S5_LANES), F32), pltpu.VMEM((rows, S5_LANES), F32)],
        compiler_params=_cparams(("arbitrary",)),
        name="s5_scan",
    )(*ins)


def _s5_carry_kernel(er_ref, ei_ref, lre_ref, lim_ref, lstep_ref, ir_ref, ii_ref, fr_ref, fi_ref, *, nseg, nsteps):
    ar, ai, _, _ = _s5_discretize(lre_ref, lim_ref, lstep_ref)
    pr, pi_ = ar, ai
    n = 1
    while n < nsteps:
        pr, pi_ = pr * pr - pi_ * pi_, 2.0 * pr * pi_
        n *= 2
    for b in range(er_ref.shape[0] // nseg):
        cr = jnp.zeros((1, S5_LANES), F32)
        ci = jnp.zeros((1, S5_LANES), F32)
        for s in range(nseg):
            r = b * nseg + s
            ir_ref[r:r + 1, :] = cr
            ii_ref[r:r + 1, :] = ci
            er = er_ref[r:r + 1, :]
            ei = ei_ref[r:r + 1, :]
            cr, ci = pr * cr - pi_ * ci + er, pr * ci + pi_ * cr + ei
        fr_ref[b:b + 1, :] = cr
        fi_ref[b:b + 1, :] = ci


def _s5_carry(er, ei, prm, *, nseg, nsteps):
    assert nsteps & (nsteps - 1) == 0
    nb = er.shape[0] // nseg
    full = lambda a: pl.BlockSpec(a.shape, lambda: (0,) * a.ndim)
    ins = [er, ei, prm["lre"], prm["lim"], prm["lstep"]]
    outs = [jax.ShapeDtypeStruct(er.shape, F32)] * 2 + [jax.ShapeDtypeStruct((nb, S5_LANES), F32)] * 2
    return pl.pallas_call(
        functools.partial(_s5_carry_kernel, nseg=nseg, nsteps=nsteps),
        in_specs=[full(a) for a in ins],
        out_specs=[full(o) for o in outs],
        out_shape=outs,
        name="s5_carry",
    )(*ins)


S5_SEGS = 8


def _s5_prep(lre, lim, lstep, bre, bim, cre, cim, d, wglu):
    eye = jnp.eye(S5_GROUPS, dtype=F32)
    bexp = lambda b: jnp.einsum("gpc,gh->gchp", b, eye).reshape(S5_WIDTH, S5_LANES)
    cexp = lambda c: jnp.einsum("gcp,gh->gphc", c, eye).reshape(S5_LANES, S5_WIDTH).astype(BF16)
    return {"lre": lre.reshape(1, S5_LANES), "lim": lim.reshape(1, S5_LANES),
            "lstep": jnp.repeat(lstep, S5_STATE).reshape(1, S5_LANES),
            "bre": bexp(bre), "bim": bexp(bim), "cre": cexp(cre), "cim": cexp(cim),
            "d": d.reshape(1, S5_WIDTH), "wglu": wglu.astype(BF16)}


def _s5_prompt(u, prm, nb, lq, steps_per_tile=32):
    nsteps = lq // S5_SEGS
    R = nb * S5_SEGS
    u_rows = u.reshape(nb, S5_SEGS, nsteps, S5_WIDTH).transpose(2, 0, 1, 3).reshape(nsteps * R, S5_WIDTH)
    zero = jnp.zeros((R, S5_LANES), F32)
    er, ei = _s5_scan(u_rows, zero, zero, prm, R=R, S=steps_per_tile, with_y=False)
    ir, ii, fr, fi = _s5_carry(er, ei, prm, nseg=S5_SEGS, nsteps=nsteps)
    y_rows, _, _ = _s5_scan(u_rows, ir, ii, prm, R=R, S=steps_per_tile, with_y=True)
    y = y_rows.reshape(nsteps, nb, S5_SEGS, S5_WIDTH).transpose(1, 2, 0, 3).reshape(nb * lq, S5_WIDTH)
    return y, fr, fi


def _s5_sample(u, h0r, h0i, prm, nb, lq, steps_per_tile=4):
    u_rows = u.reshape(nb, lq, S5_WIDTH).transpose(1, 0, 2).reshape(lq * nb, S5_WIDTH)
    y_rows, fr, fi = _s5_scan(u_rows, h0r, h0i, prm, R=nb, S=steps_per_tile, with_y=True)
    y = y_rows.reshape(lq, nb, S5_WIDTH).transpose(1, 0, 2).reshape(nb * lq, S5_WIDTH)
    return y, fr, fi


def _sample_conv_hist(buf):
    nb = buf.shape[0]
    blk = buf.reshape(nb // 8, 8, DN_CONV_W - 1, 768)
    blk = jnp.roll(blk, -1, axis=1)
    blk = jnp.pad(blk, ((0, 0), (0, 0), (8 - (DN_CONV_W - 1), 0), (0, 0)))
    return blk.reshape(nb * 8, 768)


def _rms(x, g):
    return x * lax.rsqrt(jnp.mean(x * x, axis=-1, keepdims=True) + NORM_EPS) * g


def _mlp_kernel(x_ref, on_ref, od_ref, os_ref, wn_ref, wd_ref, ws_ref, ln2_ref, up_ref, dn_ref, lnf_ref,
                o_ref, h2_s, *, final_norm):
    j = pl.program_id(1)

    @pl.when(j == 0)
    def _():
        x1 = (x_ref[...] + _dotf(on_ref[...], wn_ref[...]) + _dotf(od_ref[...], wd_ref[...])
              + _dotf(os_ref[...], ws_ref[...]))
        o_ref[...] = x1
        h2_s[...] = _rms(x1, ln2_ref[...]).astype(BF16)

    a = jnp.maximum(_dotf(h2_s[...], up_ref[...]), 0.0)
    o_ref[...] += _dotf((a * a).astype(BF16), dn_ref[...])

    if final_norm:
        @pl.when(j == pl.num_programs(1) - 1)
        def _():
            o_ref[...] = _rms(o_ref[...], lnf_ref[...])


def _mlp(x, o_nsa, o_dn, o_s5, wn, wd, ws, ln2, up, dn, lnf, *, final_norm, tm=512, tf=1024):
    n = x.shape[0]
    row = lambda i, j: (i, 0)
    const = lambda i, j: (0, 0)
    return pl.pallas_call(
        functools.partial(_mlp_kernel, final_norm=final_norm),
        grid=(n // tm, D_FF // tf),
        in_specs=[pl.BlockSpec((tm, D_MODEL), row), pl.BlockSpec((tm, 1024), row),
                  pl.BlockSpec((tm, DN_WIDTH), row), pl.BlockSpec((tm, S5_WIDTH), row),
                  pl.BlockSpec(wn.shape, const), pl.BlockSpec(wd.shape, const), pl.BlockSpec(ws.shape, const),
                  pl.BlockSpec((1, D_MODEL), const),
                  pl.BlockSpec((D_MODEL, tf), lambda i, j: (0, j)),
                  pl.BlockSpec((tf, D_MODEL), lambda i, j: (j, 0)),
                  pl.BlockSpec((1, D_MODEL), const)],
        out_specs=pl.BlockSpec((tm, D_MODEL), row),
        out_shape=jax.ShapeDtypeStruct((n, D_MODEL), F32),
        scratch_shapes=[pltpu.VMEM((tm, D_MODEL), BF16)],
        compiler_params=_cparams(("parallel", "arbitrary")),
        name="out_mlp",
    )(x, o_nsa, o_dn, o_s5, wn, wd, ws, ln2, up, dn, lnf)


_NSA_W = NSA_HEADS * HEAD_DIM
_KV_W = 6 * NSA_KV_HEADS * HEAD_DIM
_GATE_W = NSA_HEADS * 3
_OFF_KV = _NSA_W
_OFF_GATE = _OFF_KV + _KV_W
_OFF_DN = _OFF_GATE + _GATE_W
_OFF_A = _OFF_DN + 3 * DN_WIDTH
_OFF_B = _OFF_A + DN_HEADS
_OFF_Z = _OFF_B + DN_HEADS
_OFF_U = _OFF_Z + DN_WIDTH


def _pad_heads(w, axis):
    w = jnp.moveaxis(w, axis, 0).reshape((NSA_HEADS, HEAD_DIM) + w.shape[:axis] + w.shape[axis + 1:])
    rest = [(0, 0)] * (w.ndim - 2)
    out = jnp.concatenate(
        [jnp.pad(w[g * NSA_GROUP:(g + 1) * NSA_GROUP],
                 [(0, 0), (g * HEAD_DIM, LANE - (g + 1) * HEAD_DIM)] + rest) for g in range(NSA_KV_HEADS)], axis=0)
    out = out.reshape((NSA_HEADS * LANE,) + w.shape[2:])
    return jnp.moveaxis(out, 0, axis)


def _layer_weights(w_in, w_out, wck, wcv):
    wq = _pad_heads(w_in[:, :_NSA_W], 1).astype(BF16)
    wkv = w_in[:, _OFF_KV:_OFF_GATE].astype(BF16)
    wdn = w_in[:, _OFF_DN:_OFF_A].astype(BF16)
    wz = w_in[:, _OFF_Z:_OFF_U].astype(BF16)
    wu = w_in[:, _OFF_U:].astype(BF16)
    wsm = jnp.zeros((D_MODEL, LANE), F32)
    wsm = wsm.at[:, :_GATE_W].set(w_in[:, _OFF_GATE:_OFF_DN])
    wsm = wsm.at[:, _GATE_W:_GATE_W + 2 * DN_HEADS].set(w_in[:, _OFF_A:_OFF_Z]).astype(BF16)
    wexp = jnp.concatenate([jnp.repeat(wck.T, HEAD_DIM, axis=1), jnp.repeat(wcv.T, HEAD_DIM, axis=1)], axis=1)
    won = _pad_heads(w_out[:_NSA_W], 0).astype(BF16)
    wod = w_out[_NSA_W:_NSA_W + DN_WIDTH].astype(BF16)
    wos = w_out[_NSA_W + DN_WIDTH:].astype(BF16)
    return wq, wkv, wdn, wz, wu, wsm, wexp, won, wod, wos


def kernel(x_prompt, x_sample, cache_nsa_kv, cache_win_kv, state_dn_conv, state_dn, state_s5_re, state_s5_im,
           page_table, ln1, ln2, ln_f, w_in, w_out, nsa_wck, nsa_wcv, dn_conv_w, dn_a_log, dn_dt_bias,
           dn_norm_w, s5_lambda_re, s5_lambda_im, s5_log_step, s5_b_re, s5_b_im, s5_c_re, s5_c_im, s5_d,
           s5_w_glu, mlp_up, mlp_down):
    nbp, lp, _ = x_prompt.shape
    nbs, ls, _ = x_sample.shape
    depth = w_in.shape[0]
    rows_p = nbp * lp
    rows_s = nbs * ls
    n_phys, page = cache_nsa_kv.shape[1], cache_nsa_kv.shape[2]
    past = page_table.shape[1] * page
    wlen = cache_win_kv.shape[2]
    G, dh = NSA_KV_HEADS, HEAD_DIM
    assert lp % (S5_SEGS * 32) == 0 and nbs % 8 == 0 and ls == 8 and wlen == WINDOW and lp >= WINDOW

    xp = x_prompt.reshape(rows_p, D_MODEL)
    xs = x_sample.reshape(rows_s, D_MODEL)
    cache_t = cache_nsa_kv.transpose(0, 1, 3, 4, 5, 2).reshape(depth, n_phys, 4 * G * dh, page)
    win_t = cache_win_kv.transpose(0, 1, 3, 4, 5, 2).reshape(depth, nbs, 2 * G * dh, wlen)
    outs_p, outs_s = [], []
    kvt_main = nwin_t = None
    for l in range(depth):
        wq, wkv, wdn, wz, wu, wsm, wexp, won, wod, wos = _layer_weights(w_in[l], w_out[l], nsa_wck[l], nsa_wcv[l])
        pw = (ln1[l][None], wq, wkv, wdn, wz, wu, wsm, wexp)
        qh_p, kv_p, dnx_p, z_p, u_p, small_p, pool_p, kvt_main, kvt_win, kaug, vt = _proj(
            xp, *pw, nseq=nbp, prev_kvt=kvt_main, sel_tile=NSA_TQ)
        qh_s, kv_s, dnx_s, z_s, u_s, small_s, _, _, _ = _proj(xs, *pw, nseq=1)

        oc, sel, anyblk = _cmp_prompt(qh_p, pool_p, nbp, lp, NSA_TQ)
        osel = _sel_prompt(qh_p, sel, anyblk, kaug, vt, nbp, lp, NSA_TQ)
        on_p = _win_prompt(qh_p, kv_p, kvt_win, oc, osel, small_p, nbp, lp)
        wt = _pool_weights_t(nsa_wck[l], nsa_wcv[l], past)
        on_s, nwin_t = _nsa_sample(page_table, qh_s, kv_s, small_s, cache_t, win_t, wt, nwin_t,
                                   layer=l, lq=ls, past=past)

        gdn_w = (dn_conv_w[l].T, dn_a_log[l][None], dn_dt_bias[l][None], dn_norm_w[l][None])
        od_p, dn_p = _gdn_prompt(dnx_p, small_p, z_p, *gdn_w, jnp.zeros((nbp, DN_HEADS, dh, dh), F32), nb=nbp, lq=lp)
        od_s, dn_s = _gdn(dnx_s, _sample_conv_hist(state_dn_conv[l]), small_s, z_s, *gdn_w, state_dn[l],
                          seg=ls, nb=rows_s // GDN_ROWS, nchunk=1, row0=0,
                          hist_map=lambda b, c: (b, 0), zero_first_hist=False)

        prm = _s5_prep(s5_lambda_re[l], s5_lambda_im[l], s5_log_step[l], s5_b_re[l], s5_b_im[l],
                       s5_c_re[l], s5_c_im[l], s5_d[l], s5_w_glu[l])
        os_p, s5r_p, s5i_p = _s5_prompt(u_p, prm, nbp, lp)
        os_s, s5r_s, s5i_s = _s5_sample(u_s, state_s5_re[l].reshape(nbs, S5_LANES),
                                        state_s5_im[l].reshape(nbs, S5_LANES), prm, nbs, ls)

        mw = (won, wod, wos, ln2[l][None], mlp_up[l].astype(BF16), mlp_down[l].astype(BF16), ln_f[None])
        xp = _mlp(xp, on_p, od_p, os_p, *mw, final_norm=(l == depth - 1))
        xs = _mlp(xs, on_s, od_s, os_s, *mw, final_norm=(l == depth - 1))

        kv_s6 = kv_s.reshape(nbs, ls, 6, G, dh)
        win_p = kvt_win[:, :, lp - WINDOW:].reshape(nbp, 2, G, dh, WINDOW).transpose(0, 4, 1, 2, 3)
        outs_p.append((win_p, dnx_p.reshape(nbp, lp, 3 * DN_WIDTH)[:, -(DN_CONV_W - 1):], dn_p,
                       s5r_p.reshape(nbp, S5_GROUPS, S5_STATE), s5i_p.reshape(nbp, S5_GROUPS, S5_STATE)))
        outs_s.append((kv_s6[:, :, :4], dnx_s.reshape(nbs, ls, 3 * DN_WIDTH)[:, -(DN_CONV_W - 1):], dn_s,
                       s5r_s.reshape(nbs, S5_GROUPS, S5_STATE), s5i_s.reshape(nbs, S5_GROUPS, S5_STATE)))

    stack = lambda outs, i: jnp.stack([o[i] for o in outs], axis=0)
    kv_prompt = kvt_main.reshape(depth, nbp, 4, G, dh, lp).transpose(0, 1, 5, 2, 3, 4)
    win_sample = nwin_t.reshape(depth, nbs, 2, G, dh, wlen).transpose(0, 1, 5, 2, 3, 4)
    return (xp.reshape(nbp, lp, D_MODEL), xs.reshape(nbs, ls, D_MODEL),
            kv_prompt, stack(outs_s, 0), stack(outs_p, 0), win_sample,
            stack(outs_p, 1), stack(outs_s, 1), stack(outs_p, 2), stack(outs_s, 2),
            stack(outs_p, 3), stack(outs_s, 3), stack(outs_p, 4), stack(outs_s, 4))
```
